```python
import math
import jax
import jax.numpy as jnp
from jax import lax
import numpy as np

D_MODEL = 2048
BATCH = 4
SEQ = 2048
DEPTH = 2
DEC_BATCH = 32
DEC_SEQ = 8
PAST_LEN = 8192
PAGE_SIZE = 128

N_A_LAYERS = DEPTH // 2
N_B_LAYERS = DEPTH - N_A_LAYERS

GLA_HEADS = 4
GLA_DK = D_MODEL // (2 * GLA_HEADS)
GLA_DV = D_MODEL // GLA_HEADS
GLA_GATE_RANK = 16
GLA_TAU = 16.0
GLA_CHUNK = 64

NSA_HEADS = 16
NSA_GROUPS = 4
NSA_HPG = NSA_HEADS // NSA_GROUPS
NSA_HEAD_DIM = D_MODEL // NSA_HEADS
CMP_STRIDE = 16
CMP_BLOCK = 2 * CMP_STRIDE
CMP_HIDDEN = 2 * NSA_HEAD_DIM
SEL_BLOCK = 64
N_SEL = 16
WINDOW = 512
Q_BLOCK = 32

REL_BUCKETS = 32
REL_MAX_DIST = 128

PEER_HEADS = 8
PEER_NKEYS = 128
PEER_EXPERTS = PEER_NKEYS * PEER_NKEYS
PEER_TOPK = 16
PEER_KEY_DIM = 256
PEER_TOK_BLOCK = 128

EPS = 1e-6
NEG = -1e30
BIG = 1e30

kernel_name = 'yoco_gla_nsa_peer_step'


def rmsnorm(x, g):
    xf = x.astype(jnp.float32)
    y = xf * lax.rsqrt(jnp.mean(xf * xf, axis=-1, keepdims=True) + EPS)
    return (y * g.astype(jnp.float32)).astype(x.dtype)


def masked_softmax(s, mask):
    s = jnp.where(mask, s, NEG)
    m = jnp.max(s, axis=-1, keepdims=True)
    e = jnp.where(mask, jnp.exp(s - m), 0.0)
    return e / jnp.maximum(jnp.sum(e, axis=-1, keepdims=True), 1e-30)


def t5_bucket(rel):
    n = jnp.maximum(rel, 0)
    exact = REL_BUCKETS // 2
    nf = jnp.maximum(n, exact).astype(jnp.float32)
    large = exact + (jnp.log(nf / exact) / math.log(REL_MAX_DIST / exact) * (REL_BUCKETS - exact)).astype(jnp.int32)
    large = jnp.minimum(large, REL_BUCKETS - 1)
    return jnp.where(n < exact, n, large)


def head_bias(table, rel):
    b = table[t5_bucket(rel)].astype(jnp.float32)
    return jnp.transpose(b, (2, 0, 1)).reshape(NSA_GROUPS, NSA_HPG, rel.shape[0], rel.shape[1])


def gla_chunked(q, k, v, g, s0):
    B, H, T, DK = q.shape
    C = GLA_CHUNK if T % GLA_CHUNK == 0 else T
    n = T // C

    def chunks(t):
        return jnp.moveaxis(t.reshape(B, H, n, C, t.shape[-1]), 2, 0)

    causal = jnp.tril(jnp.ones((C, C), dtype=bool))

    def step(s, inp):
        qc, kc, vc, gc = inp
        b = jnp.cumsum(gc, axis=2)
        b_last = b[:, :, -1:, :]
        qe = qc * jnp.exp(b)
        a = jnp.einsum('bhid,bhjd->bhij', qe, kc * jnp.exp(-b))
        a = jnp.where(causal, a, 0.0)
        o = jnp.einsum('bhij,bhje->bhie', a, vc) + jnp.einsum('bhid,bhde->bhie', qe, s)
        s = s * jnp.exp(b_last[:, :, 0, :, None]) + jnp.einsum('bhjd,bhje->bhde', kc * jnp.exp(b_last - b), vc)
        return s, o

    s_fin, o = lax.scan(step, s0, (chunks(q), chunks(k), chunks(v), chunks(g)))
    o = jnp.moveaxis(o, 0, 2).reshape(B, H, T, v.shape[-1])
    return o, s_fin


def gla_mixer(xn, s0, w_in, w_gate_up, b_gate, norm_g, w_out):
    B, T, _ = xn.shape
    dk_all = GLA_HEADS * GLA_DK
    dv_all = GLA_HEADS * GLA_DV
    proj = xn @ w_in
    q, k, v, r, gz = jnp.split(proj, [dk_all, 2 * dk_all, 2 * dk_all + dv_all, 2 * dk_all + 2 * dv_all], axis=-1)
    glog = jax.nn.log_sigmoid((gz @ w_gate_up + b_gate).astype(jnp.float32)) / GLA_TAU

    def heads(t):
        return jnp.transpose(t.reshape(B, T, GLA_HEADS, -1), (0, 2, 1, 3)).astype(jnp.float32)

    o, s_new = gla_chunked(heads(q) * GLA_DK ** -0.5, heads(k), heads(v), heads(glog), s0.astype(jnp.float32))
    o = o * lax.rsqrt(jnp.mean(o * o, axis=-1, keepdims=True) + EPS) * norm_g.astype(jnp.float32)
    o = jnp.transpose(o, (0, 2, 1, 3)).reshape(B, T, dv_all).astype(xn.dtype)
    o = o * jax.nn.silu(r)
    return o @ w_out, s_new.astype(s0.dtype)


def peer_ffn(xn, w_q, subkeys, u_tab, v_tab):
    B, T, D = xn.shape
    n = B * T
    xt = xn.reshape(n, D)
    q = (xt @ w_q).reshape(n, PEER_HEADS, 2, PEER_KEY_DIM // 2)
    s = jnp.einsum('nhcd,hckd->nhck', q, subkeys).astype(jnp.float32)
    s1, i1 = lax.top_k(s[:, :, 0], PEER_TOPK)
    s2, i2 = lax.top_k(s[:, :, 1], PEER_TOPK)
    n_cand = PEER_TOPK * PEER_TOPK
    cand = (s1[..., :, None] + s2[..., None, :]).reshape(n, PEER_HEADS, n_cand)
    cidx = (i1[..., :, None] * PEER_NKEYS + i2[..., None, :]).reshape(n, PEER_HEADS, n_cand)
    top, pos = lax.top_k(cand, PEER_TOPK)
    K = PEER_HEADS * PEER_TOPK
    eidx = jnp.take_along_axis(cidx, pos, axis=-1).reshape(n, K)
    gate = jax.nn.softmax(top, axis=-1).reshape(n, K)
    tb = PEER_TOK_BLOCK if n % PEER_TOK_BLOCK == 0 else n

    def block(args):
        xb, eb, gb = args
        hid = jnp.einsum('td,tkd->tk', xb, u_tab[eb]).astype(jnp.float32)
        a = (jax.nn.gelu(hid, approximate=False) * gb).astype(xb.dtype)
        return jnp.einsum('tk,tkd->td', a, v_tab[eb])

    out = lax.map(block, (xt.reshape(-1, tb, D), eidx.reshape(-1, tb, K), gate.reshape(-1, tb, K)))
    return out.reshape(B, T, D)


def compress_rows(rows, w1, b1, w2, b2, pe):
    B, L, G, dh = rows.shape
    nch = L // CMP_STRIDE
    c = rows[:, :nch * CMP_STRIDE].reshape(B, nch, CMP_STRIDE, G, dh)
    w1h = w1.reshape(2, CMP_STRIDE, dh, CMP_HIDDEN)
    peh = pe.reshape(2, CMP_STRIDE, dh)
    h_first = jnp.einsum('bcsgd,sdf->bcgf', c + peh[0][:, None, :], w1h[0])
    h_second = jnp.einsum('bcsgd,sdf->bcgf', c + peh[1][:, None, :], w1h[1])
    hid = jax.nn.gelu(h_first[:, :-1] + h_second[:, 1:] + b1, approximate=False)
    return hid @ w2 + b2


def shared_kv(h, past_cmp, past_slc, win_buf, P):
    B, T, _ = h.shape
    rows = (rmsnorm(h, P['norm_kv']) @ P['kv_w']).reshape(B, T, 3, 2, NSA_GROUPS, NSA_HEAD_DIM)
    cmp_rows, slc_rows, win_rows = rows[:, :, 0], rows[:, :, 1], rows[:, :, 2]
    full_cmp = jnp.concatenate([past_cmp, cmp_rows], axis=1)
    full_slc = jnp.concatenate([past_slc, slc_rows], axis=1)
    win_all = jnp.concatenate([win_buf, win_rows], axis=1)
    win_pad = jnp.pad(win_all, ((0, 0), (WINDOW - win_buf.shape[1], 0), (0, 0), (0, 0), (0, 0)))
    kc = compress_rows(full_cmp[:, :, 0], P['cmp_k_w1'], P['cmp_k_b1'], P['cmp_k_w2'], P['cmp_k_b2'], P['cmp_k_pe'])
    vc = compress_rows(full_cmp[:, :, 1], P['cmp_v_w1'], P['cmp_v_b1'], P['cmp_v_w2'], P['cmp_v_b2'], P['cmp_v_pe'])
    n_keep = min(WINDOW, win_all.shape[1])
    win_new = win_all[:, win_all.shape[1] - n_keep:]
    kv = (kc, vc, full_slc[:, :, 0], full_slc[:, :, 1], win_pad[:, :, 0], win_pad[:, :, 1])
    return kv, cmp_rows, slc_rows, win_new


def nsa_attend(q, gates, kc, vc, ks, vs, kw, vw, q_start, table):
    B, T = q.shape[0], q.shape[1]
    G, dh = NSA_GROUPS, NSA_HEAD_DIM
    L = ks.shape[1]
    Nc = kc.shape[1]
    Ns = -(-L // SEL_BLOCK)
    pad = ((0, 0), (0, Ns * SEL_BLOCK - L), (0, 0), (0, 0))
    ks_b = jnp.pad(ks, pad).reshape(B, Ns, SEL_BLOCK, G, dh)
    vs_b = jnp.pad(vs, pad).reshape(B, Ns, SEL_BLOCK, G, dh)
    qh = jnp.transpose(q.reshape(B, T, G, NSA_HPG, dh), (0, 2, 3, 1, 4))
    gh = jnp.transpose(gates.reshape(B, T, G, NSA_HPG, 3), (0, 2, 3, 1, 4))
    kw_t = jnp.transpose(kw, (0, 2, 1, 3))
    vw_t = jnp.transpose(vw, (0, 2, 1, 3))
    cend = jnp.arange(Nc) * CMP_STRIDE + (CMP_BLOCK - 1)
    cstart = jnp.arange(Nc)[:, None] * CMP_STRIDE
    sstart = jnp.arange(Ns)[None, :] * SEL_BLOCK
    overlap = ((cstart < sstart + SEL_BLOCK) & (cstart + CMP_BLOCK > sstart)).astype(jnp.float32)
    n_pick = min(N_SEL, Ns)
    tab_g = table.astype(jnp.float32).reshape(REL_BUCKETS, G, NSA_HPG)
    bi = jnp.arange(B)[:, None, None, None]
    gi = jnp.arange(G)[None, :, None, None]
    blk = jnp.arange(Ns)[None, :]
    qb = Q_BLOCK if T % Q_BLOCK == 0 else T

    def body(o0):
        qblk = lax.dynamic_slice_in_dim(qh, o0, qb, axis=3)
        gblk = lax.dynamic_slice_in_dim(gh, o0, qb, axis=3)
        qpos = q_start + o0 + jnp.arange(qb)
        rel_c = qpos[:, None] - cend[None, :]
        s_c = jnp.einsum('bghqd,bngd->bghqn', qblk, kc).astype(jnp.float32) + head_bias(table, rel_c)
        p_c = masked_softmax(s_c, rel_c >= 0)
        o_c = jnp.einsum('bghqn,bngd->bghqd', p_c.astype(vc.dtype), vc)
        imp = jnp.einsum('bgqn,ns->bgqs', p_c.sum(axis=2), overlap)
        cur = qpos[:, None] // SEL_BLOCK
        valid = blk * SEL_BLOCK <= qpos[:, None]
        forced = (blk == 0) | (blk == cur) | (blk == cur - 1)
        score = jnp.where(valid, jnp.where(forced, BIG, imp), -BIG)
        top_s, idx = lax.top_k(score, n_pick)
        ok = top_s > -0.5 * BIG
        kg = ks_b[bi, idx, :, gi, :].reshape(B, G, qb, n_pick * SEL_BLOCK, dh)
        vg = vs_b[bi, idx, :, gi, :].reshape(B, G, qb, n_pick * SEL_BLOCK, dh)
        kpos = (idx[..., None] * SEL_BLOCK + jnp.arange(SEL_BLOCK)).reshape(B, G, qb, n_pick * SEL_BLOCK)
        rel_s = qpos[None, None, :, None] - kpos
        mask_s = jnp.repeat(ok, SEL_BLOCK, axis=-1) & (rel_s >= 0)
        bias_s = jnp.einsum('bgqkn,ngh->bghqk', jax.nn.one_hot(t5_bucket(rel_s), REL_BUCKETS, dtype=jnp.float32), tab_g)
        s_s = jnp.einsum('bghqd,bgqkd->bghqk', qblk, kg).astype(jnp.float32) + bias_s
        p_s = masked_softmax(s_s, mask_s[:, :, None])
        o_s = jnp.einsum('bghqk,bgqkd->bghqd', p_s.astype(vg.dtype), vg)
        kwb = lax.dynamic_slice_in_dim(kw_t, o0, qb + WINDOW, axis=2)
        vwb = lax.dynamic_slice_in_dim(vw_t, o0, qb + WINDOW, axis=2)
        wpos = q_start - WINDOW + o0 + jnp.arange(qb + WINDOW)
        rel_w = qpos[:, None] - wpos[None, :]
        mask_w = (rel_w >= 0) & (rel_w < WINDOW) & (wpos[None, :] >= 0)
        s_w = jnp.einsum('bghqd,bgkd->bghqk', qblk, kwb).astype(jnp.float32) + head_bias(table, rel_w)
        p_w = masked_softmax(s_w, mask_w)
        o_w = jnp.einsum('bghqk,bgkd->bghqd', p_w.astype(vwb.dtype), vwb)
        return gblk[..., 0:1] * o_c + gblk[..., 1:2] * o_s + gblk[..., 2:3] * o_w

    out = lax.map(body, jnp.arange(T // qb) * qb)
    return jnp.transpose(out, (1, 0, 4, 2, 3, 5)).reshape(B, T, NSA_HEADS * dh)


def run_trunk(x, gla_s0, past_cmp, past_slc, win_buf, P):
    q_start = past_slc.shape[1]
    B, T, _ = x.shape
    h = x
    gla_states = []
    kv = None
    for layer in range(DEPTH):
        xn = rmsnorm(h, P['norm_mix'][layer])
        if layer < N_A_LAYERS:
            o, s = gla_mixer(xn, gla_s0[layer], P['gla_w_in'][layer], P['gla_w_gate_up'][layer],
                             P['gla_b_gate'][layer], P['gla_norm'][layer], P['gla_w_out'][layer])
            gla_states.append(s)
        else:
            j = layer - N_A_LAYERS
            qd = NSA_HEADS * NSA_HEAD_DIM
            proj = xn @ P['nsa_w_in'][j]
            q = proj[..., :qd].reshape(B, T, NSA_HEADS, NSA_HEAD_DIM) * NSA_HEAD_DIM ** -0.5
            gates = jax.nn.sigmoid(proj[..., qd:].astype(jnp.float32)).reshape(B, T, NSA_HEADS, 3).astype(x.dtype)
            o = nsa_attend(q, gates, *kv, q_start, P['rel_bias']) @ P['nsa_w_out'][j]
        h = h + o
        h = h + peer_ffn(rmsnorm(h, P['norm_ffn'][layer]), P['peer_w_q'][layer], P['peer_subkeys'][layer],
                         P['peer_u'][layer], P['peer_v'][layer])
        if layer == N_A_LAYERS - 1:
            kv, cmp_rows, slc_rows, win_new = shared_kv(h, past_cmp, past_slc, win_buf, P)
    return rmsnorm(h, P['norm_final']), jnp.stack(gla_states), cmp_rows, slc_rows, win_new


def setup_inputs(seed: int = 0) -> dict:
    key = jax.random.key(seed)
    k = jax.random.split(key, 34)

    def nrm(i, shape, scale):
        return scale * jax.random.normal(k[i], shape, jnp.float32)

    def gain(i, shape):
        return 1.0 + nrm(i, shape, 0.02)

    n_pages = PAST_LEN // PAGE_SIZE
    n_used = DEC_BATCH * n_pages
    n_pool = n_used + (n_used + 3) // 4
    win_rows = min(WINDOW, PAST_LEN)
    kv_row = (2, NSA_GROUPS, NSA_HEAD_DIM)
    dk_all = GLA_HEADS * GLA_DK
    dv_all = GLA_HEADS * GLA_DV
    in_a = 2 * dk_all + 2 * dv_all + GLA_GATE_RANK
    qd = NSA_HEADS * NSA_HEAD_DIM
    flat_blk = CMP_BLOCK * NSA_HEAD_DIM
    page_table = jax.random.permutation(k[6], n_pool)[:n_used].reshape(DEC_BATCH, n_pages).astype(jnp.int32)
    return {
        'x_prompt': nrm(0, (BATCH, SEQ, D_MODEL), 1.0),
        'x_sample': nrm(1, (DEC_BATCH, DEC_SEQ, D_MODEL), 1.0),
        'state_gla': nrm(2, (N_A_LAYERS, DEC_BATCH, GLA_HEADS, GLA_DK, GLA_DV), 1.0),
        'cache_cmp_kv': nrm(3, (n_pool, PAGE_SIZE) + kv_row, 1.0),
        'cache_slc_kv': nrm(4, (n_pool, PAGE_SIZE) + kv_row, 1.0),
        'cache_win_kv': nrm(5, (DEC_BATCH, win_rows) + kv_row, 1.0),
        'page_table': page_table,
        'norm_mix': gain(7, (DEPTH, D_MODEL)),
        'norm_ffn': gain(8, (DEPTH, D_MODEL)),
        'norm_kv': gain(9, (D_MODEL,)),
        'norm_final': gain(10, (D_MODEL,)),
        'gla_w_in': nrm(11, (N_A_LAYERS, D_MODEL, in_a), D_MODEL ** -0.5),
        'gla_w_gate_up': nrm(12, (N_A_LAYERS, GLA_GATE_RANK, dk_all), GLA_GATE_RANK ** -0.5),
        'gla_b_gate': nrm(13, (N_A_LAYERS, dk_all), 0.1),
        'gla_norm': gain(14, (N_A_LAYERS, GLA_DV)),
        'gla_w_out': nrm(15, (N_A_LAYERS, dv_all, D_MODEL), dv_all ** -0.5),
        'kv_w': nrm(16, (D_MODEL, 3 * 2 * NSA_GROUPS * NSA_HEAD_DIM), D_MODEL ** -0.5),
        'cmp_k_w1': nrm(17, (flat_blk, CMP_HIDDEN), flat_blk ** -0.5),
        'cmp_k_b1': nrm(18, (CMP_HIDDEN,), 0.01),
        'cmp_k_w2': nrm(19, (CMP_HIDDEN, NSA_HEAD_DIM), CMP_HIDDEN ** -0.5),
        'cmp_k_b2': nrm(20, (NSA_HEAD_DIM,), 0.01),
        'cmp_k_pe': nrm(21, (CMP_BLOCK, NSA_HEAD_DIM), 0.1),
        'cmp_v_w1': nrm(22, (flat_blk, CMP_HIDDEN), flat_blk ** -0.5),
        'cmp_v_b1': nrm(23, (CMP_HIDDEN,), 0.01),
        'cmp_v_w2': nrm(24, (CMP_HIDDEN, NSA_HEAD_DIM), CMP_HIDDEN ** -0.5),
        'cmp_v_b2': nrm(25, (NSA_HEAD_DIM,), 0.01),
        'cmp_v_pe': nrm(26, (CMP_BLOCK, NSA_HEAD_DIM), 0.1),
        'nsa_w_in': nrm(27, (N_B_LAYERS, D_MODEL, qd + 3 * NSA_HEADS), D_MODEL ** -0.5),
        'nsa_w_out': nrm(28, (N_B_LAYERS, qd, D_MODEL), qd ** -0.5),
        'rel_bias': nrm(29, (REL_BUCKETS, NSA_HEADS), 0.5),
        'peer_w_q': nrm(30, (DEPTH, D_MODEL, PEER_HEADS * PEER_KEY_DIM), D_MODEL ** -0.5),
        'peer_subkeys': nrm(31, (DEPTH, PEER_HEADS, 2, PEER_NKEYS, PEER_KEY_DIM // 2), (PEER_KEY_DIM // 2) ** -0.5),
        'peer_u': nrm(32, (DEPTH, PEER_EXPERTS, D_MODEL), D_MODEL ** -0.5),
        'peer_v': nrm(33, (DEPTH, PEER_EXPERTS, D_MODEL), PEER_HEADS ** -0.5),
    }


def reference(x_prompt, x_sample, state_gla, cache_cmp_kv, cache_slc_kv, cache_win_kv, page_table,
              norm_mix, norm_ffn, norm_kv, norm_final, gla_w_in, gla_w_gate_up, gla_b_gate, gla_norm, gla_w_out,
              kv_w, cmp_k_w1, cmp_k_b1, cmp_k_w2, cmp_k_b2, cmp_k_pe, cmp_v_w1, cmp_v_b1, cmp_v_w2, cmp_v_b2, cmp_v_pe,
              nsa_w_in, nsa_w_out, rel_bias, peer_w_q, peer_subkeys, peer_u, peer_v):
    P = dict(norm_mix=norm_mix, norm_ffn=norm_ffn, norm_kv=norm_kv, norm_final=norm_final,
             gla_w_in=gla_w_in, gla_w_gate_up=gla_w_gate_up, gla_b_gate=gla_b_gate, gla_norm=gla_norm,
             gla_w_out=gla_w_out, kv_w=kv_w,
             cmp_k_w1=cmp_k_w1, cmp_k_b1=cmp_k_b1, cmp_k_w2=cmp_k_w2, cmp_k_b2=cmp_k_b2, cmp_k_pe=cmp_k_pe,
             cmp_v_w1=cmp_v_w1, cmp_v_b1=cmp_v_b1, cmp_v_w2=cmp_v_w2, cmp_v_b2=cmp_v_b2, cmp_v_pe=cmp_v_pe,
             nsa_w_in=nsa_w_in, nsa_w_out=nsa_w_out, rel_bias=rel_bias,
             peer_w_q=peer_w_q, peer_subkeys=peer_subkeys, peer_u=peer_u, peer_v=peer_v)
    past_rows = page_table.shape[1] * PAGE_SIZE

    def gather_pages(cache):
        return cache[page_table].reshape(DEC_BATCH, past_rows, 2, NSA_GROUPS, NSA_HEAD_DIM)

    empty = jnp.zeros((BATCH, 0, 2, NSA_GROUPS, NSA_HEAD_DIM), x_prompt.dtype)
    gla_zero = jnp.zeros((N_A_LAYERS, BATCH, GLA_HEADS, GLA_DK, GLA_DV), x_prompt.dtype)
    y_prompt, gla_p, cmp_p, slc_p, win_p = run_trunk(x_prompt, gla_zero, empty, empty, empty, P)
    y_sample, gla_s, cmp_s, slc_s, win_s = run_trunk(x_sample, state_gla, gather_pages(cache_cmp_kv),
                                                     gather_pages(cache_slc_kv), cache_win_kv, P)
    return (y_prompt, y_sample, gla_p, gla_s, cmp_p, cmp_s, slc_p, slc_s, win_p, win_s)
```

```python
import functools
import math

import jax
import jax.numpy as jnp
from jax import lax
from jax.experimental import pallas as pl
from jax.experimental.pallas import tpu as pltpu

D_MODEL = 2048
BATCH = 4
SEQ = 2048
DEPTH = 2
DEC_BATCH = 32
DEC_SEQ = 8
PAST_LEN = 8192
PAGE_SIZE = 128
N_A_LAYERS = DEPTH // 2
N_B_LAYERS = DEPTH - N_A_LAYERS
GLA_HEADS = 4
GLA_DK = D_MODEL // (2 * GLA_HEADS)
GLA_DV = D_MODEL // GLA_HEADS
GLA_GATE_RANK = 16
GLA_TAU = 16.0
GLA_CHUNK = 64
NSA_HEADS = 16
NSA_GROUPS = 4
NSA_HPG = NSA_HEADS // NSA_GROUPS
NSA_HEAD_DIM = D_MODEL // NSA_HEADS
CMP_STRIDE = 16
CMP_BLOCK = 2 * CMP_STRIDE
CMP_HIDDEN = 2 * NSA_HEAD_DIM
SEL_BLOCK = 64
N_SEL = 16
WINDOW = 512
Q_BLOCK = 32
REL_BUCKETS = 32
REL_MAX_DIST = 128
PEER_HEADS = 8
PEER_NKEYS = 128
PEER_EXPERTS = PEER_NKEYS * PEER_NKEYS
PEER_TOPK = 16
PEER_KEY_DIM = 256
PEER_TOK_BLOCK = 128
EPS = 1e-6
NEG = -1e30
BIG = 1e30

LANE = 128
VMEM_LIMIT = 48 * 1024 * 1024


def _mm_kernel(a_ref, b_ref, o_ref):
    a = a_ref[...].astype(jnp.bfloat16)
    b = b_ref[...].astype(jnp.bfloat16)
    o_ref[...] = jnp.dot(a, b, preferred_element_type=jnp.float32)


def pmm(a, b, tm=512, tn=512):
    M, K = a.shape
    N = b.shape[1]
    tm = min(tm, M)
    Mp = -(-M // tm) * tm
    Np = -(-N // LANE) * LANE
    tn = math.gcd(Np, tn)
    if Mp != M:
        a = jnp.pad(a, ((0, Mp - M), (0, 0)))
    if Np != N:
        b = jnp.pad(b, ((0, 0), (0, Np - N)))
    out = pl.pallas_call(
        _mm_kernel,
        grid=(Mp // tm, Np // tn),
        in_specs=[pl.BlockSpec((tm, K), lambda i, j: (i, 0)),
                  pl.BlockSpec((K, tn), lambda i, j: (0, j))],
        out_specs=pl.BlockSpec((tm, tn), lambda i, j: (i, j)),
        out_shape=jax.ShapeDtypeStruct((Mp, Np), jnp.float32),
        compiler_params=pltpu.CompilerParams(
            dimension_semantics=("parallel", "parallel"), vmem_limit_bytes=VMEM_LIMIT),
        name="proj_matmul",
    )(a, b)
    return out[:M, :N]


def mm(x, w):
    lead = x.shape[:-1]
    return pmm(x.reshape(-1, x.shape[-1]), w).reshape(lead + (w.shape[1],))


def rmsnorm(x, g):
    xf = x.astype(jnp.float32)
    y = xf * lax.rsqrt(jnp.mean(xf * xf, axis=-1, keepdims=True) + EPS)
    return (y * g.astype(jnp.float32)).astype(x.dtype)


def masked_softmax(s, mask):
    s = jnp.where(mask, s, NEG)
    m = jnp.max(s, axis=-1, keepdims=True)
    e = jnp.where(mask, jnp.exp(s - m), 0.0)
    return e / jnp.maximum(jnp.sum(e, axis=-1, keepdims=True), 1e-30)


def t5_bucket(rel):
    n = jnp.maximum(rel, 0)
    exact = REL_BUCKETS // 2
    nf = jnp.maximum(n, exact).astype(jnp.float32)
    large = exact + (jnp.log(nf / exact) / math.log(REL_MAX_DIST / exact) * (REL_BUCKETS - exact)).astype(jnp.int32)
    large = jnp.minimum(large, REL_BUCKETS - 1)
    return jnp.where(n < exact, n, large)


def head_bias(table, rel):
    b = table[t5_bucket(rel)].astype(jnp.float32)
    return jnp.transpose(b, (2, 0, 1)).reshape(NSA_GROUPS, NSA_HPG, rel.shape[0], rel.shape[1])


def gla_chunked(q, k, v, g, s0):
    B, H, T, DK = q.shape
    C = GLA_CHUNK if T % GLA_CHUNK == 0 else T
    n = T // C

    def chunks(t):
        return jnp.moveaxis(t.reshape(B, H, n, C, t.shape[-1]), 2, 0)

    causal = jnp.tril(jnp.ones((C, C), dtype=bool))

    def step(s, inp):
        qc, kc, vc, gc = inp
        b = jnp.cumsum(gc, axis=2)
        b_last = b[:, :, -1:, :]
        qe = qc * jnp.exp(b)
        a = jnp.einsum('bhid,bhjd->bhij', qe, kc * jnp.exp(-b))
        a = jnp.where(causal, a, 0.0)
        o = jnp.einsum('bhij,bhje->bhie', a, vc) + jnp.einsum('bhid,bhde->bhie', qe, s)
        s = s * jnp.exp(b_last[:, :, 0, :, None]) + jnp.einsum('bhjd,bhje->bhde', kc * jnp.exp(b_last - b), vc)
        return s, o

    s_fin, o = lax.scan(step, s0, (chunks(q), chunks(k), chunks(v), chunks(g)))
    o = jnp.moveaxis(o, 0, 2).reshape(B, H, T, v.shape[-1])
    return o, s_fin


def gla_mixer(xn, s0, w_in, w_gate_up, b_gate, norm_g, w_out):
    B, T, _ = xn.shape
    dk_all = GLA_HEADS * GLA_DK
    dv_all = GLA_HEADS * GLA_DV
    proj = mm(xn, w_in)
    q, k, v, r, gz = jnp.split(proj, [dk_all, 2 * dk_all, 2 * dk_all + dv_all, 2 * dk_all + 2 * dv_all], axis=-1)
    glog = jax.nn.log_sigmoid((gz @ w_gate_up + b_gate).astype(jnp.float32)) / GLA_TAU

    def heads(t):
        return jnp.transpose(t.reshape(B, T, GLA_HEADS, -1), (0, 2, 1, 3)).astype(jnp.float32)

    o, s_new = gla_chunked(heads(q) * GLA_DK ** -0.5, heads(k), heads(v), heads(glog), s0.astype(jnp.float32))
    o = o * lax.rsqrt(jnp.mean(o * o, axis=-1, keepdims=True) + EPS) * norm_g.astype(jnp.float32)
    o = jnp.transpose(o, (0, 2, 1, 3)).reshape(B, T, dv_all).astype(xn.dtype)
    o = o * jax.nn.silu(r)
    return mm(o, w_out), s_new.astype(s0.dtype)


def peer_ffn(xn, w_q, subkeys, u_tab, v_tab):
    B, T, D = xn.shape
    n = B * T
    xt = xn.reshape(n, D)
    q = pmm(xt, w_q).reshape(n, PEER_HEADS, 2, PEER_KEY_DIM // 2)
    s = jnp.einsum('nhcd,hckd->nhck', q, subkeys).astype(jnp.float32)
    s1, i1 = lax.top_k(s[:, :, 0], PEER_TOPK)
    s2, i2 = lax.top_k(s[:, :, 1], PEER_TOPK)
    n_cand = PEER_TOPK * PEER_TOPK
    cand = (s1[..., :, None] + s2[..., None, :]).reshape(n, PEER_HEADS, n_cand)
    cidx = (i1[..., :, None] * PEER_NKEYS + i2[..., None, :]).reshape(n, PEER_HEADS, n_cand)
    top, pos = lax.top_k(cand, PEER_TOPK)
    K = PEER_HEADS * PEER_TOPK
    eidx = jnp.take_along_axis(cidx, pos, axis=-1).reshape(n, K)
    gate = jax.nn.softmax(top, axis=-1).reshape(n, K)
    tb = PEER_TOK_BLOCK if n % PEER_TOK_BLOCK == 0 else n

    def block(args):
        xb, eb, gb = args
        hid = jnp.einsum('td,tkd->tk', xb, u_tab[eb]).astype(jnp.float32)
        a = (jax.nn.gelu(hid, approximate=False) * gb).astype(xb.dtype)
        return jnp.einsum('tk,tkd->td', a, v_tab[eb])

    out = lax.map(block, (xt.reshape(-1, tb, D), eidx.reshape(-1, tb, K), gate.reshape(-1, tb, K)))
    return out.reshape(B, T, D)


def compress_rows(rows, w1, b1, w2, b2, pe):
    B, L, G, dh = rows.shape
    nch = L // CMP_STRIDE
    c = rows[:, :nch * CMP_STRIDE].reshape(B, nch, CMP_STRIDE, G, dh)
    w1h = w1.reshape(2, CMP_STRIDE, dh, CMP_HIDDEN)
    peh = pe.reshape(2, CMP_STRIDE, dh)
    h_first = jnp.einsum('bcsgd,sdf->bcgf', c + peh[0][:, None, :], w1h[0])
    h_second = jnp.einsum('bcsgd,sdf->bcgf', c + peh[1][:, None, :], w1h[1])
    hid = jax.nn.gelu(h_first[:, :-1] + h_second[:, 1:] + b1, approximate=False)
    return hid @ w2 + b2


def shared_kv(h, past_cmp, past_slc, win_buf, P):
    B, T, _ = h.shape
    rows = mm(rmsnorm(h, P['norm_kv']), P['kv_w']).reshape(B, T, 3, 2, NSA_GROUPS, NSA_HEAD_DIM)
    cmp_rows, slc_rows, win_rows = rows[:, :, 0], rows[:, :, 1], rows[:, :, 2]
    full_cmp = jnp.concatenate([past_cmp, cmp_rows], axis=1)
    full_slc = jnp.concatenate([past_slc, slc_rows], axis=1)
    win_all = jnp.concatenate([win_buf, win_rows], axis=1)
    win_pad = jnp.pad(win_all, ((0, 0), (WINDOW - win_buf.shape[1], 0), (0, 0), (0, 0), (0, 0)))
    kc = compress_rows(full_cmp[:, :, 0], P['cmp_k_w1'], P['cmp_k_b1'], P['cmp_k_w2'], P['cmp_k_b2'], P['cmp_k_pe'])
    vc = compress_rows(full_cmp[:, :, 1], P['cmp_v_w1'], P['cmp_v_b1'], P['cmp_v_w2'], P['cmp_v_b2'], P['cmp_v_pe'])
    n_keep = min(WINDOW, win_all.shape[1])
    win_new = win_all[:, win_all.shape[1] - n_keep:]
    kv = (kc, vc, full_slc[:, :, 0], full_slc[:, :, 1], win_pad[:, :, 0], win_pad[:, :, 1])
    return kv, cmp_rows, slc_rows, win_new


def nsa_attend(q, gates, kc, vc, ks, vs, kw, vw, q_start, table):
    B, T = q.shape[0], q.shape[1]
    G, dh = NSA_GROUPS, NSA_HEAD_DIM
    L = ks.shape[1]
    Nc = kc.shape[1]
    Ns = -(-L // SEL_BLOCK)
    pad = ((0, 0), (0, Ns * SEL_BLOCK - L), (0, 0), (0, 0))
    ks_b = jnp.pad(ks, pad).reshape(B, Ns, SEL_BLOCK, G, dh)
    vs_b = jnp.pad(vs, pad).reshape(B, Ns, SEL_BLOCK, G, dh)
    qh = jnp.transpose(q.reshape(B, T, G, NSA_HPG, dh), (0, 2, 3, 1, 4))
    gh = jnp.transpose(gates.reshape(B, T, G, NSA_HPG, 3), (0, 2, 3, 1, 4))
    kw_t = jnp.transpose(kw, (0, 2, 1, 3))
    vw_t = jnp.transpose(vw, (0, 2, 1, 3))
    cend = jnp.arange(Nc) * CMP_STRIDE + (CMP_BLOCK - 1)
    cstart = jnp.arange(Nc)[:, None] * CMP_STRIDE
    sstart = jnp.arange(Ns)[None, :] * SEL_BLOCK
    overlap = ((cstart < sstart + SEL_BLOCK) & (cstart + CMP_BLOCK > sstart)).astype(jnp.float32)
    n_pick = min(N_SEL, Ns)
    tab_g = table.astype(jnp.float32).reshape(REL_BUCKETS, G, NSA_HPG)
    bi = jnp.arange(B)[:, None, None, None]
    gi = jnp.arange(G)[None, :, None, None]
    blk = jnp.arange(Ns)[None, :]
    qb = Q_BLOCK if T % Q_BLOCK == 0 else T

    def body(o0):
        qblk = lax.dynamic_slice_in_dim(qh, o0, qb, axis=3)
        gblk = lax.dynamic_slice_in_dim(gh, o0, qb, axis=3)
        qpos = q_start + o0 + jnp.arange(qb)
        rel_c = qpos[:, None] - cend[None, :]
        s_c = jnp.einsum('bghqd,bngd->bghqn', qblk, kc).astype(jnp.float32) + head_bias(table, rel_c)
        p_c = masked_softmax(s_c, rel_c >= 0)
        o_c = jnp.einsum('bghqn,bngd->bghqd', p_c.astype(vc.dtype), vc)
        imp = jnp.einsum('bgqn,ns->bgqs', p_c.sum(axis=2), overlap)
        cur = qpos[:, None] // SEL_BLOCK
        valid = blk * SEL_BLOCK <= qpos[:, None]
        forced = (blk == 0) | (blk == cur) | (blk == cur - 1)
        score = jnp.where(valid, jnp.where(forced, BIG, imp), -BIG)
        top_s, idx = lax.top_k(score, n_pick)
        ok = top_s > -0.5 * BIG
        kg = ks_b[bi, idx, :, gi, :].reshape(B, G, qb, n_pick * SEL_BLOCK, dh)
        vg = vs_b[bi, idx, :, gi, :].reshape(B, G, qb, n_pick * SEL_BLOCK, dh)
        kpos = (idx[..., None] * SEL_BLOCK + jnp.arange(SEL_BLOCK)).reshape(B, G, qb, n_pick * SEL_BLOCK)
        rel_s = qpos[None, None, :, None] - kpos
        mask_s = jnp.repeat(ok, SEL_BLOCK, axis=-1) & (rel_s >= 0)
        bias_s = jnp.einsum('bgqkn,ngh->bghqk', jax.nn.one_hot(t5_bucket(rel_s), REL_BUCKETS, dtype=jnp.float32), tab_g)
        s_s = jnp.einsum('bghqd,bgqkd->bghqk', qblk, kg).astype(jnp.float32) + bias_s
        p_s = masked_softmax(s_s, mask_s[:, :, None])
        o_s = jnp.einsum('bghqk,bgqkd->bghqd', p_s.astype(vg.dtype), vg)
        kwb = lax.dynamic_slice_in_dim(kw_t, o0, qb + WINDOW, axis=2)
        vwb = lax.dynamic_slice_in_dim(vw_t, o0, qb + WINDOW, axis=2)
        wpos = q_start - WINDOW + o0 + jnp.arange(qb + WINDOW)
        rel_w = qpos[:, None] - wpos[None, :]
        mask_w = (rel_w >= 0) & (rel_w < WINDOW) & (wpos[None, :] >= 0)
        s_w = jnp.einsum('bghqd,bgkd->bghqk', qblk, kwb).astype(jnp.float32) + head_bias(table, rel_w)
        p_w = masked_softmax(s_w, mask_w)
        o_w = jnp.einsum('bghqk,bgkd->bghqd', p_w.astype(vwb.dtype), vwb)
        return gblk[..., 0:1] * o_c + gblk[..., 1:2] * o_s + gblk[..., 2:3] * o_w

    out = lax.map(body, jnp.arange(T // qb) * qb)
    return jnp.transpose(out, (1, 0, 4, 2, 3, 5)).reshape(B, T, NSA_HEADS * dh)


def run_trunk(x, gla_s0, past_cmp, past_slc, win_buf, P):
    q_start = past_slc.shape[1]
    B, T, _ = x.shape
    h = x
    gla_states = []
    kv = None
    for layer in range(DEPTH):
        xn = rmsnorm(h, P['norm_mix'][layer])
        if layer < N_A_LAYERS:
            o, s = gla_mixer(xn, gla_s0[layer], P['gla_w_in'][layer], P['gla_w_gate_up'][layer],
                             P['gla_b_gate'][layer], P['gla_norm'][layer], P['gla_w_out'][layer])
            gla_states.append(s)
        else:
            j = layer - N_A_LAYERS
            qd = NSA_HEADS * NSA_HEAD_DIM
            proj = mm(xn, P['nsa_w_in'][j])
            q = proj[..., :qd].reshape(B, T, NSA_HEADS, NSA_HEAD_DIM) * NSA_HEAD_DIM ** -0.5
            gates = jax.nn.sigmoid(proj[..., qd:].astype(jnp.float32)).reshape(B, T, NSA_HEADS, 3).astype(x.dtype)
            o = mm(nsa_attend(q, gates, *kv, q_start, P['rel_bias']), P['nsa_w_out'][j])
        h = h + o
        h = h + peer_ffn(rmsnorm(h, P['norm_ffn'][layer]), P['peer_w_q'][layer], P['peer_subkeys'][layer],
                         P['peer_u'][layer], P['peer_v'][layer])
        if layer == N_A_LAYERS - 1:
            kv, cmp_rows, slc_rows, win_new = shared_kv(h, past_cmp, past_slc, win_buf, P)
    return rmsnorm(h, P['norm_final']), jnp.stack(gla_states), cmp_rows, slc_rows, win_new


def kernel(x_prompt, x_sample, state_gla, cache_cmp_kv, cache_slc_kv, cache_win_kv, page_table,
           norm_mix, norm_ffn, norm_kv, norm_final, gla_w_in, gla_w_gate_up, gla_b_gate, gla_norm, gla_w_out,
           kv_w, cmp_k_w1, cmp_k_b1, cmp_k_w2, cmp_k_b2, cmp_k_pe, cmp_v_w1, cmp_v_b1, cmp_v_w2, cmp_v_b2, cmp_v_pe,
           nsa_w_in, nsa_w_out, rel_bias, peer_w_q, peer_subkeys, peer_u, peer_v):
    P = dict(norm_mix=norm_mix, norm_ffn=norm_ffn, norm_kv=norm_kv, norm_final=norm_final,
             gla_w_in=gla_w_in, gla_w_gate_up=gla_w_gate_up, gla_b_gate=gla_b_gate, gla_norm=gla_norm,
             gla_w_out=gla_w_out, kv_w=kv_w,
             cmp_k_w1=cmp_k_w1, cmp_k_b1=cmp_k_b1, cmp_k_w2=cmp_k_w2, cmp_k_b2=cmp_k_b2, cmp_k_pe=cmp_k_pe,
             cmp_v_w1=cmp_v_w1, cmp_v_b1=cmp_v_b1, cmp_v_w2=cmp_v_w2, cmp_v_b2=cmp_v_b2, cmp_v_pe=cmp_v_pe,
             nsa_w_in=nsa_w_in, nsa_w_out=nsa_w_out, rel_bias=rel_bias,
             peer_w_q=peer_w_q, peer_subkeys=peer_subkeys, peer_u=peer_u, peer_v=peer_v)
    past_rows = page_table.shape[1] * PAGE_SIZE

    def gather_pages(cache):
        return cache[page_table].reshape(DEC_BATCH, past_rows, 2, NSA_GROUPS, NSA_HEAD_DIM)

    empty = jnp.zeros((BATCH, 0, 2, NSA_GROUPS, NSA_HEAD_DIM), x_prompt.dtype)
    gla_zero = jnp.zeros((N_A_LAYERS, BATCH, GLA_HEADS, GLA_DK, GLA_DV), x_prompt.dtype)
    y_prompt, gla_p, cmp_p, slc_p, win_p = run_trunk(x_prompt, gla_zero, empty, empty, empty, P)
    y_sample, gla_s, cmp_s, slc_s, win_s = run_trunk(x_sample, state_gla, gather_pages(cache_cmp_kv),
                                                     gather_pages(cache_slc_kv), cache_win_kv, P)
    return (y_prompt, y_sample, gla_p, gla_s, cmp_p, cmp_s, slc_p, slc_s, win_p, win_s)
```

```python
import functools
import math

import jax
import jax.numpy as jnp
from jax import lax
from jax.experimental import pallas as pl
from jax.experimental.pallas import tpu as pltpu

D_MODEL = 2048
BATCH = 4
SEQ = 2048
DEPTH = 2
DEC_BATCH = 32
DEC_SEQ = 8
PAST_LEN = 8192
PAGE_SIZE = 128
N_A_LAYERS = DEPTH // 2
N_B_LAYERS = DEPTH - N_A_LAYERS
GLA_HEADS = 4
GLA_DK = D_MODEL // (2 * GLA_HEADS)
GLA_DV = D_MODEL // GLA_HEADS
GLA_GATE_RANK = 16
GLA_TAU = 16.0
GLA_CHUNK = 64
NSA_HEADS = 16
NSA_GROUPS = 4
NSA_HPG = NSA_HEADS // NSA_GROUPS
NSA_HEAD_DIM = D_MODEL // NSA_HEADS
CMP_STRIDE = 16
CMP_BLOCK = 2 * CMP_STRIDE
CMP_HIDDEN = 2 * NSA_HEAD_DIM
SEL_BLOCK = 64
N_SEL = 16
WINDOW = 512
Q_BLOCK = 32
REL_BUCKETS = 32
REL_MAX_DIST = 128
PEER_HEADS = 8
PEER_NKEYS = 128
PEER_EXPERTS = PEER_NKEYS * PEER_NKEYS
PEER_TOPK = 16
PEER_KEY_DIM = 256
PEER_TOK_BLOCK = 128
EPS = 1e-6
NEG = -1e30
BIG = 1e30

LANE = 128
VMEM_LIMIT = 56 * 1024 * 1024
TQ = 128
assert REL_MAX_DIST <= TQ and WINDOW % TQ == 0 and TQ % SEL_BLOCK == 0


def _mm_kernel(a_ref, b_ref, o_ref):
    a = a_ref[...].astype(jnp.bfloat16)
    b = b_ref[...].astype(jnp.bfloat16)
    o_ref[...] = jnp.dot(a, b, preferred_element_type=jnp.float32)


def pmm(a, b, tm=512, tn=512):
    M, K = a.shape
    N = b.shape[1]
    tm = min(tm, M)
    Mp = -(-M // tm) * tm
    Np = -(-N // LANE) * LANE
    tn = math.gcd(Np, tn)
    if Mp != M:
        a = jnp.pad(a, ((0, Mp - M), (0, 0)))
    if Np != N:
        b = jnp.pad(b, ((0, 0), (0, Np - N)))
    out = pl.pallas_call(
        _mm_kernel,
        grid=(Mp // tm, Np // tn),
        in_specs=[pl.BlockSpec((tm, K), lambda i, j: (i, 0)),
                  pl.BlockSpec((K, tn), lambda i, j: (0, j))],
        out_specs=pl.BlockSpec((tm, tn), lambda i, j: (i, j)),
        out_shape=jax.ShapeDtypeStruct((Mp, Np), jnp.float32),
        compiler_params=pltpu.CompilerParams(
            dimension_semantics=("parallel", "parallel"), vmem_limit_bytes=VMEM_LIMIT),
        name="proj_matmul",
    )(a, b)
    return out[:M, :N]


def mm(x, w):
    lead = x.shape[:-1]
    return pmm(x.reshape(-1, x.shape[-1]), w).reshape(lead + (w.shape[1],))


def rmsnorm(x, g):
    xf = x.astype(jnp.float32)
    y = xf * lax.rsqrt(jnp.mean(xf * xf, axis=-1, keepdims=True) + EPS)
    return (y * g.astype(jnp.float32)).astype(x.dtype)


def masked_softmax(s, mask):
    s = jnp.where(mask, s, NEG)
    m = jnp.max(s, axis=-1, keepdims=True)
    e = jnp.where(mask, jnp.exp(s - m), 0.0)
    return e / jnp.maximum(jnp.sum(e, axis=-1, keepdims=True), 1e-30)


def t5_bucket(rel):
    n = jnp.maximum(rel, 0)
    exact = REL_BUCKETS // 2
    nf = jnp.maximum(n, exact).astype(jnp.float32)
    large = exact + (jnp.log(nf / exact) / math.log(REL_MAX_DIST / exact) * (REL_BUCKETS - exact)).astype(jnp.int32)
    large = jnp.minimum(large, REL_BUCKETS - 1)
    return jnp.where(n < exact, n, large)


def head_bias(table, rel):
    b = table[t5_bucket(rel)].astype(jnp.float32)
    return jnp.transpose(b, (2, 0, 1)).reshape(NSA_GROUPS, NSA_HPG, rel.shape[0], rel.shape[1])


def gla_chunked(q, k, v, g, s0):
    B, H, T, DK = q.shape
    C = GLA_CHUNK if T % GLA_CHUNK == 0 else T
    n = T // C

    def chunks(t):
        return jnp.moveaxis(t.reshape(B, H, n, C, t.shape[-1]), 2, 0)

    causal = jnp.tril(jnp.ones((C, C), dtype=bool))

    def step(s, inp):
        qc, kc, vc, gc = inp
        b = jnp.cumsum(gc, axis=2)
        b_last = b[:, :, -1:, :]
        qe = qc * jnp.exp(b)
        a = jnp.einsum('bhid,bhjd->bhij', qe, kc * jnp.exp(-b))
        a = jnp.where(causal, a, 0.0)
        o = jnp.einsum('bhij,bhje->bhie', a, vc) + jnp.einsum('bhid,bhde->bhie', qe, s)
        s = s * jnp.exp(b_last[:, :, 0, :, None]) + jnp.einsum('bhjd,bhje->bhde', kc * jnp.exp(b_last - b), vc)
        return s, o

    s_fin, o = lax.scan(step, s0, (chunks(q), chunks(k), chunks(v), chunks(g)))
    o = jnp.moveaxis(o, 0, 2).reshape(B, H, T, v.shape[-1])
    return o, s_fin


def gla_mixer(xn, s0, w_in, w_gate_up, b_gate, norm_g, w_out):
    B, T, _ = xn.shape
    dk_all = GLA_HEADS * GLA_DK
    dv_all = GLA_HEADS * GLA_DV
    proj = mm(xn, w_in)
    q, k, v, r, gz = jnp.split(proj, [dk_all, 2 * dk_all, 2 * dk_all + dv_all, 2 * dk_all + 2 * dv_all], axis=-1)
    glog = jax.nn.log_sigmoid((gz @ w_gate_up + b_gate).astype(jnp.float32)) / GLA_TAU

    def heads(t):
        return jnp.transpose(t.reshape(B, T, GLA_HEADS, -1), (0, 2, 1, 3)).astype(jnp.float32)

    o, s_new = gla_chunked(heads(q) * GLA_DK ** -0.5, heads(k), heads(v), heads(glog), s0.astype(jnp.float32))
    o = o * lax.rsqrt(jnp.mean(o * o, axis=-1, keepdims=True) + EPS) * norm_g.astype(jnp.float32)
    o = jnp.transpose(o, (0, 2, 1, 3)).reshape(B, T, dv_all).astype(xn.dtype)
    o = o * jax.nn.silu(r)
    return mm(o, w_out), s_new.astype(s0.dtype)


def _wsum_kernel(a_ref, b_ref, g_ref, o_ref, *, tb):
    sub = lax.broadcasted_iota(jnp.int32, (PEER_NKEYS, PEER_NKEYS), 0)

    def body(t, c):
        a = a_ref[pl.ds(t, 1), :]
        b = b_ref[pl.ds(t, 1), :]
        g = g_ref[pl.ds(t, 1), :]
        at = jnp.where(sub == a, 1.0, 0.0).astype(jnp.bfloat16)
        bt = jnp.where(sub == b, g, 0.0).astype(jnp.bfloat16)
        o_ref[t] = lax.dot_general(at, bt, (((1,), (1,)), ((), ())), preferred_element_type=jnp.float32)
        return c

    lax.fori_loop(0, tb, body, 0)


def peer_wsum(i1, i2, gate, tb=64):
    n, K = i1.shape
    tb = math.gcd(n, tb)
    return pl.pallas_call(
        functools.partial(_wsum_kernel, tb=tb),
        grid=(n // tb,),
        in_specs=[pl.BlockSpec((tb, K), lambda i: (i, 0))] * 3,
        out_specs=pl.BlockSpec((tb, PEER_NKEYS, PEER_NKEYS), lambda i: (i, 0, 0)),
        out_shape=jax.ShapeDtypeStruct((n, PEER_NKEYS, PEER_NKEYS), jnp.float32),
        compiler_params=pltpu.CompilerParams(dimension_semantics=("parallel",), vmem_limit_bytes=VMEM_LIMIT),
        name="peer_wsum",
    )(i1, i2, gate)


def _peer_kernel(x_ref, u_ref, v_ref, w_ref, h_ref, o_ref, acc_ref, *, te):
    j = pl.program_id(1)

    @pl.when(j == 0)
    def _():
        acc_ref[...] = jnp.zeros_like(acc_ref)

    hid = lax.dot_general(x_ref[...], u_ref[...], (((1,), (1,)), ((), ())), preferred_element_type=jnp.float32)
    parts = []
    for r in range(te // PEER_NKEYS):
        hr = hid[:, r * PEER_NKEYS:(r + 1) * PEER_NKEYS]
        ar = 0.5 * hr * (1.0 + lax.erf(hr * (2.0 ** -0.5))) * w_ref[:, r, :]
        parts.append(ar.astype(jnp.bfloat16))
    a = jnp.concatenate(parts, axis=1)
    acc_ref[...] += jnp.dot(a, v_ref[...], preferred_element_type=jnp.float32)

    @pl.when(j == pl.num_programs(1) - 1)
    def _():
        o_ref[...] = h_ref[...] + acc_ref[...]


def peer_dense(xn, u, v, wsum, h, tb=512, te=1024):
    n, D = xn.shape
    E = u.shape[0]
    tb = math.gcd(n, tb)
    return pl.pallas_call(
        functools.partial(_peer_kernel, te=te),
        grid=(n // tb, E // te),
        in_specs=[pl.BlockSpec((tb, D), lambda i, j: (i, 0)),
                  pl.BlockSpec((te, D), lambda i, j: (j, 0)),
                  pl.BlockSpec((te, D), lambda i, j: (j, 0)),
                  pl.BlockSpec((tb, te // PEER_NKEYS, PEER_NKEYS), lambda i, j: (i, j, 0)),
                  pl.BlockSpec((tb, D), lambda i, j: (i, 0))],
        out_specs=pl.BlockSpec((tb, D), lambda i, j: (i, 0)),
        out_shape=jax.ShapeDtypeStruct((n, D), jnp.float32),
        scratch_shapes=[pltpu.VMEM((tb, D), jnp.float32)],
        compiler_params=pltpu.CompilerParams(dimension_semantics=("parallel", "arbitrary"),
                                             vmem_limit_bytes=VMEM_LIMIT),
        name="peer_dense",
    )(xn, u, v, wsum, h)


def peer_ffn(h, xn, w_q, subkeys, u_bf, v_bf):
    B, T, D = xn.shape
    n = B * T
    xt = xn.reshape(n, D)
    q = pmm(xt, w_q).reshape(n, PEER_HEADS, 2, PEER_KEY_DIM // 2)
    s = jnp.einsum('nhcd,hckd->nhck', q, subkeys).astype(jnp.float32)
    s1, i1 = lax.top_k(s[:, :, 0], PEER_TOPK)
    s2, i2 = lax.top_k(s[:, :, 1], PEER_TOPK)
    n_cand = PEER_TOPK * PEER_TOPK
    cand = (s1[..., :, None] + s2[..., None, :]).reshape(n, PEER_HEADS, n_cand)
    top, pos = lax.top_k(cand, PEER_TOPK)
    K = PEER_HEADS * PEER_TOPK
    e1 = jnp.take_along_axis(i1, pos // PEER_TOPK, axis=-1).reshape(n, K)
    e2 = jnp.take_along_axis(i2, pos % PEER_TOPK, axis=-1).reshape(n, K)
    gate = jax.nn.softmax(top, axis=-1).reshape(n, K)
    wsum = peer_wsum(e1.astype(jnp.int32), e2.astype(jnp.int32), gate)
    out = peer_dense(xt.astype(jnp.bfloat16), u_bf, v_bf, wsum, h.reshape(n, D))
    return out.reshape(B, T, D)


def compress_rows(rows, w1, b1, w2, b2, pe):
    B, L, G, dh = rows.shape
    nch = L // CMP_STRIDE
    c = rows[:, :nch * CMP_STRIDE].reshape(B, nch, CMP_STRIDE, G, dh)
    w1h = w1.reshape(2, CMP_STRIDE, dh, CMP_HIDDEN)
    peh = pe.reshape(2, CMP_STRIDE, dh)
    h_first = jnp.einsum('bcsgd,sdf->bcgf', c + peh[0][:, None, :], w1h[0])
    h_second = jnp.einsum('bcsgd,sdf->bcgf', c + peh[1][:, None, :], w1h[1])
    hid = jax.nn.gelu(h_first[:, :-1] + h_second[:, 1:] + b1, approximate=False)
    return hid @ w2 + b2


def shared_kv(h, past_cmp, past_slc, win_buf, P):
    B, T, _ = h.shape
    rows = mm(rmsnorm(h, P['norm_kv']), P['kv_w']).reshape(B, T, 3, 2, NSA_GROUPS, NSA_HEAD_DIM)
    cmp_rows, slc_rows, win_rows = rows[:, :, 0], rows[:, :, 1], rows[:, :, 2]
    full_cmp = jnp.concatenate([past_cmp, cmp_rows], axis=1)
    full_slc = jnp.concatenate([past_slc, slc_rows], axis=1)
    win_all = jnp.concatenate([win_buf, win_rows], axis=1)
    kc = compress_rows(full_cmp[:, :, 0], P['cmp_k_w1'], P['cmp_k_b1'], P['cmp_k_w2'], P['cmp_k_b2'], P['cmp_k_pe'])
    vc = compress_rows(full_cmp[:, :, 1], P['cmp_v_w1'], P['cmp_v_b1'], P['cmp_v_w2'], P['cmp_v_b2'], P['cmp_v_pe'])
    n_keep = min(WINDOW, win_all.shape[1])
    win_new = win_all[:, win_all.shape[1] - n_keep:]
    kv = (kc, vc, full_slc[:, :, 0], full_slc[:, :, 1], win_all[:, :, 0], win_all[:, :, 1])
    return kv, cmp_rows, slc_rows, win_new


def _softmax_rows(s, valid):
    s = jnp.where(valid, s, NEG)
    m = jnp.max(s, axis=-1, keepdims=True)
    e = jnp.where(valid, jnp.exp(s - m), 0.0)
    return e / jnp.maximum(jnp.sum(e, axis=-1, keepdims=True), 1e-30)


def _nsa_cmp_kernel(q_ref, kc_ref, vc_ref, bias_ref, ov_ref, oc_ref, imp_ref, *, nc):
    i = pl.program_id(2)
    ncp = kc_ref.shape[2]
    qpos = i * TQ + lax.broadcasted_iota(jnp.int32, (TQ, ncp), 0)
    col = lax.broadcasted_iota(jnp.int32, (TQ, ncp), 1)
    valid = (col * CMP_STRIDE + (CMP_BLOCK - 1) <= qpos) & (col < nc)
    kc = kc_ref[0, 0]
    vc = vc_ref[0, 0]
    psum = jnp.zeros((TQ, ncp), jnp.float32)
    for hh in range(NSA_HPG):
        s = lax.dot_general(q_ref[0, 0, hh], kc, (((1,), (1,)), ((), ())), preferred_element_type=jnp.float32)
        p = _softmax_rows(s + bias_ref[0, hh], valid)
        psum = psum + p
        oc_ref[0, :, hh * NSA_HEAD_DIM:(hh + 1) * NSA_HEAD_DIM] = jnp.dot(
            p.astype(jnp.bfloat16), vc, preferred_element_type=jnp.float32)
    imp_ref[0, 0] = jnp.dot(psum.astype(jnp.bfloat16), ov_ref[...], preferred_element_type=jnp.float32)


def nsa_cmp(qh, kc, vc, bias_c, overlap, nc):
    B, G, HPG, T, dh = qh.shape
    ncp = kc.shape[2]
    nsp = overlap.shape[1]
    return pl.pallas_call(
        functools.partial(_nsa_cmp_kernel, nc=nc),
        grid=(B, G, T // TQ),
        in_specs=[pl.BlockSpec((1, 1, HPG, TQ, dh), lambda b, g, i: (b, g, 0, i, 0)),
                  pl.BlockSpec((1, 1, ncp, dh), lambda b, g, i: (b, g, 0, 0)),
                  pl.BlockSpec((1, 1, ncp, dh), lambda b, g, i: (b, g, 0, 0)),
                  pl.BlockSpec((1, HPG, TQ, ncp), lambda b, g, i: (g, 0, i, 0)),
                  pl.BlockSpec((ncp, nsp), lambda b, g, i: (0, 0))],
        out_specs=[pl.BlockSpec((1, TQ, HPG * dh), lambda b, g, i: (b, i, g)),
                   pl.BlockSpec((1, 1, TQ, nsp), lambda b, g, i: (b, g, i, 0))],
        out_shape=[jax.ShapeDtypeStruct((B, T, G * HPG * dh), jnp.float32),
                   jax.ShapeDtypeStruct((B, G, T, nsp), jnp.float32)],
        compiler_params=pltpu.CompilerParams(dimension_semantics=("parallel", "parallel", "parallel"),
                                             vmem_limit_bytes=VMEM_LIMIT),
        name="nsa_cmp",
    )(qh, kc, vc, bias_c, overlap)


def _flash_tile(q_all, k, v, bias_ref, dd_b, valid, m_ref, l_ref, acc_ref):
    s_all = lax.dot_general(q_all, k, (((1,), (1,)), ((), ())), preferred_element_type=jnp.float32)
    ps = []
    for hh in range(NSA_HPG):
        rows = slice(hh * TQ, (hh + 1) * TQ)
        s = jnp.where(valid, s_all[rows] + bias_ref[0, dd_b, hh], NEG)
        m_old = m_ref[rows]
        m_new = jnp.maximum(m_old, jnp.max(s, axis=-1, keepdims=True))
        p = jnp.where(valid, jnp.exp(s - m_new), 0.0)
        alpha = jnp.exp(m_old - m_new)
        l_ref[rows] = alpha * l_ref[rows] + jnp.sum(p, axis=-1, keepdims=True)
        acc_ref[rows] = alpha * acc_ref[rows]
        m_ref[rows] = m_new
        ps.append(p.astype(jnp.bfloat16))
    acc_ref[...] += jnp.dot(jnp.concatenate(ps, axis=0), v, preferred_element_type=jnp.float32)


def _nsa_sw_kernel(q_ref, ks_ref, vs_ref, kw_ref, vw_ref, msel_ref, exp_ref, bias_ref, selm_ref, winm_ref,
                   os_ref, ow_ref, msk_ref, m_ref, l_ref, acc_ref, *, nt):
    i = pl.program_id(2)
    q_all = q_ref[0, 0].reshape(NSA_HPG * TQ, NSA_HEAD_DIM)
    msel = msel_ref[0, 0].astype(jnp.bfloat16)
    for j in range(nt):
        msk_ref[j] = jnp.dot(msel, exp_ref[:, j * TQ:(j + 1) * TQ], preferred_element_type=jnp.float32)

    def reset():
        m_ref[...] = jnp.full_like(m_ref, NEG)
        l_ref[...] = jnp.zeros_like(l_ref)
        acc_ref[...] = jnp.zeros_like(acc_ref)

    def finish(o_ref):
        o = acc_ref[...] / jnp.maximum(l_ref[...], 1e-30)
        for hh in range(NSA_HPG):
            o_ref[0, :, hh * NSA_HEAD_DIM:(hh + 1) * NSA_HEAD_DIM] = o[hh * TQ:(hh + 1) * TQ]

    reset()

    def sel_body(j, c):
        dd = jnp.minimum(i - j, 2)
        valid = (msk_ref[j] * selm_ref[dd]) > 0.5
        _flash_tile(q_all, ks_ref[0, 0, j], vs_ref[0, 0, j], bias_ref, dd, valid, m_ref, l_ref, acc_ref)
        return c

    lax.fori_loop(0, i + 1, sel_body, 0)
    finish(os_ref)

    reset()

    def win_body(j, c):
        dd = i - j
        valid = winm_ref[dd] > 0.5
        _flash_tile(q_all, kw_ref[0, 0, j], vw_ref[0, 0, j], bias_ref, jnp.minimum(dd, 2), valid, m_ref, l_ref, acc_ref)
        return c

    lax.fori_loop(jnp.maximum(i - WINDOW // TQ, 0), i + 1, win_body, 0)
    finish(ow_ref)


def nsa_sel_win(qh, ks, vs, kw, vw, msel, expander, bias_t, selm, winm):
    B, G, HPG, T, dh = qh.shape
    nt = T // TQ
    nsp = msel.shape[-1]
    kv_spec = pl.BlockSpec((1, 1, nt, TQ, dh), lambda b, g, i: (b, g, 0, 0, 0))
    out_spec = pl.BlockSpec((1, TQ, HPG * dh), lambda b, g, i: (b, i, g))
    return pl.pallas_call(
        functools.partial(_nsa_sw_kernel, nt=nt),
        grid=(B, G, T // TQ),
        in_specs=[pl.BlockSpec((1, 1, HPG, TQ, dh), lambda b, g, i: (b, g, 0, i, 0)),
                  kv_spec, kv_spec, kv_spec, kv_spec,
                  pl.BlockSpec((1, 1, TQ, nsp), lambda b, g, i: (b, g, i, 0)),
                  pl.BlockSpec((nsp, T), lambda b, g, i: (0, 0)),
                  pl.BlockSpec((1, 3, HPG, TQ, TQ), lambda b, g, i: (g, 0, 0, 0, 0)),
                  pl.BlockSpec((3, TQ, TQ), lambda b, g, i: (0, 0, 0)),
                  pl.BlockSpec((WINDOW // TQ + 1, TQ, TQ), lambda b, g, i: (0, 0, 0))],
        out_specs=[out_spec, out_spec],
        out_shape=[jax.ShapeDtypeStruct((B, T, G * HPG * dh), jnp.float32)] * 2,
        scratch_shapes=[pltpu.VMEM((nt, TQ, TQ), jnp.float32),
                        pltpu.VMEM((HPG * TQ, 1), jnp.float32),
                        pltpu.VMEM((HPG * TQ, 1), jnp.float32),
                        pltpu.VMEM((HPG * TQ, dh), jnp.float32)],
        compiler_params=pltpu.CompilerParams(dimension_semantics=("parallel", "parallel", "arbitrary"),
                                             vmem_limit_bytes=VMEM_LIMIT),
        name="nsa_sel_win",
    )(qh, ks, vs, kw, vw, msel, expander, bias_t, selm, winm)


def nsa_prompt(q, gates, kc, vc, ks, vs, kw_rows, vw_rows, table):
    B, T = q.shape[0], q.shape[1]
    G, HPG, dh = NSA_GROUPS, NSA_HPG, NSA_HEAD_DIM
    Nc = kc.shape[1]
    Ns = -(-T // SEL_BLOCK)
    ncp = -(-Nc // LANE) * LANE
    nsp = -(-Ns // LANE) * LANE
    bf = jnp.bfloat16
    qh = jnp.transpose(q.reshape(B, T, G, HPG, dh), (0, 2, 3, 1, 4)).astype(bf)

    def grp(t, n_pad):
        t = jnp.transpose(t, (0, 2, 1, 3)).astype(bf)
        return jnp.pad(t, ((0, 0), (0, 0), (0, n_pad - t.shape[2]), (0, 0)))

    qpos = jnp.arange(T)
    cend = jnp.arange(ncp) * CMP_STRIDE + (CMP_BLOCK - 1)
    bias_c = head_bias(table, qpos[:, None] - cend[None, :])
    cstart = jnp.arange(ncp)[:, None] * CMP_STRIDE
    sstart = jnp.arange(nsp)[None, :] * SEL_BLOCK
    overlap = ((cstart < sstart + SEL_BLOCK) & (cstart + CMP_BLOCK > sstart)
               & (jnp.arange(ncp)[:, None] < Nc) & (jnp.arange(nsp)[None, :] < Ns)).astype(bf)
    o_c, imp = nsa_cmp(qh, grp(kc, ncp), grp(vc, ncp), bias_c, overlap, Nc)
    imp = imp[..., :Ns]
    blk = jnp.arange(Ns)[None, :]
    cur = qpos[:, None] // SEL_BLOCK
    valid = blk * SEL_BLOCK <= qpos[:, None]
    forced = (blk == 0) | (blk == cur) | (blk == cur - 1)
    score = jnp.where(valid, jnp.where(forced, BIG, imp), -BIG)
    n_pick = min(N_SEL, Ns)
    top_s, idx = lax.top_k(score, n_pick)
    ok = top_s > -0.5 * BIG
    msel = jnp.sum(jax.nn.one_hot(idx, nsp, dtype=jnp.float32) * ok[..., None], axis=-2)
    expander = (jnp.arange(nsp)[:, None] == (jnp.arange(T)[None, :] // SEL_BLOCK)).astype(bf)
    r = jnp.arange(TQ)
    rel3 = (jnp.arange(3) * TQ)[:, None, None] + r[None, :, None] - r[None, None, :]
    bias_t = jnp.transpose(table[t5_bucket(rel3)].astype(jnp.float32), (3, 0, 1, 2)).reshape(G, HPG, 3, TQ, TQ)
    bias_t = jnp.transpose(bias_t, (0, 2, 1, 3, 4))
    selm = (rel3 >= 0).astype(jnp.float32)
    nw = WINDOW // TQ + 1
    relw = (jnp.arange(nw) * TQ)[:, None, None] + r[None, :, None] - r[None, None, :]
    winm = ((relw >= 0) & (relw < WINDOW)).astype(jnp.float32)

    def tiles(t):
        return jnp.transpose(t, (0, 2, 1, 3)).astype(bf).reshape(B, G, T // TQ, TQ, dh)

    o_s, o_w = nsa_sel_win(qh, tiles(ks), tiles(vs), tiles(kw_rows), tiles(vw_rows), msel, expander, bias_t, selm, winm)
    gx = jnp.repeat(gates.reshape(B, T, G * HPG, 3), dh, axis=2)
    return gx[..., 0] * o_c + gx[..., 1] * o_s + gx[..., 2] * o_w


def nsa_attend(q, gates, kc, vc, ks, vs, kw, vw, q_start, table):
    B, T = q.shape[0], q.shape[1]
    G, dh = NSA_GROUPS, NSA_HEAD_DIM
    L = ks.shape[1]
    Nc = kc.shape[1]
    Ns = -(-L // SEL_BLOCK)
    pad = ((0, 0), (0, Ns * SEL_BLOCK - L), (0, 0), (0, 0))
    ks_b = jnp.pad(ks, pad).reshape(B, Ns, SEL_BLOCK, G, dh)
    vs_b = jnp.pad(vs, pad).reshape(B, Ns, SEL_BLOCK, G, dh)
    qh = jnp.transpose(q.reshape(B, T, G, NSA_HPG, dh), (0, 2, 3, 1, 4))
    gh = jnp.transpose(gates.reshape(B, T, G, NSA_HPG, 3), (0, 2, 3, 1, 4))
    kw_t = jnp.transpose(kw, (0, 2, 1, 3))
    vw_t = jnp.transpose(vw, (0, 2, 1, 3))
    cend = jnp.arange(Nc) * CMP_STRIDE + (CMP_BLOCK - 1)
    cstart = jnp.arange(Nc)[:, None] * CMP_STRIDE
    sstart = jnp.arange(Ns)[None, :] * SEL_BLOCK
    overlap = ((cstart < sstart + SEL_BLOCK) & (cstart + CMP_BLOCK > sstart)).astype(jnp.float32)
    n_pick = min(N_SEL, Ns)
    tab_g = table.astype(jnp.float32).reshape(REL_BUCKETS, G, NSA_HPG)
    bi = jnp.arange(B)[:, None, None, None]
    gi = jnp.arange(G)[None, :, None, None]
    blk = jnp.arange(Ns)[None, :]
    qb = Q_BLOCK if T % Q_BLOCK == 0 else T

    def body(o0):
        qblk = lax.dynamic_slice_in_dim(qh, o0, qb, axis=3)
        gblk = lax.dynamic_slice_in_dim(gh, o0, qb, axis=3)
        qpos = q_start + o0 + jnp.arange(qb)
        rel_c = qpos[:, None] - cend[None, :]
        s_c = jnp.einsum('bghqd,bngd->bghqn', qblk, kc).astype(jnp.float32) + head_bias(table, rel_c)
        p_c = masked_softmax(s_c, rel_c >= 0)
        o_c = jnp.einsum('bghqn,bngd->bghqd', p_c.astype(vc.dtype), vc)
        imp = jnp.einsum('bgqn,ns->bgqs', p_c.sum(axis=2), overlap)
        cur = qpos[:, None] // SEL_BLOCK
        valid = blk * SEL_BLOCK <= qpos[:, None]
        forced = (blk == 0) | (blk == cur) | (blk == cur - 1)
        score = jnp.where(valid, jnp.where(forced, BIG, imp), -BIG)
        top_s, idx = lax.top_k(score, n_pick)
        ok = top_s > -0.5 * BIG
        kg = ks_b[bi, idx, :, gi, :].reshape(B, G, qb, n_pick * SEL_BLOCK, dh)
        vg = vs_b[bi, idx, :, gi, :].reshape(B, G, qb, n_pick * SEL_BLOCK, dh)
        kpos = (idx[..., None] * SEL_BLOCK + jnp.arange(SEL_BLOCK)).reshape(B, G, qb, n_pick * SEL_BLOCK)
        rel_s = qpos[None, None, :, None] - kpos
        mask_s = jnp.repeat(ok, SEL_BLOCK, axis=-1) & (rel_s >= 0)
        bias_s = jnp.einsum('bgqkn,ngh->bghqk', jax.nn.one_hot(t5_bucket(rel_s), REL_BUCKETS, dtype=jnp.float32), tab_g)
        s_s = jnp.einsum('bghqd,bgqkd->bghqk', qblk, kg).astype(jnp.float32) + bias_s
        p_s = masked_softmax(s_s, mask_s[:, :, None])
        o_s = jnp.einsum('bghqk,bgqkd->bghqd', p_s.astype(vg.dtype), vg)
        kwb = lax.dynamic_slice_in_dim(kw_t, o0, qb + WINDOW, axis=2)
        vwb = lax.dynamic_slice_in_dim(vw_t, o0, qb + WINDOW, axis=2)
        wpos = q_start - WINDOW + o0 + jnp.arange(qb + WINDOW)
        rel_w = qpos[:, None] - wpos[None, :]
        mask_w = (rel_w >= 0) & (rel_w < WINDOW) & (wpos[None, :] >= 0)
        s_w = jnp.einsum('bghqd,bgkd->bghqk', qblk, kwb).astype(jnp.float32) + head_bias(table, rel_w)
        p_w = masked_softmax(s_w, mask_w)
        o_w = jnp.einsum('bghqk,bgkd->bghqd', p_w.astype(vwb.dtype), vwb)
        return gblk[..., 0:1] * o_c + gblk[..., 1:2] * o_s + gblk[..., 2:3] * o_w

    out = lax.map(body, jnp.arange(T // qb) * qb)
    return jnp.transpose(out, (1, 0, 4, 2, 3, 5)).reshape(B, T, NSA_HEADS * dh)


def run_trunk(x, gla_s0, past_cmp, past_slc, win_buf, P):
    q_start = past_slc.shape[1]
    B, T, _ = x.shape
    h = x
    gla_states = []
    kv = None
    for layer in range(DEPTH):
        xn = rmsnorm(h, P['norm_mix'][layer])
        if layer < N_A_LAYERS:
            o, s = gla_mixer(xn, gla_s0[layer], P['gla_w_in'][layer], P['gla_w_gate_up'][layer],
                             P['gla_b_gate'][layer], P['gla_norm'][layer], P['gla_w_out'][layer])
            gla_states.append(s)
        else:
            j = layer - N_A_LAYERS
            qd = NSA_HEADS * NSA_HEAD_DIM
            proj = mm(xn, P['nsa_w_in'][j])
            q = proj[..., :qd].reshape(B, T, NSA_HEADS, NSA_HEAD_DIM) * NSA_HEAD_DIM ** -0.5
            gates = jax.nn.sigmoid(proj[..., qd:].astype(jnp.float32)).reshape(B, T, NSA_HEADS, 3).astype(x.dtype)
            kc, vc, ks, vs, kw, vw = kv
            if q_start == 0 and win_buf.shape[1] == 0 and T % TQ == 0:
                att = nsa_prompt(q, gates, kc, vc, ks, vs, kw, vw, P['rel_bias'])
            else:
                wpad = ((0, 0), (WINDOW - win_buf.shape[1], 0), (0, 0), (0, 0))
                att = nsa_attend(q, gates, kc, vc, ks, vs, jnp.pad(kw, wpad), jnp.pad(vw, wpad), q_start, P['rel_bias'])
            o = mm(att, P['nsa_w_out'][j])
        h = h + o
        h = peer_ffn(h, rmsnorm(h, P['norm_ffn'][layer]), P['peer_w_q'][layer], P['peer_subkeys'][layer],
                     P['peer_u_bf'][layer], P['peer_v_bf'][layer])
        if layer == N_A_LAYERS - 1:
            kv, cmp_rows, slc_rows, win_new = shared_kv(h, past_cmp, past_slc, win_buf, P)
    return rmsnorm(h, P['norm_final']), jnp.stack(gla_states), cmp_rows, slc_rows, win_new


def kernel(x_prompt, x_sample, state_gla, cache_cmp_kv, cache_slc_kv, cache_win_kv, page_table,
           norm_mix, norm_ffn, norm_kv, norm_final, gla_w_in, gla_w_gate_up, gla_b_gate, gla_norm, gla_w_out,
           kv_w, cmp_k_w1, cmp_k_b1, cmp_k_w2, cmp_k_b2, cmp_k_pe, cmp_v_w1, cmp_v_b1, cmp_v_w2, cmp_v_b2, cmp_v_pe,
           nsa_w_in, nsa_w_out, rel_bias, peer_w_q, peer_subkeys, peer_u, peer_v):
    P = dict(norm_mix=norm_mix, norm_ffn=norm_ffn, norm_kv=norm_kv, norm_final=norm_final,
             gla_w_in=gla_w_in, gla_w_gate_up=gla_w_gate_up, gla_b_gate=gla_b_gate, gla_norm=gla_norm,
             gla_w_out=gla_w_out, kv_w=kv_w,
             cmp_k_w1=cmp_k_w1, cmp_k_b1=cmp_k_b1, cmp_k_w2=cmp_k_w2, cmp_k_b2=cmp_k_b2, cmp_k_pe=cmp_k_pe,
             cmp_v_w1=cmp_v_w1, cmp_v_b1=cmp_v_b1, cmp_v_w2=cmp_v_w2, cmp_v_b2=cmp_v_b2, cmp_v_pe=cmp_v_pe,
             nsa_w_in=nsa_w_in, nsa_w_out=nsa_w_out, rel_bias=rel_bias,
             peer_w_q=peer_w_q, peer_subkeys=peer_subkeys,
             peer_u_bf=peer_u.astype(jnp.bfloat16), peer_v_bf=peer_v.astype(jnp.bfloat16))
    past_rows = page_table.shape[1] * PAGE_SIZE

    def gather_pages(cache):
        return cache[page_table].reshape(DEC_BATCH, past_rows, 2, NSA_GROUPS, NSA_HEAD_DIM)

    empty = jnp.zeros((BATCH, 0, 2, NSA_GROUPS, NSA_HEAD_DIM), x_prompt.dtype)
    gla_zero = jnp.zeros((N_A_LAYERS, BATCH, GLA_HEADS, GLA_DK, GLA_DV), x_prompt.dtype)
    y_prompt, gla_p, cmp_p, slc_p, win_p = run_trunk(x_prompt, gla_zero, empty, empty, empty, P)
    y_sample, gla_s, cmp_s, slc_s, win_s = run_trunk(x_sample, state_gla, gather_pages(cache_cmp_kv),
                                                     gather_pages(cache_slc_kv), cache_win_kv, P)
    return (y_prompt, y_sample, gla_p, gla_s, cmp_p, cmp_s, slc_p, slc_s, win_p, win_s)
```

```python
import functools
import math

import jax
import jax.numpy as jnp
from jax import lax
from jax.experimental import pallas as pl
from jax.experimental.pallas import tpu as pltpu

D_MODEL = 2048
BATCH = 4
SEQ = 2048
DEPTH = 2
DEC_BATCH = 32
DEC_SEQ = 8
PAST_LEN = 8192
PAGE_SIZE = 128
N_A_LAYERS = DEPTH // 2
N_B_LAYERS = DEPTH - N_A_LAYERS
GLA_HEADS = 4
GLA_DK = D_MODEL // (2 * GLA_HEADS)
GLA_DV = D_MODEL // GLA_HEADS
GLA_GATE_RANK = 16
GLA_TAU = 16.0
GLA_CHUNK = 64
NSA_HEADS = 16
NSA_GROUPS = 4
NSA_HPG = NSA_HEADS // NSA_GROUPS
NSA_HEAD_DIM = D_MODEL // NSA_HEADS
CMP_STRIDE = 16
CMP_BLOCK = 2 * CMP_STRIDE
CMP_HIDDEN = 2 * NSA_HEAD_DIM
SEL_BLOCK = 64
N_SEL = 16
WINDOW = 512
Q_BLOCK = 32
REL_BUCKETS = 32
REL_MAX_DIST = 128
PEER_HEADS = 8
PEER_NKEYS = 128
PEER_EXPERTS = PEER_NKEYS * PEER_NKEYS
PEER_TOPK = 16
PEER_KEY_DIM = 256
PEER_TOK_BLOCK = 128
EPS = 1e-6
NEG = -1e30
BIG = 1e30

LANE = 128
VMEM_LIMIT = 56 * 1024 * 1024
TQ = 128
assert REL_MAX_DIST <= TQ and WINDOW % TQ == 0 and TQ % SEL_BLOCK == 0


def _mm_kernel(a_ref, b_ref, o_ref):
    a = a_ref[...].astype(jnp.bfloat16)
    b = b_ref[...].astype(jnp.bfloat16)
    o_ref[...] = jnp.dot(a, b, preferred_element_type=jnp.float32)


def pmm(a, b, tm=512, tn=512):
    M, K = a.shape
    N = b.shape[1]
    tm = min(tm, M)
    Mp = -(-M // tm) * tm
    Np = -(-N // LANE) * LANE
    tn = math.gcd(Np, tn)
    if Mp != M:
        a = jnp.pad(a, ((0, Mp - M), (0, 0)))
    if Np != N:
        b = jnp.pad(b, ((0, 0), (0, Np - N)))
    out = pl.pallas_call(
        _mm_kernel,
        grid=(Mp // tm, Np // tn),
        in_specs=[pl.BlockSpec((tm, K), lambda i, j: (i, 0)),
                  pl.BlockSpec((K, tn), lambda i, j: (0, j))],
        out_specs=pl.BlockSpec((tm, tn), lambda i, j: (i, j)),
        out_shape=jax.ShapeDtypeStruct((Mp, Np), jnp.float32),
        compiler_params=pltpu.CompilerParams(
            dimension_semantics=("parallel", "parallel"), vmem_limit_bytes=VMEM_LIMIT),
        name="proj_matmul",
    )(a, b)
    return out[:M, :N]


def mm(x, w):
    lead = x.shape[:-1]
    return pmm(x.reshape(-1, x.shape[-1]), w).reshape(lead + (w.shape[1],))


def rmsnorm(x, g):
    xf = x.astype(jnp.float32)
    y = xf * lax.rsqrt(jnp.mean(xf * xf, axis=-1, keepdims=True) + EPS)
    return (y * g.astype(jnp.float32)).astype(x.dtype)


def masked_softmax(s, mask):
    s = jnp.where(mask, s, NEG)
    m = jnp.max(s, axis=-1, keepdims=True)
    e = jnp.where(mask, jnp.exp(s - m), 0.0)
    return e / jnp.maximum(jnp.sum(e, axis=-1, keepdims=True), 1e-30)


def t5_bucket(rel):
    n = jnp.maximum(rel, 0)
    exact = REL_BUCKETS // 2
    nf = jnp.maximum(n, exact).astype(jnp.float32)
    large = exact + (jnp.log(nf / exact) / math.log(REL_MAX_DIST / exact) * (REL_BUCKETS - exact)).astype(jnp.int32)
    large = jnp.minimum(large, REL_BUCKETS - 1)
    return jnp.where(n < exact, n, large)


def head_bias(table, rel):
    b = table[t5_bucket(rel)].astype(jnp.float32)
    return jnp.transpose(b, (2, 0, 1)).reshape(NSA_GROUPS, NSA_HPG, rel.shape[0], rel.shape[1])


def gla_chunked(q, k, v, g, s0):
    B, H, T, DK = q.shape
    C = GLA_CHUNK if T % GLA_CHUNK == 0 else T
    n = T // C

    def chunks(t):
        return jnp.moveaxis(t.reshape(B, H, n, C, t.shape[-1]), 2, 0)

    causal = jnp.tril(jnp.ones((C, C), dtype=bool))

    def step(s, inp):
        qc, kc, vc, gc = inp
        b = jnp.cumsum(gc, axis=2)
        b_last = b[:, :, -1:, :]
        qe = qc * jnp.exp(b)
        a = jnp.einsum('bhid,bhjd->bhij', qe, kc * jnp.exp(-b))
        a = jnp.where(causal, a, 0.0)
        o = jnp.einsum('bhij,bhje->bhie', a, vc) + jnp.einsum('bhid,bhde->bhie', qe, s)
        s = s * jnp.exp(b_last[:, :, 0, :, None]) + jnp.einsum('bhjd,bhje->bhde', kc * jnp.exp(b_last - b), vc)
        return s, o

    s_fin, o = lax.scan(step, s0, (chunks(q), chunks(k), chunks(v), chunks(g)))
    o = jnp.moveaxis(o, 0, 2).reshape(B, H, T, v.shape[-1])
    return o, s_fin


def gla_mixer(xn, s0, w_in, w_gate_up, b_gate, norm_g, w_out):
    B, T, _ = xn.shape
    dk_all = GLA_HEADS * GLA_DK
    dv_all = GLA_HEADS * GLA_DV
    proj = mm(xn, w_in)
    q, k, v, r, gz = jnp.split(proj, [dk_all, 2 * dk_all, 2 * dk_all + dv_all, 2 * dk_all + 2 * dv_all], axis=-1)
    glog = jax.nn.log_sigmoid((gz @ w_gate_up + b_gate).astype(jnp.float32)) / GLA_TAU

    def heads(t):
        return jnp.transpose(t.reshape(B, T, GLA_HEADS, -1), (0, 2, 1, 3)).astype(jnp.float32)

    o, s_new = gla_chunked(heads(q) * GLA_DK ** -0.5, heads(k), heads(v), heads(glog), s0.astype(jnp.float32))
    o = o * lax.rsqrt(jnp.mean(o * o, axis=-1, keepdims=True) + EPS) * norm_g.astype(jnp.float32)
    o = jnp.transpose(o, (0, 2, 1, 3)).reshape(B, T, dv_all).astype(xn.dtype)
    o = o * jax.nn.silu(r)
    return mm(o, w_out), s_new.astype(s0.dtype)


WSUM_UNROLL = 8


def _wsum_kernel(a_ref, b_ref, g_ref, o_ref, *, tb):
    sub = lax.broadcasted_iota(jnp.int32, (PEER_NKEYS, PEER_NKEYS), 0)

    def body(tt, c):
        t0 = pl.multiple_of(tt * WSUM_UNROLL, WSUM_UNROLL)
        a8 = a_ref[pl.ds(t0, WSUM_UNROLL), :]
        b8 = b_ref[pl.ds(t0, WSUM_UNROLL), :]
        g8 = g_ref[pl.ds(t0, WSUM_UNROLL), :]
        for u in range(WSUM_UNROLL):
            at = jnp.where(sub == a8[u:u + 1], 1.0, 0.0).astype(jnp.bfloat16)
            bt = jnp.where(sub == b8[u:u + 1], g8[u:u + 1], 0.0).astype(jnp.bfloat16)
            o_ref[t0 + u] = lax.dot_general(at, bt, (((1,), (1,)), ((), ())), preferred_element_type=jnp.float32)
        return c

    lax.fori_loop(0, tb // WSUM_UNROLL, body, 0)


def peer_wsum(i1, i2, gate, tb=64):
    n, K = i1.shape
    tb = math.gcd(n, tb)
    assert tb % WSUM_UNROLL == 0
    return pl.pallas_call(
        functools.partial(_wsum_kernel, tb=tb),
        grid=(n // tb,),
        in_specs=[pl.BlockSpec((tb, K), lambda i: (i, 0))] * 3,
        out_specs=pl.BlockSpec((tb, PEER_NKEYS, PEER_NKEYS), lambda i: (i, 0, 0)),
        out_shape=jax.ShapeDtypeStruct((n, PEER_NKEYS, PEER_NKEYS), jnp.float32),
        compiler_params=pltpu.CompilerParams(dimension_semantics=("parallel",), vmem_limit_bytes=VMEM_LIMIT),
        name="peer_wsum",
    )(i1, i2, gate)


def _peer_kernel(x_ref, u_ref, v_ref, w_ref, h_ref, o_ref, acc_ref, *, te):
    j = pl.program_id(1)

    @pl.when(j == 0)
    def _():
        acc_ref[...] = jnp.zeros_like(acc_ref)

    hid = lax.dot_general(x_ref[...], u_ref[...], (((1,), (1,)), ((), ())), preferred_element_type=jnp.float32)
    parts = []
    for r in range(te // PEER_NKEYS):
        hr = hid[:, r * PEER_NKEYS:(r + 1) * PEER_NKEYS]
        ar = 0.5 * hr * (1.0 + lax.erf(hr * (2.0 ** -0.5))) * w_ref[:, r, :]
        parts.append(ar.astype(jnp.bfloat16))
    a = jnp.concatenate(parts, axis=1)
    acc_ref[...] += jnp.dot(a, v_ref[...], preferred_element_type=jnp.float32)

    @pl.when(j == pl.num_programs(1) - 1)
    def _():
        o_ref[...] = h_ref[...] + acc_ref[...]


def peer_dense(xn, u, v, wsum, h, tb=512, te=1024):
    n, D = xn.shape
    E = u.shape[0]
    tb = math.gcd(n, tb)
    return pl.pallas_call(
        functools.partial(_peer_kernel, te=te),
        grid=(n // tb, E // te),
        in_specs=[pl.BlockSpec((tb, D), lambda i, j: (i, 0)),
                  pl.BlockSpec((te, D), lambda i, j: (j, 0)),
                  pl.BlockSpec((te, D), lambda i, j: (j, 0)),
                  pl.BlockSpec((tb, te // PEER_NKEYS, PEER_NKEYS), lambda i, j: (i, j, 0)),
                  pl.BlockSpec((tb, D), lambda i, j: (i, 0))],
        out_specs=pl.BlockSpec((tb, D), lambda i, j: (i, 0)),
        out_shape=jax.ShapeDtypeStruct((n, D), jnp.float32),
        scratch_shapes=[pltpu.VMEM((tb, D), jnp.float32)],
        compiler_params=pltpu.CompilerParams(dimension_semantics=("parallel", "arbitrary"),
                                             vmem_limit_bytes=VMEM_LIMIT),
        name="peer_dense",
    )(xn, u, v, wsum, h)


RT = LANE
NHC = 2 * PEER_HEADS
_CAND_GROUPS = [(0, 0), (0, 8), (1, 0)] + [(a, 0) for a in range(2, 8)] + [(-1, 0)]
assert PEER_TOPK == 16 and PEER_KEY_DIM // 2 == LANE and PEER_NKEYS == LANE


def _top_rounds(s, n_rounds, extra=()):
    R = s.shape[0]
    iota = lax.broadcasted_iota(jnp.int32, s.shape, 0)
    vals, idxs, ex = [], [], [[] for _ in extra]
    for _ in range(n_rounds):
        m = jnp.max(s, axis=0, keepdims=True)
        idx = jnp.min(jnp.where(s == m, iota, R), axis=0, keepdims=True)
        sel = iota == idx
        for e, lst in zip(extra, ex):
            lst.append(jnp.max(jnp.where(sel, e, -1), axis=0, keepdims=True))
        s = jnp.where(sel, -jnp.inf, s)
        vals.append(m)
        idxs.append(idx)
    cat = lambda l: jnp.concatenate(l, axis=0)
    return cat(vals), cat(idxs), [cat(l) for l in ex]


def _route_kernel(q_ref, sub_ref, e1_ref, e2_ref, g_ref, v_scr, i_scr):
    K = PEER_TOPK

    def stage1(hc, c):
        off = pl.multiple_of(hc * LANE, LANE)
        qb = q_ref[:, pl.ds(off, LANE)].astype(jnp.bfloat16)
        s = lax.dot_general(sub_ref[hc], qb, (((1,), (1,)), ((), ())), preferred_element_type=jnp.float32)
        v, i, _ = _top_rounds(s, K)
        v_scr[hc] = v
        i_scr[hc] = i
        return c

    lax.fori_loop(0, NHC, stage1, 0)

    row8 = lax.broadcasted_iota(jnp.int32, (8, RT), 0)

    def stage2(h, c):
        v1, v2 = v_scr[2 * h], v_scr[2 * h + 1]
        i1, i2 = i_scr[2 * h], i_scr[2 * h + 1]
        cand, c1, c2 = [], [], []
        for a, b0 in _CAND_GROUPS:
            if a >= 0:
                nb = K // (a + 1)
                sm = v1[a:a + 1] + v2[b0:b0 + 8]
                if nb - b0 < 8:
                    sm = jnp.where(row8 < nb - b0, sm, -jnp.inf)
                cand.append(sm)
                c1.append(jnp.broadcast_to(i1[a:a + 1], (8, RT)))
                c2.append(i2[b0:b0 + 8])
            else:
                cand.append(v1[8:16] + v2[0:1])
                c1.append(i1[8:16])
                c2.append(jnp.broadcast_to(i2[0:1], (8, RT)))
        cat = lambda l: jnp.concatenate(l, axis=0)
        top, _, (e1, e2) = _top_rounds(cat(cand), K, extra=(cat(c1), cat(c2)))
        ex = jnp.exp(top - top[0:1])
        g = ex / jnp.sum(ex, axis=0, keepdims=True)
        r0 = pl.multiple_of(h * K, K)
        e1_ref[0, pl.ds(r0, K), :] = e1
        e2_ref[0, pl.ds(r0, K), :] = e2
        g_ref[0, pl.ds(r0, K), :] = g
        return c

    lax.fori_loop(0, PEER_HEADS, stage2, 0)


def peer_route(q, sub_bf):
    n = q.shape[0]
    assert n % RT == 0
    nb = n // RT
    slots = PEER_HEADS * PEER_TOPK
    out = jax.ShapeDtypeStruct((nb, slots, RT), jnp.int32)
    ospec = pl.BlockSpec((1, slots, RT), lambda i: (i, 0, 0))
    return pl.pallas_call(
        _route_kernel,
        grid=(nb,),
        in_specs=[pl.BlockSpec((RT, q.shape[1]), lambda i: (i, 0)),
                  pl.BlockSpec(sub_bf.shape, lambda i: (0, 0, 0))],
        out_specs=[ospec, ospec, ospec],
        out_shape=[out, out, jax.ShapeDtypeStruct((nb, slots, RT), jnp.float32)],
        scratch_shapes=[pltpu.VMEM((NHC, PEER_TOPK, RT), jnp.float32),
                        pltpu.VMEM((NHC, PEER_TOPK, RT), jnp.int32)],
        compiler_params=pltpu.CompilerParams(dimension_semantics=("parallel",), vmem_limit_bytes=VMEM_LIMIT),
        name="peer_route",
    )(q, sub_bf)


def peer_ffn(h, xn, w_q, subkeys, u_bf, v_bf):
    B, T, D = xn.shape
    n = B * T
    xt = xn.reshape(n, D)
    q = pmm(xt, w_q)
    sub_bf = subkeys.astype(jnp.bfloat16).reshape(NHC, PEER_NKEYS, PEER_KEY_DIM // 2)
    e1, e2, gate = peer_route(q, sub_bf)
    tok_major = lambda t: jnp.transpose(t, (0, 2, 1)).reshape(n, PEER_HEADS * PEER_TOPK)
    wsum = peer_wsum(tok_major(e1), tok_major(e2), tok_major(gate))
    out = peer_dense(xt.astype(jnp.bfloat16), u_bf, v_bf, wsum, h.reshape(n, D))
    return out.reshape(B, T, D)


def compress_rows(rows, w1, b1, w2, b2, pe):
    B, L, G, dh = rows.shape
    nch = L // CMP_STRIDE
    c = rows[:, :nch * CMP_STRIDE].reshape(B, nch, CMP_STRIDE, G, dh)
    w1h = w1.reshape(2, CMP_STRIDE, dh, CMP_HIDDEN)
    peh = pe.reshape(2, CMP_STRIDE, dh)
    h_first = jnp.einsum('bcsgd,sdf->bcgf', c + peh[0][:, None, :], w1h[0])
    h_second = jnp.einsum('bcsgd,sdf->bcgf', c + peh[1][:, None, :], w1h[1])
    hid = jax.nn.gelu(h_first[:, :-1] + h_second[:, 1:] + b1, approximate=False)
    return hid @ w2 + b2


def shared_kv(h, past_cmp, past_slc, win_buf, P):
    B, T, _ = h.shape
    rows = mm(rmsnorm(h, P['norm_kv']), P['kv_w']).reshape(B, T, 3, 2, NSA_GROUPS, NSA_HEAD_DIM)
    cmp_rows, slc_rows, win_rows = rows[:, :, 0], rows[:, :, 1], rows[:, :, 2]
    full_cmp = jnp.concatenate([past_cmp, cmp_rows], axis=1)
    full_slc = jnp.concatenate([past_slc, slc_rows], axis=1)
    win_all = jnp.concatenate([win_buf, win_rows], axis=1)
    kc = compress_rows(full_cmp[:, :, 0], P['cmp_k_w1'], P['cmp_k_b1'], P['cmp_k_w2'], P['cmp_k_b2'], P['cmp_k_pe'])
    vc = compress_rows(full_cmp[:, :, 1], P['cmp_v_w1'], P['cmp_v_b1'], P['cmp_v_w2'], P['cmp_v_b2'], P['cmp_v_pe'])
    n_keep = min(WINDOW, win_all.shape[1])
    win_new = win_all[:, win_all.shape[1] - n_keep:]
    kv = (kc, vc, full_slc[:, :, 0], full_slc[:, :, 1], win_all[:, :, 0], win_all[:, :, 1])
    return kv, cmp_rows, slc_rows, win_new


def _softmax_rows(s, valid):
    s = jnp.where(valid, s, NEG)
    m = jnp.max(s, axis=-1, keepdims=True)
    e = jnp.where(valid, jnp.exp(s - m), 0.0)
    return e / jnp.maximum(jnp.sum(e, axis=-1, keepdims=True), 1e-30)


def _nsa_cmp_kernel(q_ref, kc_ref, vc_ref, bias_ref, ov_ref, oc_ref, imp_ref, *, nc):
    i = pl.program_id(2)
    ncp = kc_ref.shape[2]
    qpos = i * TQ + lax.broadcasted_iota(jnp.int32, (TQ, ncp), 0)
    col = lax.broadcasted_iota(jnp.int32, (TQ, ncp), 1)
    valid = (col * CMP_STRIDE + (CMP_BLOCK - 1) <= qpos) & (col < nc)
    kc = kc_ref[0, 0]
    vc = vc_ref[0, 0]
    psum = jnp.zeros((TQ, ncp), jnp.float32)
    for hh in range(NSA_HPG):
        s = lax.dot_general(q_ref[0, 0, hh], kc, (((1,), (1,)), ((), ())), preferred_element_type=jnp.float32)
        p = _softmax_rows(s + bias_ref[0, hh], valid)
        psum = psum + p
        oc_ref[0, :, hh * NSA_HEAD_DIM:(hh + 1) * NSA_HEAD_DIM] = jnp.dot(
            p.astype(jnp.bfloat16), vc, preferred_element_type=jnp.float32)
    imp_ref[0, 0] = jnp.dot(psum.astype(jnp.bfloat16), ov_ref[...], preferred_element_type=jnp.float32)


def nsa_cmp(qh, kc, vc, bias_c, overlap, nc):
    B, G, HPG, T, dh = qh.shape
    ncp = kc.shape[2]
    nsp = overlap.shape[1]
    return pl.pallas_call(
        functools.partial(_nsa_cmp_kernel, nc=nc),
        grid=(B, G, T // TQ),
        in_specs=[pl.BlockSpec((1, 1, HPG, TQ, dh), lambda b, g, i: (b, g, 0, i, 0)),
                  pl.BlockSpec((1, 1, ncp, dh), lambda b, g, i: (b, g, 0, 0)),
                  pl.BlockSpec((1, 1, ncp, dh), lambda b, g, i: (b, g, 0, 0)),
                  pl.BlockSpec((1, HPG, TQ, ncp), lambda b, g, i: (g, 0, i, 0)),
                  pl.BlockSpec((ncp, nsp), lambda b, g, i: (0, 0))],
        out_specs=[pl.BlockSpec((1, TQ, HPG * dh), lambda b, g, i: (b, i, g)),
                   pl.BlockSpec((1, 1, TQ, nsp), lambda b, g, i: (b, g, i, 0))],
        out_shape=[jax.ShapeDtypeStruct((B, T, G * HPG * dh), jnp.float32),
                   jax.ShapeDtypeStruct((B, G, T, nsp), jnp.float32)],
        compiler_params=pltpu.CompilerParams(dimension_semantics=("parallel", "parallel", "parallel"),
                                             vmem_limit_bytes=VMEM_LIMIT),
        name="nsa_cmp",
    )(qh, kc, vc, bias_c, overlap)


def _flash_tile(q_all, k, v, bias_ref, dd_b, valid, m_ref, l_ref, acc_ref):
    s_all = lax.dot_general(q_all, k, (((1,), (1,)), ((), ())), preferred_element_type=jnp.float32)
    ps = []
    for hh in range(NSA_HPG):
        rows = slice(hh * TQ, (hh + 1) * TQ)
        s = jnp.where(valid, s_all[rows] + bias_ref[0, dd_b, hh], NEG)
        m_old = m_ref[rows]
        m_new = jnp.maximum(m_old, jnp.max(s, axis=-1, keepdims=True))
        p = jnp.where(valid, jnp.exp(s - m_new), 0.0)
        alpha = jnp.exp(m_old - m_new)
        l_ref[rows] = alpha * l_ref[rows] + jnp.sum(p, axis=-1, keepdims=True)
        acc_ref[rows] = alpha * acc_ref[rows]
        m_ref[rows] = m_new
        ps.append(p.astype(jnp.bfloat16))
    acc_ref[...] += jnp.dot(jnp.concatenate(ps, axis=0), v, preferred_element_type=jnp.float32)


def _nsa_sw_kernel(q_ref, ks_ref, vs_ref, kw_ref, vw_ref, msel_ref, exp_ref, bias_ref, selm_ref, winm_ref,
                   os_ref, ow_ref, msk_ref, m_ref, l_ref, acc_ref, *, nt):
    i = pl.program_id(2)
    q_all = q_ref[0, 0].reshape(NSA_HPG * TQ, NSA_HEAD_DIM)
    msel = msel_ref[0, 0].astype(jnp.bfloat16)
    for j in range(nt):
        msk_ref[j] = jnp.dot(msel, exp_ref[:, j * TQ:(j + 1) * TQ], preferred_element_type=jnp.float32)

    def reset():
        m_ref[...] = jnp.full_like(m_ref, NEG)
        l_ref[...] = jnp.zeros_like(l_ref)
        acc_ref[...] = jnp.zeros_like(acc_ref)

    def finish(o_ref):
        o = acc_ref[...] / jnp.maximum(l_ref[...], 1e-30)
        for hh in range(NSA_HPG):
            o_ref[0, :, hh * NSA_HEAD_DIM:(hh + 1) * NSA_HEAD_DIM] = o[hh * TQ:(hh + 1) * TQ]

    reset()

    def sel_body(j, c):
        dd = jnp.minimum(i - j, 2)
        valid = (msk_ref[j] * selm_ref[dd]) > 0.5
        _flash_tile(q_all, ks_ref[0, 0, j], vs_ref[0, 0, j], bias_ref, dd, valid, m_ref, l_ref, acc_ref)
        return c

    lax.fori_loop(0, i + 1, sel_body, 0)
    finish(os_ref)

    reset()

    def win_body(j, c):
        dd = i - j
        valid = winm_ref[dd] > 0.5
        _flash_tile(q_all, kw_ref[0, 0, j], vw_ref[0, 0, j], bias_ref, jnp.minimum(dd, 2), valid, m_ref, l_ref, acc_ref)
        return c

    lax.fori_loop(jnp.maximum(i - WINDOW // TQ, 0), i + 1, win_body, 0)
    finish(ow_ref)


def nsa_sel_win(qh, ks, vs, kw, vw, msel, expander, bias_t, selm, winm):
    B, G, HPG, T, dh = qh.shape
    nt = T // TQ
    nsp = msel.shape[-1]
    kv_spec = pl.BlockSpec((1, 1, nt, TQ, dh), lambda b, g, i: (b, g, 0, 0, 0))
    out_spec = pl.BlockSpec((1, TQ, HPG * dh), lambda b, g, i: (b, i, g))
    return pl.pallas_call(
        functools.partial(_nsa_sw_kernel, nt=nt),
        grid=(B, G, T // TQ),
        in_specs=[pl.BlockSpec((1, 1, HPG, TQ, dh), lambda b, g, i: (b, g, 0, i, 0)),
                  kv_spec, kv_spec, kv_spec, kv_spec,
                  pl.BlockSpec((1, 1, TQ, nsp), lambda b, g, i: (b, g, i, 0)),
                  pl.BlockSpec((nsp, T), lambda b, g, i: (0, 0)),
                  pl.BlockSpec((1, 3, HPG, TQ, TQ), lambda b, g, i: (g, 0, 0, 0, 0)),
                  pl.BlockSpec((3, TQ, TQ), lambda b, g, i: (0, 0, 0)),
                  pl.BlockSpec((WINDOW // TQ + 1, TQ, TQ), lambda b, g, i: (0, 0, 0))],
        out_specs=[out_spec, out_spec],
        out_shape=[jax.ShapeDtypeStruct((B, T, G * HPG * dh), jnp.float32)] * 2,
        scratch_shapes=[pltpu.VMEM((nt, TQ, TQ), jnp.float32),
                        pltpu.VMEM((HPG * TQ, 1), jnp.float32),
                        pltpu.VMEM((HPG * TQ, 1), jnp.float32),
                        pltpu.VMEM((HPG * TQ, dh), jnp.float32)],
        compiler_params=pltpu.CompilerParams(dimension_semantics=("parallel", "parallel", "arbitrary"),
                                             vmem_limit_bytes=VMEM_LIMIT),
        name="nsa_sel_win",
    )(qh, ks, vs, kw, vw, msel, expander, bias_t, selm, winm)


def nsa_prompt(q, gates, kc, vc, ks, vs, kw_rows, vw_rows, table):
    B, T = q.shape[0], q.shape[1]
    G, HPG, dh = NSA_GROUPS, NSA_HPG, NSA_HEAD_DIM
    Nc = kc.shape[1]
    Ns = -(-T // SEL_BLOCK)
    ncp = -(-Nc // LANE) * LANE
    nsp = -(-Ns // LANE) * LANE
    bf = jnp.bfloat16
    qh = jnp.transpose(q.reshape(B, T, G, HPG, dh), (0, 2, 3, 1, 4)).astype(bf)

    def grp(t, n_pad):
        t = jnp.transpose(t, (0, 2, 1, 3)).astype(bf)
        return jnp.pad(t, ((0, 0), (0, 0), (0, n_pad - t.shape[2]), (0, 0)))

    qpos = jnp.arange(T)
    cend = jnp.arange(ncp) * CMP_STRIDE + (CMP_BLOCK - 1)
    bias_c = head_bias(table, qpos[:, None] - cend[None, :])
    cstart = jnp.arange(ncp)[:, None] * CMP_STRIDE
    sstart = jnp.arange(nsp)[None, :] * SEL_BLOCK
    overlap = ((cstart < sstart + SEL_BLOCK) & (cstart + CMP_BLOCK > sstart)
               & (jnp.arange(ncp)[:, None] < Nc) & (jnp.arange(nsp)[None, :] < Ns)).astype(bf)
    o_c, imp = nsa_cmp(qh, grp(kc, ncp), grp(vc, ncp), bias_c, overlap, Nc)
    imp = imp[..., :Ns]
    blk = jnp.arange(Ns)[None, :]
    cur = qpos[:, None] // SEL_BLOCK
    valid = blk * SEL_BLOCK <= qpos[:, None]
    forced = (blk == 0) | (blk == cur) | (blk == cur - 1)
    score = jnp.where(valid, jnp.where(forced, BIG, imp), -BIG)
    n_pick = min(N_SEL, Ns)
    top_s, idx = lax.top_k(score, n_pick)
    ok = top_s > -0.5 * BIG
    msel = jnp.sum(jax.nn.one_hot(idx, nsp, dtype=jnp.float32) * ok[..., None], axis=-2)
    expander = (jnp.arange(nsp)[:, None] == (jnp.arange(T)[None, :] // SEL_BLOCK)).astype(bf)
    r = jnp.arange(TQ)
    rel3 = (jnp.arange(3) * TQ)[:, None, None] + r[None, :, None] - r[None, None, :]
    bias_t = jnp.transpose(table[t5_bucket(rel3)].astype(jnp.float32), (3, 0, 1, 2)).reshape(G, HPG, 3, TQ, TQ)
    bias_t = jnp.transpose(bias_t, (0, 2, 1, 3, 4))
    selm = (rel3 >= 0).astype(jnp.float32)
    nw = WINDOW // TQ + 1
    relw = (jnp.arange(nw) * TQ)[:, None, None] + r[None, :, None] - r[None, None, :]
    winm = ((relw >= 0) & (relw < WINDOW)).astype(jnp.float32)

    def tiles(t):
        return jnp.transpose(t, (0, 2, 1, 3)).astype(bf).reshape(B, G, T // TQ, TQ, dh)

    o_s, o_w = nsa_sel_win(qh, tiles(ks), tiles(vs), tiles(kw_rows), tiles(vw_rows), msel, expander, bias_t, selm, winm)
    gx = jnp.repeat(gates.reshape(B, T, G * HPG, 3), dh, axis=2)
    return gx[..., 0] * o_c + gx[..., 1] * o_s + gx[..., 2] * o_w


def nsa_attend(q, gates, kc, vc, ks, vs, kw, vw, q_start, table):
    B, T = q.shape[0], q.shape[1]
    G, dh = NSA_GROUPS, NSA_HEAD_DIM
    L = ks.shape[1]
    Nc = kc.shape[1]
    Ns = -(-L // SEL_BLOCK)
    pad = ((0, 0), (0, Ns * SEL_BLOCK - L), (0, 0), (0, 0))
    ks_b = jnp.pad(ks, pad).reshape(B, Ns, SEL_BLOCK, G, dh)
    vs_b = jnp.pad(vs, pad).reshape(B, Ns, SEL_BLOCK, G, dh)
    qh = jnp.transpose(q.reshape(B, T, G, NSA_HPG, dh), (0, 2, 3, 1, 4))
    gh = jnp.transpose(gates.reshape(B, T, G, NSA_HPG, 3), (0, 2, 3, 1, 4))
    kw_t = jnp.transpose(kw, (0, 2, 1, 3))
    vw_t = jnp.transpose(vw, (0, 2, 1, 3))
    cend = jnp.arange(Nc) * CMP_STRIDE + (CMP_BLOCK - 1)
    cstart = jnp.arange(Nc)[:, None] * CMP_STRIDE
    sstart = jnp.arange(Ns)[None, :] * SEL_BLOCK
    overlap = ((cstart < sstart + SEL_BLOCK) & (cstart + CMP_BLOCK > sstart)).astype(jnp.float32)
    n_pick = min(N_SEL, Ns)
    tab_g = table.astype(jnp.float32).reshape(REL_BUCKETS, G, NSA_HPG)
    bi = jnp.arange(B)[:, None, None, None]
    gi = jnp.arange(G)[None, :, None, None]
    blk = jnp.arange(Ns)[None, :]
    qb = Q_BLOCK if T % Q_BLOCK == 0 else T

    def body(o0):
        qblk = lax.dynamic_slice_in_dim(qh, o0, qb, axis=3)
        gblk = lax.dynamic_slice_in_dim(gh, o0, qb, axis=3)
        qpos = q_start + o0 + jnp.arange(qb)
        rel_c = qpos[:, None] - cend[None, :]
        s_c = jnp.einsum('bghqd,bngd->bghqn', qblk, kc).astype(jnp.float32) + head_bias(table, rel_c)
        p_c = masked_softmax(s_c, rel_c >= 0)
        o_c = jnp.einsum('bghqn,bngd->bghqd', p_c.astype(vc.dtype), vc)
        imp = jnp.einsum('bgqn,ns->bgqs', p_c.sum(axis=2), overlap)
        cur = qpos[:, None] // SEL_BLOCK
        valid = blk * SEL_BLOCK <= qpos[:, None]
        forced = (blk == 0) | (blk == cur) | (blk == cur - 1)
        score = jnp.where(valid, jnp.where(forced, BIG, imp), -BIG)
        top_s, idx = lax.top_k(score, n_pick)
        ok = top_s > -0.5 * BIG
        kg = ks_b[bi, idx, :, gi, :].reshape(B, G, qb, n_pick * SEL_BLOCK, dh)
        vg = vs_b[bi, idx, :, gi, :].reshape(B, G, qb, n_pick * SEL_BLOCK, dh)
        kpos = (idx[..., None] * SEL_BLOCK + jnp.arange(SEL_BLOCK)).reshape(B, G, qb, n_pick * SEL_BLOCK)
        rel_s = qpos[None, None, :, None] - kpos
        mask_s = jnp.repeat(ok, SEL_BLOCK, axis=-1) & (rel_s >= 0)
        bias_s = jnp.einsum('bgqkn,ngh->bghqk', jax.nn.one_hot(t5_bucket(rel_s), REL_BUCKETS, dtype=jnp.float32), tab_g)
        s_s = jnp.einsum('bghqd,bgqkd->bghqk', qblk, kg).astype(jnp.float32) + bias_s
        p_s = masked_softmax(s_s, mask_s[:, :, None])
        o_s = jnp.einsum('bghqk,bgqkd->bghqd', p_s.astype(vg.dtype), vg)
        kwb = lax.dynamic_slice_in_dim(kw_t, o0, qb + WINDOW, axis=2)
        vwb = lax.dynamic_slice_in_dim(vw_t, o0, qb + WINDOW, axis=2)
        wpos = q_start - WINDOW + o0 + jnp.arange(qb + WINDOW)
        rel_w = qpos[:, None] - wpos[None, :]
        mask_w = (rel_w >= 0) & (rel_w < WINDOW) & (wpos[None, :] >= 0)
        s_w = jnp.einsum('bghqd,bgkd->bghqk', qblk, kwb).astype(jnp.float32) + head_bias(table, rel_w)
        p_w = masked_softmax(s_w, mask_w)
        o_w = jnp.einsum('bghqk,bgkd->bghqd', p_w.astype(vwb.dtype), vwb)
        return gblk[..., 0:1] * o_c + gblk[..., 1:2] * o_s + gblk[..., 2:3] * o_w

    out = lax.map(body, jnp.arange(T // qb) * qb)
    return jnp.transpose(out, (1, 0, 4, 2, 3, 5)).reshape(B, T, NSA_HEADS * dh)


def run_trunk(x, gla_s0, past_cmp, past_slc, win_buf, P):
    q_start = past_slc.shape[1]
    B, T, _ = x.shape
    h = x
    gla_states = []
    kv = None
    for layer in range(DEPTH):
        xn = rmsnorm(h, P['norm_mix'][layer])
        if layer < N_A_LAYERS:
            o, s = gla_mixer(xn, gla_s0[layer], P['gla_w_in'][layer], P['gla_w_gate_up'][layer],
                             P['gla_b_gate'][layer], P['gla_norm'][layer], P['gla_w_out'][layer])
            gla_states.append(s)
        else:
            j = layer - N_A_LAYERS
            qd = NSA_HEADS * NSA_HEAD_DIM
            proj = mm(xn, P['nsa_w_in'][j])
            q = proj[..., :qd].reshape(B, T, NSA_HEADS, NSA_HEAD_DIM) * NSA_HEAD_DIM ** -0.5
            gates = jax.nn.sigmoid(proj[..., qd:].astype(jnp.float32)).reshape(B, T, NSA_HEADS, 3).astype(x.dtype)
            kc, vc, ks, vs, kw, vw = kv
            if q_start == 0 and win_buf.shape[1] == 0 and T % TQ == 0:
                att = nsa_prompt(q, gates, kc, vc, ks, vs, kw, vw, P['rel_bias'])
            else:
                wpad = ((0, 0), (WINDOW - win_buf.shape[1], 0), (0, 0), (0, 0))
                att = nsa_attend(q, gates, kc, vc, ks, vs, jnp.pad(kw, wpad), jnp.pad(vw, wpad), q_start, P['rel_bias'])
            o = mm(att, P['nsa_w_out'][j])
        h = h + o
        h = peer_ffn(h, rmsnorm(h, P['norm_ffn'][layer]), P['peer_w_q'][layer], P['peer_subkeys'][layer],
                     P['peer_u_bf'][layer], P['peer_v_bf'][layer])
        if layer == N_A_LAYERS - 1:
            kv, cmp_rows, slc_rows, win_new = shared_kv(h, past_cmp, past_slc, win_buf, P)
    return rmsnorm(h, P['norm_final']), jnp.stack(gla_states), cmp_rows, slc_rows, win_new


def kernel(x_prompt, x_sample, state_gla, cache_cmp_kv, cache_slc_kv, cache_win_kv, page_table,
           norm_mix, norm_ffn, norm_kv, norm_final, gla_w_in, gla_w_gate_up, gla_b_gate, gla_norm, gla_w_out,
           kv_w, cmp_k_w1, cmp_k_b1, cmp_k_w2, cmp_k_b2, cmp_k_pe, cmp_v_w1, cmp_v_b1, cmp_v_w2, cmp_v_b2, cmp_v_pe,
           nsa_w_in, nsa_w_out, rel_bias, peer_w_q, peer_subkeys, peer_u, peer_v):
    P = dict(norm_mix=norm_mix, norm_ffn=norm_ffn, norm_kv=norm_kv, norm_final=norm_final,
             gla_w_in=gla_w_in, gla_w_gate_up=gla_w_gate_up, gla_b_gate=gla_b_gate, gla_norm=gla_norm,
             gla_w_out=gla_w_out, kv_w=kv_w,
             cmp_k_w1=cmp_k_w1, cmp_k_b1=cmp_k_b1, cmp_k_w2=cmp_k_w2, cmp_k_b2=cmp_k_b2, cmp_k_pe=cmp_k_pe,
             cmp_v_w1=cmp_v_w1, cmp_v_b1=cmp_v_b1, cmp_v_w2=cmp_v_w2, cmp_v_b2=cmp_v_b2, cmp_v_pe=cmp_v_pe,
             nsa_w_in=nsa_w_in, nsa_w_out=nsa_w_out, rel_bias=rel_bias,
             peer_w_q=peer_w_q, peer_subkeys=peer_subkeys,
             peer_u_bf=peer_u.astype(jnp.bfloat16), peer_v_bf=peer_v.astype(jnp.bfloat16))
    past_rows = page_table.shape[1] * PAGE_SIZE

    def gather_pages(cache):
        return cache[page_table].reshape(DEC_BATCH, past_rows, 2, NSA_GROUPS, NSA_HEAD_DIM)

    empty = jnp.zeros((BATCH, 0, 2, NSA_GROUPS, NSA_HEAD_DIM), x_prompt.dtype)
    gla_zero = jnp.zeros((N_A_LAYERS, BATCH, GLA_HEADS, GLA_DK, GLA_DV), x_prompt.dtype)
    y_prompt, gla_p, cmp_p, slc_p, win_p = run_trunk(x_prompt, gla_zero, empty, empty, empty, P)
    y_sample, gla_s, cmp_s, slc_s, win_s = run_trunk(x_sample, state_gla, gather_pages(cache_cmp_kv),
                                                     gather_pages(cache_slc_kv), cache_win_kv, P)
    return (y_prompt, y_sample, gla_p, gla_s, cmp_p, cmp_s, slc_p, slc_s, win_p, win_s)
```

```python
import functools
import math

import jax
import jax.numpy as jnp
from jax import lax
from jax.experimental import pallas as pl
from jax.experimental.pallas import tpu as pltpu

D_MODEL = 2048
BATCH = 4
SEQ = 2048
DEPTH = 2
DEC_BATCH = 32
DEC_SEQ = 8
PAST_LEN = 8192
PAGE_SIZE = 128
N_A_LAYERS = DEPTH // 2
N_B_LAYERS = DEPTH - N_A_LAYERS
GLA_HEADS = 4
GLA_DK = D_MODEL // (2 * GLA_HEADS)
GLA_DV = D_MODEL // GLA_HEADS
GLA_GATE_RANK = 16
GLA_TAU = 16.0
GLA_CHUNK = 64
NSA_HEADS = 16
NSA_GROUPS = 4
NSA_HPG = NSA_HEADS // NSA_GROUPS
NSA_HEAD_DIM = D_MODEL // NSA_HEADS
CMP_STRIDE = 16
CMP_BLOCK = 2 * CMP_STRIDE
CMP_HIDDEN = 2 * NSA_HEAD_DIM
SEL_BLOCK = 64
N_SEL = 16
WINDOW = 512
Q_BLOCK = 32
REL_BUCKETS = 32
REL_MAX_DIST = 128
PEER_HEADS = 8
PEER_NKEYS = 128
PEER_EXPERTS = PEER_NKEYS * PEER_NKEYS
PEER_TOPK = 16
PEER_KEY_DIM = 256
PEER_TOK_BLOCK = 128
EPS = 1e-6
NEG = -1e30
BIG = 1e30

LANE = 128
VMEM_LIMIT = 56 * 1024 * 1024
TQ = 128
assert REL_MAX_DIST <= TQ and WINDOW % TQ == 0 and TQ % SEL_BLOCK == 0


def _mm_kernel(a_ref, b_ref, o_ref):
    a = a_ref[...].astype(jnp.bfloat16)
    b = b_ref[...].astype(jnp.bfloat16)
    o_ref[...] = jnp.dot(a, b, preferred_element_type=jnp.float32)


def pmm(a, b, tm=512, tn=512):
    M, K = a.shape
    N = b.shape[1]
    tm = min(tm, M)
    Mp = -(-M // tm) * tm
    Np = -(-N // LANE) * LANE
    tn = math.gcd(Np, tn)
    if Mp != M:
        a = jnp.pad(a, ((0, Mp - M), (0, 0)))
    if Np != N:
        b = jnp.pad(b, ((0, 0), (0, Np - N)))
    out = pl.pallas_call(
        _mm_kernel,
        grid=(Mp // tm, Np // tn),
        in_specs=[pl.BlockSpec((tm, K), lambda i, j: (i, 0)),
                  pl.BlockSpec((K, tn), lambda i, j: (0, j))],
        out_specs=pl.BlockSpec((tm, tn), lambda i, j: (i, j)),
        out_shape=jax.ShapeDtypeStruct((Mp, Np), jnp.float32),
        compiler_params=pltpu.CompilerParams(
            dimension_semantics=("parallel", "parallel"), vmem_limit_bytes=VMEM_LIMIT),
        name="proj_matmul",
    )(a, b)
    return out[:M, :N]


def mm(x, w):
    lead = x.shape[:-1]
    return pmm(x.reshape(-1, x.shape[-1]), w).reshape(lead + (w.shape[1],))


def rmsnorm(x, g):
    xf = x.astype(jnp.float32)
    y = xf * lax.rsqrt(jnp.mean(xf * xf, axis=-1, keepdims=True) + EPS)
    return (y * g.astype(jnp.float32)).astype(x.dtype)


def masked_softmax(s, mask):
    s = jnp.where(mask, s, NEG)
    m = jnp.max(s, axis=-1, keepdims=True)
    e = jnp.where(mask, jnp.exp(s - m), 0.0)
    return e / jnp.maximum(jnp.sum(e, axis=-1, keepdims=True), 1e-30)


def t5_bucket(rel):
    n = jnp.maximum(rel, 0)
    exact = REL_BUCKETS // 2
    nf = jnp.maximum(n, exact).astype(jnp.float32)
    large = exact + (jnp.log(nf / exact) / math.log(REL_MAX_DIST / exact) * (REL_BUCKETS - exact)).astype(jnp.int32)
    large = jnp.minimum(large, REL_BUCKETS - 1)
    return jnp.where(n < exact, n, large)


def head_bias(table, rel):
    b = table[t5_bucket(rel)].astype(jnp.float32)
    return jnp.transpose(b, (2, 0, 1)).reshape(NSA_GROUPS, NSA_HPG, rel.shape[0], rel.shape[1])


def gla_chunked(q, k, v, g, s0):
    B, H, T, DK = q.shape
    C = GLA_CHUNK if T % GLA_CHUNK == 0 else T
    n = T // C

    def chunks(t):
        return jnp.moveaxis(t.reshape(B, H, n, C, t.shape[-1]), 2, 0)

    causal = jnp.tril(jnp.ones((C, C), dtype=bool))

    def step(s, inp):
        qc, kc, vc, gc = inp
        b = jnp.cumsum(gc, axis=2)
        b_last = b[:, :, -1:, :]
        qe = qc * jnp.exp(b)
        a = jnp.einsum('bhid,bhjd->bhij', qe, kc * jnp.exp(-b))
        a = jnp.where(causal, a, 0.0)
        o = jnp.einsum('bhij,bhje->bhie', a, vc) + jnp.einsum('bhid,bhde->bhie', qe, s)
        s = s * jnp.exp(b_last[:, :, 0, :, None]) + jnp.einsum('bhjd,bhje->bhde', kc * jnp.exp(b_last - b), vc)
        return s, o

    s_fin, o = lax.scan(step, s0, (chunks(q), chunks(k), chunks(v), chunks(g)))
    o = jnp.moveaxis(o, 0, 2).reshape(B, H, T, v.shape[-1])
    return o, s_fin


def gla_mixer(xn, s0, w_in, w_gate_up, b_gate, norm_g, w_out):
    B, T, _ = xn.shape
    dk_all = GLA_HEADS * GLA_DK
    dv_all = GLA_HEADS * GLA_DV
    proj = mm(xn, w_in)
    q, k, v, r, gz = jnp.split(proj, [dk_all, 2 * dk_all, 2 * dk_all + dv_all, 2 * dk_all + 2 * dv_all], axis=-1)
    glog = jax.nn.log_sigmoid((gz @ w_gate_up + b_gate).astype(jnp.float32)) / GLA_TAU

    def heads(t):
        return jnp.transpose(t.reshape(B, T, GLA_HEADS, -1), (0, 2, 1, 3)).astype(jnp.float32)

    o, s_new = gla_chunked(heads(q) * GLA_DK ** -0.5, heads(k), heads(v), heads(glog), s0.astype(jnp.float32))
    o = o * lax.rsqrt(jnp.mean(o * o, axis=-1, keepdims=True) + EPS) * norm_g.astype(jnp.float32)
    o = jnp.transpose(o, (0, 2, 1, 3)).reshape(B, T, dv_all).astype(xn.dtype)
    o = o * jax.nn.silu(r)
    return mm(o, w_out), s_new.astype(s0.dtype)


WSUM_UNROLL = 8


def _wsum_kernel(a_ref, b_ref, g_ref, o_ref, *, tb):
    sub = lax.broadcasted_iota(jnp.int32, (PEER_NKEYS, PEER_NKEYS), 0)

    def body(tt, c):
        t0 = pl.multiple_of(tt * WSUM_UNROLL, WSUM_UNROLL)
        a8 = a_ref[pl.ds(t0, WSUM_UNROLL), :]
        b8 = b_ref[pl.ds(t0, WSUM_UNROLL), :]
        g8 = g_ref[pl.ds(t0, WSUM_UNROLL), :]
        for u in range(WSUM_UNROLL):
            at = jnp.where(sub == a8[u:u + 1], 1.0, 0.0).astype(jnp.bfloat16)
            bt = jnp.where(sub == b8[u:u + 1], g8[u:u + 1], 0.0).astype(jnp.bfloat16)
            o_ref[t0 + u] = lax.dot_general(at, bt, (((1,), (1,)), ((), ())), preferred_element_type=jnp.float32)
        return c

    lax.fori_loop(0, tb // WSUM_UNROLL, body, 0)


def peer_wsum(i1, i2, gate, tb=64):
    n, K = i1.shape
    tb = math.gcd(n, tb)
    assert tb % WSUM_UNROLL == 0
    return pl.pallas_call(
        functools.partial(_wsum_kernel, tb=tb),
        grid=(n // tb,),
        in_specs=[pl.BlockSpec((tb, K), lambda i: (i, 0))] * 3,
        out_specs=pl.BlockSpec((tb, PEER_NKEYS, PEER_NKEYS), lambda i: (i, 0, 0)),
        out_shape=jax.ShapeDtypeStruct((n, PEER_NKEYS, PEER_NKEYS), jnp.float32),
        compiler_params=pltpu.CompilerParams(dimension_semantics=("parallel",), vmem_limit_bytes=VMEM_LIMIT),
        name="peer_wsum",
    )(i1, i2, gate)


def _peer_kernel(x_ref, u_ref, v_ref, w_ref, h_ref, o_ref, acc_ref, *, te):
    j = pl.program_id(1)

    @pl.when(j == 0)
    def _():
        acc_ref[...] = jnp.zeros_like(acc_ref)

    hid = lax.dot_general(x_ref[...], u_ref[...], (((1,), (1,)), ((), ())), preferred_element_type=jnp.float32)
    parts = []
    for r in range(te // PEER_NKEYS):
        hr = hid[:, r * PEER_NKEYS:(r + 1) * PEER_NKEYS]
        ar = 0.5 * hr * (1.0 + lax.erf(hr * (2.0 ** -0.5))) * w_ref[:, r, :]
        parts.append(ar.astype(jnp.bfloat16))
    a = jnp.concatenate(parts, axis=1)
    acc_ref[...] += jnp.dot(a, v_ref[...], preferred_element_type=jnp.float32)

    @pl.when(j == pl.num_programs(1) - 1)
    def _():
        o_ref[...] = h_ref[...] + acc_ref[...]


def peer_dense(xn, u, v, wsum, h, tb=512, te=1024):
    n, D = xn.shape
    E = u.shape[0]
    tb = math.gcd(n, tb)
    return pl.pallas_call(
        functools.partial(_peer_kernel, te=te),
        grid=(n // tb, E // te),
        in_specs=[pl.BlockSpec((tb, D), lambda i, j: (i, 0)),
                  pl.BlockSpec((te, D), lambda i, j: (j, 0)),
                  pl.BlockSpec((te, D), lambda i, j: (j, 0)),
                  pl.BlockSpec((tb, te // PEER_NKEYS, PEER_NKEYS), lambda i, j: (i, j, 0)),
                  pl.BlockSpec((tb, D), lambda i, j: (i, 0))],
        out_specs=pl.BlockSpec((tb, D), lambda i, j: (i, 0)),
        out_shape=jax.ShapeDtypeStruct((n, D), jnp.float32),
        scratch_shapes=[pltpu.VMEM((tb, D), jnp.float32)],
        compiler_params=pltpu.CompilerParams(dimension_semantics=("parallel", "arbitrary"),
                                             vmem_limit_bytes=VMEM_LIMIT),
        name="peer_dense",
    )(xn, u, v, wsum, h)


RT = LANE
NHC = 2 * PEER_HEADS
_CAND_GROUPS = [(0, 0), (0, 8), (1, 0)] + [(a, 0) for a in range(2, 8)] + [(-1, 0)]
assert PEER_TOPK == 16 and PEER_KEY_DIM // 2 == LANE and PEER_NKEYS == LANE


def _top_rounds(s, n_rounds, extra=()):
    R = s.shape[0]
    iota = lax.broadcasted_iota(jnp.int32, s.shape, 0)
    vals, idxs, ex = [], [], [[] for _ in extra]
    for _ in range(n_rounds):
        m = jnp.max(s, axis=0, keepdims=True)
        idx = jnp.min(jnp.where(s == m, iota, R), axis=0, keepdims=True)
        sel = iota == idx
        for e, lst in zip(extra, ex):
            lst.append(jnp.max(jnp.where(sel, e, -1), axis=0, keepdims=True))
        s = jnp.where(sel, -jnp.inf, s)
        vals.append(m)
        idxs.append(idx)
    cat = lambda l: jnp.concatenate(l, axis=0)
    return cat(vals), cat(idxs), [cat(l) for l in ex]


def _route_kernel(q_ref, sub_ref, e1_ref, e2_ref, g_ref, v_scr, i_scr):
    K = PEER_TOPK

    def stage1(hc, c):
        off = pl.multiple_of(hc * LANE, LANE)
        qb = q_ref[:, pl.ds(off, LANE)].astype(jnp.bfloat16)
        s = lax.dot_general(sub_ref[hc], qb, (((1,), (1,)), ((), ())), preferred_element_type=jnp.float32)
        v, i, _ = _top_rounds(s, K)
        v_scr[hc] = v
        i_scr[hc] = i
        return c

    lax.fori_loop(0, NHC, stage1, 0)

    row8 = lax.broadcasted_iota(jnp.int32, (8, RT), 0)

    def stage2(h, c):
        v1, v2 = v_scr[2 * h], v_scr[2 * h + 1]
        i1, i2 = i_scr[2 * h], i_scr[2 * h + 1]
        cand, c1, c2 = [], [], []
        for a, b0 in _CAND_GROUPS:
            if a >= 0:
                nb = K // (a + 1)
                sm = v1[a:a + 1] + v2[b0:b0 + 8]
                if nb - b0 < 8:
                    sm = jnp.where(row8 < nb - b0, sm, -jnp.inf)
                cand.append(sm)
                c1.append(jnp.broadcast_to(i1[a:a + 1], (8, RT)))
                c2.append(i2[b0:b0 + 8])
            else:
                cand.append(v1[8:16] + v2[0:1])
                c1.append(i1[8:16])
                c2.append(jnp.broadcast_to(i2[0:1], (8, RT)))
        cat = lambda l: jnp.concatenate(l, axis=0)
        top, _, (e1, e2) = _top_rounds(cat(cand), K, extra=(cat(c1), cat(c2)))
        ex = jnp.exp(top - top[0:1])
        g = ex / jnp.sum(ex, axis=0, keepdims=True)
        r0 = pl.multiple_of(h * K, K)
        e1_ref[0, pl.ds(r0, K), :] = e1
        e2_ref[0, pl.ds(r0, K), :] = e2
        g_ref[0, pl.ds(r0, K), :] = g
        return c

    lax.fori_loop(0, PEER_HEADS, stage2, 0)


def peer_route(q, sub_bf):
    n = q.shape[0]
    assert n % RT == 0
    nb = n // RT
    slots = PEER_HEADS * PEER_TOPK
    out = jax.ShapeDtypeStruct((nb, slots, RT), jnp.int32)
    ospec = pl.BlockSpec((1, slots, RT), lambda i: (i, 0, 0))
    return pl.pallas_call(
        _route_kernel,
        grid=(nb,),
        in_specs=[pl.BlockSpec((RT, q.shape[1]), lambda i: (i, 0)),
                  pl.BlockSpec(sub_bf.shape, lambda i: (0, 0, 0))],
        out_specs=[ospec, ospec, ospec],
        out_shape=[out, out, jax.ShapeDtypeStruct((nb, slots, RT), jnp.float32)],
        scratch_shapes=[pltpu.VMEM((NHC, PEER_TOPK, RT), jnp.float32),
                        pltpu.VMEM((NHC, PEER_TOPK, RT), jnp.int32)],
        compiler_params=pltpu.CompilerParams(dimension_semantics=("parallel",), vmem_limit_bytes=VMEM_LIMIT),
        name="peer_route",
    )(q, sub_bf)


def peer_ffn(h, xn, w_q, subkeys, u_bf, v_bf):
    B, T, D = xn.shape
    n = B * T
    xt = xn.reshape(n, D)
    q = pmm(xt, w_q)
    sub_bf = subkeys.astype(jnp.bfloat16).reshape(NHC, PEER_NKEYS, PEER_KEY_DIM // 2)
    e1, e2, gate = peer_route(q, sub_bf)
    tok_major = lambda t: jnp.transpose(t, (0, 2, 1)).reshape(n, PEER_HEADS * PEER_TOPK)
    wsum = peer_wsum(tok_major(e1), tok_major(e2), tok_major(gate))
    out = peer_dense(xt.astype(jnp.bfloat16), u_bf, v_bf, wsum, h.reshape(n, D))
    return out.reshape(B, T, D)


KVG = 2 * NSA_GROUPS
ROW_W = KVG * NSA_HEAD_DIM
CPP = PAGE_SIZE // CMP_STRIDE
PPS = 4
assert CPP == 8 and PAGE_SIZE == LANE and NSA_HEAD_DIM == LANE


def _compress_kernel(pt_ref, *refs, n_steps):
    pages = refs[:PPS]
    w1_ref, c1_ref, w2_ref, b2_ref, kc_ref, vc_ref, xc_ref = refs[PPS:]
    s_idx = pl.program_id(1)
    for pair in range(PPS // 2):
        row0 = pl.multiple_of((s_idx * (PPS // 2) + pair) * 2 * CPP, 2 * CPP)
        for kvg in range(KVG):
            for s in range(CMP_STRIDE):
                lo = pages[2 * pair][0, pl.ds(s * KVG + kvg, CPP, stride=KVG * CMP_STRIDE), :]
                hi = pages[2 * pair + 1][0, pl.ds(s * KVG + kvg, CPP, stride=KVG * CMP_STRIDE), :]
                xc_ref[kvg, pl.ds(row0, 2 * CPP), pl.ds(s * LANE, LANE)] = (
                    jnp.concatenate([lo, hi], axis=0).astype(jnp.bfloat16))

    @pl.when(s_idx == n_steps - 1)
    def _():
        nch = xc_ref.shape[1]
        for kv in range(2):
            out_ref = kc_ref if kv == 0 else vc_ref
            for g in range(NSA_GROUPS):
                hh = jnp.dot(xc_ref[kv * NSA_GROUPS + g], w1_ref[kv], preferred_element_type=jnp.float32)
                h1 = hh[:, :CMP_HIDDEN]
                h2 = pltpu.roll(hh[:, CMP_HIDDEN:], nch - 1, 0)
                x = h1 + h2 + c1_ref[kv]
                hid = 0.5 * x * (1.0 + lax.erf(x * (2.0 ** -0.5)))
                o = jnp.dot(hid.astype(jnp.bfloat16), w2_ref[kv], preferred_element_type=jnp.float32) + b2_ref[kv]
                out_ref[0, g] = o.astype(out_ref.dtype)


def compress_paged(pool, page_table, w1, c1, w2, b2):
    B, NP = page_table.shape
    assert NP % PPS == 0
    n_steps = NP // PPS
    nch = NP * CPP
    dh = NSA_HEAD_DIM
    pool = pool.reshape(pool.shape[0], PAGE_SIZE * KVG, dh)

    def page_spec(u):
        return pl.BlockSpec((1, PAGE_SIZE * KVG, dh), lambda b, s, pt: (pt[b, s * PPS + u], 0, 0))

    full = lambda a: pl.BlockSpec(a.shape, lambda b, s, pt: (0,) * a.ndim)
    out_spec = pl.BlockSpec((1, NSA_GROUPS, nch, dh), lambda b, s, pt: (b, 0, 0, 0))
    grid_spec = pltpu.PrefetchScalarGridSpec(
        num_scalar_prefetch=1,
        grid=(B, n_steps),
        in_specs=[page_spec(u) for u in range(PPS)] + [full(w1), full(c1), full(w2), full(b2)],
        out_specs=[out_spec, out_spec],
        scratch_shapes=[pltpu.VMEM((KVG, nch, CMP_STRIDE * dh), jnp.bfloat16)],
    )
    out = jax.ShapeDtypeStruct((B, NSA_GROUPS, nch, dh), jnp.bfloat16)
    return pl.pallas_call(
        functools.partial(_compress_kernel, n_steps=n_steps),
        grid_spec=grid_spec,
        out_shape=[out, out],
        compiler_params=pltpu.CompilerParams(dimension_semantics=("parallel", "arbitrary"),
                                             vmem_limit_bytes=VMEM_LIMIT),
        name="nsa_compress",
    )(page_table, *([pool] * PPS), w1, c1, w2, b2)


def compress_params(P):
    dh, F = NSA_HEAD_DIM, CMP_HIDDEN
    w1s, c1s, w2s, b2s = [], [], [], []
    for n in ('k', 'v'):
        w1h = P['cmp_%s_w1' % n].reshape(2, CMP_STRIDE * dh, F)
        pe = P['cmp_%s_pe' % n].reshape(2, CMP_STRIDE * dh)
        w1s.append(jnp.concatenate([w1h[0], w1h[1]], axis=1))
        c1s.append(P['cmp_%s_b1' % n] + jnp.einsum('hk,hkf->f', pe, w1h, precision=lax.Precision.HIGHEST))
        w2s.append(P['cmp_%s_w2' % n])
        b2s.append(P['cmp_%s_b2' % n])
    return (jnp.stack(w1s).astype(jnp.bfloat16), jnp.stack(c1s)[:, None, :],
            jnp.stack(w2s).astype(jnp.bfloat16), jnp.stack(b2s)[:, None, :])


def _softmax_rows(s, valid):
    s = jnp.where(valid, s, NEG)
    m = jnp.max(s, axis=-1, keepdims=True)
    e = jnp.where(valid, jnp.exp(s - m), 0.0)
    return e / jnp.maximum(jnp.sum(e, axis=-1, keepdims=True), 1e-30)


def _nsa_cmp_kernel(q_ref, kc_ref, vc_ref, bias_ref, ov_ref, oc_ref, imp_ref, *, nc):
    i = pl.program_id(2)
    ncp = kc_ref.shape[2]
    qpos = i * TQ + lax.broadcasted_iota(jnp.int32, (TQ, ncp), 0)
    col = lax.broadcasted_iota(jnp.int32, (TQ, ncp), 1)
    valid = (col * CMP_STRIDE + (CMP_BLOCK - 1) <= qpos) & (col < nc)
    kc = kc_ref[0, 0]
    vc = vc_ref[0, 0]
    psum = jnp.zeros((TQ, ncp), jnp.float32)
    for hh in range(NSA_HPG):
        s = lax.dot_general(q_ref[0, 0, hh], kc, (((1,), (1,)), ((), ())), preferred_element_type=jnp.float32)
        p = _softmax_rows(s + bias_ref[0, hh], valid)
        psum = psum + p
        oc_ref[0, :, hh * NSA_HEAD_DIM:(hh + 1) * NSA_HEAD_DIM] = jnp.dot(
            p.astype(jnp.bfloat16), vc, preferred_element_type=jnp.float32)
    imp_ref[0, 0] = jnp.dot(psum.astype(jnp.bfloat16), ov_ref[...], preferred_element_type=jnp.float32)


def nsa_cmp(qh, kc, vc, bias_c, overlap, nc):
    B, G, HPG, T, dh = qh.shape
    ncp = kc.shape[2]
    nsp = overlap.shape[1]
    return pl.pallas_call(
        functools.partial(_nsa_cmp_kernel, nc=nc),
        grid=(B, G, T // TQ),
        in_specs=[pl.BlockSpec((1, 1, HPG, TQ, dh), lambda b, g, i: (b, g, 0, i, 0)),
                  pl.BlockSpec((1, 1, ncp, dh), lambda b, g, i: (b, g, 0, 0)),
                  pl.BlockSpec((1, 1, ncp, dh), lambda b, g, i: (b, g, 0, 0)),
                  pl.BlockSpec((1, HPG, TQ, ncp), lambda b, g, i: (g, 0, i, 0)),
                  pl.BlockSpec((ncp, nsp), lambda b, g, i: (0, 0))],
        out_specs=[pl.BlockSpec((1, TQ, HPG * dh), lambda b, g, i: (b, i, g)),
                   pl.BlockSpec((1, 1, TQ, nsp), lambda b, g, i: (b, g, i, 0))],
        out_shape=[jax.ShapeDtypeStruct((B, T, G * HPG * dh), jnp.float32),
                   jax.ShapeDtypeStruct((B, G, T, nsp), jnp.float32)],
        compiler_params=pltpu.CompilerParams(dimension_semantics=("parallel", "parallel", "parallel"),
                                             vmem_limit_bytes=VMEM_LIMIT),
        name="nsa_cmp",
    )(qh, kc, vc, bias_c, overlap)


def _flash_tile(q_all, k, v, bias_ref, dd_b, valid, m_ref, l_ref, acc_ref):
    s_all = lax.dot_general(q_all, k, (((1,), (1,)), ((), ())), preferred_element_type=jnp.float32)
    ps = []
    for hh in range(NSA_HPG):
        rows = slice(hh * TQ, (hh + 1) * TQ)
        s = jnp.where(valid, s_all[rows] + bias_ref[0, dd_b, hh], NEG)
        m_old = m_ref[rows]
        m_new = jnp.maximum(m_old, jnp.max(s, axis=-1, keepdims=True))
        p = jnp.where(valid, jnp.exp(s - m_new), 0.0)
        alpha = jnp.exp(m_old - m_new)
        l_ref[rows] = alpha * l_ref[rows] + jnp.sum(p, axis=-1, keepdims=True)
        acc_ref[rows] = alpha * acc_ref[rows]
        m_ref[rows] = m_new
        ps.append(p.astype(jnp.bfloat16))
    acc_ref[...] += jnp.dot(jnp.concatenate(ps, axis=0), v, preferred_element_type=jnp.float32)


def _nsa_sw_kernel(q_ref, ks_ref, vs_ref, kw_ref, vw_ref, msel_ref, exp_ref, bias_ref, selm_ref, winm_ref,
                   os_ref, ow_ref, msk_ref, m_ref, l_ref, acc_ref, *, nt):
    i = pl.program_id(2)
    q_all = q_ref[0, 0].reshape(NSA_HPG * TQ, NSA_HEAD_DIM)
    msel = msel_ref[0, 0].astype(jnp.bfloat16)
    for j in range(nt):
        msk_ref[j] = jnp.dot(msel, exp_ref[:, j * TQ:(j + 1) * TQ], preferred_element_type=jnp.float32)

    def reset():
        m_ref[...] = jnp.full_like(m_ref, NEG)
        l_ref[...] = jnp.zeros_like(l_ref)
        acc_ref[...] = jnp.zeros_like(acc_ref)

    def finish(o_ref):
        o = acc_ref[...] / jnp.maximum(l_ref[...], 1e-30)
        for hh in range(NSA_HPG):
            o_ref[0, :, hh * NSA_HEAD_DIM:(hh + 1) * NSA_HEAD_DIM] = o[hh * TQ:(hh + 1) * TQ]

    reset()

    def sel_body(j, c):
        dd = jnp.minimum(i - j, 2)
        valid = (msk_ref[j] * selm_ref[dd]) > 0.5
        _flash_tile(q_all, ks_ref[0, 0, j], vs_ref[0, 0, j], bias_ref, dd, valid, m_ref, l_ref, acc_ref)
        return c

    lax.fori_loop(0, i + 1, sel_body, 0)
    finish(os_ref)

    reset()

    def win_body(j, c):
        dd = i - j
        valid = winm_ref[dd] > 0.5
        _flash_tile(q_all, kw_ref[0, 0, j], vw_ref[0, 0, j], bias_ref, jnp.minimum(dd, 2), valid, m_ref, l_ref, acc_ref)
        return c

    lax.fori_loop(jnp.maximum(i - WINDOW // TQ, 0), i + 1, win_body, 0)
    finish(ow_ref)


def nsa_sel_win(qh, ks, vs, kw, vw, msel, expander, bias_t, selm, winm):
    B, G, HPG, T, dh = qh.shape
    nt = T // TQ
    nsp = msel.shape[-1]
    kv_spec = pl.BlockSpec((1, 1, nt, TQ, dh), lambda b, g, i: (b, g, 0, 0, 0))
    out_spec = pl.BlockSpec((1, TQ, HPG * dh), lambda b, g, i: (b, i, g))
    return pl.pallas_call(
        functools.partial(_nsa_sw_kernel, nt=nt),
        grid=(B, G, T // TQ),
        in_specs=[pl.BlockSpec((1, 1, HPG, TQ, dh), lambda b, g, i: (b, g, 0, i, 0)),
                  kv_spec, kv_spec, kv_spec, kv_spec,
                  pl.BlockSpec((1, 1, TQ, nsp), lambda b, g, i: (b, g, i, 0)),
                  pl.BlockSpec((nsp, T), lambda b, g, i: (0, 0)),
                  pl.BlockSpec((1, 3, HPG, TQ, TQ), lambda b, g, i: (g, 0, 0, 0, 0)),
                  pl.BlockSpec((3, TQ, TQ), lambda b, g, i: (0, 0, 0)),
                  pl.BlockSpec((WINDOW // TQ + 1, TQ, TQ), lambda b, g, i: (0, 0, 0))],
        out_specs=[out_spec, out_spec],
        out_shape=[jax.ShapeDtypeStruct((B, T, G * HPG * dh), jnp.float32)] * 2,
        scratch_shapes=[pltpu.VMEM((nt, TQ, TQ), jnp.float32),
                        pltpu.VMEM((HPG * TQ, 1), jnp.float32),
                        pltpu.VMEM((HPG * TQ, 1), jnp.float32),
                        pltpu.VMEM((HPG * TQ, dh), jnp.float32)],
        compiler_params=pltpu.CompilerParams(dimension_semantics=("parallel", "parallel", "arbitrary"),
                                             vmem_limit_bytes=VMEM_LIMIT),
        name="nsa_sel_win",
    )(qh, ks, vs, kw, vw, msel, expander, bias_t, selm, winm)


def nsa_prompt(q, gates, kc, vc, ks, vs, kw_rows, vw_rows, table):
    B, T = q.shape[0], q.shape[1]
    G, HPG, dh = NSA_GROUPS, NSA_HPG, NSA_HEAD_DIM
    Nc = T // CMP_STRIDE - 1
    Ns = -(-T // SEL_BLOCK)
    ncp = kc.shape[2]
    assert ncp % LANE == 0 and ncp >= Nc
    nsp = -(-Ns // LANE) * LANE
    bf = jnp.bfloat16
    qh = jnp.transpose(q.reshape(B, T, G, HPG, dh), (0, 2, 3, 1, 4)).astype(bf)

    qpos = jnp.arange(T)
    cend = jnp.arange(ncp) * CMP_STRIDE + (CMP_BLOCK - 1)
    bias_c = head_bias(table, qpos[:, None] - cend[None, :])
    cstart = jnp.arange(ncp)[:, None] * CMP_STRIDE
    sstart = jnp.arange(nsp)[None, :] * SEL_BLOCK
    overlap = ((cstart < sstart + SEL_BLOCK) & (cstart + CMP_BLOCK > sstart)
               & (jnp.arange(ncp)[:, None] < Nc) & (jnp.arange(nsp)[None, :] < Ns)).astype(bf)
    o_c, imp = nsa_cmp(qh, kc, vc, bias_c, overlap, Nc)
    imp = imp[..., :Ns]
    blk = jnp.arange(Ns)[None, :]
    cur = qpos[:, None] // SEL_BLOCK
    valid = blk * SEL_BLOCK <= qpos[:, None]
    forced = (blk == 0) | (blk == cur) | (blk == cur - 1)
    score = jnp.where(valid, jnp.where(forced, BIG, imp), -BIG)
    n_pick = min(N_SEL, Ns)
    top_s, idx = lax.top_k(score, n_pick)
    ok = top_s > -0.5 * BIG
    msel = jnp.sum(jax.nn.one_hot(idx, nsp, dtype=jnp.float32) * ok[..., None], axis=-2)
    expander = (jnp.arange(nsp)[:, None] == (jnp.arange(T)[None, :] // SEL_BLOCK)).astype(bf)
    r = jnp.arange(TQ)
    rel3 = (jnp.arange(3) * TQ)[:, None, None] + r[None, :, None] - r[None, None, :]
    bias_t = jnp.transpose(table[t5_bucket(rel3)].astype(jnp.float32), (3, 0, 1, 2)).reshape(G, HPG, 3, TQ, TQ)
    bias_t = jnp.transpose(bias_t, (0, 2, 1, 3, 4))
    selm = (rel3 >= 0).astype(jnp.float32)
    nw = WINDOW // TQ + 1
    relw = (jnp.arange(nw) * TQ)[:, None, None] + r[None, :, None] - r[None, None, :]
    winm = ((relw >= 0) & (relw < WINDOW)).astype(jnp.float32)

    def tiles(t):
        return jnp.transpose(t, (0, 2, 1, 3)).astype(bf).reshape(B, G, T // TQ, TQ, dh)

    o_s, o_w = nsa_sel_win(qh, tiles(ks), tiles(vs), tiles(kw_rows), tiles(vw_rows), msel, expander, bias_t, selm, winm)
    gx = jnp.repeat(gates.reshape(B, T, G * HPG, 3), dh, axis=2)
    return gx[..., 0] * o_c + gx[..., 1] * o_s + gx[..., 2] * o_w


def _dec_cmp_kernel(q_ref, kc_ref, vc_ref, bias_ref, valid_ref, ov_ref, oc_ref, imp_ref, *, t):
    valid = valid_ref[...] > 0.5
    for g in range(NSA_GROUPS):
        s = lax.dot_general(q_ref[0, g], kc_ref[0, g], (((1,), (1,)), ((), ())), preferred_element_type=jnp.float32)
        p = _softmax_rows(s + bias_ref[g], valid)
        oc_ref[0, g] = jnp.dot(p.astype(jnp.bfloat16), vc_ref[0, g], preferred_element_type=jnp.float32)
        psum = p[0:t]
        for hh in range(1, NSA_HPG):
            psum = psum + p[hh * t:(hh + 1) * t]
        imp_ref[0, g] = jnp.dot(psum.astype(jnp.bfloat16), ov_ref[...], preferred_element_type=jnp.float32)


def dec_cmp(qg, kc, vc, bias_c, valid_c, overlap, t):
    B, G, R_, dh = qg.shape
    ncp, nsp = overlap.shape
    per_b = lambda a: pl.BlockSpec((1,) + a.shape[1:], lambda b: (b,) + (0,) * (a.ndim - 1))
    full = lambda a: pl.BlockSpec(a.shape, lambda b: (0,) * a.ndim)
    return pl.pallas_call(
        functools.partial(_dec_cmp_kernel, t=t),
        grid=(B,),
        in_specs=[per_b(qg), per_b(kc), per_b(vc), full(bias_c), full(valid_c), full(overlap)],
        out_specs=[pl.BlockSpec((1, G, R_, dh), lambda b: (b, 0, 0, 0)),
                   pl.BlockSpec((1, G, t, nsp), lambda b: (b, 0, 0, 0))],
        out_shape=[jax.ShapeDtypeStruct((B, G, R_, dh), jnp.float32),
                   jax.ShapeDtypeStruct((B, G, t, nsp), jnp.float32)],
        compiler_params=pltpu.CompilerParams(dimension_semantics=("parallel",), vmem_limit_bytes=VMEM_LIMIT),
        name="nsa_dec_cmp",
    )(qg, kc, vc, bias_c, valid_c, overlap)


def _flash_update(s, valid, v, m_ref, l_ref, acc_ref, g):
    s = jnp.where(valid, s, NEG)
    m_old = m_ref[g]
    m_new = jnp.maximum(m_old, jnp.max(s, axis=-1, keepdims=True))
    p = jnp.where(valid, jnp.exp(s - m_new), 0.0)
    alpha = jnp.exp(m_old - m_new)
    l_ref[g] = alpha * l_ref[g] + jnp.sum(p, axis=-1, keepdims=True)
    acc_ref[g] = alpha * acc_ref[g] + jnp.dot(p.astype(jnp.bfloat16), v, preferred_element_type=jnp.float32)
    m_ref[g] = m_new


def _dec_sel_kernel(pt_ref, q_ref, *refs, n_steps, t):
    pages = refs[:PPS]
    new_ref, bias_ref, mask_ref, biasn_ref, maskn_ref, o_ref, m_ref, l_ref, acc_ref = refs[PPS:]
    s_idx = pl.program_id(1)
    dh = NSA_HEAD_DIM

    @pl.when(s_idx == 0)
    def _():
        m_ref[...] = jnp.full_like(m_ref, NEG)
        l_ref[...] = jnp.zeros_like(l_ref)
        acc_ref[...] = jnp.zeros_like(acc_ref)

    def tile_mask(mref, g):
        mk = mref[0, g] > 0.5
        return jnp.concatenate([mk] * NSA_HPG, axis=0)

    for g in range(NSA_GROUPS):
        k = jnp.concatenate([pg[0, :, pl.ds(g * dh, dh)] for pg in pages], axis=0).astype(jnp.bfloat16)
        v = jnp.concatenate([pg[0, :, pl.ds((NSA_GROUPS + g) * dh, dh)] for pg in pages], axis=0).astype(jnp.bfloat16)
        s = lax.dot_general(q_ref[0, g], k, (((1,), (1,)), ((), ())), preferred_element_type=jnp.float32)
        _flash_update(s + bias_ref[g], tile_mask(mask_ref, g), v, m_ref, l_ref, acc_ref, g)

    @pl.when(s_idx == n_steps - 1)
    def _():
        for g in range(NSA_GROUPS):
            k = new_ref[0, :, pl.ds(g * dh, dh)].astype(jnp.bfloat16)
            v = new_ref[0, :, pl.ds((NSA_GROUPS + g) * dh, dh)].astype(jnp.bfloat16)
            s = lax.dot_general(q_ref[0, g], k, (((1,), (1,)), ((), ())), preferred_element_type=jnp.float32)
            _flash_update(s + biasn_ref[g], tile_mask(maskn_ref, g), v, m_ref, l_ref, acc_ref, g)
            o_ref[0, g] = acc_ref[g] / jnp.maximum(l_ref[g], 1e-30)


def dec_sel(qg, pool, page_table, new_rows, bias_p, mask_p, bias_n, mask_n, t):
    B, G, R_, dh = qg.shape
    NP = page_table.shape[1]
    assert NP % PPS == 0
    n_steps = NP // PPS
    W = PPS * PAGE_SIZE

    def page_spec(u):
        return pl.BlockSpec((1, PAGE_SIZE, ROW_W), lambda b, s, pt: (pt[b, s * PPS + u], 0, 0))

    grid_spec = pltpu.PrefetchScalarGridSpec(
        num_scalar_prefetch=1,
        grid=(B, n_steps),
        in_specs=[pl.BlockSpec((1, G, R_, dh), lambda b, s, pt: (b, 0, 0, 0))]
        + [page_spec(u) for u in range(PPS)]
        + [pl.BlockSpec((1, PAGE_SIZE, ROW_W), lambda b, s, pt: (b, 0, 0)),
           pl.BlockSpec((G, R_, W), lambda b, s, pt: (0, 0, s)),
           pl.BlockSpec((1, G, t, W), lambda b, s, pt: (b, 0, 0, s)),
           pl.BlockSpec((G, R_, PAGE_SIZE), lambda b, s, pt: (0, 0, 0)),
           pl.BlockSpec((1, G, t, PAGE_SIZE), lambda b, s, pt: (b, 0, 0, 0))],
        out_specs=pl.BlockSpec((1, G, R_, dh), lambda b, s, pt: (b, 0, 0, 0)),
        scratch_shapes=[pltpu.VMEM((G, R_, 1), jnp.float32), pltpu.VMEM((G, R_, 1), jnp.float32),
                        pltpu.VMEM((G, R_, dh), jnp.float32)],
    )
    return pl.pallas_call(
        functools.partial(_dec_sel_kernel, n_steps=n_steps, t=t),
        grid_spec=grid_spec,
        out_shape=jax.ShapeDtypeStruct((B, G, R_, dh), jnp.float32),
        compiler_params=pltpu.CompilerParams(dimension_semantics=("parallel", "arbitrary"),
                                             vmem_limit_bytes=VMEM_LIMIT),
        name="nsa_dec_sel",
    )(page_table, qg, *([pool] * PPS), new_rows, bias_p, mask_p, bias_n, mask_n)


def _dec_win_kernel(q_ref, win_ref, new_ref, bias_ref, valid_ref, o_ref):
    dh = NSA_HEAD_DIM
    valid = valid_ref[...] > 0.5
    for g in range(NSA_GROUPS):
        k = jnp.concatenate([win_ref[0, :, pl.ds(g * dh, dh)], new_ref[0, :, pl.ds(g * dh, dh)]], axis=0)
        v = jnp.concatenate([win_ref[0, :, pl.ds((NSA_GROUPS + g) * dh, dh)],
                             new_ref[0, :, pl.ds((NSA_GROUPS + g) * dh, dh)]], axis=0)
        s = lax.dot_general(q_ref[0, g], k.astype(jnp.bfloat16), (((1,), (1,)), ((), ())),
                            preferred_element_type=jnp.float32)
        p = _softmax_rows(s + bias_ref[g], valid)
        o_ref[0, g] = jnp.dot(p.astype(jnp.bfloat16), v.astype(jnp.bfloat16), preferred_element_type=jnp.float32)


def dec_win(qg, win_rows, new_rows, bias_w, valid_w):
    B, G, R_, dh = qg.shape
    per_b = lambda a: pl.BlockSpec((1,) + a.shape[1:], lambda b: (b,) + (0,) * (a.ndim - 1))
    full = lambda a: pl.BlockSpec(a.shape, lambda b: (0,) * a.ndim)
    return pl.pallas_call(
        _dec_win_kernel,
        grid=(B,),
        in_specs=[per_b(qg), per_b(win_rows), per_b(new_rows), full(bias_w), full(valid_w)],
        out_specs=pl.BlockSpec((1, G, R_, dh), lambda b: (b, 0, 0, 0)),
        out_shape=jax.ShapeDtypeStruct((B, G, R_, dh), jnp.float32),
        compiler_params=pltpu.CompilerParams(dimension_semantics=("parallel",), vmem_limit_bytes=VMEM_LIMIT),
        name="nsa_dec_win",
    )(qg, win_rows, new_rows, bias_w, valid_w)


def nsa_decode(q, gates, kc, vc, slc_pool, page_table, slc_new, win_buf, win_new_rows, table):
    B, t = q.shape[0], q.shape[1]
    G, HPG, dh = NSA_GROUPS, NSA_HPG, NSA_HEAD_DIM
    NP = page_table.shape[1]
    past = NP * PAGE_SIZE
    Wn = win_buf.shape[1]
    assert t <= PAGE_SIZE and Wn == WINDOW
    L = past + t
    nc = L // CMP_STRIDE - 1
    ncp = kc.shape[2]
    Ns = -(-L // SEL_BLOCK)
    nsp = -(-Ns // LANE) * LANE
    bf = jnp.bfloat16
    R_ = HPG * t
    qg = jnp.transpose(q.reshape(B, t, G, HPG, dh), (0, 2, 3, 1, 4)).reshape(B, G, R_, dh).astype(bf)
    qpos = past + jnp.arange(t)

    def rows_bias(rel):
        return head_bias(table, rel).reshape(G, R_, rel.shape[1])

    rep = lambda m: jnp.tile(m, (HPG, 1))
    cidx = jnp.arange(ncp)
    cend = cidx * CMP_STRIDE + (CMP_BLOCK - 1)
    rel_c = qpos[:, None] - cend[None, :]
    valid_c = rep(((rel_c >= 0) & (cidx[None, :] < nc)).astype(jnp.float32))
    cstart = cidx[:, None] * CMP_STRIDE
    sstart = jnp.arange(nsp)[None, :] * SEL_BLOCK
    overlap = ((cstart < sstart + SEL_BLOCK) & (cstart + CMP_BLOCK > sstart)
               & (cidx[:, None] < nc) & (jnp.arange(nsp)[None, :] < Ns)).astype(bf)
    o_c, imp = dec_cmp(qg, kc, vc, rows_bias(rel_c), valid_c, overlap, t)
    imp = imp[..., :Ns]
    blk = jnp.arange(Ns)[None, :]
    cur = qpos[:, None] // SEL_BLOCK
    valid = blk * SEL_BLOCK <= qpos[:, None]
    forced = (blk == 0) | (blk == cur) | (blk == cur - 1)
    score = jnp.where(valid, jnp.where(forced, BIG, imp), -BIG)
    n_pick = min(N_SEL, Ns)
    top_s, idx = lax.top_k(score, n_pick)
    ok = top_s > -0.5 * BIG
    msel = jnp.sum(jax.nn.one_hot(idx, Ns, dtype=jnp.float32) * ok[..., None], axis=-2)
    kpos = jnp.arange(past + PAGE_SIZE)
    mkey = jnp.take(msel, jnp.minimum(kpos // SEL_BLOCK, Ns - 1), axis=-1)
    mkey = mkey * ((kpos[None, :] <= qpos[:, None]) & (kpos[None, :] < L)).astype(jnp.float32)
    bias_s = rows_bias(qpos[:, None] - kpos[None, :])
    pad_new = lambda r: jnp.pad(r, ((0, 0), (0, PAGE_SIZE - t), (0, 0)))
    o_s = dec_sel(qg, slc_pool, page_table, pad_new(slc_new), bias_s[..., :past], mkey[..., :past],
                  bias_s[..., past:], mkey[..., past:], t)
    wpos = jnp.concatenate([past - Wn + jnp.arange(Wn), past + jnp.arange(PAGE_SIZE)])
    rel_w = qpos[:, None] - wpos[None, :]
    valid_w = rep(((rel_w >= 0) & (rel_w < WINDOW) & (wpos[None, :] >= 0) & (wpos[None, :] < L)).astype(jnp.float32))
    o_w = dec_win(qg, win_buf, pad_new(win_new_rows), rows_bias(rel_w), valid_w)
    back = lambda o: jnp.transpose(o.reshape(B, G, HPG, t, dh), (0, 3, 1, 2, 4)).reshape(B, t, G * HPG * dh)
    gx = jnp.repeat(gates.reshape(B, t, G * HPG, 3), dh, axis=2)
    return gx[..., 0] * back(o_c) + gx[..., 1] * back(o_s) + gx[..., 2] * back(o_w)


def kv_rows(h, P):
    B, T, _ = h.shape
    return mm(rmsnorm(h, P['norm_kv']), P['kv_w']).reshape(B, T, 3, ROW_W)


def run_trunk(x, gla_s0, past, P):
    B, T, _ = x.shape
    h = x
    gla_states = []
    for layer in range(DEPTH):
        xn = rmsnorm(h, P['norm_mix'][layer])
        if layer < N_A_LAYERS:
            o, s = gla_mixer(xn, gla_s0[layer], P['gla_w_in'][layer], P['gla_w_gate_up'][layer],
                             P['gla_b_gate'][layer], P['gla_norm'][layer], P['gla_w_out'][layer])
            gla_states.append(s)
        else:
            j = layer - N_A_LAYERS
            qd = NSA_HEADS * NSA_HEAD_DIM
            proj = mm(xn, P['nsa_w_in'][j])
            q = proj[..., :qd].reshape(B, T, NSA_HEADS, NSA_HEAD_DIM) * NSA_HEAD_DIM ** -0.5
            gates = jax.nn.sigmoid(proj[..., qd:].astype(jnp.float32)).reshape(B, T, NSA_HEADS, 3).astype(x.dtype)
            if past is None:
                r5 = rows.reshape(B, T, 3, 2, NSA_GROUPS, NSA_HEAD_DIM)
                att = nsa_prompt(q, gates, kc, vc, r5[:, :, 1, 0], r5[:, :, 1, 1], r5[:, :, 2, 0], r5[:, :, 2, 1],
                                 P['rel_bias'])
            else:
                att = nsa_decode(q, gates, kc, vc, past['slc_pool'], past['page_table'], rows[:, :, 1],
                                 past['win_buf'], rows[:, :, 2], P['rel_bias'])
            o = mm(att, P['nsa_w_out'][j])
        h = h + o
        h = peer_ffn(h, rmsnorm(h, P['norm_ffn'][layer]), P['peer_w_q'][layer], P['peer_subkeys'][layer],
                     P['peer_u_bf'][layer], P['peer_v_bf'][layer])
        if layer == N_A_LAYERS - 1:
            rows = kv_rows(h, P)
            if past is None:
                assert T % (PAGE_SIZE * PPS) == 0 and T % TQ == 0
                npg = T // PAGE_SIZE
                pool = rows[:, :, 0].reshape(B * npg, PAGE_SIZE, ROW_W)
                table = jnp.arange(B * npg, dtype=jnp.int32).reshape(B, npg)
                win_new = rows[:, T - min(WINDOW, T):, 2]
            else:
                past_len = past['page_table'].shape[1] * PAGE_SIZE
                assert (past_len + T) // CMP_STRIDE == past_len // CMP_STRIDE
                pool, table = past['cmp_pool'], past['page_table']
                win_all = jnp.concatenate([past['win_buf'], rows[:, :, 2]], axis=1)
                win_new = win_all[:, win_all.shape[1] - min(WINDOW, win_all.shape[1]):]
            kc, vc = compress_paged(pool, table, *P['cmp_stack'])
    kv5 = lambda r: r.reshape(r.shape[0], r.shape[1], 2, NSA_GROUPS, NSA_HEAD_DIM)
    return (rmsnorm(h, P['norm_final']), jnp.stack(gla_states), kv5(rows[:, :, 0]), kv5(rows[:, :, 1]), kv5(win_new))


def kernel(x_prompt, x_sample, state_gla, cache_cmp_kv, cache_slc_kv, cache_win_kv, page_table,
           norm_mix, norm_ffn, norm_kv, norm_final, gla_w_in, gla_w_gate_up, gla_b_gate, gla_norm, gla_w_out,
           kv_w, cmp_k_w1, cmp_k_b1, cmp_k_w2, cmp_k_b2, cmp_k_pe, cmp_v_w1, cmp_v_b1, cmp_v_w2, cmp_v_b2, cmp_v_pe,
           nsa_w_in, nsa_w_out, rel_bias, peer_w_q, peer_subkeys, peer_u, peer_v):
    P = dict(norm_mix=norm_mix, norm_ffn=norm_ffn, norm_kv=norm_kv, norm_final=norm_final,
             gla_w_in=gla_w_in, gla_w_gate_up=gla_w_gate_up, gla_b_gate=gla_b_gate, gla_norm=gla_norm,
             gla_w_out=gla_w_out, kv_w=kv_w,
             cmp_k_w1=cmp_k_w1, cmp_k_b1=cmp_k_b1, cmp_k_w2=cmp_k_w2, cmp_k_b2=cmp_k_b2, cmp_k_pe=cmp_k_pe,
             cmp_v_w1=cmp_v_w1, cmp_v_b1=cmp_v_b1, cmp_v_w2=cmp_v_w2, cmp_v_b2=cmp_v_b2, cmp_v_pe=cmp_v_pe,
             nsa_w_in=nsa_w_in, nsa_w_out=nsa_w_out, rel_bias=rel_bias,
             peer_w_q=peer_w_q, peer_subkeys=peer_subkeys,
             peer_u_bf=peer_u.astype(jnp.bfloat16), peer_v_bf=peer_v.astype(jnp.bfloat16))
    P['cmp_stack'] = compress_params(P)
    n_pool = cache_cmp_kv.shape[0]
    past = dict(cmp_pool=cache_cmp_kv.reshape(n_pool, PAGE_SIZE, ROW_W),
                slc_pool=cache_slc_kv.reshape(n_pool, PAGE_SIZE, ROW_W),
                win_buf=cache_win_kv.reshape(cache_win_kv.shape[0], cache_win_kv.shape[1], ROW_W),
                page_table=page_table.astype(jnp.int32))
    gla_zero = jnp.zeros((N_A_LAYERS, BATCH, GLA_HEADS, GLA_DK, GLA_DV), x_prompt.dtype)
    y_prompt, gla_p, cmp_p, slc_p, win_p = run_trunk(x_prompt, gla_zero, None, P)
    y_sample, gla_s, cmp_s, slc_s, win_s = run_trunk(x_sample, state_gla, past, P)
    return (y_prompt, y_sample, gla_p, gla_s, cmp_p, cmp_s, slc_p, slc_s, win_p, win_s)
```

```python
import functools
import math

import jax
import jax.numpy as jnp
from jax import lax
from jax.experimental import pallas as pl
from jax.experimental.pallas import tpu as pltpu

D_MODEL = 2048
BATCH = 4
SEQ = 2048
DEPTH = 2
DEC_BATCH = 32
DEC_SEQ = 8
PAST_LEN = 8192
PAGE_SIZE = 128
N_A_LAYERS = DEPTH // 2
N_B_LAYERS = DEPTH - N_A_LAYERS
GLA_HEADS = 4
GLA_DK = D_MODEL // (2 * GLA_HEADS)
GLA_DV = D_MODEL // GLA_HEADS
GLA_GATE_RANK = 16
GLA_TAU = 16.0
GLA_CHUNK = 64
NSA_HEADS = 16
NSA_GROUPS = 4
NSA_HPG = NSA_HEADS // NSA_GROUPS
NSA_HEAD_DIM = D_MODEL // NSA_HEADS
CMP_STRIDE = 16
CMP_BLOCK = 2 * CMP_STRIDE
CMP_HIDDEN = 2 * NSA_HEAD_DIM
SEL_BLOCK = 64
N_SEL = 16
WINDOW = 512
Q_BLOCK = 32
REL_BUCKETS = 32
REL_MAX_DIST = 128
PEER_HEADS = 8
PEER_NKEYS = 128
PEER_EXPERTS = PEER_NKEYS * PEER_NKEYS
PEER_TOPK = 16
PEER_KEY_DIM = 256
PEER_TOK_BLOCK = 128
EPS = 1e-6
NEG = -1e30
BIG = 1e30

LANE = 128
VMEM_LIMIT = 56 * 1024 * 1024
TQ = 128
assert REL_MAX_DIST <= TQ and WINDOW % TQ == 0 and TQ % SEL_BLOCK == 0


def _mm_kernel(a_ref, b_ref, o_ref):
    a = a_ref[...].astype(jnp.bfloat16)
    b = b_ref[...].astype(jnp.bfloat16)
    o_ref[...] = jnp.dot(a, b, preferred_element_type=jnp.float32)


def pmm(a, b, tm=512, tn=512):
    M, K = a.shape
    N = b.shape[1]
    tm = min(tm, M)
    Mp = -(-M // tm) * tm
    Np = -(-N // LANE) * LANE
    tn = math.gcd(Np, tn)
    if Mp != M:
        a = jnp.pad(a, ((0, Mp - M), (0, 0)))
    if Np != N:
        b = jnp.pad(b, ((0, 0), (0, Np - N)))
    out = pl.pallas_call(
        _mm_kernel,
        grid=(Mp // tm, Np // tn),
        in_specs=[pl.BlockSpec((tm, K), lambda i, j: (i, 0)),
                  pl.BlockSpec((K, tn), lambda i, j: (0, j))],
        out_specs=pl.BlockSpec((tm, tn), lambda i, j: (i, j)),
        out_shape=jax.ShapeDtypeStruct((Mp, Np), jnp.float32),
        compiler_params=pltpu.CompilerParams(
            dimension_semantics=("parallel", "parallel"), vmem_limit_bytes=VMEM_LIMIT),
        name="proj_matmul",
    )(a, b)
    return out[:M, :N]


def mm(x, w):
    lead = x.shape[:-1]
    return pmm(x.reshape(-1, x.shape[-1]), w).reshape(lead + (w.shape[1],))


def rmsnorm(x, g):
    xf = x.astype(jnp.float32)
    y = xf * lax.rsqrt(jnp.mean(xf * xf, axis=-1, keepdims=True) + EPS)
    return (y * g.astype(jnp.float32)).astype(x.dtype)


def masked_softmax(s, mask):
    s = jnp.where(mask, s, NEG)
    m = jnp.max(s, axis=-1, keepdims=True)
    e = jnp.where(mask, jnp.exp(s - m), 0.0)
    return e / jnp.maximum(jnp.sum(e, axis=-1, keepdims=True), 1e-30)


def t5_bucket(rel):
    n = jnp.maximum(rel, 0)
    exact = REL_BUCKETS // 2
    nf = jnp.maximum(n, exact).astype(jnp.float32)
    large = exact + (jnp.log(nf / exact) / math.log(REL_MAX_DIST / exact) * (REL_BUCKETS - exact)).astype(jnp.int32)
    large = jnp.minimum(large, REL_BUCKETS - 1)
    return jnp.where(n < exact, n, large)


def head_bias(table, rel):
    b = table[t5_bucket(rel)].astype(jnp.float32)
    return jnp.transpose(b, (2, 0, 1)).reshape(NSA_GROUPS, NSA_HPG, rel.shape[0], rel.shape[1])


def gla_chunked(q, k, v, g, s0):
    B, H, T, DK = q.shape
    C = GLA_CHUNK if T % GLA_CHUNK == 0 else T
    n = T // C

    def chunks(t):
        return jnp.moveaxis(t.reshape(B, H, n, C, t.shape[-1]), 2, 0)

    causal = jnp.tril(jnp.ones((C, C), dtype=bool))

    def step(s, inp):
        qc, kc, vc, gc = inp
        b = jnp.cumsum(gc, axis=2)
        b_last = b[:, :, -1:, :]
        qe = qc * jnp.exp(b)
        a = jnp.einsum('bhid,bhjd->bhij', qe, kc * jnp.exp(-b))
        a = jnp.where(causal, a, 0.0)
        o = jnp.einsum('bhij,bhje->bhie', a, vc) + jnp.einsum('bhid,bhde->bhie', qe, s)
        s = s * jnp.exp(b_last[:, :, 0, :, None]) + jnp.einsum('bhjd,bhje->bhde', kc * jnp.exp(b_last - b), vc)
        return s, o

    s_fin, o = lax.scan(step, s0, (chunks(q), chunks(k), chunks(v), chunks(g)))
    o = jnp.moveaxis(o, 0, 2).reshape(B, H, T, v.shape[-1])
    return o, s_fin


def gla_mixer(xn, s0, w_in, w_gate_up, b_gate, norm_g, w_out):
    B, T, _ = xn.shape
    dk_all = GLA_HEADS * GLA_DK
    dv_all = GLA_HEADS * GLA_DV
    proj = mm(xn, w_in)
    q, k, v, r, gz = jnp.split(proj, [dk_all, 2 * dk_all, 2 * dk_all + dv_all, 2 * dk_all + 2 * dv_all], axis=-1)
    glog = jax.nn.log_sigmoid((gz @ w_gate_up + b_gate).astype(jnp.float32)) / GLA_TAU

    def heads(t):
        return jnp.transpose(t.reshape(B, T, GLA_HEADS, -1), (0, 2, 1, 3)).astype(jnp.float32)

    o, s_new = gla_chunked(heads(q) * GLA_DK ** -0.5, heads(k), heads(v), heads(glog), s0.astype(jnp.float32))
    o = o * lax.rsqrt(jnp.mean(o * o, axis=-1, keepdims=True) + EPS) * norm_g.astype(jnp.float32)
    o = jnp.transpose(o, (0, 2, 1, 3)).reshape(B, T, dv_all).astype(xn.dtype)
    o = o * jax.nn.silu(r)
    return mm(o, w_out), s_new.astype(s0.dtype)


WSUM_UNROLL = 8
PEER_CHUNK = 512


def _wsum_kernel(a_ref, b_ref, g_ref, o_ref, *, tb):
    sub = lax.broadcasted_iota(jnp.int32, (PEER_NKEYS, PEER_NKEYS), 0)

    def body(tt, c):
        t0 = pl.multiple_of(tt * WSUM_UNROLL, WSUM_UNROLL)
        a8 = a_ref[pl.ds(t0, WSUM_UNROLL), :]
        b8 = b_ref[pl.ds(t0, WSUM_UNROLL), :]
        g8 = g_ref[pl.ds(t0, WSUM_UNROLL), :]
        for u in range(WSUM_UNROLL):
            at = jnp.where(sub == a8[u:u + 1], 1.0, 0.0).astype(jnp.bfloat16)
            bt = jnp.where(sub == b8[u:u + 1], g8[u:u + 1], 0.0).astype(jnp.bfloat16)
            o_ref[t0 + u] = lax.dot_general(at, bt, (((1,), (1,)), ((), ())), preferred_element_type=jnp.float32)
        return c

    lax.fori_loop(0, tb // WSUM_UNROLL, body, 0)


def peer_wsum(i1, i2, gate, tb=64):
    n, K = i1.shape
    tb = math.gcd(n, tb)
    assert tb % WSUM_UNROLL == 0
    return pl.pallas_call(
        functools.partial(_wsum_kernel, tb=tb),
        grid=(n // tb,),
        in_specs=[pl.BlockSpec((tb, K), lambda i: (i, 0))] * 3,
        out_specs=pl.BlockSpec((tb, PEER_NKEYS, PEER_NKEYS), lambda i: (i, 0, 0)),
        out_shape=jax.ShapeDtypeStruct((n, PEER_NKEYS, PEER_NKEYS), jnp.float32),
        compiler_params=pltpu.CompilerParams(dimension_semantics=("parallel",), vmem_limit_bytes=VMEM_LIMIT),
        name="peer_wsum",
    )(i1, i2, gate)


def _peer_kernel(x_ref, u_ref, v_ref, w_ref, h_ref, o_ref, acc_ref, *, te):
    j = pl.program_id(1)

    @pl.when(j == 0)
    def _():
        acc_ref[...] = jnp.zeros_like(acc_ref)

    x = x_ref[...]
    acc = None
    for c in range(te // PEER_CHUNK):
        e0 = c * PEER_CHUNK
        hid = lax.dot_general(x, u_ref[e0:e0 + PEER_CHUNK, :], (((1,), (1,)), ((), ())),
                              preferred_element_type=jnp.float32)
        parts = []
        for r in range(PEER_CHUNK // PEER_NKEYS):
            hr = hid[:, r * PEER_NKEYS:(r + 1) * PEER_NKEYS]
            ar = 0.5 * hr * (1.0 + lax.erf(hr * (2.0 ** -0.5))) * w_ref[:, e0 // PEER_NKEYS + r, :]
            parts.append(ar.astype(jnp.bfloat16))
        d = jnp.dot(jnp.concatenate(parts, axis=1), v_ref[e0:e0 + PEER_CHUNK, :], preferred_element_type=jnp.float32)
        acc = d if acc is None else acc + d
    acc_ref[...] += acc

    @pl.when(j == pl.num_programs(1) - 1)
    def _():
        o_ref[...] = h_ref[...] + acc_ref[...]


def peer_dense(xn, u, v, wsum, h, tb=512, te=1024):
    n, D = xn.shape
    E = u.shape[0]
    tb = math.gcd(n, tb)
    return pl.pallas_call(
        functools.partial(_peer_kernel, te=te),
        grid=(n // tb, E // te),
        in_specs=[pl.BlockSpec((tb, D), lambda i, j: (i, 0)),
                  pl.BlockSpec((te, D), lambda i, j: (j, 0)),
                  pl.BlockSpec((te, D), lambda i, j: (j, 0)),
                  pl.BlockSpec((tb, te // PEER_NKEYS, PEER_NKEYS), lambda i, j: (i, j, 0)),
                  pl.BlockSpec((tb, D), lambda i, j: (i, 0))],
        out_specs=pl.BlockSpec((tb, D), lambda i, j: (i, 0)),
        out_shape=jax.ShapeDtypeStruct((n, D), jnp.float32),
        scratch_shapes=[pltpu.VMEM((tb, D), jnp.float32)],
        compiler_params=pltpu.CompilerParams(dimension_semantics=("parallel", "arbitrary"),
                                             vmem_limit_bytes=VMEM_LIMIT),
        name="peer_dense",
    )(xn, u, v, wsum, h)


RT = LANE
NHC = 2 * PEER_HEADS
_CAND_GROUPS = [(0, 0), (0, 8), (1, 0)] + [(a, 0) for a in range(2, 8)] + [(-1, 0)]
assert PEER_TOPK == 16 and PEER_KEY_DIM // 2 == LANE and PEER_NKEYS == LANE


def _top_rounds(s, n_rounds, extra=()):
    R = s.shape[0]
    iota = lax.broadcasted_iota(jnp.int32, s.shape, 0)
    vals, idxs, ex = [], [], [[] for _ in extra]
    for _ in range(n_rounds):
        m = jnp.max(s, axis=0, keepdims=True)
        idx = jnp.min(jnp.where(s == m, iota, R), axis=0, keepdims=True)
        sel = iota == idx
        for e, lst in zip(extra, ex):
            lst.append(jnp.max(jnp.where(sel, e, -1), axis=0, keepdims=True))
        s = jnp.where(sel, -jnp.inf, s)
        vals.append(m)
        idxs.append(idx)
    cat = lambda l: jnp.concatenate(l, axis=0)
    return cat(vals), cat(idxs), [cat(l) for l in ex]


def _route_kernel(q_ref, sub_ref, e1_ref, e2_ref, g_ref, v_scr, i_scr):
    K = PEER_TOPK

    def stage1(hc, c):
        off = pl.multiple_of(hc * LANE, LANE)
        qb = q_ref[:, pl.ds(off, LANE)].astype(jnp.bfloat16)
        s = lax.dot_general(sub_ref[hc], qb, (((1,), (1,)), ((), ())), preferred_element_type=jnp.float32)
        v, i, _ = _top_rounds(s, K)
        v_scr[hc] = v
        i_scr[hc] = i
        return c

    lax.fori_loop(0, NHC, stage1, 0)

    row8 = lax.broadcasted_iota(jnp.int32, (8, RT), 0)

    def stage2(h, c):
        v1, v2 = v_scr[2 * h], v_scr[2 * h + 1]
        i1, i2 = i_scr[2 * h], i_scr[2 * h + 1]
        cand, c1, c2 = [], [], []
        for a, b0 in _CAND_GROUPS:
            if a >= 0:
                nb = K // (a + 1)
                sm = v1[a:a + 1] + v2[b0:b0 + 8]
                if nb - b0 < 8:
                    sm = jnp.where(row8 < nb - b0, sm, -jnp.inf)
                cand.append(sm)
                c1.append(jnp.broadcast_to(i1[a:a + 1], (8, RT)))
                c2.append(i2[b0:b0 + 8])
            else:
                cand.append(v1[8:16] + v2[0:1])
                c1.append(i1[8:16])
                c2.append(jnp.broadcast_to(i2[0:1], (8, RT)))
        cat = lambda l: jnp.concatenate(l, axis=0)
        top, _, (e1, e2) = _top_rounds(cat(cand), K, extra=(cat(c1), cat(c2)))
        ex = jnp.exp(top - top[0:1])
        g = ex / jnp.sum(ex, axis=0, keepdims=True)
        r0 = pl.multiple_of(h * K, K)
        e1_ref[0, pl.ds(r0, K), :] = e1
        e2_ref[0, pl.ds(r0, K), :] = e2
        g_ref[0, pl.ds(r0, K), :] = g
        return c

    lax.fori_loop(0, PEER_HEADS, stage2, 0)


def peer_route(q, sub_bf):
    n = q.shape[0]
    assert n % RT == 0
    nb = n // RT
    slots = PEER_HEADS * PEER_TOPK
    out = jax.ShapeDtypeStruct((nb, slots, RT), jnp.int32)
    ospec = pl.BlockSpec((1, slots, RT), lambda i: (i, 0, 0))
    return pl.pallas_call(
        _route_kernel,
        grid=(nb,),
        in_specs=[pl.BlockSpec((RT, q.shape[1]), lambda i: (i, 0)),
                  pl.BlockSpec(sub_bf.shape, lambda i: (0, 0, 0))],
        out_specs=[ospec, ospec, ospec],
        out_shape=[out, out, jax.ShapeDtypeStruct((nb, slots, RT), jnp.float32)],
        scratch_shapes=[pltpu.VMEM((NHC, PEER_TOPK, RT), jnp.float32),
                        pltpu.VMEM((NHC, PEER_TOPK, RT), jnp.int32)],
        compiler_params=pltpu.CompilerParams(dimension_semantics=("parallel",), vmem_limit_bytes=VMEM_LIMIT),
        name="peer_route",
    )(q, sub_bf)


def peer_ffn(h, xn, w_q, subkeys, u_bf, v_bf):
    B, T, D = xn.shape
    n = B * T
    xt = xn.reshape(n, D)
    q = pmm(xt, w_q)
    sub_bf = subkeys.astype(jnp.bfloat16).reshape(NHC, PEER_NKEYS, PEER_KEY_DIM // 2)
    e1, e2, gate = peer_route(q, sub_bf)
    tok_major = lambda t: jnp.transpose(t, (0, 2, 1)).reshape(n, PEER_HEADS * PEER_TOPK)
    wsum = peer_wsum(tok_major(e1), tok_major(e2), tok_major(gate))
    out = peer_dense(xt.astype(jnp.bfloat16), u_bf, v_bf, wsum, h.reshape(n, D))
    return out.reshape(B, T, D)


KVG = 2 * NSA_GROUPS
ROW_W = KVG * NSA_HEAD_DIM
CPP = PAGE_SIZE // CMP_STRIDE
PPS = 4
assert CPP == 8 and PAGE_SIZE == LANE and NSA_HEAD_DIM == LANE


def _compress_kernel(pt_ref, *refs, n_steps):
    pages = refs[:PPS]
    perm_ref, w1_ref, c1_ref, w2_ref, b2_ref, kc_ref, vc_ref, xc_ref = refs[PPS:]
    s_idx = pl.program_id(1)
    for pair in range(PPS // 2):
        row0 = pl.multiple_of((s_idx * (PPS // 2) + pair) * 2 * CPP, 2 * CPP)
        x2 = jnp.concatenate([pages[2 * pair][0], pages[2 * pair + 1][0]], axis=0).astype(jnp.bfloat16)
        y = jnp.dot(perm_ref[...], x2, preferred_element_type=jnp.float32).astype(jnp.bfloat16)
        for kvg in range(KVG):
            for s in range(CMP_STRIDE):
                xc_ref[kvg, pl.ds(row0, 2 * CPP), pl.ds(s * LANE, LANE)] = (
                    y[s * 2 * CPP:(s + 1) * 2 * CPP, kvg * LANE:(kvg + 1) * LANE])

    @pl.when(s_idx == n_steps - 1)
    def _():
        nch = xc_ref.shape[1]
        for kv in range(2):
            out_ref = kc_ref if kv == 0 else vc_ref
            for g in range(NSA_GROUPS):
                hh = jnp.dot(xc_ref[kv * NSA_GROUPS + g], w1_ref[kv], preferred_element_type=jnp.float32)
                h1 = hh[:, :CMP_HIDDEN]
                h2 = pltpu.roll(hh[:, CMP_HIDDEN:], nch - 1, 0)
                x = h1 + h2 + c1_ref[kv]
                hid = 0.5 * x * (1.0 + lax.erf(x * (2.0 ** -0.5)))
                o = jnp.dot(hid.astype(jnp.bfloat16), w2_ref[kv], preferred_element_type=jnp.float32) + b2_ref[kv]
                out_ref[0, g] = o.astype(out_ref.dtype)


def compress_paged(pool, page_table, w1, c1, w2, b2):
    B, NP = page_table.shape
    assert NP % PPS == 0
    n_steps = NP // PPS
    nch = NP * CPP
    dh = NSA_HEAD_DIM
    r = jnp.arange(2 * PAGE_SIZE)
    s_, u_, c_ = r // (2 * CPP), (r // CPP) % 2, r % CPP
    perm = (r[None, :] == (u_ * PAGE_SIZE + c_ * CMP_STRIDE + s_)[:, None]).astype(jnp.bfloat16)

    def page_spec(u):
        return pl.BlockSpec((1, PAGE_SIZE, ROW_W), lambda b, s, pt: (pt[b, s * PPS + u], 0, 0))

    full = lambda a: pl.BlockSpec(a.shape, lambda b, s, pt: (0,) * a.ndim)
    out_spec = pl.BlockSpec((1, NSA_GROUPS, nch, dh), lambda b, s, pt: (b, 0, 0, 0))
    grid_spec = pltpu.PrefetchScalarGridSpec(
        num_scalar_prefetch=1,
        grid=(B, n_steps),
        in_specs=[page_spec(u) for u in range(PPS)] + [full(perm), full(w1), full(c1), full(w2), full(b2)],
        out_specs=[out_spec, out_spec],
        scratch_shapes=[pltpu.VMEM((KVG, nch, CMP_STRIDE * dh), jnp.bfloat16)],
    )
    out = jax.ShapeDtypeStruct((B, NSA_GROUPS, nch, dh), jnp.bfloat16)
    return pl.pallas_call(
        functools.partial(_compress_kernel, n_steps=n_steps),
        grid_spec=grid_spec,
        out_shape=[out, out],
        compiler_params=pltpu.CompilerParams(dimension_semantics=("parallel", "arbitrary"),
                                             vmem_limit_bytes=VMEM_LIMIT),
        name="nsa_compress",
    )(page_table, *([pool] * PPS), perm, w1, c1, w2, b2)


def compress_params(P):
    dh, F = NSA_HEAD_DIM, CMP_HIDDEN
    w1s, c1s, w2s, b2s = [], [], [], []
    for n in ('k', 'v'):
        w1h = P['cmp_%s_w1' % n].reshape(2, CMP_STRIDE * dh, F)
        pe = P['cmp_%s_pe' % n].reshape(2, CMP_STRIDE * dh)
        w1s.append(jnp.concatenate([w1h[0], w1h[1]], axis=1))
        c1s.append(P['cmp_%s_b1' % n] + jnp.einsum('hk,hkf->f', pe, w1h, precision=lax.Precision.HIGHEST))
        w2s.append(P['cmp_%s_w2' % n])
        b2s.append(P['cmp_%s_b2' % n])
    return (jnp.stack(w1s).astype(jnp.bfloat16), jnp.stack(c1s)[:, None, :],
            jnp.stack(w2s).astype(jnp.bfloat16), jnp.stack(b2s)[:, None, :])


def _softmax_rows(s, valid):
    s = jnp.where(valid, s, NEG)
    m = jnp.max(s, axis=-1, keepdims=True)
    e = jnp.where(valid, jnp.exp(s - m), 0.0)
    return e / jnp.maximum(jnp.sum(e, axis=-1, keepdims=True), 1e-30)


def _nsa_cmp_kernel(q_ref, kc_ref, vc_ref, bias_ref, ov_ref, oc_ref, imp_ref, *, nc):
    i = pl.program_id(2)
    ncp = kc_ref.shape[2]
    qpos = i * TQ + lax.broadcasted_iota(jnp.int32, (TQ, ncp), 0)
    col = lax.broadcasted_iota(jnp.int32, (TQ, ncp), 1)
    valid = (col * CMP_STRIDE + (CMP_BLOCK - 1) <= qpos) & (col < nc)
    kc = kc_ref[0, 0]
    vc = vc_ref[0, 0]
    psum = jnp.zeros((TQ, ncp), jnp.float32)
    for hh in range(NSA_HPG):
        s = lax.dot_general(q_ref[0, 0, hh], kc, (((1,), (1,)), ((), ())), preferred_element_type=jnp.float32)
        p = _softmax_rows(s + bias_ref[0, hh], valid)
        psum = psum + p
        oc_ref[0, :, hh * NSA_HEAD_DIM:(hh + 1) * NSA_HEAD_DIM] = jnp.dot(
            p.astype(jnp.bfloat16), vc, preferred_element_type=jnp.float32)
    imp_ref[0, 0] = jnp.dot(psum.astype(jnp.bfloat16), ov_ref[...], preferred_element_type=jnp.float32)


def nsa_cmp(qh, kc, vc, bias_c, overlap, nc):
    B, G, HPG, T, dh = qh.shape
    ncp = kc.shape[2]
    nsp = overlap.shape[1]
    return pl.pallas_call(
        functools.partial(_nsa_cmp_kernel, nc=nc),
        grid=(B, G, T // TQ),
        in_specs=[pl.BlockSpec((1, 1, HPG, TQ, dh), lambda b, g, i: (b, g, 0, i, 0)),
                  pl.BlockSpec((1, 1, ncp, dh), lambda b, g, i: (b, g, 0, 0)),
                  pl.BlockSpec((1, 1, ncp, dh), lambda b, g, i: (b, g, 0, 0)),
                  pl.BlockSpec((1, HPG, TQ, ncp), lambda b, g, i: (g, 0, i, 0)),
                  pl.BlockSpec((ncp, nsp), lambda b, g, i: (0, 0))],
        out_specs=[pl.BlockSpec((1, TQ, HPG * dh), lambda b, g, i: (b, i, g)),
                   pl.BlockSpec((1, 1, TQ, nsp), lambda b, g, i: (b, g, i, 0))],
        out_shape=[jax.ShapeDtypeStruct((B, T, G * HPG * dh), jnp.float32),
                   jax.ShapeDtypeStruct((B, G, T, nsp), jnp.float32)],
        compiler_params=pltpu.CompilerParams(dimension_semantics=("parallel", "parallel", "parallel"),
                                             vmem_limit_bytes=VMEM_LIMIT),
        name="nsa_cmp",
    )(qh, kc, vc, bias_c, overlap)


def _flash_tile(q_all, k, v, bias_ref, dd_b, valid, m_ref, l_ref, acc_ref):
    s_all = lax.dot_general(q_all, k, (((1,), (1,)), ((), ())), preferred_element_type=jnp.float32)
    ps = []
    for hh in range(NSA_HPG):
        rows = slice(hh * TQ, (hh + 1) * TQ)
        s = jnp.where(valid, s_all[rows] + bias_ref[0, dd_b, hh], NEG)
        m_old = m_ref[rows]
        m_new = jnp.maximum(m_old, jnp.max(s, axis=-1, keepdims=True))
        p = jnp.where(valid, jnp.exp(s - m_new), 0.0)
        alpha = jnp.exp(m_old - m_new)
        l_ref[rows] = alpha * l_ref[rows] + jnp.sum(p, axis=-1, keepdims=True)
        acc_ref[rows] = alpha * acc_ref[rows]
        m_ref[rows] = m_new
        ps.append(p.astype(jnp.bfloat16))
    acc_ref[...] += jnp.dot(jnp.concatenate(ps, axis=0), v, preferred_element_type=jnp.float32)


def _nsa_sw_kernel(q_ref, ks_ref, vs_ref, kw_ref, vw_ref, msel_ref, exp_ref, bias_ref, selm_ref, winm_ref,
                   os_ref, ow_ref, msk_ref, m_ref, l_ref, acc_ref, *, nt):
    i = pl.program_id(2)
    q_all = q_ref[0, 0].reshape(NSA_HPG * TQ, NSA_HEAD_DIM)
    msel = msel_ref[0, 0].astype(jnp.bfloat16)
    for j in range(nt):
        msk_ref[j] = jnp.dot(msel, exp_ref[:, j * TQ:(j + 1) * TQ], preferred_element_type=jnp.float32)

    def reset():
        m_ref[...] = jnp.full_like(m_ref, NEG)
        l_ref[...] = jnp.zeros_like(l_ref)
        acc_ref[...] = jnp.zeros_like(acc_ref)

    def finish(o_ref):
        o = acc_ref[...] / jnp.maximum(l_ref[...], 1e-30)
        for hh in range(NSA_HPG):
            o_ref[0, :, hh * NSA_HEAD_DIM:(hh + 1) * NSA_HEAD_DIM] = o[hh * TQ:(hh + 1) * TQ]

    reset()

    def sel_body(j, c):
        dd = jnp.minimum(i - j, 2)
        valid = (msk_ref[j] * selm_ref[dd]) > 0.5
        _flash_tile(q_all, ks_ref[0, 0, j], vs_ref[0, 0, j], bias_ref, dd, valid, m_ref, l_ref, acc_ref)
        return c

    lax.fori_loop(0, i + 1, sel_body, 0)
    finish(os_ref)

    reset()

    def win_body(j, c):
        dd = i - j
        valid = winm_ref[dd] > 0.5
        _flash_tile(q_all, kw_ref[0, 0, j], vw_ref[0, 0, j], bias_ref, jnp.minimum(dd, 2), valid, m_ref, l_ref, acc_ref)
        return c

    lax.fori_loop(jnp.maximum(i - WINDOW // TQ, 0), i + 1, win_body, 0)
    finish(ow_ref)


def nsa_sel_win(qh, ks, vs, kw, vw, msel, expander, bias_t, selm, winm):
    B, G, HPG, T, dh = qh.shape
    nt = T // TQ
    nsp = msel.shape[-1]
    kv_spec = pl.BlockSpec((1, 1, nt, TQ, dh), lambda b, g, i: (b, g, 0, 0, 0))
    out_spec = pl.BlockSpec((1, TQ, HPG * dh), lambda b, g, i: (b, i, g))
    return pl.pallas_call(
        functools.partial(_nsa_sw_kernel, nt=nt),
        grid=(B, G, T // TQ),
        in_specs=[pl.BlockSpec((1, 1, HPG, TQ, dh), lambda b, g, i: (b, g, 0, i, 0)),
                  kv_spec, kv_spec, kv_spec, kv_spec,
                  pl.BlockSpec((1, 1, TQ, nsp), lambda b, g, i: (b, g, i, 0)),
                  pl.BlockSpec((nsp, T), lambda b, g, i: (0, 0)),
                  pl.BlockSpec((1, 3, HPG, TQ, TQ), lambda b, g, i: (g, 0, 0, 0, 0)),
                  pl.BlockSpec((3, TQ, TQ), lambda b, g, i: (0, 0, 0)),
                  pl.BlockSpec((WINDOW // TQ + 1, TQ, TQ), lambda b, g, i: (0, 0, 0))],
        out_specs=[out_spec, out_spec],
        out_shape=[jax.ShapeDtypeStruct((B, T, G * HPG * dh), jnp.float32)] * 2,
        scratch_shapes=[pltpu.VMEM((nt, TQ, TQ), jnp.float32),
                        pltpu.VMEM((HPG * TQ, 1), jnp.float32),
                        pltpu.VMEM((HPG * TQ, 1), jnp.float32),
                        pltpu.VMEM((HPG * TQ, dh), jnp.float32)],
        compiler_params=pltpu.CompilerParams(dimension_semantics=("parallel", "parallel", "arbitrary"),
                                             vmem_limit_bytes=VMEM_LIMIT),
        name="nsa_sel_win",
    )(qh, ks, vs, kw, vw, msel, expander, bias_t, selm, winm)


def nsa_prompt(q, gates, kc, vc, ks, vs, kw_rows, vw_rows, table):
    B, T = q.shape[0], q.shape[1]
    G, HPG, dh = NSA_GROUPS, NSA_HPG, NSA_HEAD_DIM
    Nc = T // CMP_STRIDE - 1
    Ns = -(-T // SEL_BLOCK)
    ncp = kc.shape[2]
    assert ncp % LANE == 0 and ncp >= Nc
    nsp = -(-Ns // LANE) * LANE
    bf = jnp.bfloat16
    qh = jnp.transpose(q.reshape(B, T, G, HPG, dh), (0, 2, 3, 1, 4)).astype(bf)

    qpos = jnp.arange(T)
    cend = jnp.arange(ncp) * CMP_STRIDE + (CMP_BLOCK - 1)
    bias_c = head_bias(table, qpos[:, None] - cend[None, :])
    cstart = jnp.arange(ncp)[:, None] * CMP_STRIDE
    sstart = jnp.arange(nsp)[None, :] * SEL_BLOCK
    overlap = ((cstart < sstart + SEL_BLOCK) & (cstart + CMP_BLOCK > sstart)
               & (jnp.arange(ncp)[:, None] < Nc) & (jnp.arange(nsp)[None, :] < Ns)).astype(bf)
    o_c, imp = nsa_cmp(qh, kc, vc, bias_c, overlap, Nc)
    imp = imp[..., :Ns]
    blk = jnp.arange(Ns)[None, :]
    cur = qpos[:, None] // SEL_BLOCK
    valid = blk * SEL_BLOCK <= qpos[:, None]
    forced = (blk == 0) | (blk == cur) | (blk == cur - 1)
    score = jnp.where(valid, jnp.where(forced, BIG, imp), -BIG)
    n_pick = min(N_SEL, Ns)
    top_s, idx = lax.top_k(score, n_pick)
    ok = top_s > -0.5 * BIG
    msel = jnp.sum(jax.nn.one_hot(idx, nsp, dtype=jnp.float32) * ok[..., None], axis=-2)
    expander = (jnp.arange(nsp)[:, None] == (jnp.arange(T)[None, :] // SEL_BLOCK)).astype(bf)
    r = jnp.arange(TQ)
    rel3 = (jnp.arange(3) * TQ)[:, None, None] + r[None, :, None] - r[None, None, :]
    bias_t = jnp.transpose(table[t5_bucket(rel3)].astype(jnp.float32), (3, 0, 1, 2)).reshape(G, HPG, 3, TQ, TQ)
    bias_t = jnp.transpose(bias_t, (0, 2, 1, 3, 4))
    selm = (rel3 >= 0).astype(jnp.float32)
    nw = WINDOW // TQ + 1
    relw = (jnp.arange(nw) * TQ)[:, None, None] + r[None, :, None] - r[None, None, :]
    winm = ((relw >= 0) & (relw < WINDOW)).astype(jnp.float32)

    def tiles(t):
        return jnp.transpose(t, (0, 2, 1, 3)).astype(bf).reshape(B, G, T // TQ, TQ, dh)

    o_s, o_w = nsa_sel_win(qh, tiles(ks), tiles(vs), tiles(kw_rows), tiles(vw_rows), msel, expander, bias_t, selm, winm)
    gx = jnp.repeat(gates.reshape(B, T, G * HPG, 3), dh, axis=2)
    return gx[..., 0] * o_c + gx[..., 1] * o_s + gx[..., 2] * o_w


def _dec_cmp_kernel(q_ref, kc_ref, vc_ref, bias_ref, valid_ref, ov_ref, oc_ref, imp_ref, *, t):
    valid = valid_ref[...] > 0.5
    for g in range(NSA_GROUPS):
        s = lax.dot_general(q_ref[0, g], kc_ref[0, g], (((1,), (1,)), ((), ())), preferred_element_type=jnp.float32)
        p = _softmax_rows(s + bias_ref[g], valid)
        oc_ref[0, g] = jnp.dot(p.astype(jnp.bfloat16), vc_ref[0, g], preferred_element_type=jnp.float32)
        psum = p[0:t]
        for hh in range(1, NSA_HPG):
            psum = psum + p[hh * t:(hh + 1) * t]
        imp_ref[0, g] = jnp.dot(psum.astype(jnp.bfloat16), ov_ref[...], preferred_element_type=jnp.float32)


def dec_cmp(qg, kc, vc, bias_c, valid_c, overlap, t):
    B, G, R_, dh = qg.shape
    ncp, nsp = overlap.shape
    per_b = lambda a: pl.BlockSpec((1,) + a.shape[1:], lambda b: (b,) + (0,) * (a.ndim - 1))
    full = lambda a: pl.BlockSpec(a.shape, lambda b: (0,) * a.ndim)
    return pl.pallas_call(
        functools.partial(_dec_cmp_kernel, t=t),
        grid=(B,),
        in_specs=[per_b(qg), per_b(kc), per_b(vc), full(bias_c), full(valid_c), full(overlap)],
        out_specs=[pl.BlockSpec((1, G, R_, dh), lambda b: (b, 0, 0, 0)),
                   pl.BlockSpec((1, G, t, nsp), lambda b: (b, 0, 0, 0))],
        out_shape=[jax.ShapeDtypeStruct((B, G, R_, dh), jnp.float32),
                   jax.ShapeDtypeStruct((B, G, t, nsp), jnp.float32)],
        compiler_params=pltpu.CompilerParams(dimension_semantics=("parallel",), vmem_limit_bytes=VMEM_LIMIT),
        name="nsa_dec_cmp",
    )(qg, kc, vc, bias_c, valid_c, overlap)


def _flash_update(s, valid, v, m_ref, l_ref, acc_ref, g):
    s = jnp.where(valid, s, NEG)
    m_old = m_ref[g]
    m_new = jnp.maximum(m_old, jnp.max(s, axis=-1, keepdims=True))
    p = jnp.where(valid, jnp.exp(s - m_new), 0.0)
    alpha = jnp.exp(m_old - m_new)
    l_ref[g] = alpha * l_ref[g] + jnp.sum(p, axis=-1, keepdims=True)
    acc_ref[g] = alpha * acc_ref[g] + jnp.dot(p.astype(jnp.bfloat16), v, preferred_element_type=jnp.float32)
    m_ref[g] = m_new


def _dec_sel_kernel(pt_ref, q_ref, *refs, n_steps, t):
    pages = refs[:PPS]
    new_ref, bias_ref, mask_ref, biasn_ref, maskn_ref, o_ref, m_ref, l_ref, acc_ref = refs[PPS:]
    s_idx = pl.program_id(1)
    dh = NSA_HEAD_DIM

    @pl.when(s_idx == 0)
    def _():
        m_ref[...] = jnp.full_like(m_ref, NEG)
        l_ref[...] = jnp.zeros_like(l_ref)
        acc_ref[...] = jnp.zeros_like(acc_ref)

    def tile_mask(mref, g):
        mk = mref[0, g] > 0.5
        return jnp.concatenate([mk] * NSA_HPG, axis=0)

    for g in range(NSA_GROUPS):
        k = jnp.concatenate([pg[0, :, pl.ds(g * dh, dh)] for pg in pages], axis=0).astype(jnp.bfloat16)
        v = jnp.concatenate([pg[0, :, pl.ds((NSA_GROUPS + g) * dh, dh)] for pg in pages], axis=0).astype(jnp.bfloat16)
        s = lax.dot_general(q_ref[0, g], k, (((1,), (1,)), ((), ())), preferred_element_type=jnp.float32)
        _flash_update(s + bias_ref[g], tile_mask(mask_ref, g), v, m_ref, l_ref, acc_ref, g)

    @pl.when(s_idx == n_steps - 1)
    def _():
        for g in range(NSA_GROUPS):
            k = new_ref[0, :, pl.ds(g * dh, dh)].astype(jnp.bfloat16)
            v = new_ref[0, :, pl.ds((NSA_GROUPS + g) * dh, dh)].astype(jnp.bfloat16)
            s = lax.dot_general(q_ref[0, g], k, (((1,), (1,)), ((), ())), preferred_element_type=jnp.float32)
            _flash_update(s + biasn_ref[g], tile_mask(maskn_ref, g), v, m_ref, l_ref, acc_ref, g)
            o_ref[0, g] = acc_ref[g] / jnp.maximum(l_ref[g], 1e-30)


def dec_sel(qg, pool, page_table, new_rows, bias_p, mask_p, bias_n, mask_n, t):
    B, G, R_, dh = qg.shape
    NP = page_table.shape[1]
    assert NP % PPS == 0
    n_steps = NP // PPS
    W = PPS * PAGE_SIZE

    def page_spec(u):
        return pl.BlockSpec((1, PAGE_SIZE, ROW_W), lambda b, s, pt: (pt[b, s * PPS + u], 0, 0))

    grid_spec = pltpu.PrefetchScalarGridSpec(
        num_scalar_prefetch=1,
        grid=(B, n_steps),
        in_specs=[pl.BlockSpec((1, G, R_, dh), lambda b, s, pt: (b, 0, 0, 0))]
        + [page_spec(u) for u in range(PPS)]
        + [pl.BlockSpec((1, PAGE_SIZE, ROW_W), lambda b, s, pt: (b, 0, 0)),
           pl.BlockSpec((G, R_, W), lambda b, s, pt: (0, 0, s)),
           pl.BlockSpec((1, G, t, W), lambda b, s, pt: (b, 0, 0, s)),
           pl.BlockSpec((G, R_, PAGE_SIZE), lambda b, s, pt: (0, 0, 0)),
           pl.BlockSpec((1, G, t, PAGE_SIZE), lambda b, s, pt: (b, 0, 0, 0))],
        out_specs=pl.BlockSpec((1, G, R_, dh), lambda b, s, pt: (b, 0, 0, 0)),
        scratch_shapes=[pltpu.VMEM((G, R_, 1), jnp.float32), pltpu.VMEM((G, R_, 1), jnp.float32),
                        pltpu.VMEM((G, R_, dh), jnp.float32)],
    )
    return pl.pallas_call(
        functools.partial(_dec_sel_kernel, n_steps=n_steps, t=t),
        grid_spec=grid_spec,
        out_shape=jax.ShapeDtypeStruct((B, G, R_, dh), jnp.float32),
        compiler_params=pltpu.CompilerParams(dimension_semantics=("parallel", "arbitrary"),
                                             vmem_limit_bytes=VMEM_LIMIT),
        name="nsa_dec_sel",
    )(page_table, qg, *([pool] * PPS), new_rows, bias_p, mask_p, bias_n, mask_n)


def _dec_win_kernel(q_ref, win_ref, new_ref, bias_ref, valid_ref, o_ref):
    dh = NSA_HEAD_DIM
    valid = valid_ref[...] > 0.5
    for g in range(NSA_GROUPS):
        k = jnp.concatenate([win_ref[0, :, pl.ds(g * dh, dh)], new_ref[0, :, pl.ds(g * dh, dh)]], axis=0)
        v = jnp.concatenate([win_ref[0, :, pl.ds((NSA_GROUPS + g) * dh, dh)],
                             new_ref[0, :, pl.ds((NSA_GROUPS + g) * dh, dh)]], axis=0)
        s = lax.dot_general(q_ref[0, g], k.astype(jnp.bfloat16), (((1,), (1,)), ((), ())),
                            preferred_element_type=jnp.float32)
        p = _softmax_rows(s + bias_ref[g], valid)
        o_ref[0, g] = jnp.dot(p.astype(jnp.bfloat16), v.astype(jnp.bfloat16), preferred_element_type=jnp.float32)


def dec_win(qg, win_rows, new_rows, bias_w, valid_w):
    B, G, R_, dh = qg.shape
    per_b = lambda a: pl.BlockSpec((1,) + a.shape[1:], lambda b: (b,) + (0,) * (a.ndim - 1))
    full = lambda a: pl.BlockSpec(a.shape, lambda b: (0,) * a.ndim)
    return pl.pallas_call(
        _dec_win_kernel,
        grid=(B,),
        in_specs=[per_b(qg), per_b(win_rows), per_b(new_rows), full(bias_w), full(valid_w)],
        out_specs=pl.BlockSpec((1, G, R_, dh), lambda b: (b, 0, 0, 0)),
        out_shape=jax.ShapeDtypeStruct((B, G, R_, dh), jnp.float32),
        compiler_params=pltpu.CompilerParams(dimension_semantics=("parallel",), vmem_limit_bytes=VMEM_LIMIT),
        name="nsa_dec_win",
    )(qg, win_rows, new_rows, bias_w, valid_w)


def nsa_decode(q, gates, kc, vc, slc_pool, page_table, slc_new, win_buf, win_new_rows, table):
    B, t = q.shape[0], q.shape[1]
    G, HPG, dh = NSA_GROUPS, NSA_HPG, NSA_HEAD_DIM
    NP = page_table.shape[1]
    past = NP * PAGE_SIZE
    Wn = win_buf.shape[1]
    assert t <= PAGE_SIZE and Wn == WINDOW
    L = past + t
    nc = L // CMP_STRIDE - 1
    ncp = kc.shape[2]
    Ns = -(-L // SEL_BLOCK)
    nsp = -(-Ns // LANE) * LANE
    bf = jnp.bfloat16
    R_ = HPG * t
    qg = jnp.transpose(q.reshape(B, t, G, HPG, dh), (0, 2, 3, 1, 4)).reshape(B, G, R_, dh).astype(bf)
    qpos = past + jnp.arange(t)

    r_lo, r_hi = -(PAGE_SIZE + WINDOW), past + t
    lut_rev = table[t5_bucket(jnp.arange(r_hi, r_lo - 1, -1))].astype(jnp.float32)

    def rows_bias(p0, step, n):
        rows = []
        for i in range(t):
            j0 = r_hi - (past + i - p0)
            assert j0 >= 0 and j0 + step * (n - 1) < lut_rev.shape[0]
            rows.append(lax.slice(lut_rev, (j0, 0), (j0 + step * (n - 1) + 1, NSA_HEADS), (step, 1)))
        b = jnp.transpose(jnp.stack(rows), (2, 0, 1))
        return b.reshape(G, R_, n)

    rep = lambda m: jnp.tile(m, (HPG, 1))
    cidx = jnp.arange(ncp)
    cend = cidx * CMP_STRIDE + (CMP_BLOCK - 1)
    rel_c = qpos[:, None] - cend[None, :]
    valid_c = rep(((rel_c >= 0) & (cidx[None, :] < nc)).astype(jnp.float32))
    cstart = cidx[:, None] * CMP_STRIDE
    sstart = jnp.arange(nsp)[None, :] * SEL_BLOCK
    overlap = ((cstart < sstart + SEL_BLOCK) & (cstart + CMP_BLOCK > sstart)
               & (cidx[:, None] < nc) & (jnp.arange(nsp)[None, :] < Ns)).astype(bf)
    o_c, imp = dec_cmp(qg, kc, vc, rows_bias(CMP_BLOCK - 1, CMP_STRIDE, ncp), valid_c, overlap, t)
    imp = imp[..., :Ns]
    blk = jnp.arange(Ns)[None, :]
    cur = qpos[:, None] // SEL_BLOCK
    valid = blk * SEL_BLOCK <= qpos[:, None]
    forced = (blk == 0) | (blk == cur) | (blk == cur - 1)
    score = jnp.where(valid, jnp.where(forced, BIG, imp), -BIG)
    n_pick = min(N_SEL, Ns)
    top_s, idx = lax.top_k(score, n_pick)
    ok = top_s > -0.5 * BIG
    msel = jnp.sum(jax.nn.one_hot(idx, Ns, dtype=jnp.float32) * ok[..., None], axis=-2)
    kpos = jnp.arange(past + PAGE_SIZE)
    nblk = -(-(past + PAGE_SIZE) // SEL_BLOCK)
    mkey = jnp.repeat(jnp.pad(msel, ((0, 0), (0, 0), (0, 0), (0, nblk - Ns))), SEL_BLOCK, axis=-1)
    mkey = mkey[..., :past + PAGE_SIZE]
    mkey = mkey * ((kpos[None, :] <= qpos[:, None]) & (kpos[None, :] < L)).astype(jnp.float32)
    bias_s = rows_bias(0, 1, past + PAGE_SIZE)
    pad_new = lambda r: jnp.pad(r, ((0, 0), (0, PAGE_SIZE - t), (0, 0)))
    o_s = dec_sel(qg, slc_pool, page_table, pad_new(slc_new), bias_s[..., :past], mkey[..., :past],
                  bias_s[..., past:], mkey[..., past:], t)
    wpos = jnp.concatenate([past - Wn + jnp.arange(Wn), past + jnp.arange(PAGE_SIZE)])
    rel_w = qpos[:, None] - wpos[None, :]
    valid_w = rep(((rel_w >= 0) & (rel_w < WINDOW) & (wpos[None, :] >= 0) & (wpos[None, :] < L)).astype(jnp.float32))
    bias_w = jnp.concatenate([rows_bias(past - Wn, 1, Wn), rows_bias(past, 1, PAGE_SIZE)], axis=-1)
    o_w = dec_win(qg, win_buf, pad_new(win_new_rows), bias_w, valid_w)
    back = lambda o: jnp.transpose(o.reshape(B, G, HPG, t, dh), (0, 3, 1, 2, 4)).reshape(B, t, G * HPG * dh)
    gx = jnp.repeat(gates.reshape(B, t, G * HPG, 3), dh, axis=2)
    return gx[..., 0] * back(o_c) + gx[..., 1] * back(o_s) + gx[..., 2] * back(o_w)


def kv_rows(h, P):
    B, T, _ = h.shape
    return mm(rmsnorm(h, P['norm_kv']), P['kv_w']).reshape(B, T, 3, ROW_W)


def run_trunk(x, gla_s0, past, P):
    B, T, _ = x.shape
    h = x
    gla_states = []
    for layer in range(DEPTH):
        xn = rmsnorm(h, P['norm_mix'][layer])
        if layer < N_A_LAYERS:
            o, s = gla_mixer(xn, gla_s0[layer], P['gla_w_in'][layer], P['gla_w_gate_up'][layer],
                             P['gla_b_gate'][layer], P['gla_norm'][layer], P['gla_w_out'][layer])
            gla_states.append(s)
        else:
            j = layer - N_A_LAYERS
            qd = NSA_HEADS * NSA_HEAD_DIM
            proj = mm(xn, P['nsa_w_in'][j])
            q = proj[..., :qd].reshape(B, T, NSA_HEADS, NSA_HEAD_DIM) * NSA_HEAD_DIM ** -0.5
            gates = jax.nn.sigmoid(proj[..., qd:].astype(jnp.float32)).reshape(B, T, NSA_HEADS, 3).astype(x.dtype)
            if past is None:
                r5 = rows.reshape(B, T, 3, 2, NSA_GROUPS, NSA_HEAD_DIM)
                att = nsa_prompt(q, gates, kc, vc, r5[:, :, 1, 0], r5[:, :, 1, 1], r5[:, :, 2, 0], r5[:, :, 2, 1],
                                 P['rel_bias'])
            else:
                att = nsa_decode(q, gates, kc, vc, past['slc_pool'], past['page_table'], rows[:, :, 1],
                                 past['win_buf'], rows[:, :, 2], P['rel_bias'])
            o = mm(att, P['nsa_w_out'][j])
        h = h + o
        h = peer_ffn(h, rmsnorm(h, P['norm_ffn'][layer]), P['peer_w_q'][layer], P['peer_subkeys'][layer],
                     P['peer_u_bf'][layer], P['peer_v_bf'][layer])
        if layer == N_A_LAYERS - 1:
            rows = kv_rows(h, P)
            if past is None:
                assert T % (PAGE_SIZE * PPS) == 0 and T % TQ == 0
                npg = T // PAGE_SIZE
                pool = rows[:, :, 0].reshape(B * npg, PAGE_SIZE, ROW_W)
                table = jnp.arange(B * npg, dtype=jnp.int32).reshape(B, npg)
                win_new = rows[:, T - min(WINDOW, T):, 2]
            else:
                past_len = past['page_table'].shape[1] * PAGE_SIZE
                assert (past_len + T) // CMP_STRIDE == past_len // CMP_STRIDE
                pool, table = past['cmp_pool'], past['page_table']
                win_all = jnp.concatenate([past['win_buf'], rows[:, :, 2]], axis=1)
                win_new = win_all[:, win_all.shape[1] - min(WINDOW, win_all.shape[1]):]
            kc, vc = compress_paged(pool, table, *P['cmp_stack'])
    kv5 = lambda r: r.reshape(r.shape[0], r.shape[1], 2, NSA_GROUPS, NSA_HEAD_DIM)
    return (rmsnorm(h, P['norm_final']), jnp.stack(gla_states), kv5(rows[:, :, 0]), kv5(rows[:, :, 1]), kv5(win_new))


def kernel(x_prompt, x_sample, state_gla, cache_cmp_kv, cache_slc_kv, cache_win_kv, page_table,
           norm_mix, norm_ffn, norm_kv, norm_final, gla_w_in, gla_w_gate_up, gla_b_gate, gla_norm, gla_w_out,
           kv_w, cmp_k_w1, cmp_k_b1, cmp_k_w2, cmp_k_b2, cmp_k_pe, cmp_v_w1, cmp_v_b1, cmp_v_w2, cmp_v_b2, cmp_v_pe,
           nsa_w_in, nsa_w_out, rel_bias, peer_w_q, peer_subkeys, peer_u, peer_v):
    P = dict(norm_mix=norm_mix, norm_ffn=norm_ffn, norm_kv=norm_kv, norm_final=norm_final,
             gla_w_in=gla_w_in, gla_w_gate_up=gla_w_gate_up, gla_b_gate=gla_b_gate, gla_norm=gla_norm,
             gla_w_out=gla_w_out, kv_w=kv_w,
             cmp_k_w1=cmp_k_w1, cmp_k_b1=cmp_k_b1, cmp_k_w2=cmp_k_w2, cmp_k_b2=cmp_k_b2, cmp_k_pe=cmp_k_pe,
             cmp_v_w1=cmp_v_w1, cmp_v_b1=cmp_v_b1, cmp_v_w2=cmp_v_w2, cmp_v_b2=cmp_v_b2, cmp_v_pe=cmp_v_pe,
             nsa_w_in=nsa_w_in, nsa_w_out=nsa_w_out, rel_bias=rel_bias,
             peer_w_q=peer_w_q, peer_subkeys=peer_subkeys,
             peer_u_bf=peer_u.astype(jnp.bfloat16), peer_v_bf=peer_v.astype(jnp.bfloat16))
    P['cmp_stack'] = compress_params(P)
    n_pool = cache_cmp_kv.shape[0]
    past = dict(cmp_pool=cache_cmp_kv.reshape(n_pool, PAGE_SIZE, ROW_W),
                slc_pool=cache_slc_kv.reshape(n_pool, PAGE_SIZE, ROW_W),
                win_buf=cache_win_kv.reshape(cache_win_kv.shape[0], cache_win_kv.shape[1], ROW_W),
                page_table=page_table.astype(jnp.int32))
    gla_zero = jnp.zeros((N_A_LAYERS, BATCH, GLA_HEADS, GLA_DK, GLA_DV), x_prompt.dtype)
    y_prompt, gla_p, cmp_p, slc_p, win_p = run_trunk(x_prompt, gla_zero, None, P)
    y_sample, gla_s, cmp_s, slc_s, win_s = run_trunk(x_sample, state_gla, past, P)
    return (y_prompt, y_sample, gla_p, gla_s, cmp_p, cmp_s, slc_p, slc_s, win_p, win_s)
```

```python
import functools
import math

import jax
import jax.numpy as jnp
from jax import lax
from jax.experimental import pallas as pl
from jax.experimental.pallas import tpu as pltpu

D_MODEL = 2048
BATCH = 4
SEQ = 2048
DEPTH = 2
DEC_BATCH = 32
DEC_SEQ = 8
PAST_LEN = 8192
PAGE_SIZE = 128
N_A_LAYERS = DEPTH // 2
N_B_LAYERS = DEPTH - N_A_LAYERS
GLA_HEADS = 4
GLA_DK = D_MODEL // (2 * GLA_HEADS)
GLA_DV = D_MODEL // GLA_HEADS
GLA_GATE_RANK = 16
GLA_TAU = 16.0
GLA_CHUNK = 64
NSA_HEADS = 16
NSA_GROUPS = 4
NSA_HPG = NSA_HEADS // NSA_GROUPS
NSA_HEAD_DIM = D_MODEL // NSA_HEADS
CMP_STRIDE = 16
CMP_BLOCK = 2 * CMP_STRIDE
CMP_HIDDEN = 2 * NSA_HEAD_DIM
SEL_BLOCK = 64
N_SEL = 16
WINDOW = 512
Q_BLOCK = 32
REL_BUCKETS = 32
REL_MAX_DIST = 128
PEER_HEADS = 8
PEER_NKEYS = 128
PEER_EXPERTS = PEER_NKEYS * PEER_NKEYS
PEER_TOPK = 16
PEER_KEY_DIM = 256
PEER_TOK_BLOCK = 128
EPS = 1e-6
NEG = -1e30
BIG = 1e30

LANE = 128
VMEM_LIMIT = 56 * 1024 * 1024
TQ = 128
assert REL_MAX_DIST <= TQ and WINDOW % TQ == 0 and TQ % SEL_BLOCK == 0


def _mm_kernel(a_ref, b_ref, o_ref):
    a = a_ref[...].astype(jnp.bfloat16)
    b = b_ref[...].astype(jnp.bfloat16)
    o_ref[...] = jnp.dot(a, b, preferred_element_type=jnp.float32)


def pmm(a, b, tm=512, tn=512):
    M, K = a.shape
    N = b.shape[1]
    tm = min(tm, M)
    Mp = -(-M // tm) * tm
    Np = -(-N // LANE) * LANE
    tn = math.gcd(Np, tn)
    if Mp != M:
        a = jnp.pad(a, ((0, Mp - M), (0, 0)))
    if Np != N:
        b = jnp.pad(b, ((0, 0), (0, Np - N)))
    out = pl.pallas_call(
        _mm_kernel,
        grid=(Mp // tm, Np // tn),
        in_specs=[pl.BlockSpec((tm, K), lambda i, j: (i, 0)),
                  pl.BlockSpec((K, tn), lambda i, j: (0, j))],
        out_specs=pl.BlockSpec((tm, tn), lambda i, j: (i, j)),
        out_shape=jax.ShapeDtypeStruct((Mp, Np), jnp.float32),
        compiler_params=pltpu.CompilerParams(
            dimension_semantics=("parallel", "parallel"), vmem_limit_bytes=VMEM_LIMIT),
        name="proj_matmul",
    )(a, b)
    return out[:M, :N]


def mm(x, w):
    lead = x.shape[:-1]
    return pmm(x.reshape(-1, x.shape[-1]), w).reshape(lead + (w.shape[1],))


def rmsnorm(x, g):
    xf = x.astype(jnp.float32)
    y = xf * lax.rsqrt(jnp.mean(xf * xf, axis=-1, keepdims=True) + EPS)
    return (y * g.astype(jnp.float32)).astype(x.dtype)


def masked_softmax(s, mask):
    s = jnp.where(mask, s, NEG)
    m = jnp.max(s, axis=-1, keepdims=True)
    e = jnp.where(mask, jnp.exp(s - m), 0.0)
    return e / jnp.maximum(jnp.sum(e, axis=-1, keepdims=True), 1e-30)


def t5_bucket(rel):
    n = jnp.maximum(rel, 0)
    exact = REL_BUCKETS // 2
    nf = jnp.maximum(n, exact).astype(jnp.float32)
    large = exact + (jnp.log(nf / exact) / math.log(REL_MAX_DIST / exact) * (REL_BUCKETS - exact)).astype(jnp.int32)
    large = jnp.minimum(large, REL_BUCKETS - 1)
    return jnp.where(n < exact, n, large)


def head_bias(table, rel):
    onehot = jax.nn.one_hot(t5_bucket(rel), REL_BUCKETS, dtype=jnp.float32)
    b = jnp.einsum('qkn,nh->hqk', onehot, table.astype(jnp.float32), precision=lax.Precision.HIGHEST)
    return b.reshape(NSA_GROUPS, NSA_HPG, rel.shape[0], rel.shape[1])


def gla_chunked(q, k, v, g, s0):
    B, H, T, DK = q.shape
    C = GLA_CHUNK if T % GLA_CHUNK == 0 else T
    n = T // C

    def chunks(t):
        return jnp.moveaxis(t.reshape(B, H, n, C, t.shape[-1]), 2, 0)

    causal = jnp.tril(jnp.ones((C, C), dtype=bool))

    def step(s, inp):
        qc, kc, vc, gc = inp
        b = jnp.cumsum(gc, axis=2)
        b_last = b[:, :, -1:, :]
        qe = qc * jnp.exp(b)
        a = jnp.einsum('bhid,bhjd->bhij', qe, kc * jnp.exp(-b))
        a = jnp.where(causal, a, 0.0)
        o = jnp.einsum('bhij,bhje->bhie', a, vc) + jnp.einsum('bhid,bhde->bhie', qe, s)
        s = s * jnp.exp(b_last[:, :, 0, :, None]) + jnp.einsum('bhjd,bhje->bhde', kc * jnp.exp(b_last - b), vc)
        return s, o

    s_fin, o = lax.scan(step, s0, (chunks(q), chunks(k), chunks(v), chunks(g)))
    o = jnp.moveaxis(o, 0, 2).reshape(B, H, T, v.shape[-1])
    return o, s_fin


def gla_mixer(xn, s0, w_in, w_gate_up, b_gate, norm_g, w_out):
    B, T, _ = xn.shape
    dk_all = GLA_HEADS * GLA_DK
    dv_all = GLA_HEADS * GLA_DV
    proj = mm(xn, w_in)
    q, k, v, r, gz = jnp.split(proj, [dk_all, 2 * dk_all, 2 * dk_all + dv_all, 2 * dk_all + 2 * dv_all], axis=-1)
    glog = jax.nn.log_sigmoid((gz @ w_gate_up + b_gate).astype(jnp.float32)) / GLA_TAU

    def heads(t):
        return jnp.transpose(t.reshape(B, T, GLA_HEADS, -1), (0, 2, 1, 3)).astype(jnp.float32)

    o, s_new = gla_chunked(heads(q) * GLA_DK ** -0.5, heads(k), heads(v), heads(glog), s0.astype(jnp.float32))
    o = o * lax.rsqrt(jnp.mean(o * o, axis=-1, keepdims=True) + EPS) * norm_g.astype(jnp.float32)
    o = jnp.transpose(o, (0, 2, 1, 3)).reshape(B, T, dv_all).astype(xn.dtype)
    o = o * jax.nn.silu(r)
    return mm(o, w_out), s_new.astype(s0.dtype)


WSUM_UNROLL = 8
PEER_CHUNK = 512


def _wsum_kernel(a_ref, b_ref, g_ref, o_ref, *, tb):
    sub = lax.broadcasted_iota(jnp.int32, (PEER_NKEYS, PEER_NKEYS), 0)

    def body(tt, c):
        t0 = pl.multiple_of(tt * WSUM_UNROLL, WSUM_UNROLL)
        a8 = a_ref[pl.ds(t0, WSUM_UNROLL), :]
        b8 = b_ref[pl.ds(t0, WSUM_UNROLL), :]
        g8 = g_ref[pl.ds(t0, WSUM_UNROLL), :]
        for u in range(WSUM_UNROLL):
            at = jnp.where(sub == a8[u:u + 1], 1.0, 0.0).astype(jnp.bfloat16)
            bt = jnp.where(sub == b8[u:u + 1], g8[u:u + 1], 0.0).astype(jnp.bfloat16)
            o_ref[t0 + u] = lax.dot_general(at, bt, (((1,), (1,)), ((), ())), preferred_element_type=jnp.float32)
        return c

    lax.fori_loop(0, tb // WSUM_UNROLL, body, 0)


def peer_wsum(i1, i2, gate, tb=64):
    n, K = i1.shape
    tb = math.gcd(n, tb)
    assert tb % WSUM_UNROLL == 0
    return pl.pallas_call(
        functools.partial(_wsum_kernel, tb=tb),
        grid=(n // tb,),
        in_specs=[pl.BlockSpec((tb, K), lambda i: (i, 0))] * 3,
        out_specs=pl.BlockSpec((tb, PEER_NKEYS, PEER_NKEYS), lambda i: (i, 0, 0)),
        out_shape=jax.ShapeDtypeStruct((n, PEER_NKEYS, PEER_NKEYS), jnp.float32),
        compiler_params=pltpu.CompilerParams(dimension_semantics=("parallel",), vmem_limit_bytes=VMEM_LIMIT),
        name="peer_wsum",
    )(i1, i2, gate)


def _peer_kernel(x_ref, u_ref, v_ref, w_ref, h_ref, o_ref, acc_ref, *, te):
    j = pl.program_id(1)

    @pl.when(j == 0)
    def _():
        acc_ref[...] = jnp.zeros_like(acc_ref)

    x = x_ref[...]
    acc = None
    for c in range(te // PEER_CHUNK):
        e0 = c * PEER_CHUNK
        hid = lax.dot_general(x, u_ref[e0:e0 + PEER_CHUNK, :], (((1,), (1,)), ((), ())),
                              preferred_element_type=jnp.float32)
        parts = []
        for r in range(PEER_CHUNK // PEER_NKEYS):
            hr = hid[:, r * PEER_NKEYS:(r + 1) * PEER_NKEYS]
            ar = 0.5 * hr * (1.0 + lax.erf(hr * (2.0 ** -0.5))) * w_ref[:, e0 // PEER_NKEYS + r, :]
            parts.append(ar.astype(jnp.bfloat16))
        d = jnp.dot(jnp.concatenate(parts, axis=1), v_ref[e0:e0 + PEER_CHUNK, :], preferred_element_type=jnp.float32)
        acc = d if acc is None else acc + d
    acc_ref[...] += acc

    @pl.when(j == pl.num_programs(1) - 1)
    def _():
        o_ref[...] = h_ref[...] + acc_ref[...]


def peer_dense(xn, u, v, wsum, h, tb=512, te=1024):
    n, D = xn.shape
    E = u.shape[0]
    tb = math.gcd(n, tb)
    return pl.pallas_call(
        functools.partial(_peer_kernel, te=te),
        grid=(n // tb, E // te),
        in_specs=[pl.BlockSpec((tb, D), lambda i, j: (i, 0)),
                  pl.BlockSpec((te, D), lambda i, j: (j, 0)),
                  pl.BlockSpec((te, D), lambda i, j: (j, 0)),
                  pl.BlockSpec((tb, te // PEER_NKEYS, PEER_NKEYS), lambda i, j: (i, j, 0)),
                  pl.BlockSpec((tb, D), lambda i, j: (i, 0))],
        out_specs=pl.BlockSpec((tb, D), lambda i, j: (i, 0)),
        out_shape=jax.ShapeDtypeStruct((n, D), jnp.float32),
        scratch_shapes=[pltpu.VMEM((tb, D), jnp.float32)],
        compiler_params=pltpu.CompilerParams(dimension_semantics=("parallel", "arbitrary"),
                                             vmem_limit_bytes=VMEM_LIMIT),
        name="peer_dense",
    )(xn, u, v, wsum, h)


RT = LANE
NHC = 2 * PEER_HEADS
_CAND_GROUPS = [(0, 0), (0, 8), (1, 0)] + [(a, 0) for a in range(2, 8)] + [(-1, 0)]
assert PEER_TOPK == 16 and PEER_KEY_DIM // 2 == LANE and PEER_NKEYS == LANE


def _top_rounds(s, n_rounds, extra=()):
    R = s.shape[0]
    iota = lax.broadcasted_iota(jnp.int32, s.shape, 0)
    vals, idxs, ex = [], [], [[] for _ in extra]
    for _ in range(n_rounds):
        m = jnp.max(s, axis=0, keepdims=True)
        idx = jnp.min(jnp.where(s == m, iota, R), axis=0, keepdims=True)
        sel = iota == idx
        for e, lst in zip(extra, ex):
            lst.append(jnp.max(jnp.where(sel, e, -1), axis=0, keepdims=True))
        s = jnp.where(sel, -jnp.inf, s)
        vals.append(m)
        idxs.append(idx)
    cat = lambda l: jnp.concatenate(l, axis=0)
    return cat(vals), cat(idxs), [cat(l) for l in ex]


def _route_kernel(q_ref, sub_ref, e1_ref, e2_ref, g_ref, v_scr, i_scr):
    K = PEER_TOPK

    def stage1(hc, c):
        off = pl.multiple_of(hc * LANE, LANE)
        qb = q_ref[:, pl.ds(off, LANE)].astype(jnp.bfloat16)
        s = lax.dot_general(sub_ref[hc], qb, (((1,), (1,)), ((), ())), preferred_element_type=jnp.float32)
        v, i, _ = _top_rounds(s, K)
        v_scr[hc] = v
        i_scr[hc] = i
        return c

    lax.fori_loop(0, NHC, stage1, 0)

    row8 = lax.broadcasted_iota(jnp.int32, (8, RT), 0)

    def stage2(h, c):
        v1, v2 = v_scr[2 * h], v_scr[2 * h + 1]
        i1, i2 = i_scr[2 * h], i_scr[2 * h + 1]
        cand, c1, c2 = [], [], []
        for a, b0 in _CAND_GROUPS:
            if a >= 0:
                nb = K // (a + 1)
                sm = v1[a:a + 1] + v2[b0:b0 + 8]
                if nb - b0 < 8:
                    sm = jnp.where(row8 < nb - b0, sm, -jnp.inf)
                cand.append(sm)
                c1.append(jnp.broadcast_to(i1[a:a + 1], (8, RT)))
                c2.append(i2[b0:b0 + 8])
            else:
                cand.append(v1[8:16] + v2[0:1])
                c1.append(i1[8:16])
                c2.append(jnp.broadcast_to(i2[0:1], (8, RT)))
        cat = lambda l: jnp.concatenate(l, axis=0)
        top, _, (e1, e2) = _top_rounds(cat(cand), K, extra=(cat(c1), cat(c2)))
        ex = jnp.exp(top - top[0:1])
        g = ex / jnp.sum(ex, axis=0, keepdims=True)
        r0 = pl.multiple_of(h * K, K)
        e1_ref[0, pl.ds(r0, K), :] = e1
        e2_ref[0, pl.ds(r0, K), :] = e2
        g_ref[0, pl.ds(r0, K), :] = g
        return c

    lax.fori_loop(0, PEER_HEADS, stage2, 0)


def peer_route(q, sub_bf):
    n = q.shape[0]
    assert n % RT == 0
    nb = n // RT
    slots = PEER_HEADS * PEER_TOPK
    out = jax.ShapeDtypeStruct((nb, slots, RT), jnp.int32)
    ospec = pl.BlockSpec((1, slots, RT), lambda i: (i, 0, 0))
    return pl.pallas_call(
        _route_kernel,
        grid=(nb,),
        in_specs=[pl.BlockSpec((RT, q.shape[1]), lambda i: (i, 0)),
                  pl.BlockSpec(sub_bf.shape, lambda i: (0, 0, 0))],
        out_specs=[ospec, ospec, ospec],
        out_shape=[out, out, jax.ShapeDtypeStruct((nb, slots, RT), jnp.float32)],
        scratch_shapes=[pltpu.VMEM((NHC, PEER_TOPK, RT), jnp.float32),
                        pltpu.VMEM((NHC, PEER_TOPK, RT), jnp.int32)],
        compiler_params=pltpu.CompilerParams(dimension_semantics=("parallel",), vmem_limit_bytes=VMEM_LIMIT),
        name="peer_route",
    )(q, sub_bf)


def peer_ffn(h, xn, w_q, subkeys, u_bf, v_bf):
    B, T, D = xn.shape
    n = B * T
    xt = xn.reshape(n, D)
    q = pmm(xt, w_q)
    sub_bf = subkeys.astype(jnp.bfloat16).reshape(NHC, PEER_NKEYS, PEER_KEY_DIM // 2)
    e1, e2, gate = peer_route(q, sub_bf)
    tok_major = lambda t: jnp.transpose(t, (0, 2, 1)).reshape(n, PEER_HEADS * PEER_TOPK)
    wsum = peer_wsum(tok_major(e1), tok_major(e2), tok_major(gate))
    out = peer_dense(xt.astype(jnp.bfloat16), u_bf, v_bf, wsum, h.reshape(n, D))
    return out.reshape(B, T, D)


KVG = 2 * NSA_GROUPS
ROW_W = KVG * NSA_HEAD_DIM
CPP = PAGE_SIZE // CMP_STRIDE
PPS = 4
assert CPP == 8 and PAGE_SIZE == LANE and NSA_HEAD_DIM == LANE


def _compress_kernel(pt_ref, *refs, n_steps):
    pages = refs[:PPS]
    perm_ref, w1_ref, c1_ref, w2_ref, b2_ref, kc_ref, vc_ref, xc_ref = refs[PPS:]
    s_idx = pl.program_id(1)
    for pair in range(PPS // 2):
        row0 = pl.multiple_of((s_idx * (PPS // 2) + pair) * 2 * CPP, 2 * CPP)
        for kvg in range(KVG):
            kv, g = divmod(kvg, NSA_GROUPS)
            x2 = jnp.concatenate([pages[2 * pair][0, :, kv, g, :], pages[2 * pair + 1][0, :, kv, g, :]],
                                 axis=0).astype(jnp.bfloat16)
            y = jnp.dot(perm_ref[...], x2, preferred_element_type=jnp.float32).astype(jnp.bfloat16)
            for s in range(CMP_STRIDE):
                xc_ref[kvg, pl.ds(row0, 2 * CPP), pl.ds(s * LANE, LANE)] = y[s * 2 * CPP:(s + 1) * 2 * CPP]

    @pl.when(s_idx == n_steps - 1)
    def _():
        nch = xc_ref.shape[1]
        for kv in range(2):
            out_ref = kc_ref if kv == 0 else vc_ref
            for g in range(NSA_GROUPS):
                hh = jnp.dot(xc_ref[kv * NSA_GROUPS + g], w1_ref[kv], preferred_element_type=jnp.float32)
                h1 = hh[:, :CMP_HIDDEN]
                h2 = pltpu.roll(hh[:, CMP_HIDDEN:], nch - 1, 0)
                x = h1 + h2 + c1_ref[kv]
                hid = 0.5 * x * (1.0 + lax.erf(x * (2.0 ** -0.5)))
                o = jnp.dot(hid.astype(jnp.bfloat16), w2_ref[kv], preferred_element_type=jnp.float32) + b2_ref[kv]
                out_ref[0, g] = o.astype(out_ref.dtype)


def compress_paged(pool, page_table, w1, c1, w2, b2):
    B, NP = page_table.shape
    assert NP % PPS == 0
    n_steps = NP // PPS
    nch = NP * CPP
    dh = NSA_HEAD_DIM
    r = jnp.arange(2 * PAGE_SIZE)
    s_, u_, c_ = r // (2 * CPP), (r // CPP) % 2, r % CPP
    perm = (r[None, :] == (u_ * PAGE_SIZE + c_ * CMP_STRIDE + s_)[:, None]).astype(jnp.bfloat16)

    def page_spec(u):
        return pl.BlockSpec((1, PAGE_SIZE, 2, NSA_GROUPS, NSA_HEAD_DIM),
                            lambda b, s, pt: (pt[b, s * PPS + u], 0, 0, 0, 0))

    full = lambda a: pl.BlockSpec(a.shape, lambda b, s, pt: (0,) * a.ndim)
    out_spec = pl.BlockSpec((1, NSA_GROUPS, nch, dh), lambda b, s, pt: (b, 0, 0, 0))
    grid_spec = pltpu.PrefetchScalarGridSpec(
        num_scalar_prefetch=1,
        grid=(B, n_steps),
        in_specs=[page_spec(u) for u in range(PPS)] + [full(perm), full(w1), full(c1), full(w2), full(b2)],
        out_specs=[out_spec, out_spec],
        scratch_shapes=[pltpu.VMEM((KVG, nch, CMP_STRIDE * dh), jnp.bfloat16)],
    )
    out = jax.ShapeDtypeStruct((B, NSA_GROUPS, nch, dh), jnp.bfloat16)
    return pl.pallas_call(
        functools.partial(_compress_kernel, n_steps=n_steps),
        grid_spec=grid_spec,
        out_shape=[out, out],
        compiler_params=pltpu.CompilerParams(dimension_semantics=("parallel", "arbitrary"),
                                             vmem_limit_bytes=VMEM_LIMIT),
        name="nsa_compress",
    )(page_table, *([pool] * PPS), perm, w1, c1, w2, b2)


def compress_params(P):
    dh, F = NSA_HEAD_DIM, CMP_HIDDEN
    w1s, c1s, w2s, b2s = [], [], [], []
    for n in ('k', 'v'):
        w1h = P['cmp_%s_w1' % n].reshape(2, CMP_STRIDE * dh, F)
        pe = P['cmp_%s_pe' % n].reshape(2, CMP_STRIDE * dh)
        w1s.append(jnp.concatenate([w1h[0], w1h[1]], axis=1))
        c1s.append(P['cmp_%s_b1' % n] + jnp.einsum('hk,hkf->f', pe, w1h, precision=lax.Precision.HIGHEST))
        w2s.append(P['cmp_%s_w2' % n])
        b2s.append(P['cmp_%s_b2' % n])
    return (jnp.stack(w1s).astype(jnp.bfloat16), jnp.stack(c1s)[:, None, :],
            jnp.stack(w2s).astype(jnp.bfloat16), jnp.stack(b2s)[:, None, :])


def _softmax_rows(s, valid):
    s = jnp.where(valid, s, NEG)
    m = jnp.max(s, axis=-1, keepdims=True)
    e = jnp.where(valid, jnp.exp(s - m), 0.0)
    return e / jnp.maximum(jnp.sum(e, axis=-1, keepdims=True), 1e-30)


def _nsa_cmp_kernel(q_ref, kc_ref, vc_ref, bias_ref, ov_ref, oc_ref, imp_ref, *, nc):
    i = pl.program_id(2)
    ncp = kc_ref.shape[2]
    qpos = i * TQ + lax.broadcasted_iota(jnp.int32, (TQ, ncp), 0)
    col = lax.broadcasted_iota(jnp.int32, (TQ, ncp), 1)
    valid = (col * CMP_STRIDE + (CMP_BLOCK - 1) <= qpos) & (col < nc)
    kc = kc_ref[0, 0]
    vc = vc_ref[0, 0]
    psum = jnp.zeros((TQ, ncp), jnp.float32)
    for hh in range(NSA_HPG):
        s = lax.dot_general(q_ref[0, 0, hh], kc, (((1,), (1,)), ((), ())), preferred_element_type=jnp.float32)
        p = _softmax_rows(s + bias_ref[0, hh], valid)
        psum = psum + p
        oc_ref[0, :, hh * NSA_HEAD_DIM:(hh + 1) * NSA_HEAD_DIM] = jnp.dot(
            p.astype(jnp.bfloat16), vc, preferred_element_type=jnp.float32)
    imp_ref[0, 0] = jnp.dot(psum.astype(jnp.bfloat16), ov_ref[...], preferred_element_type=jnp.float32)


def nsa_cmp(qh, kc, vc, bias_c, overlap, nc):
    B, G, HPG, T, dh = qh.shape
    ncp = kc.shape[2]
    nsp = overlap.shape[1]
    return pl.pallas_call(
        functools.partial(_nsa_cmp_kernel, nc=nc),
        grid=(B, G, T // TQ),
        in_specs=[pl.BlockSpec((1, 1, HPG, TQ, dh), lambda b, g, i: (b, g, 0, i, 0)),
                  pl.BlockSpec((1, 1, ncp, dh), lambda b, g, i: (b, g, 0, 0)),
                  pl.BlockSpec((1, 1, ncp, dh), lambda b, g, i: (b, g, 0, 0)),
                  pl.BlockSpec((1, HPG, TQ, ncp), lambda b, g, i: (g, 0, i, 0)),
                  pl.BlockSpec((ncp, nsp), lambda b, g, i: (0, 0))],
        out_specs=[pl.BlockSpec((1, TQ, HPG * dh), lambda b, g, i: (b, i, g)),
                   pl.BlockSpec((1, 1, TQ, nsp), lambda b, g, i: (b, g, i, 0))],
        out_shape=[jax.ShapeDtypeStruct((B, T, G * HPG * dh), jnp.float32),
                   jax.ShapeDtypeStruct((B, G, T, nsp), jnp.float32)],
        compiler_params=pltpu.CompilerParams(dimension_semantics=("parallel", "parallel", "parallel"),
                                             vmem_limit_bytes=VMEM_LIMIT),
        name="nsa_cmp",
    )(qh, kc, vc, bias_c, overlap)


def _flash_tile(q_all, k, v, bias_ref, dd_b, valid, m_ref, l_ref, acc_ref):
    s_all = lax.dot_general(q_all, k, (((1,), (1,)), ((), ())), preferred_element_type=jnp.float32)
    ps = []
    for hh in range(NSA_HPG):
        rows = slice(hh * TQ, (hh + 1) * TQ)
        s = jnp.where(valid, s_all[rows] + bias_ref[0, dd_b, hh], NEG)
        m_old = m_ref[rows]
        m_new = jnp.maximum(m_old, jnp.max(s, axis=-1, keepdims=True))
        p = jnp.where(valid, jnp.exp(s - m_new), 0.0)
        alpha = jnp.exp(m_old - m_new)
        l_ref[rows] = alpha * l_ref[rows] + jnp.sum(p, axis=-1, keepdims=True)
        acc_ref[rows] = alpha * acc_ref[rows]
        m_ref[rows] = m_new
        ps.append(p.astype(jnp.bfloat16))
    acc_ref[...] += jnp.dot(jnp.concatenate(ps, axis=0), v, preferred_element_type=jnp.float32)


def _nsa_sw_kernel(q_ref, ks_ref, vs_ref, kw_ref, vw_ref, msel_ref, exp_ref, bias_ref, selm_ref, winm_ref,
                   os_ref, ow_ref, msk_ref, m_ref, l_ref, acc_ref, *, nt):
    i = pl.program_id(2)
    q_all = q_ref[0, 0].reshape(NSA_HPG * TQ, NSA_HEAD_DIM)
    msel = msel_ref[0, 0].astype(jnp.bfloat16)
    for j in range(nt):
        msk_ref[j] = jnp.dot(msel, exp_ref[:, j * TQ:(j + 1) * TQ], preferred_element_type=jnp.float32)

    def reset():
        m_ref[...] = jnp.full_like(m_ref, NEG)
        l_ref[...] = jnp.zeros_like(l_ref)
        acc_ref[...] = jnp.zeros_like(acc_ref)

    def finish(o_ref):
        o = acc_ref[...] / jnp.maximum(l_ref[...], 1e-30)
        for hh in range(NSA_HPG):
            o_ref[0, :, hh * NSA_HEAD_DIM:(hh + 1) * NSA_HEAD_DIM] = o[hh * TQ:(hh + 1) * TQ]

    reset()

    def sel_body(j, c):
        dd = jnp.minimum(i - j, 2)
        valid = (msk_ref[j] * selm_ref[dd]) > 0.5
        _flash_tile(q_all, ks_ref[0, 0, j], vs_ref[0, 0, j], bias_ref, dd, valid, m_ref, l_ref, acc_ref)
        return c

    lax.fori_loop(0, i + 1, sel_body, 0)
    finish(os_ref)

    reset()

    def win_body(j, c):
        dd = i - j
        valid = winm_ref[dd] > 0.5
        _flash_tile(q_all, kw_ref[0, 0, j], vw_ref[0, 0, j], bias_ref, jnp.minimum(dd, 2), valid, m_ref, l_ref, acc_ref)
        return c

    lax.fori_loop(jnp.maximum(i - WINDOW // TQ, 0), i + 1, win_body, 0)
    finish(ow_ref)


def nsa_sel_win(qh, ks, vs, kw, vw, msel, expander, bias_t, selm, winm):
    B, G, HPG, T, dh = qh.shape
    nt = T // TQ
    nsp = msel.shape[-1]
    kv_spec = pl.BlockSpec((1, 1, nt, TQ, dh), lambda b, g, i: (b, g, 0, 0, 0))
    out_spec = pl.BlockSpec((1, TQ, HPG * dh), lambda b, g, i: (b, i, g))
    return pl.pallas_call(
        functools.partial(_nsa_sw_kernel, nt=nt),
        grid=(B, G, T // TQ),
        in_specs=[pl.BlockSpec((1, 1, HPG, TQ, dh), lambda b, g, i: (b, g, 0, i, 0)),
                  kv_spec, kv_spec, kv_spec, kv_spec,
                  pl.BlockSpec((1, 1, TQ, nsp), lambda b, g, i: (b, g, i, 0)),
                  pl.BlockSpec((nsp, T), lambda b, g, i: (0, 0)),
                  pl.BlockSpec((1, 3, HPG, TQ, TQ), lambda b, g, i: (g, 0, 0, 0, 0)),
                  pl.BlockSpec((3, TQ, TQ), lambda b, g, i: (0, 0, 0)),
                  pl.BlockSpec((WINDOW // TQ + 1, TQ, TQ), lambda b, g, i: (0, 0, 0))],
        out_specs=[out_spec, out_spec],
        out_shape=[jax.ShapeDtypeStruct((B, T, G * HPG * dh), jnp.float32)] * 2,
        scratch_shapes=[pltpu.VMEM((nt, TQ, TQ), jnp.float32),
                        pltpu.VMEM((HPG * TQ, 1), jnp.float32),
                        pltpu.VMEM((HPG * TQ, 1), jnp.float32),
                        pltpu.VMEM((HPG * TQ, dh), jnp.float32)],
        compiler_params=pltpu.CompilerParams(dimension_semantics=("parallel", "parallel", "arbitrary"),
                                             vmem_limit_bytes=VMEM_LIMIT),
        name="nsa_sel_win",
    )(qh, ks, vs, kw, vw, msel, expander, bias_t, selm, winm)


def nsa_prompt(q, gates, kc, vc, ks, vs, kw_rows, vw_rows, table):
    B, T = q.shape[0], q.shape[1]
    G, HPG, dh = NSA_GROUPS, NSA_HPG, NSA_HEAD_DIM
    Nc = T // CMP_STRIDE - 1
    Ns = -(-T // SEL_BLOCK)
    ncp = kc.shape[2]
    assert ncp % LANE == 0 and ncp >= Nc
    nsp = -(-Ns // LANE) * LANE
    bf = jnp.bfloat16
    qh = jnp.transpose(q.reshape(B, T, G, HPG, dh), (0, 2, 3, 1, 4)).astype(bf)

    qpos = jnp.arange(T)
    cend = jnp.arange(ncp) * CMP_STRIDE + (CMP_BLOCK - 1)
    bias_c = head_bias(table, qpos[:, None] - cend[None, :])
    cstart = jnp.arange(ncp)[:, None] * CMP_STRIDE
    sstart = jnp.arange(nsp)[None, :] * SEL_BLOCK
    overlap = ((cstart < sstart + SEL_BLOCK) & (cstart + CMP_BLOCK > sstart)
               & (jnp.arange(ncp)[:, None] < Nc) & (jnp.arange(nsp)[None, :] < Ns)).astype(bf)
    o_c, imp = nsa_cmp(qh, kc, vc, bias_c, overlap, Nc)
    imp = imp[..., :Ns]
    blk = jnp.arange(Ns)[None, :]
    cur = qpos[:, None] // SEL_BLOCK
    valid = blk * SEL_BLOCK <= qpos[:, None]
    forced = (blk == 0) | (blk == cur) | (blk == cur - 1)
    score = jnp.where(valid, jnp.where(forced, BIG, imp), -BIG)
    n_pick = min(N_SEL, Ns)
    top_s, idx = lax.top_k(score, n_pick)
    ok = top_s > -0.5 * BIG
    msel = jnp.sum(jax.nn.one_hot(idx, nsp, dtype=jnp.float32) * ok[..., None], axis=-2)
    expander = (jnp.arange(nsp)[:, None] == (jnp.arange(T)[None, :] // SEL_BLOCK)).astype(bf)
    r = jnp.arange(TQ)
    rel3 = (jnp.arange(3) * TQ)[:, None, None] + r[None, :, None] - r[None, None, :]
    bias_t = jnp.transpose(table[t5_bucket(rel3)].astype(jnp.float32), (3, 0, 1, 2)).reshape(G, HPG, 3, TQ, TQ)
    bias_t = jnp.transpose(bias_t, (0, 2, 1, 3, 4))
    selm = (rel3 >= 0).astype(jnp.float32)
    nw = WINDOW // TQ + 1
    relw = (jnp.arange(nw) * TQ)[:, None, None] + r[None, :, None] - r[None, None, :]
    winm = ((relw >= 0) & (relw < WINDOW)).astype(jnp.float32)

    def tiles(t):
        return jnp.transpose(t, (0, 2, 1, 3)).astype(bf).reshape(B, G, T // TQ, TQ, dh)

    o_s, o_w = nsa_sel_win(qh, tiles(ks), tiles(vs), tiles(kw_rows), tiles(vw_rows), msel, expander, bias_t, selm, winm)
    gx = jnp.repeat(gates.reshape(B, T, G * HPG, 3), dh, axis=2)
    return gx[..., 0] * o_c + gx[..., 1] * o_s + gx[..., 2] * o_w


def _dec_cmp_kernel(q_ref, kc_ref, vc_ref, bias_ref, valid_ref, ov_ref, oc_ref, imp_ref, *, t):
    valid = valid_ref[...] > 0.5
    for g in range(NSA_GROUPS):
        s = lax.dot_general(q_ref[0, g], kc_ref[0, g], (((1,), (1,)), ((), ())), preferred_element_type=jnp.float32)
        p = _softmax_rows(s + bias_ref[g], valid)
        oc_ref[0, g] = jnp.dot(p.astype(jnp.bfloat16), vc_ref[0, g], preferred_element_type=jnp.float32)
        psum = p[0:t]
        for hh in range(1, NSA_HPG):
            psum = psum + p[hh * t:(hh + 1) * t]
        imp_ref[0, g] = jnp.dot(psum.astype(jnp.bfloat16), ov_ref[...], preferred_element_type=jnp.float32)


def dec_cmp(qg, kc, vc, bias_c, valid_c, overlap, t):
    B, G, R_, dh = qg.shape
    ncp, nsp = overlap.shape
    per_b = lambda a: pl.BlockSpec((1,) + a.shape[1:], lambda b: (b,) + (0,) * (a.ndim - 1))
    full = lambda a: pl.BlockSpec(a.shape, lambda b: (0,) * a.ndim)
    return pl.pallas_call(
        functools.partial(_dec_cmp_kernel, t=t),
        grid=(B,),
        in_specs=[per_b(qg), per_b(kc), per_b(vc), full(bias_c), full(valid_c), full(overlap)],
        out_specs=[pl.BlockSpec((1, G, R_, dh), lambda b: (b, 0, 0, 0)),
                   pl.BlockSpec((1, G, t, nsp), lambda b: (b, 0, 0, 0))],
        out_shape=[jax.ShapeDtypeStruct((B, G, R_, dh), jnp.float32),
                   jax.ShapeDtypeStruct((B, G, t, nsp), jnp.float32)],
        compiler_params=pltpu.CompilerParams(dimension_semantics=("parallel",), vmem_limit_bytes=VMEM_LIMIT),
        name="nsa_dec_cmp",
    )(qg, kc, vc, bias_c, valid_c, overlap)


def _flash_update(s, valid, v, m_ref, l_ref, acc_ref, g):
    s = jnp.where(valid, s, NEG)
    m_old = m_ref[g]
    m_new = jnp.maximum(m_old, jnp.max(s, axis=-1, keepdims=True))
    p = jnp.where(valid, jnp.exp(s - m_new), 0.0)
    alpha = jnp.exp(m_old - m_new)
    l_ref[g] = alpha * l_ref[g] + jnp.sum(p, axis=-1, keepdims=True)
    acc_ref[g] = alpha * acc_ref[g] + jnp.dot(p.astype(jnp.bfloat16), v, preferred_element_type=jnp.float32)
    m_ref[g] = m_new


def _dec_sel_kernel(pt_ref, q_ref, *refs, n_steps, t):
    pages = refs[:PPS]
    new_ref, bias_ref, mask_ref, biasn_ref, maskn_ref, o_ref, m_ref, l_ref, acc_ref = refs[PPS:]
    s_idx = pl.program_id(1)
    dh = NSA_HEAD_DIM

    @pl.when(s_idx == 0)
    def _():
        m_ref[...] = jnp.full_like(m_ref, NEG)
        l_ref[...] = jnp.zeros_like(l_ref)
        acc_ref[...] = jnp.zeros_like(acc_ref)

    def tile_mask(mref, g):
        mk = mref[0, g] > 0.5
        return jnp.concatenate([mk] * NSA_HPG, axis=0)

    for g in range(NSA_GROUPS):
        k = jnp.concatenate([pg[0, :, 0, g, :] for pg in pages], axis=0).astype(jnp.bfloat16)
        v = jnp.concatenate([pg[0, :, 1, g, :] for pg in pages], axis=0).astype(jnp.bfloat16)
        s = lax.dot_general(q_ref[0, g], k, (((1,), (1,)), ((), ())), preferred_element_type=jnp.float32)
        _flash_update(s + bias_ref[g], tile_mask(mask_ref, g), v, m_ref, l_ref, acc_ref, g)

    @pl.when(s_idx == n_steps - 1)
    def _():
        for g in range(NSA_GROUPS):
            k = new_ref[0, :, pl.ds(g * dh, dh)].astype(jnp.bfloat16)
            v = new_ref[0, :, pl.ds((NSA_GROUPS + g) * dh, dh)].astype(jnp.bfloat16)
            s = lax.dot_general(q_ref[0, g], k, (((1,), (1,)), ((), ())), preferred_element_type=jnp.float32)
            _flash_update(s + biasn_ref[g], tile_mask(maskn_ref, g), v, m_ref, l_ref, acc_ref, g)
            o_ref[0, g] = acc_ref[g] / jnp.maximum(l_ref[g], 1e-30)


def dec_sel(qg, pool, page_table, new_rows, bias_p, mask_p, bias_n, mask_n, t):
    B, G, R_, dh = qg.shape
    NP = page_table.shape[1]
    assert NP % PPS == 0
    n_steps = NP // PPS
    W = PPS * PAGE_SIZE

    def page_spec(u):
        return pl.BlockSpec((1, PAGE_SIZE, 2, NSA_GROUPS, NSA_HEAD_DIM),
                            lambda b, s, pt: (pt[b, s * PPS + u], 0, 0, 0, 0))

    grid_spec = pltpu.PrefetchScalarGridSpec(
        num_scalar_prefetch=1,
        grid=(B, n_steps),
        in_specs=[pl.BlockSpec((1, G, R_, dh), lambda b, s, pt: (b, 0, 0, 0))]
        + [page_spec(u) for u in range(PPS)]
        + [pl.BlockSpec((1, PAGE_SIZE, ROW_W), lambda b, s, pt: (b, 0, 0)),
           pl.BlockSpec((G, R_, W), lambda b, s, pt: (0, 0, s)),
           pl.BlockSpec((1, G, t, W), lambda b, s, pt: (b, 0, 0, s)),
           pl.BlockSpec((G, R_, PAGE_SIZE), lambda b, s, pt: (0, 0, 0)),
           pl.BlockSpec((1, G, t, PAGE_SIZE), lambda b, s, pt: (b, 0, 0, 0))],
        out_specs=pl.BlockSpec((1, G, R_, dh), lambda b, s, pt: (b, 0, 0, 0)),
        scratch_shapes=[pltpu.VMEM((G, R_, 1), jnp.float32), pltpu.VMEM((G, R_, 1), jnp.float32),
                        pltpu.VMEM((G, R_, dh), jnp.float32)],
    )
    return pl.pallas_call(
        functools.partial(_dec_sel_kernel, n_steps=n_steps, t=t),
        grid_spec=grid_spec,
        out_shape=jax.ShapeDtypeStruct((B, G, R_, dh), jnp.float32),
        compiler_params=pltpu.CompilerParams(dimension_semantics=("parallel", "arbitrary"),
                                             vmem_limit_bytes=VMEM_LIMIT),
        name="nsa_dec_sel",
    )(page_table, qg, *([pool] * PPS), new_rows, bias_p, mask_p, bias_n, mask_n)


def _dec_win_kernel(q_ref, win_ref, new_ref, bias_ref, valid_ref, o_ref):
    dh = NSA_HEAD_DIM
    valid = valid_ref[...] > 0.5
    for g in range(NSA_GROUPS):
        k = jnp.concatenate([win_ref[0, :, 0, g, :], new_ref[0, :, pl.ds(g * dh, dh)]], axis=0)
        v = jnp.concatenate([win_ref[0, :, 1, g, :], new_ref[0, :, pl.ds((NSA_GROUPS + g) * dh, dh)]], axis=0)
        s = lax.dot_general(q_ref[0, g], k.astype(jnp.bfloat16), (((1,), (1,)), ((), ())),
                            preferred_element_type=jnp.float32)
        p = _softmax_rows(s + bias_ref[g], valid)
        o_ref[0, g] = jnp.dot(p.astype(jnp.bfloat16), v.astype(jnp.bfloat16), preferred_element_type=jnp.float32)


def dec_win(qg, win_rows, new_rows, bias_w, valid_w):
    B, G, R_, dh = qg.shape
    per_b = lambda a: pl.BlockSpec((1,) + a.shape[1:], lambda b: (b,) + (0,) * (a.ndim - 1))
    full = lambda a: pl.BlockSpec(a.shape, lambda b: (0,) * a.ndim)
    return pl.pallas_call(
        _dec_win_kernel,
        grid=(B,),
        in_specs=[per_b(qg), per_b(win_rows), per_b(new_rows), full(bias_w), full(valid_w)],
        out_specs=pl.BlockSpec((1, G, R_, dh), lambda b: (b, 0, 0, 0)),
        out_shape=jax.ShapeDtypeStruct((B, G, R_, dh), jnp.float32),
        compiler_params=pltpu.CompilerParams(dimension_semantics=("parallel",), vmem_limit_bytes=VMEM_LIMIT),
        name="nsa_dec_win",
    )(qg, win_rows, new_rows, bias_w, valid_w)


def nsa_decode(q, gates, kc, vc, slc_pool, page_table, slc_new, win_buf, win_new_rows, table):
    B, t = q.shape[0], q.shape[1]
    G, HPG, dh = NSA_GROUPS, NSA_HPG, NSA_HEAD_DIM
    NP = page_table.shape[1]
    past = NP * PAGE_SIZE
    Wn = win_buf.shape[1]
    assert t <= PAGE_SIZE and Wn == WINDOW
    L = past + t
    nc = L // CMP_STRIDE - 1
    ncp = kc.shape[2]
    Ns = -(-L // SEL_BLOCK)
    nsp = -(-Ns // LANE) * LANE
    bf = jnp.bfloat16
    R_ = HPG * t
    qg = jnp.transpose(q.reshape(B, t, G, HPG, dh), (0, 2, 3, 1, 4)).reshape(B, G, R_, dh).astype(bf)
    qpos = past + jnp.arange(t)

    r_lo, r_hi = -(PAGE_SIZE + WINDOW), past + t
    lut_rev = table[t5_bucket(jnp.arange(r_hi, r_lo - 1, -1))].astype(jnp.float32)

    def rows_bias(p0, step, n):
        rows = []
        for i in range(t):
            j0 = r_hi - (past + i - p0)
            assert j0 >= 0 and j0 + step * (n - 1) < lut_rev.shape[0]
            rows.append(lax.slice(lut_rev, (j0, 0), (j0 + step * (n - 1) + 1, NSA_HEADS), (step, 1)))
        b = jnp.transpose(jnp.stack(rows), (2, 0, 1))
        return b.reshape(G, R_, n)

    rep = lambda m: jnp.tile(m, (HPG, 1))
    cidx = jnp.arange(ncp)
    cend = cidx * CMP_STRIDE + (CMP_BLOCK - 1)
    rel_c = qpos[:, None] - cend[None, :]
    valid_c = rep(((rel_c >= 0) & (cidx[None, :] < nc)).astype(jnp.float32))
    cstart = cidx[:, None] * CMP_STRIDE
    sstart = jnp.arange(nsp)[None, :] * SEL_BLOCK
    overlap = ((cstart < sstart + SEL_BLOCK) & (cstart + CMP_BLOCK > sstart)
               & (cidx[:, None] < nc) & (jnp.arange(nsp)[None, :] < Ns)).astype(bf)
    o_c, imp = dec_cmp(qg, kc, vc, rows_bias(CMP_BLOCK - 1, CMP_STRIDE, ncp), valid_c, overlap, t)
    imp = imp[..., :Ns]
    blk = jnp.arange(Ns)[None, :]
    cur = qpos[:, None] // SEL_BLOCK
    valid = blk * SEL_BLOCK <= qpos[:, None]
    forced = (blk == 0) | (blk == cur) | (blk == cur - 1)
    score = jnp.where(valid, jnp.where(forced, BIG, imp), -BIG)
    n_pick = min(N_SEL, Ns)
    top_s, idx = lax.top_k(score, n_pick)
    ok = top_s > -0.5 * BIG
    msel = jnp.sum(jax.nn.one_hot(idx, Ns, dtype=jnp.float32) * ok[..., None], axis=-2)
    kpos = jnp.arange(past + PAGE_SIZE)
    nblk = -(-(past + PAGE_SIZE) // SEL_BLOCK)
    mkey = jnp.repeat(jnp.pad(msel, ((0, 0), (0, 0), (0, 0), (0, nblk - Ns))), SEL_BLOCK, axis=-1)
    mkey = mkey[..., :past + PAGE_SIZE]
    mkey = mkey * ((kpos[None, :] <= qpos[:, None]) & (kpos[None, :] < L)).astype(jnp.float32)
    bias_s = rows_bias(0, 1, past + PAGE_SIZE)
    pad_new = lambda r: jnp.pad(r, ((0, 0), (0, PAGE_SIZE - t), (0, 0)))
    o_s = dec_sel(qg, slc_pool, page_table, pad_new(slc_new), bias_s[..., :past], mkey[..., :past],
                  bias_s[..., past:], mkey[..., past:], t)
    wpos = jnp.concatenate([past - Wn + jnp.arange(Wn), past + jnp.arange(PAGE_SIZE)])
    rel_w = qpos[:, None] - wpos[None, :]
    valid_w = rep(((rel_w >= 0) & (rel_w < WINDOW) & (wpos[None, :] >= 0) & (wpos[None, :] < L)).astype(jnp.float32))
    bias_w = jnp.concatenate([rows_bias(past - Wn, 1, Wn), rows_bias(past, 1, PAGE_SIZE)], axis=-1)
    o_w = dec_win(qg, win_buf, pad_new(win_new_rows), bias_w, valid_w)
    back = lambda o: jnp.transpose(o.reshape(B, G, HPG, t, dh), (0, 3, 1, 2, 4)).reshape(B, t, G * HPG * dh)
    gx = jnp.repeat(gates.reshape(B, t, G * HPG, 3), dh, axis=2)
    return gx[..., 0] * back(o_c) + gx[..., 1] * back(o_s) + gx[..., 2] * back(o_w)


def kv_rows(h, P):
    B, T, _ = h.shape
    return mm(rmsnorm(h, P['norm_kv']), P['kv_w']).reshape(B, T, 3, ROW_W)


def run_trunk(x, gla_s0, past, P):
    B, T, _ = x.shape
    h = x
    gla_states = []
    kv5 = lambda r: r.reshape(r.shape[0], r.shape[1], 2, NSA_GROUPS, NSA_HEAD_DIM)
    for layer in range(DEPTH):
        xn = rmsnorm(h, P['norm_mix'][layer])
        if layer < N_A_LAYERS:
            o, s = gla_mixer(xn, gla_s0[layer], P['gla_w_in'][layer], P['gla_w_gate_up'][layer],
                             P['gla_b_gate'][layer], P['gla_norm'][layer], P['gla_w_out'][layer])
            gla_states.append(s)
        else:
            j = layer - N_A_LAYERS
            qd = NSA_HEADS * NSA_HEAD_DIM
            proj = mm(xn, P['nsa_w_in'][j])
            q = proj[..., :qd].reshape(B, T, NSA_HEADS, NSA_HEAD_DIM) * NSA_HEAD_DIM ** -0.5
            gates = jax.nn.sigmoid(proj[..., qd:].astype(jnp.float32)).reshape(B, T, NSA_HEADS, 3).astype(x.dtype)
            if past is None:
                r5 = rows.reshape(B, T, 3, 2, NSA_GROUPS, NSA_HEAD_DIM)
                att = nsa_prompt(q, gates, kc, vc, r5[:, :, 1, 0], r5[:, :, 1, 1], r5[:, :, 2, 0], r5[:, :, 2, 1],
                                 P['rel_bias'])
            else:
                att = nsa_decode(q, gates, kc, vc, past['slc_pool'], past['page_table'], rows[:, :, 1],
                                 past['win_buf'], rows[:, :, 2], P['rel_bias'])
            o = mm(att, P['nsa_w_out'][j])
        h = h + o
        h = peer_ffn(h, rmsnorm(h, P['norm_ffn'][layer]), P['peer_w_q'][layer], P['peer_subkeys'][layer],
                     P['peer_u_bf'][layer], P['peer_v_bf'][layer])
        if layer == N_A_LAYERS - 1:
            rows = kv_rows(h, P)
            if past is None:
                assert T % (PAGE_SIZE * PPS) == 0 and T % TQ == 0
                npg = T // PAGE_SIZE
                pool = rows[:, :, 0].reshape(B * npg, PAGE_SIZE, 2, NSA_GROUPS, NSA_HEAD_DIM)
                table = jnp.arange(B * npg, dtype=jnp.int32).reshape(B, npg)
                win_new = kv5(rows[:, T - min(WINDOW, T):, 2])
            else:
                past_len = past['page_table'].shape[1] * PAGE_SIZE
                assert (past_len + T) // CMP_STRIDE == past_len // CMP_STRIDE
                pool, table = past['cmp_pool'], past['page_table']
                win_all = jnp.concatenate([past['win_buf'], kv5(rows[:, :, 2])], axis=1)
                win_new = win_all[:, win_all.shape[1] - min(WINDOW, win_all.shape[1]):]
            kc, vc = compress_paged(pool, table, *P['cmp_stack'])
    return (rmsnorm(h, P['norm_final']), jnp.stack(gla_states), kv5(rows[:, :, 0]), kv5(rows[:, :, 1]), win_new)


def kernel(x_prompt, x_sample, state_gla, cache_cmp_kv, cache_slc_kv, cache_win_kv, page_table,
           norm_mix, norm_ffn, norm_kv, norm_final, gla_w_in, gla_w_gate_up, gla_b_gate, gla_norm, gla_w_out,
           kv_w, cmp_k_w1, cmp_k_b1, cmp_k_w2, cmp_k_b2, cmp_k_pe, cmp_v_w1, cmp_v_b1, cmp_v_w2, cmp_v_b2, cmp_v_pe,
           nsa_w_in, nsa_w_out, rel_bias, peer_w_q, peer_subkeys, peer_u, peer_v):
    P = dict(norm_mix=norm_mix, norm_ffn=norm_ffn, norm_kv=norm_kv, norm_final=norm_final,
             gla_w_in=gla_w_in, gla_w_gate_up=gla_w_gate_up, gla_b_gate=gla_b_gate, gla_norm=gla_norm,
             gla_w_out=gla_w_out, kv_w=kv_w,
             cmp_k_w1=cmp_k_w1, cmp_k_b1=cmp_k_b1, cmp_k_w2=cmp_k_w2, cmp_k_b2=cmp_k_b2, cmp_k_pe=cmp_k_pe,
             cmp_v_w1=cmp_v_w1, cmp_v_b1=cmp_v_b1, cmp_v_w2=cmp_v_w2, cmp_v_b2=cmp_v_b2, cmp_v_pe=cmp_v_pe,
             nsa_w_in=nsa_w_in, nsa_w_out=nsa_w_out, rel_bias=rel_bias,
             peer_w_q=peer_w_q, peer_subkeys=peer_subkeys,
             peer_u_bf=peer_u.astype(jnp.bfloat16), peer_v_bf=peer_v.astype(jnp.bfloat16))
    P['cmp_stack'] = compress_params(P)
    past = dict(cmp_pool=cache_cmp_kv, slc_pool=cache_slc_kv, win_buf=cache_win_kv,
                page_table=page_table.astype(jnp.int32))
    gla_zero = jnp.zeros((N_A_LAYERS, BATCH, GLA_HEADS, GLA_DK, GLA_DV), x_prompt.dtype)
    y_prompt, gla_p, cmp_p, slc_p, win_p = run_trunk(x_prompt, gla_zero, None, P)
    y_sample, gla_s, cmp_s, slc_s, win_s = run_trunk(x_sample, state_gla, past, P)
    return (y_prompt, y_sample, gla_p, gla_s, cmp_p, cmp_s, slc_p, slc_s, win_p, win_s)
```

```python
import functools
import math

import jax
import jax.numpy as jnp
from jax import lax
from jax.experimental import pallas as pl
from jax.experimental.pallas import tpu as pltpu

D_MODEL = 2048
BATCH = 4
SEQ = 2048
DEPTH = 2
DEC_BATCH = 32
DEC_SEQ = 8
PAST_LEN = 8192
PAGE_SIZE = 128
N_A_LAYERS = DEPTH // 2
N_B_LAYERS = DEPTH - N_A_LAYERS
GLA_HEADS = 4
GLA_DK = D_MODEL // (2 * GLA_HEADS)
GLA_DV = D_MODEL // GLA_HEADS
GLA_GATE_RANK = 16
GLA_TAU = 16.0
GLA_CHUNK = 64
NSA_HEADS = 16
NSA_GROUPS = 4
NSA_HPG = NSA_HEADS // NSA_GROUPS
NSA_HEAD_DIM = D_MODEL // NSA_HEADS
CMP_STRIDE = 16
CMP_BLOCK = 2 * CMP_STRIDE
CMP_HIDDEN = 2 * NSA_HEAD_DIM
SEL_BLOCK = 64
N_SEL = 16
WINDOW = 512
Q_BLOCK = 32
REL_BUCKETS = 32
REL_MAX_DIST = 128
PEER_HEADS = 8
PEER_NKEYS = 128
PEER_EXPERTS = PEER_NKEYS * PEER_NKEYS
PEER_TOPK = 16
PEER_KEY_DIM = 256
PEER_TOK_BLOCK = 128
EPS = 1e-6
NEG = -1e30
BIG = 1e30

LANE = 128
VMEM_LIMIT = 56 * 1024 * 1024
TQ = 128
assert REL_MAX_DIST <= TQ and WINDOW % TQ == 0 and TQ % SEL_BLOCK == 0


def _mm_kernel(a_ref, b_ref, o_ref):
    a = a_ref[...].astype(jnp.bfloat16)
    b = b_ref[...].astype(jnp.bfloat16)
    o_ref[...] = jnp.dot(a, b, preferred_element_type=jnp.float32)


def pmm(a, b, tm=512, tn=512, keep_cols=False):
    M, K = a.shape
    N = b.shape[1]
    tm = min(tm, M)
    Mp = -(-M // tm) * tm
    Np = -(-N // LANE) * LANE
    wide = [d * LANE for d in range(1, 2 * tn // LANE + 1) if Np % (d * LANE) == 0]
    if max(wide) < tn:
        Np = -(-N // tn) * tn
        wide = [tn]
    tn = max(wide)
    if Mp != M:
        a = jnp.pad(a, ((0, Mp - M), (0, 0)))
    if Np != N:
        b = jnp.pad(b, ((0, 0), (0, Np - N)))
    out = pl.pallas_call(
        _mm_kernel,
        grid=(Mp // tm, Np // tn),
        in_specs=[pl.BlockSpec((tm, K), lambda i, j: (i, 0)),
                  pl.BlockSpec((K, tn), lambda i, j: (0, j))],
        out_specs=pl.BlockSpec((tm, tn), lambda i, j: (i, j)),
        out_shape=jax.ShapeDtypeStruct((Mp, Np), jnp.float32),
        compiler_params=pltpu.CompilerParams(
            dimension_semantics=("parallel", "parallel"), vmem_limit_bytes=VMEM_LIMIT),
        name="proj_matmul",
    )(a, b)
    return out[:M] if keep_cols else out[:M, :N]


def mm(x, w):
    lead = x.shape[:-1]
    return pmm(x.reshape(-1, x.shape[-1]), w).reshape(lead + (w.shape[1],))


def rmsnorm(x, g):
    xf = x.astype(jnp.float32)
    y = xf * lax.rsqrt(jnp.mean(xf * xf, axis=-1, keepdims=True) + EPS)
    return (y * g.astype(jnp.float32)).astype(x.dtype)


def masked_softmax(s, mask):
    s = jnp.where(mask, s, NEG)
    m = jnp.max(s, axis=-1, keepdims=True)
    e = jnp.where(mask, jnp.exp(s - m), 0.0)
    return e / jnp.maximum(jnp.sum(e, axis=-1, keepdims=True), 1e-30)


def t5_bucket(rel):
    n = jnp.maximum(rel, 0)
    exact = REL_BUCKETS // 2
    nf = jnp.maximum(n, exact).astype(jnp.float32)
    large = exact + (jnp.log(nf / exact) / math.log(REL_MAX_DIST / exact) * (REL_BUCKETS - exact)).astype(jnp.int32)
    large = jnp.minimum(large, REL_BUCKETS - 1)
    return jnp.where(n < exact, n, large)


def head_bias(table, rel):
    onehot = jax.nn.one_hot(t5_bucket(rel), REL_BUCKETS, dtype=jnp.float32)
    b = jnp.einsum('qkn,nh->hqk', onehot, table.astype(jnp.float32), precision=lax.Precision.HIGHEST)
    return b.reshape(NSA_GROUPS, NSA_HPG, rel.shape[0], rel.shape[1])


DK_ALL = GLA_HEADS * GLA_DK
DV_ALL = GLA_HEADS * GLA_DV
GZ_COL = 2 * DK_ALL + 2 * DV_ALL
assert GZ_COL % LANE == 0 and GLA_GATE_RANK <= LANE and DK_ALL % LANE == 0 and (2 * DK_ALL) % DV_ALL == 0


def _gla_kernel(q_ref, k_ref, v_ref, r_ref, gz_ref, wg_ref, bg_ref, ng_ref, tri_ref, s0_ref,
                o_ref, sfin_ref, s_scr, *, C, t_valid):
    c = pl.program_id(1)

    @pl.when(c == 0)
    def _():
        s_scr[...] = s0_ref[0]

    row = lax.broadcasted_iota(jnp.int32, (C, C), 0)
    col = lax.broadcasted_iota(jnp.int32, (C, C), 1)
    bf = jnp.bfloat16
    z_all = jnp.dot(gz_ref[0].astype(bf), wg_ref[...], preferred_element_type=jnp.float32) + bg_ref[...]
    for h in range(GLA_HEADS):
        dk = slice(h * GLA_DK, (h + 1) * GLA_DK)
        dv = slice(h * GLA_DV, (h + 1) * GLA_DV)
        z = z_all[:, dk]
        g = (jnp.minimum(z, 0.0) - jnp.log(1.0 + jnp.exp(-jnp.abs(z)))) * (1.0 / GLA_TAU)
        if t_valid is not None:
            g = jnp.where(lax.broadcasted_iota(jnp.int32, g.shape, 0) < t_valid, g, 0.0)
        b = jnp.dot(tri_ref[...], g, preferred_element_type=jnp.float32, precision=lax.Precision.HIGHEST)
        b_last = b[C - 1:C, :]
        q = q_ref[0, :, dk] * (GLA_DK ** -0.5)
        k = k_ref[0, :, dk]
        v = v_ref[0, :, dv].astype(bf)
        qe = (q * jnp.exp(b)).astype(bf)
        ke = (k * jnp.exp(-b)).astype(bf)
        a = lax.dot_general(qe, ke, (((1,), (1,)), ((), ())), preferred_element_type=jnp.float32)
        a = jnp.where(col <= row, a, 0.0)
        s = s_scr[h]
        o = (jnp.dot(a.astype(bf), v, preferred_element_type=jnp.float32)
             + jnp.dot(qe, s.astype(bf), preferred_element_type=jnp.float32))
        kd = (k * jnp.exp(b_last - b)).astype(bf)
        dcol = jnp.exp(jnp.transpose(jnp.broadcast_to(b_last, (LANE, GLA_DK))))
        s_new = (s * jnp.concatenate([dcol] * (GLA_DV // LANE), axis=1)
                 + lax.dot_general(kd, v, (((0,), (0,)), ((), ())), preferred_element_type=jnp.float32))
        s_scr[h] = s_new
        o = o * lax.rsqrt(jnp.mean(o * o, axis=-1, keepdims=True) + EPS) * ng_ref[...]
        r = r_ref[0, :, dv]
        o_ref[0, :, dv] = o * (r * (1.0 / (1.0 + jnp.exp(-r))))

    @pl.when(c == pl.num_programs(1) - 1)
    def _():
        sfin_ref[0] = s_scr[...]


def gla_core(proj, s0, w_gate_up, b_gate, norm_g, T):
    B, Tp, _ = proj.shape
    C = GLA_CHUNK if T % GLA_CHUNK == 0 else Tp
    assert Tp % C == 0 and C % 16 == 0 and (Tp == T or Tp == C)
    n = Tp // C
    wg = jnp.pad(w_gate_up, ((0, LANE - GLA_GATE_RANK), (0, 0))).astype(jnp.bfloat16)
    tri = (jnp.arange(C)[:, None] >= jnp.arange(C)[None, :]).astype(jnp.float32)
    col_blk = lambda w, i: pl.BlockSpec((1, C, w), lambda b, c: (b, c, i))
    full = lambda a: pl.BlockSpec(a.shape, lambda b, c: (0,) * a.ndim)
    bg = b_gate.reshape(1, DK_ALL)
    ng = norm_g.reshape(1, GLA_DV)
    st_spec = pl.BlockSpec((1, GLA_HEADS, GLA_DK, GLA_DV), lambda b, c: (b, 0, 0, 0))
    return pl.pallas_call(
        functools.partial(_gla_kernel, C=C, t_valid=None if Tp == T else T),
        grid=(B, n),
        in_specs=[col_blk(DK_ALL, 0), col_blk(DK_ALL, 1), col_blk(DV_ALL, 2 * DK_ALL // DV_ALL),
                  col_blk(DV_ALL, (2 * DK_ALL + DV_ALL) // DV_ALL), col_blk(LANE, GZ_COL // LANE),
                  full(wg), full(bg), full(ng), full(tri), st_spec],
        out_specs=[pl.BlockSpec((1, C, DV_ALL), lambda b, c: (b, c, 0)), st_spec],
        out_shape=[jax.ShapeDtypeStruct((B, Tp, DV_ALL), jnp.float32),
                   jax.ShapeDtypeStruct((B, GLA_HEADS, GLA_DK, GLA_DV), jnp.float32)],
        scratch_shapes=[pltpu.VMEM((GLA_HEADS, GLA_DK, GLA_DV), jnp.float32)],
        compiler_params=pltpu.CompilerParams(dimension_semantics=("parallel", "arbitrary"),
                                             vmem_limit_bytes=VMEM_LIMIT),
        name="gla_core",
    )(proj, proj, proj, proj, proj, wg, bg, ng, tri, s0)


def gla_mixer(xn, s0, w_in, w_gate_up, b_gate, norm_g, w_out):
    B, T, D = xn.shape
    proj = pmm(xn.reshape(B * T, D), w_in, keep_cols=True)
    assert proj.shape[1] >= GZ_COL + LANE
    Tp = -(-T // 16) * 16
    proj = jnp.pad(proj.reshape(B, T, -1), ((0, 0), (0, Tp - T), (0, 0)))
    o, s_new = gla_core(proj, s0.astype(jnp.float32), w_gate_up, b_gate, norm_g, T)
    return mm(o[:, :T], w_out), s_new.astype(s0.dtype)


WSUM_UNROLL = 8
PEER_CHUNK = 512


def _wsum_kernel(a_ref, b_ref, g_ref, o_ref, *, tb):
    sub = lax.broadcasted_iota(jnp.int32, (PEER_NKEYS, PEER_NKEYS), 0)

    def body(tt, c):
        t0 = pl.multiple_of(tt * WSUM_UNROLL, WSUM_UNROLL)
        a8 = a_ref[pl.ds(t0, WSUM_UNROLL), :]
        b8 = b_ref[pl.ds(t0, WSUM_UNROLL), :]
        g8 = g_ref[pl.ds(t0, WSUM_UNROLL), :]
        for u in range(WSUM_UNROLL):
            at = jnp.where(sub == a8[u:u + 1], 1.0, 0.0).astype(jnp.bfloat16)
            bt = jnp.where(sub == b8[u:u + 1], g8[u:u + 1], 0.0).astype(jnp.bfloat16)
            o_ref[t0 + u] = lax.dot_general(at, bt, (((1,), (1,)), ((), ())), preferred_element_type=jnp.float32)
        return c

    lax.fori_loop(0, tb // WSUM_UNROLL, body, 0)


def peer_wsum(i1, i2, gate, tb=64):
    n, K = i1.shape
    tb = math.gcd(n, tb)
    assert tb % WSUM_UNROLL == 0
    return pl.pallas_call(
        functools.partial(_wsum_kernel, tb=tb),
        grid=(n // tb,),
        in_specs=[pl.BlockSpec((tb, K), lambda i: (i, 0))] * 3,
        out_specs=pl.BlockSpec((tb, PEER_NKEYS, PEER_NKEYS), lambda i: (i, 0, 0)),
        out_shape=jax.ShapeDtypeStruct((n, PEER_NKEYS, PEER_NKEYS), jnp.float32),
        compiler_params=pltpu.CompilerParams(dimension_semantics=("parallel",), vmem_limit_bytes=VMEM_LIMIT),
        name="peer_wsum",
    )(i1, i2, gate)


def _peer_kernel(x_ref, u_ref, v_ref, w_ref, h_ref, o_ref, acc_ref, *, te):
    j = pl.program_id(1)

    @pl.when(j == 0)
    def _():
        acc_ref[...] = jnp.zeros_like(acc_ref)

    x = x_ref[...]
    acc = None
    for c in range(te // PEER_CHUNK):
        e0 = c * PEER_CHUNK
        hid = lax.dot_general(x, u_ref[e0:e0 + PEER_CHUNK, :], (((1,), (1,)), ((), ())),
                              preferred_element_type=jnp.float32)
        parts = []
        for r in range(PEER_CHUNK // PEER_NKEYS):
            hr = hid[:, r * PEER_NKEYS:(r + 1) * PEER_NKEYS]
            ar = 0.5 * hr * (1.0 + lax.erf(hr * (2.0 ** -0.5))) * w_ref[:, e0 // PEER_NKEYS + r, :]
            parts.append(ar.astype(jnp.bfloat16))
        d = jnp.dot(jnp.concatenate(parts, axis=1), v_ref[e0:e0 + PEER_CHUNK, :], preferred_element_type=jnp.float32)
        acc = d if acc is None else acc + d
    acc_ref[...] += acc

    @pl.when(j == pl.num_programs(1) - 1)
    def _():
        o_ref[...] = h_ref[...] + acc_ref[...]


def peer_dense(xn, u, v, wsum, h, tb=512, te=1024):
    n, D = xn.shape
    E = u.shape[0]
    tb = math.gcd(n, tb)
    return pl.pallas_call(
        functools.partial(_peer_kernel, te=te),
        grid=(n // tb, E // te),
        in_specs=[pl.BlockSpec((tb, D), lambda i, j: (i, 0)),
                  pl.BlockSpec((te, D), lambda i, j: (j, 0)),
                  pl.BlockSpec((te, D), lambda i, j: (j, 0)),
                  pl.BlockSpec((tb, te // PEER_NKEYS, PEER_NKEYS), lambda i, j: (i, j, 0)),
                  pl.BlockSpec((tb, D), lambda i, j: (i, 0))],
        out_specs=pl.BlockSpec((tb, D), lambda i, j: (i, 0)),
        out_shape=jax.ShapeDtypeStruct((n, D), jnp.float32),
        scratch_shapes=[pltpu.VMEM((tb, D), jnp.float32)],
        compiler_params=pltpu.CompilerParams(dimension_semantics=("parallel", "arbitrary"),
                                             vmem_limit_bytes=VMEM_LIMIT),
        name="peer_dense",
    )(xn, u, v, wsum, h)


RT = LANE
NHC = 2 * PEER_HEADS
_CAND_GROUPS = [(0, 0), (0, 8), (1, 0)] + [(a, 0) for a in range(2, 8)] + [(-1, 0)]
assert PEER_TOPK == 16 and PEER_KEY_DIM // 2 == LANE and PEER_NKEYS == LANE


def _top_rounds(s, n_rounds, extra=()):
    R = s.shape[0]
    iota = lax.broadcasted_iota(jnp.int32, s.shape, 0)
    vals, idxs, ex = [], [], [[] for _ in extra]
    for _ in range(n_rounds):
        m = jnp.max(s, axis=0, keepdims=True)
        idx = jnp.min(jnp.where(s == m, iota, R), axis=0, keepdims=True)
        sel = iota == idx
        for e, lst in zip(extra, ex):
            lst.append(jnp.max(jnp.where(sel, e, -1), axis=0, keepdims=True))
        s = jnp.where(sel, -jnp.inf, s)
        vals.append(m)
        idxs.append(idx)
    cat = lambda l: jnp.concatenate(l, axis=0)
    return cat(vals), cat(idxs), [cat(l) for l in ex]


def _route_kernel(q_ref, sub_ref, e1_ref, e2_ref, g_ref, v_scr, i_scr):
    K = PEER_TOPK

    def stage1(hc, c):
        off = pl.multiple_of(hc * LANE, LANE)
        qb = q_ref[:, pl.ds(off, LANE)].astype(jnp.bfloat16)
        s = lax.dot_general(sub_ref[hc], qb, (((1,), (1,)), ((), ())), preferred_element_type=jnp.float32)
        v, i, _ = _top_rounds(s, K)
        v_scr[hc] = v
        i_scr[hc] = i
        return c

    lax.fori_loop(0, NHC, stage1, 0)

    row8 = lax.broadcasted_iota(jnp.int32, (8, RT), 0)

    def stage2(h, c):
        v1, v2 = v_scr[2 * h], v_scr[2 * h + 1]
        i1, i2 = i_scr[2 * h], i_scr[2 * h + 1]
        cand, c1, c2 = [], [], []
        for a, b0 in _CAND_GROUPS:
            if a >= 0:
                nb = K // (a + 1)
                sm = v1[a:a + 1] + v2[b0:b0 + 8]
                if nb - b0 < 8:
                    sm = jnp.where(row8 < nb - b0, sm, -jnp.inf)
                cand.append(sm)
                c1.append(jnp.broadcast_to(i1[a:a + 1], (8, RT)))
                c2.append(i2[b0:b0 + 8])
            else:
                cand.append(v1[8:16] + v2[0:1])
                c1.append(i1[8:16])
                c2.append(jnp.broadcast_to(i2[0:1], (8, RT)))
        cat = lambda l: jnp.concatenate(l, axis=0)
        top, _, (e1, e2) = _top_rounds(cat(cand), K, extra=(cat(c1), cat(c2)))
        ex = jnp.exp(top - top[0:1])
        g = ex / jnp.sum(ex, axis=0, keepdims=True)
        r0 = pl.multiple_of(h * K, K)
        e1_ref[0, pl.ds(r0, K), :] = e1
        e2_ref[0, pl.ds(r0, K), :] = e2
        g_ref[0, pl.ds(r0, K), :] = g
        return c

    lax.fori_loop(0, PEER_HEADS, stage2, 0)


def peer_route(q, sub_bf):
    n = q.shape[0]
    assert n % RT == 0
    nb = n // RT
    slots = PEER_HEADS * PEER_TOPK
    out = jax.ShapeDtypeStruct((nb, slots, RT), jnp.int32)
    ospec = pl.BlockSpec((1, slots, RT), lambda i: (i, 0, 0))
    return pl.pallas_call(
        _route_kernel,
        grid=(nb,),
        in_specs=[pl.BlockSpec((RT, q.shape[1]), lambda i: (i, 0)),
                  pl.BlockSpec(sub_bf.shape, lambda i: (0, 0, 0))],
        out_specs=[ospec, ospec, ospec],
        out_shape=[out, out, jax.ShapeDtypeStruct((nb, slots, RT), jnp.float32)],
        scratch_shapes=[pltpu.VMEM((NHC, PEER_TOPK, RT), jnp.float32),
                        pltpu.VMEM((NHC, PEER_TOPK, RT), jnp.int32)],
        compiler_params=pltpu.CompilerParams(dimension_semantics=("parallel",), vmem_limit_bytes=VMEM_LIMIT),
        name="peer_route",
    )(q, sub_bf)


def peer_ffn(h, xn, w_q, subkeys, u_bf, v_bf):
    B, T, D = xn.shape
    n = B * T
    xt = xn.reshape(n, D)
    q = pmm(xt, w_q)
    sub_bf = subkeys.astype(jnp.bfloat16).reshape(NHC, PEER_NKEYS, PEER_KEY_DIM // 2)
    e1, e2, gate = peer_route(q, sub_bf)
    tok_major = lambda t: jnp.transpose(t, (0, 2, 1)).reshape(n, PEER_HEADS * PEER_TOPK)
    wsum = peer_wsum(tok_major(e1), tok_major(e2), tok_major(gate))
    out = peer_dense(xt.astype(jnp.bfloat16), u_bf, v_bf, wsum, h.reshape(n, D))
    return out.reshape(B, T, D)


KVG = 2 * NSA_GROUPS
ROW_W = KVG * NSA_HEAD_DIM
CPP = PAGE_SIZE // CMP_STRIDE
PPS = 8
assert CPP == 8 and PAGE_SIZE == LANE and NSA_HEAD_DIM == LANE


def _compress_kernel(pt_ref, *refs, n_steps):
    pages = refs[:PPS]
    perm_ref, w1_ref, c1_ref, w2_ref, b2_ref, kc_ref, vc_ref, xc_ref = refs[PPS:]
    s_idx = pl.program_id(1)
    for pair in range(PPS // 2):
        row0 = pl.multiple_of((s_idx * (PPS // 2) + pair) * 2 * CPP, 2 * CPP)
        for kvg in range(KVG):
            kv, g = divmod(kvg, NSA_GROUPS)
            x2 = jnp.concatenate([pages[2 * pair][0, :, kv, g, :], pages[2 * pair + 1][0, :, kv, g, :]],
                                 axis=0).astype(jnp.bfloat16)
            y = jnp.dot(perm_ref[...], x2, preferred_element_type=jnp.float32).astype(jnp.bfloat16)
            for s in range(CMP_STRIDE):
                xc_ref[kvg, pl.ds(row0, 2 * CPP), pl.ds(s * LANE, LANE)] = y[s * 2 * CPP:(s + 1) * 2 * CPP]

    @pl.when(s_idx == n_steps - 1)
    def _():
        nch = xc_ref.shape[1]
        for kv in range(2):
            out_ref = kc_ref if kv == 0 else vc_ref
            for g in range(NSA_GROUPS):
                hh = jnp.dot(xc_ref[kv * NSA_GROUPS + g], w1_ref[kv], preferred_element_type=jnp.float32)
                h1 = hh[:, :CMP_HIDDEN]
                h2 = pltpu.roll(hh[:, CMP_HIDDEN:], nch - 1, 0)
                x = h1 + h2 + c1_ref[kv]
                hid = 0.5 * x * (1.0 + lax.erf(x * (2.0 ** -0.5)))
                o = jnp.dot(hid.astype(jnp.bfloat16), w2_ref[kv], preferred_element_type=jnp.float32) + b2_ref[kv]
                out_ref[0, g] = o.astype(out_ref.dtype)


def compress_paged(pool, page_table, w1, c1, w2, b2):
    B, NP = page_table.shape
    assert NP % PPS == 0
    n_steps = NP // PPS
    nch = NP * CPP
    dh = NSA_HEAD_DIM
    r = jnp.arange(2 * PAGE_SIZE)
    s_, u_, c_ = r // (2 * CPP), (r // CPP) % 2, r % CPP
    perm = (r[None, :] == (u_ * PAGE_SIZE + c_ * CMP_STRIDE + s_)[:, None]).astype(jnp.bfloat16)

    def page_spec(u):
        return pl.BlockSpec((1, PAGE_SIZE, 2, NSA_GROUPS, NSA_HEAD_DIM),
                            lambda b, s, pt: (pt[b, s * PPS + u], 0, 0, 0, 0))

    full = lambda a: pl.BlockSpec(a.shape, lambda b, s, pt: (0,) * a.ndim)
    out_spec = pl.BlockSpec((1, NSA_GROUPS, nch, dh), lambda b, s, pt: (b, 0, 0, 0))
    grid_spec = pltpu.PrefetchScalarGridSpec(
        num_scalar_prefetch=1,
        grid=(B, n_steps),
        in_specs=[page_spec(u) for u in range(PPS)] + [full(perm), full(w1), full(c1), full(w2), full(b2)],
        out_specs=[out_spec, out_spec],
        scratch_shapes=[pltpu.VMEM((KVG, nch, CMP_STRIDE * dh), jnp.bfloat16)],
    )
    out = jax.ShapeDtypeStruct((B, NSA_GROUPS, nch, dh), jnp.bfloat16)
    return pl.pallas_call(
        functools.partial(_compress_kernel, n_steps=n_steps),
        grid_spec=grid_spec,
        out_shape=[out, out],
        compiler_params=pltpu.CompilerParams(dimension_semantics=("parallel", "arbitrary"),
                                             vmem_limit_bytes=VMEM_LIMIT),
        name="nsa_compress",
    )(page_table, *([pool] * PPS), perm, w1, c1, w2, b2)


def compress_params(P):
    dh, F = NSA_HEAD_DIM, CMP_HIDDEN
    w1s, c1s, w2s, b2s = [], [], [], []
    for n in ('k', 'v'):
        w1h = P['cmp_%s_w1' % n].reshape(2, CMP_STRIDE * dh, F)
        pe = P['cmp_%s_pe' % n].reshape(2, CMP_STRIDE * dh)
        w1s.append(jnp.concatenate([w1h[0], w1h[1]], axis=1))
        c1s.append(P['cmp_%s_b1' % n] + jnp.einsum('hk,hkf->f', pe, w1h, precision=lax.Precision.HIGHEST))
        w2s.append(P['cmp_%s_w2' % n])
        b2s.append(P['cmp_%s_b2' % n])
    return (jnp.stack(w1s).astype(jnp.bfloat16), jnp.stack(c1s)[:, None, :],
            jnp.stack(w2s).astype(jnp.bfloat16), jnp.stack(b2s)[:, None, :])


def _softmax_rows(s, valid):
    s = jnp.where(valid, s, NEG)
    m = jnp.max(s, axis=-1, keepdims=True)
    e = jnp.where(valid, jnp.exp(s - m), 0.0)
    return e / jnp.maximum(jnp.sum(e, axis=-1, keepdims=True), 1e-30)


def _nsa_cmp_kernel(q_ref, kc_ref, vc_ref, bias_ref, ov_ref, oc_ref, imp_ref, *, nc):
    i = pl.program_id(2)
    ncp = kc_ref.shape[2]
    qpos = i * TQ + lax.broadcasted_iota(jnp.int32, (TQ, ncp), 0)
    col = lax.broadcasted_iota(jnp.int32, (TQ, ncp), 1)
    valid = (col * CMP_STRIDE + (CMP_BLOCK - 1) <= qpos) & (col < nc)
    kc = kc_ref[0, 0]
    vc = vc_ref[0, 0]
    psum = jnp.zeros((TQ, ncp), jnp.float32)
    for hh in range(NSA_HPG):
        s = lax.dot_general(q_ref[0, 0, hh], kc, (((1,), (1,)), ((), ())), preferred_element_type=jnp.float32)
        p = _softmax_rows(s + bias_ref[0, hh], valid)
        psum = psum + p
        oc_ref[0, :, hh * NSA_HEAD_DIM:(hh + 1) * NSA_HEAD_DIM] = jnp.dot(
            p.astype(jnp.bfloat16), vc, preferred_element_type=jnp.float32)
    imp_ref[0, 0] = jnp.dot(psum.astype(jnp.bfloat16), ov_ref[...], preferred_element_type=jnp.float32)


def nsa_cmp(qh, kc, vc, bias_c, overlap, nc):
    B, G, HPG, T, dh = qh.shape
    ncp = kc.shape[2]
    nsp = overlap.shape[1]
    return pl.pallas_call(
        functools.partial(_nsa_cmp_kernel, nc=nc),
        grid=(B, G, T // TQ),
        in_specs=[pl.BlockSpec((1, 1, HPG, TQ, dh), lambda b, g, i: (b, g, 0, i, 0)),
                  pl.BlockSpec((1, 1, ncp, dh), lambda b, g, i: (b, g, 0, 0)),
                  pl.BlockSpec((1, 1, ncp, dh), lambda b, g, i: (b, g, 0, 0)),
                  pl.BlockSpec((1, HPG, TQ, ncp), lambda b, g, i: (g, 0, i, 0)),
                  pl.BlockSpec((ncp, nsp), lambda b, g, i: (0, 0))],
        out_specs=[pl.BlockSpec((1, TQ, HPG * dh), lambda b, g, i: (b, i, g)),
                   pl.BlockSpec((1, 1, TQ, nsp), lambda b, g, i: (b, g, i, 0))],
        out_shape=[jax.ShapeDtypeStruct((B, T, G * HPG * dh), jnp.float32),
                   jax.ShapeDtypeStruct((B, G, T, nsp), jnp.float32)],
        compiler_params=pltpu.CompilerParams(dimension_semantics=("parallel", "parallel", "parallel"),
                                             vmem_limit_bytes=VMEM_LIMIT),
        name="nsa_cmp",
    )(qh, kc, vc, bias_c, overlap)


def _flash_tile(q_all, k, v, bias_ref, dd_b, valid, m_ref, l_ref, acc_ref):
    s_all = lax.dot_general(q_all, k, (((1,), (1,)), ((), ())), preferred_element_type=jnp.float32)
    ps = []
    for hh in range(NSA_HPG):
        rows = slice(hh * TQ, (hh + 1) * TQ)
        s = jnp.where(valid, s_all[rows] + bias_ref[0, dd_b, hh], NEG)
        m_old = m_ref[rows]
        m_new = jnp.maximum(m_old, jnp.max(s, axis=-1, keepdims=True))
        p = jnp.where(valid, jnp.exp(s - m_new), 0.0)
        alpha = jnp.exp(m_old - m_new)
        l_ref[rows] = alpha * l_ref[rows] + jnp.sum(p, axis=-1, keepdims=True)
        acc_ref[rows] = alpha * acc_ref[rows]
        m_ref[rows] = m_new
        ps.append(p.astype(jnp.bfloat16))
    acc_ref[...] += jnp.dot(jnp.concatenate(ps, axis=0), v, preferred_element_type=jnp.float32)


def _nsa_sw_kernel(q_ref, ks_ref, vs_ref, kw_ref, vw_ref, msel_ref, exp_ref, bias_ref, selm_ref, winm_ref,
                   os_ref, ow_ref, msk_ref, m_ref, l_ref, acc_ref, *, nt):
    i = pl.program_id(2)
    q_all = q_ref[0, 0].reshape(NSA_HPG * TQ, NSA_HEAD_DIM)
    msel = msel_ref[0, 0].astype(jnp.bfloat16)
    for j in range(nt):
        msk_ref[j] = jnp.dot(msel, exp_ref[:, j * TQ:(j + 1) * TQ], preferred_element_type=jnp.float32)

    def reset():
        m_ref[...] = jnp.full_like(m_ref, NEG)
        l_ref[...] = jnp.zeros_like(l_ref)
        acc_ref[...] = jnp.zeros_like(acc_ref)

    def finish(o_ref):
        o = acc_ref[...] / jnp.maximum(l_ref[...], 1e-30)
        for hh in range(NSA_HPG):
            o_ref[0, :, hh * NSA_HEAD_DIM:(hh + 1) * NSA_HEAD_DIM] = o[hh * TQ:(hh + 1) * TQ]

    reset()

    def sel_body(j, c):
        dd = jnp.minimum(i - j, 2)
        valid = (msk_ref[j] * selm_ref[dd]) > 0.5
        _flash_tile(q_all, ks_ref[0, 0, j], vs_ref[0, 0, j], bias_ref, dd, valid, m_ref, l_ref, acc_ref)
        return c

    lax.fori_loop(0, i + 1, sel_body, 0)
    finish(os_ref)

    reset()

    def win_body(j, c):
        dd = i - j
        valid = winm_ref[dd] > 0.5
        _flash_tile(q_all, kw_ref[0, 0, j], vw_ref[0, 0, j], bias_ref, jnp.minimum(dd, 2), valid, m_ref, l_ref, acc_ref)
        return c

    lax.fori_loop(jnp.maximum(i - WINDOW // TQ, 0), i + 1, win_body, 0)
    finish(ow_ref)


def nsa_sel_win(qh, ks, vs, kw, vw, msel, expander, bias_t, selm, winm):
    B, G, HPG, T, dh = qh.shape
    nt = T // TQ
    nsp = msel.shape[-1]
    kv_spec = pl.BlockSpec((1, 1, nt, TQ, dh), lambda b, g, i: (b, g, 0, 0, 0))
    out_spec = pl.BlockSpec((1, TQ, HPG * dh), lambda b, g, i: (b, i, g))
    return pl.pallas_call(
        functools.partial(_nsa_sw_kernel, nt=nt),
        grid=(B, G, T // TQ),
        in_specs=[pl.BlockSpec((1, 1, HPG, TQ, dh), lambda b, g, i: (b, g, 0, i, 0)),
                  kv_spec, kv_spec, kv_spec, kv_spec,
                  pl.BlockSpec((1, 1, TQ, nsp), lambda b, g, i: (b, g, i, 0)),
                  pl.BlockSpec((nsp, T), lambda b, g, i: (0, 0)),
                  pl.BlockSpec((1, 3, HPG, TQ, TQ), lambda b, g, i: (g, 0, 0, 0, 0)),
                  pl.BlockSpec((3, TQ, TQ), lambda b, g, i: (0, 0, 0)),
                  pl.BlockSpec((WINDOW // TQ + 1, TQ, TQ), lambda b, g, i: (0, 0, 0))],
        out_specs=[out_spec, out_spec],
        out_shape=[jax.ShapeDtypeStruct((B, T, G * HPG * dh), jnp.float32)] * 2,
        scratch_shapes=[pltpu.VMEM((nt, TQ, TQ), jnp.float32),
                        pltpu.VMEM((HPG * TQ, 1), jnp.float32),
                        pltpu.VMEM((HPG * TQ, 1), jnp.float32),
                        pltpu.VMEM((HPG * TQ, dh), jnp.float32)],
        compiler_params=pltpu.CompilerParams(dimension_semantics=("parallel", "parallel", "arbitrary"),
                                             vmem_limit_bytes=VMEM_LIMIT),
        name="nsa_sel_win",
    )(qh, ks, vs, kw, vw, msel, expander, bias_t, selm, winm)


def nsa_prompt(q, gates, kc, vc, ks, vs, kw_rows, vw_rows, table):
    B, T = q.shape[0], q.shape[1]
    G, HPG, dh = NSA_GROUPS, NSA_HPG, NSA_HEAD_DIM
    Nc = T // CMP_STRIDE - 1
    Ns = -(-T // SEL_BLOCK)
    ncp = kc.shape[2]
    assert ncp % LANE == 0 and ncp >= Nc
    nsp = -(-Ns // LANE) * LANE
    bf = jnp.bfloat16
    qh = jnp.transpose(q.reshape(B, T, G, HPG, dh), (0, 2, 3, 1, 4)).astype(bf)

    qpos = jnp.arange(T)
    cend = jnp.arange(ncp) * CMP_STRIDE + (CMP_BLOCK - 1)
    bias_c = head_bias(table, qpos[:, None] - cend[None, :])
    cstart = jnp.arange(ncp)[:, None] * CMP_STRIDE
    sstart = jnp.arange(nsp)[None, :] * SEL_BLOCK
    overlap = ((cstart < sstart + SEL_BLOCK) & (cstart + CMP_BLOCK > sstart)
               & (jnp.arange(ncp)[:, None] < Nc) & (jnp.arange(nsp)[None, :] < Ns)).astype(bf)
    o_c, imp = nsa_cmp(qh, kc, vc, bias_c, overlap, Nc)
    imp = imp[..., :Ns]
    blk = jnp.arange(Ns)[None, :]
    cur = qpos[:, None] // SEL_BLOCK
    valid = blk * SEL_BLOCK <= qpos[:, None]
    forced = (blk == 0) | (blk == cur) | (blk == cur - 1)
    score = jnp.where(valid, jnp.where(forced, BIG, imp), -BIG)
    n_pick = min(N_SEL, Ns)
    top_s, idx = lax.top_k(score, n_pick)
    ok = top_s > -0.5 * BIG
    msel = jnp.sum(jax.nn.one_hot(idx, nsp, dtype=jnp.float32) * ok[..., None], axis=-2)
    expander = (jnp.arange(nsp)[:, None] == (jnp.arange(T)[None, :] // SEL_BLOCK)).astype(bf)
    r = jnp.arange(TQ)
    rel3 = (jnp.arange(3) * TQ)[:, None, None] + r[None, :, None] - r[None, None, :]
    bias_t = jnp.transpose(table[t5_bucket(rel3)].astype(jnp.float32), (3, 0, 1, 2)).reshape(G, HPG, 3, TQ, TQ)
    bias_t = jnp.transpose(bias_t, (0, 2, 1, 3, 4))
    selm = (rel3 >= 0).astype(jnp.float32)
    nw = WINDOW // TQ + 1
    relw = (jnp.arange(nw) * TQ)[:, None, None] + r[None, :, None] - r[None, None, :]
    winm = ((relw >= 0) & (relw < WINDOW)).astype(jnp.float32)

    def tiles(t):
        return jnp.transpose(t, (0, 2, 1, 3)).astype(bf).reshape(B, G, T // TQ, TQ, dh)

    o_s, o_w = nsa_sel_win(qh, tiles(ks), tiles(vs), tiles(kw_rows), tiles(vw_rows), msel, expander, bias_t, selm, winm)
    gx = jnp.repeat(gates.reshape(B, T, G * HPG, 3), dh, axis=2)
    return gx[..., 0] * o_c + gx[..., 1] * o_s + gx[..., 2] * o_w


def _dec_cmp_kernel(q_ref, kc_ref, vc_ref, bias_ref, valid_ref, ov_ref, oc_ref, imp_ref, *, t):
    valid = valid_ref[...] > 0.5
    for g in range(NSA_GROUPS):
        s = lax.dot_general(q_ref[0, g], kc_ref[0, g], (((1,), (1,)), ((), ())), preferred_element_type=jnp.float32)
        p = _softmax_rows(s + bias_ref[g], valid)
        oc_ref[0, g] = jnp.dot(p.astype(jnp.bfloat16), vc_ref[0, g], preferred_element_type=jnp.float32)
        psum = p[0:t]
        for hh in range(1, NSA_HPG):
            psum = psum + p[hh * t:(hh + 1) * t]
        imp_ref[0, g] = jnp.dot(psum.astype(jnp.bfloat16), ov_ref[...], preferred_element_type=jnp.float32)


def dec_cmp(qg, kc, vc, bias_c, valid_c, overlap, t):
    B, G, R_, dh = qg.shape
    ncp, nsp = overlap.shape
    per_b = lambda a: pl.BlockSpec((1,) + a.shape[1:], lambda b: (b,) + (0,) * (a.ndim - 1))
    full = lambda a: pl.BlockSpec(a.shape, lambda b: (0,) * a.ndim)
    return pl.pallas_call(
        functools.partial(_dec_cmp_kernel, t=t),
        grid=(B,),
        in_specs=[per_b(qg), per_b(kc), per_b(vc), full(bias_c), full(valid_c), full(overlap)],
        out_specs=[pl.BlockSpec((1, G, R_, dh), lambda b: (b, 0, 0, 0)),
                   pl.BlockSpec((1, G, t, nsp), lambda b: (b, 0, 0, 0))],
        out_shape=[jax.ShapeDtypeStruct((B, G, R_, dh), jnp.float32),
                   jax.ShapeDtypeStruct((B, G, t, nsp), jnp.float32)],
        compiler_params=pltpu.CompilerParams(dimension_semantics=("parallel",), vmem_limit_bytes=VMEM_LIMIT),
        name="nsa_dec_cmp",
    )(qg, kc, vc, bias_c, valid_c, overlap)


def _flash_update(s, valid, v, m_ref, l_ref, acc_ref, g):
    s = jnp.where(valid, s, NEG)
    m_old = m_ref[g]
    m_new = jnp.maximum(m_old, jnp.max(s, axis=-1, keepdims=True))
    p = jnp.where(valid, jnp.exp(s - m_new), 0.0)
    alpha = jnp.exp(m_old - m_new)
    l_ref[g] = alpha * l_ref[g] + jnp.sum(p, axis=-1, keepdims=True)
    acc_ref[g] = alpha * acc_ref[g] + jnp.dot(p.astype(jnp.bfloat16), v, preferred_element_type=jnp.float32)
    m_ref[g] = m_new


def _dec_sel_kernel(pt_ref, q_ref, *refs, n_steps, t):
    pages = refs[:PPS]
    new_ref, bias_ref, mask_ref, biasn_ref, maskn_ref, o_ref, m_ref, l_ref, acc_ref = refs[PPS:]
    s_idx = pl.program_id(1)
    dh = NSA_HEAD_DIM

    @pl.when(s_idx == 0)
    def _():
        m_ref[...] = jnp.full_like(m_ref, NEG)
        l_ref[...] = jnp.zeros_like(l_ref)
        acc_ref[...] = jnp.zeros_like(acc_ref)

    def tile_mask(mref, g):
        mk = mref[0, g] > 0.5
        return jnp.concatenate([mk] * NSA_HPG, axis=0)

    for g in range(NSA_GROUPS):
        k = jnp.concatenate([pg[0, :, 0, g, :] for pg in pages], axis=0).astype(jnp.bfloat16)
        v = jnp.concatenate([pg[0, :, 1, g, :] for pg in pages], axis=0).astype(jnp.bfloat16)
        s = lax.dot_general(q_ref[0, g], k, (((1,), (1,)), ((), ())), preferred_element_type=jnp.float32)
        _flash_update(s + bias_ref[g], tile_mask(mask_ref, g), v, m_ref, l_ref, acc_ref, g)

    @pl.when(s_idx == n_steps - 1)
    def _():
        for g in range(NSA_GROUPS):
            k = new_ref[0, :, pl.ds(g * dh, dh)].astype(jnp.bfloat16)
            v = new_ref[0, :, pl.ds((NSA_GROUPS + g) * dh, dh)].astype(jnp.bfloat16)
            s = lax.dot_general(q_ref[0, g], k, (((1,), (1,)), ((), ())), preferred_element_type=jnp.float32)
            _flash_update(s + biasn_ref[g], tile_mask(maskn_ref, g), v, m_ref, l_ref, acc_ref, g)
            o_ref[0, g] = acc_ref[g] / jnp.maximum(l_ref[g], 1e-30)


def dec_sel(qg, pool, page_table, new_rows, bias_p, mask_p, bias_n, mask_n, t):
    B, G, R_, dh = qg.shape
    NP = page_table.shape[1]
    assert NP % PPS == 0
    n_steps = NP // PPS
    W = PPS * PAGE_SIZE

    def page_spec(u):
        return pl.BlockSpec((1, PAGE_SIZE, 2, NSA_GROUPS, NSA_HEAD_DIM),
                            lambda b, s, pt: (pt[b, s * PPS + u], 0, 0, 0, 0))

    grid_spec = pltpu.PrefetchScalarGridSpec(
        num_scalar_prefetch=1,
        grid=(B, n_steps),
        in_specs=[pl.BlockSpec((1, G, R_, dh), lambda b, s, pt: (b, 0, 0, 0))]
        + [page_spec(u) for u in range(PPS)]
        + [pl.BlockSpec((1, PAGE_SIZE, ROW_W), lambda b, s, pt: (b, 0, 0)),
           pl.BlockSpec((G, R_, W), lambda b, s, pt: (0, 0, s)),
           pl.BlockSpec((1, G, t, W), lambda b, s, pt: (b, 0, 0, s)),
           pl.BlockSpec((G, R_, PAGE_SIZE), lambda b, s, pt: (0, 0, 0)),
           pl.BlockSpec((1, G, t, PAGE_SIZE), lambda b, s, pt: (b, 0, 0, 0))],
        out_specs=pl.BlockSpec((1, G, R_, dh), lambda b, s, pt: (b, 0, 0, 0)),
        scratch_shapes=[pltpu.VMEM((G, R_, 1), jnp.float32), pltpu.VMEM((G, R_, 1), jnp.float32),
                        pltpu.VMEM((G, R_, dh), jnp.float32)],
    )
    return pl.pallas_call(
        functools.partial(_dec_sel_kernel, n_steps=n_steps, t=t),
        grid_spec=grid_spec,
        out_shape=jax.ShapeDtypeStruct((B, G, R_, dh), jnp.float32),
        compiler_params=pltpu.CompilerParams(dimension_semantics=("parallel", "arbitrary"),
                                             vmem_limit_bytes=VMEM_LIMIT),
        name="nsa_dec_sel",
    )(page_table, qg, *([pool] * PPS), new_rows, bias_p, mask_p, bias_n, mask_n)


def _dec_win_kernel(q_ref, win_ref, new_ref, bias_ref, valid_ref, o_ref):
    dh = NSA_HEAD_DIM
    valid = valid_ref[...] > 0.5
    for g in range(NSA_GROUPS):
        k = jnp.concatenate([win_ref[0, :, 0, g, :], new_ref[0, :, pl.ds(g * dh, dh)]], axis=0)
        v = jnp.concatenate([win_ref[0, :, 1, g, :], new_ref[0, :, pl.ds((NSA_GROUPS + g) * dh, dh)]], axis=0)
        s = lax.dot_general(q_ref[0, g], k.astype(jnp.bfloat16), (((1,), (1,)), ((), ())),
                            preferred_element_type=jnp.float32)
        p = _softmax_rows(s + bias_ref[g], valid)
        o_ref[0, g] = jnp.dot(p.astype(jnp.bfloat16), v.astype(jnp.bfloat16), preferred_element_type=jnp.float32)


def dec_win(qg, win_rows, new_rows, bias_w, valid_w):
    B, G, R_, dh = qg.shape
    per_b = lambda a: pl.BlockSpec((1,) + a.shape[1:], lambda b: (b,) + (0,) * (a.ndim - 1))
    full = lambda a: pl.BlockSpec(a.shape, lambda b: (0,) * a.ndim)
    return pl.pallas_call(
        _dec_win_kernel,
        grid=(B,),
        in_specs=[per_b(qg), per_b(win_rows), per_b(new_rows), full(bias_w), full(valid_w)],
        out_specs=pl.BlockSpec((1, G, R_, dh), lambda b: (b, 0, 0, 0)),
        out_shape=jax.ShapeDtypeStruct((B, G, R_, dh), jnp.float32),
        compiler_params=pltpu.CompilerParams(dimension_semantics=("parallel",), vmem_limit_bytes=VMEM_LIMIT),
        name="nsa_dec_win",
    )(qg, win_rows, new_rows, bias_w, valid_w)


def nsa_decode(q, gates, kc, vc, slc_pool, page_table, slc_new, win_buf, win_new_rows, table):
    B, t = q.shape[0], q.shape[1]
    G, HPG, dh = NSA_GROUPS, NSA_HPG, NSA_HEAD_DIM
    NP = page_table.shape[1]
    past = NP * PAGE_SIZE
    Wn = win_buf.shape[1]
    assert t <= PAGE_SIZE and Wn == WINDOW
    L = past + t
    nc = L // CMP_STRIDE - 1
    ncp = kc.shape[2]
    Ns = -(-L // SEL_BLOCK)
    nsp = -(-Ns // LANE) * LANE
    bf = jnp.bfloat16
    R_ = HPG * t
    qg = jnp.transpose(q.reshape(B, t, G, HPG, dh), (0, 2, 3, 1, 4)).reshape(B, G, R_, dh).astype(bf)
    qpos = past + jnp.arange(t)

    r_lo, r_hi = -(PAGE_SIZE + WINDOW), past + t
    lut_rev = table[t5_bucket(jnp.arange(r_hi, r_lo - 1, -1))].astype(jnp.float32)

    def rows_bias(p0, step, n):
        rows = []
        for i in range(t):
            j0 = r_hi - (past + i - p0)
            assert j0 >= 0 and j0 + step * (n - 1) < lut_rev.shape[0]
            rows.append(lax.slice(lut_rev, (j0, 0), (j0 + step * (n - 1) + 1, NSA_HEADS), (step, 1)))
        b = jnp.transpose(jnp.stack(rows), (2, 0, 1))
        return b.reshape(G, R_, n)

    rep = lambda m: jnp.tile(m, (HPG, 1))
    cidx = jnp.arange(ncp)
    cend = cidx * CMP_STRIDE + (CMP_BLOCK - 1)
    rel_c = qpos[:, None] - cend[None, :]
    valid_c = rep(((rel_c >= 0) & (cidx[None, :] < nc)).astype(jnp.float32))
    cstart = cidx[:, None] * CMP_STRIDE
    sstart = jnp.arange(nsp)[None, :] * SEL_BLOCK
    overlap = ((cstart < sstart + SEL_BLOCK) & (cstart + CMP_BLOCK > sstart)
               & (cidx[:, None] < nc) & (jnp.arange(nsp)[None, :] < Ns)).astype(bf)
    o_c, imp = dec_cmp(qg, kc, vc, rows_bias(CMP_BLOCK - 1, CMP_STRIDE, ncp), valid_c, overlap, t)
    imp = imp[..., :Ns]
    blk = jnp.arange(Ns)[None, :]
    cur = qpos[:, None] // SEL_BLOCK
    valid = blk * SEL_BLOCK <= qpos[:, None]
    forced = (blk == 0) | (blk == cur) | (blk == cur - 1)
    score = jnp.where(valid, jnp.where(forced, BIG, imp), -BIG)
    n_pick = min(N_SEL, Ns)
    top_s, idx = lax.top_k(score, n_pick)
    ok = top_s > -0.5 * BIG
    msel = jnp.sum(jax.nn.one_hot(idx, Ns, dtype=jnp.float32) * ok[..., None], axis=-2)
    kpos = jnp.arange(past + PAGE_SIZE)
    nblk = -(-(past + PAGE_SIZE) // SEL_BLOCK)
    mkey = jnp.repeat(jnp.pad(msel, ((0, 0), (0, 0), (0, 0), (0, nblk - Ns))), SEL_BLOCK, axis=-1)
    mkey = mkey[..., :past + PAGE_SIZE]
    mkey = mkey * ((kpos[None, :] <= qpos[:, None]) & (kpos[None, :] < L)).astype(jnp.float32)
    bias_s = rows_bias(0, 1, past + PAGE_SIZE)
    pad_new = lambda r: jnp.pad(r, ((0, 0), (0, PAGE_SIZE - t), (0, 0)))
    o_s = dec_sel(qg, slc_pool, page_table, pad_new(slc_new), bias_s[..., :past], mkey[..., :past],
                  bias_s[..., past:], mkey[..., past:], t)
    wpos = jnp.concatenate([past - Wn + jnp.arange(Wn), past + jnp.arange(PAGE_SIZE)])
    rel_w = qpos[:, None] - wpos[None, :]
    valid_w = rep(((rel_w >= 0) & (rel_w < WINDOW) & (wpos[None, :] >= 0) & (wpos[None, :] < L)).astype(jnp.float32))
    bias_w = jnp.concatenate([rows_bias(past - Wn, 1, Wn), rows_bias(past, 1, PAGE_SIZE)], axis=-1)
    o_w = dec_win(qg, win_buf, pad_new(win_new_rows), bias_w, valid_w)
    back = lambda o: jnp.transpose(o.reshape(B, G, HPG, t, dh), (0, 3, 1, 2, 4)).reshape(B, t, G * HPG * dh)
    gx = jnp.repeat(gates.reshape(B, t, G * HPG, 3), dh, axis=2)
    return gx[..., 0] * back(o_c) + gx[..., 1] * back(o_s) + gx[..., 2] * back(o_w)


def kv_rows(h, P):
    B, T, _ = h.shape
    return mm(rmsnorm(h, P['norm_kv']), P['kv_w']).reshape(B, T, 3, ROW_W)


def run_trunk(x, gla_s0, past, P):
    B, T, _ = x.shape
    h = x
    gla_states = []
    kv5 = lambda r: r.reshape(r.shape[0], r.shape[1], 2, NSA_GROUPS, NSA_HEAD_DIM)
    for layer in range(DEPTH):
        xn = rmsnorm(h, P['norm_mix'][layer])
        if layer < N_A_LAYERS:
            o, s = gla_mixer(xn, gla_s0[layer], P['gla_w_in'][layer], P['gla_w_gate_up'][layer],
                             P['gla_b_gate'][layer], P['gla_norm'][layer], P['gla_w_out'][layer])
            gla_states.append(s)
        else:
            j = layer - N_A_LAYERS
            qd = NSA_HEADS * NSA_HEAD_DIM
            proj = mm(xn, P['nsa_w_in'][j])
            q = proj[..., :qd].reshape(B, T, NSA_HEADS, NSA_HEAD_DIM) * NSA_HEAD_DIM ** -0.5
            gates = jax.nn.sigmoid(proj[..., qd:].astype(jnp.float32)).reshape(B, T, NSA_HEADS, 3).astype(x.dtype)
            if past is None:
                r5 = rows.reshape(B, T, 3, 2, NSA_GROUPS, NSA_HEAD_DIM)
                att = nsa_prompt(q, gates, kc, vc, r5[:, :, 1, 0], r5[:, :, 1, 1], r5[:, :, 2, 0], r5[:, :, 2, 1],
                                 P['rel_bias'])
            else:
                att = nsa_decode(q, gates, kc, vc, past['slc_pool'], past['page_table'], rows[:, :, 1],
                                 past['win_buf'], rows[:, :, 2], P['rel_bias'])
            o = mm(att, P['nsa_w_out'][j])
        h = h + o
        h = peer_ffn(h, rmsnorm(h, P['norm_ffn'][layer]), P['peer_w_q'][layer], P['peer_subkeys'][layer],
                     P['peer_u_bf'][layer], P['peer_v_bf'][layer])
        if layer == N_A_LAYERS - 1:
            rows = kv_rows(h, P)
            if past is None:
                assert T % (PAGE_SIZE * PPS) == 0 and T % TQ == 0
                npg = T // PAGE_SIZE
                pool = rows[:, :, 0].reshape(B * npg, PAGE_SIZE, 2, NSA_GROUPS, NSA_HEAD_DIM)
                table = jnp.arange(B * npg, dtype=jnp.int32).reshape(B, npg)
                win_new = kv5(rows[:, T - min(WINDOW, T):, 2])
            else:
                past_len = past['page_table'].shape[1] * PAGE_SIZE
                assert (past_len + T) // CMP_STRIDE == past_len // CMP_STRIDE
                pool, table = past['cmp_pool'], past['page_table']
                win_all = jnp.concatenate([past['win_buf'], kv5(rows[:, :, 2])], axis=1)
                win_new = win_all[:, win_all.shape[1] - min(WINDOW, win_all.shape[1]):]
            kc, vc = compress_paged(pool, table, *P['cmp_stack'])
    return (rmsnorm(h, P['norm_final']), jnp.stack(gla_states), kv5(rows[:, :, 0]), kv5(rows[:, :, 1]), win_new)


def kernel(x_prompt, x_sample, state_gla, cache_cmp_kv, cache_slc_kv, cache_win_kv, page_table,
           norm_mix, norm_ffn, norm_kv, norm_final, gla_w_in, gla_w_gate_up, gla_b_gate, gla_norm, gla_w_out,
           kv_w, cmp_k_w1, cmp_k_b1, cmp_k_w2, cmp_k_b2, cmp_k_pe, cmp_v_w1, cmp_v_b1, cmp_v_w2, cmp_v_b2, cmp_v_pe,
           nsa_w_in, nsa_w_out, rel_bias, peer_w_q, peer_subkeys, peer_u, peer_v):
    P = dict(norm_mix=norm_mix, norm_ffn=norm_ffn, norm_kv=norm_kv, norm_final=norm_final,
             gla_w_in=gla_w_in, gla_w_gate_up=gla_w_gate_up, gla_b_gate=gla_b_gate, gla_norm=gla_norm,
             gla_w_out=gla_w_out, kv_w=kv_w,
             cmp_k_w1=cmp_k_w1, cmp_k_b1=cmp_k_b1, cmp_k_w2=cmp_k_w2, cmp_k_b2=cmp_k_b2, cmp_k_pe=cmp_k_pe,
             cmp_v_w1=cmp_v_w1, cmp_v_b1=cmp_v_b1, cmp_v_w2=cmp_v_w2, cmp_v_b2=cmp_v_b2, cmp_v_pe=cmp_v_pe,
             nsa_w_in=nsa_w_in, nsa_w_out=nsa_w_out, rel_bias=rel_bias,
             peer_w_q=peer_w_q, peer_subkeys=peer_subkeys,
             peer_u_bf=peer_u.astype(jnp.bfloat16), peer_v_bf=peer_v.astype(jnp.bfloat16))
    P['cmp_stack'] = compress_params(P)
    past = dict(cmp_pool=cache_cmp_kv, slc_pool=cache_slc_kv, win_buf=cache_win_kv,
                page_table=page_table.astype(jnp.int32))
    gla_zero = jnp.zeros((N_A_LAYERS, BATCH, GLA_HEADS, GLA_DK, GLA_DV), x_prompt.dtype)
    y_prompt, gla_p, cmp_p, slc_p, win_p = run_trunk(x_prompt, gla_zero, None, P)
    y_sample, gla_s, cmp_s, slc_s, win_s = run_trunk(x_sample, state_gla, past, P)
    return (y_prompt, y_sample, gla_p, gla_s, cmp_p, cmp_s, slc_p, slc_s, win_p, win_s)
```

```python
import functools
import math

import jax
import jax.numpy as jnp
from jax import lax
from jax.experimental import pallas as pl
from jax.experimental.pallas import tpu as pltpu

D_MODEL = 2048
BATCH = 4
SEQ = 2048
DEPTH = 2
DEC_BATCH = 32
DEC_SEQ = 8
PAST_LEN = 8192
PAGE_SIZE = 128
N_A_LAYERS = DEPTH // 2
N_B_LAYERS = DEPTH - N_A_LAYERS
GLA_HEADS = 4
GLA_DK = D_MODEL // (2 * GLA_HEADS)
GLA_DV = D_MODEL // GLA_HEADS
GLA_GATE_RANK = 16
GLA_TAU = 16.0
GLA_CHUNK = 64
NSA_HEADS = 16
NSA_GROUPS = 4
NSA_HPG = NSA_HEADS // NSA_GROUPS
NSA_HEAD_DIM = D_MODEL // NSA_HEADS
CMP_STRIDE = 16
CMP_BLOCK = 2 * CMP_STRIDE
CMP_HIDDEN = 2 * NSA_HEAD_DIM
SEL_BLOCK = 64
N_SEL = 16
WINDOW = 512
Q_BLOCK = 32
REL_BUCKETS = 32
REL_MAX_DIST = 128
PEER_HEADS = 8
PEER_NKEYS = 128
PEER_EXPERTS = PEER_NKEYS * PEER_NKEYS
PEER_TOPK = 16
PEER_KEY_DIM = 256
PEER_TOK_BLOCK = 128
EPS = 1e-6
NEG = -1e30
BIG = 1e30

LANE = 128
VMEM_LIMIT = 56 * 1024 * 1024
TQ = 128
assert REL_MAX_DIST <= TQ and WINDOW % TQ == 0 and TQ % SEL_BLOCK == 0


def _mm_kernel(a_ref, b_ref, o_ref):
    a = a_ref[...].astype(jnp.bfloat16)
    b = b_ref[...].astype(jnp.bfloat16)
    o_ref[...] = jnp.dot(a, b, preferred_element_type=jnp.float32)


def pmm(a, b, tm=512, tn=512, keep_cols=False):
    M, K = a.shape
    N = b.shape[1]
    tm = min(tm, M)
    Mp = -(-M // tm) * tm
    Np = -(-N // LANE) * LANE
    wide = [d * LANE for d in range(1, 2 * tn // LANE + 1) if Np % (d * LANE) == 0]
    if max(wide) < tn:
        Np = -(-N // tn) * tn
        wide = [tn]
    tn = max(wide)
    if Mp != M:
        a = jnp.pad(a, ((0, Mp - M), (0, 0)))
    if Np != N:
        b = jnp.pad(b, ((0, 0), (0, Np - N)))
    out = pl.pallas_call(
        _mm_kernel,
        grid=(Mp // tm, Np // tn),
        in_specs=[pl.BlockSpec((tm, K), lambda i, j: (i, 0)),
                  pl.BlockSpec((K, tn), lambda i, j: (0, j))],
        out_specs=pl.BlockSpec((tm, tn), lambda i, j: (i, j)),
        out_shape=jax.ShapeDtypeStruct((Mp, Np), jnp.float32),
        compiler_params=pltpu.CompilerParams(
            dimension_semantics=("parallel", "parallel"), vmem_limit_bytes=VMEM_LIMIT),
        name="proj_matmul",
    )(a, b)
    return out[:M] if keep_cols else out[:M, :N]


def mm(x, w):
    lead = x.shape[:-1]
    return pmm(x.reshape(-1, x.shape[-1]), w).reshape(lead + (w.shape[1],))


def rmsnorm(x, g):
    xf = x.astype(jnp.float32)
    y = xf * lax.rsqrt(jnp.mean(xf * xf, axis=-1, keepdims=True) + EPS)
    return (y * g.astype(jnp.float32)).astype(x.dtype)


def masked_softmax(s, mask):
    s = jnp.where(mask, s, NEG)
    m = jnp.max(s, axis=-1, keepdims=True)
    e = jnp.where(mask, jnp.exp(s - m), 0.0)
    return e / jnp.maximum(jnp.sum(e, axis=-1, keepdims=True), 1e-30)


def t5_bucket(rel):
    n = jnp.maximum(rel, 0)
    exact = REL_BUCKETS // 2
    nf = jnp.maximum(n, exact).astype(jnp.float32)
    large = exact + (jnp.log(nf / exact) / math.log(REL_MAX_DIST / exact) * (REL_BUCKETS - exact)).astype(jnp.int32)
    large = jnp.minimum(large, REL_BUCKETS - 1)
    return jnp.where(n < exact, n, large)


def head_bias(table, rel):
    onehot = jax.nn.one_hot(t5_bucket(rel), REL_BUCKETS, dtype=jnp.float32)
    b = jnp.einsum('qkn,nh->hqk', onehot, table.astype(jnp.float32), precision=lax.Precision.HIGHEST)
    return b.reshape(NSA_GROUPS, NSA_HPG, rel.shape[0], rel.shape[1])


DK_ALL = GLA_HEADS * GLA_DK
DV_ALL = GLA_HEADS * GLA_DV
GZ_COL = 2 * DK_ALL + 2 * DV_ALL
assert GZ_COL % LANE == 0 and GLA_GATE_RANK <= LANE and DK_ALL % LANE == 0 and (2 * DK_ALL) % DV_ALL == 0


def _gla_kernel(q_ref, k_ref, v_ref, r_ref, gz_ref, wg_ref, bg_ref, ng_ref, tri_ref, s0_ref,
                o_ref, sfin_ref, s_scr, *, C, t_valid):
    c = pl.program_id(1)

    @pl.when(c == 0)
    def _():
        s_scr[...] = s0_ref[0]

    row = lax.broadcasted_iota(jnp.int32, (C, C), 0)
    col = lax.broadcasted_iota(jnp.int32, (C, C), 1)
    bf = jnp.bfloat16
    z_all = jnp.dot(gz_ref[0].astype(bf), wg_ref[...], preferred_element_type=jnp.float32) + bg_ref[...]
    for h in range(GLA_HEADS):
        dk = slice(h * GLA_DK, (h + 1) * GLA_DK)
        dv = slice(h * GLA_DV, (h + 1) * GLA_DV)
        z = z_all[:, dk]
        g = (jnp.minimum(z, 0.0) - jnp.log(1.0 + jnp.exp(-jnp.abs(z)))) * (1.0 / GLA_TAU)
        if t_valid is not None:
            g = jnp.where(lax.broadcasted_iota(jnp.int32, g.shape, 0) < t_valid, g, 0.0)
        b = jnp.dot(tri_ref[...], g, preferred_element_type=jnp.float32, precision=lax.Precision.HIGHEST)
        b_last = b[C - 1:C, :]
        q = q_ref[0, :, dk] * (GLA_DK ** -0.5)
        k = k_ref[0, :, dk]
        v = v_ref[0, :, dv].astype(bf)
        qe = (q * jnp.exp(b)).astype(bf)
        ke = (k * jnp.exp(-b)).astype(bf)
        a = lax.dot_general(qe, ke, (((1,), (1,)), ((), ())), preferred_element_type=jnp.float32)
        a = jnp.where(col <= row, a, 0.0)
        s = s_scr[h]
        o = (jnp.dot(a.astype(bf), v, preferred_element_type=jnp.float32)
             + jnp.dot(qe, s.astype(bf), preferred_element_type=jnp.float32))
        kd = (k * jnp.exp(b_last - b)).astype(bf)
        dcol = jnp.exp(jnp.transpose(jnp.broadcast_to(b_last, (LANE, GLA_DK))))
        s_new = (s * jnp.concatenate([dcol] * (GLA_DV // LANE), axis=1)
                 + lax.dot_general(kd, v, (((0,), (0,)), ((), ())), preferred_element_type=jnp.float32))
        s_scr[h] = s_new
        o = o * lax.rsqrt(jnp.mean(o * o, axis=-1, keepdims=True) + EPS) * ng_ref[...]
        r = r_ref[0, :, dv]
        o_ref[0, :, dv] = o * (r * (1.0 / (1.0 + jnp.exp(-r))))

    @pl.when(c == pl.num_programs(1) - 1)
    def _():
        sfin_ref[0] = s_scr[...]


def gla_core(proj, s0, w_gate_up, b_gate, norm_g, T):
    B, Tp, _ = proj.shape
    C = GLA_CHUNK if T % GLA_CHUNK == 0 else Tp
    assert Tp % C == 0 and C % 16 == 0 and (Tp == T or Tp == C)
    n = Tp // C
    wg = jnp.pad(w_gate_up, ((0, LANE - GLA_GATE_RANK), (0, 0))).astype(jnp.bfloat16)
    tri = (jnp.arange(C)[:, None] >= jnp.arange(C)[None, :]).astype(jnp.float32)
    col_blk = lambda w, i: pl.BlockSpec((1, C, w), lambda b, c: (b, c, i))
    full = lambda a: pl.BlockSpec(a.shape, lambda b, c: (0,) * a.ndim)
    bg = b_gate.reshape(1, DK_ALL)
    ng = norm_g.reshape(1, GLA_DV)
    st_spec = pl.BlockSpec((1, GLA_HEADS, GLA_DK, GLA_DV), lambda b, c: (b, 0, 0, 0))
    return pl.pallas_call(
        functools.partial(_gla_kernel, C=C, t_valid=None if Tp == T else T),
        grid=(B, n),
        in_specs=[col_blk(DK_ALL, 0), col_blk(DK_ALL, 1), col_blk(DV_ALL, 2 * DK_ALL // DV_ALL),
                  col_blk(DV_ALL, (2 * DK_ALL + DV_ALL) // DV_ALL), col_blk(LANE, GZ_COL // LANE),
                  full(wg), full(bg), full(ng), full(tri), st_spec],
        out_specs=[pl.BlockSpec((1, C, DV_ALL), lambda b, c: (b, c, 0)), st_spec],
        out_shape=[jax.ShapeDtypeStruct((B, Tp, DV_ALL), jnp.float32),
                   jax.ShapeDtypeStruct((B, GLA_HEADS, GLA_DK, GLA_DV), jnp.float32)],
        scratch_shapes=[pltpu.VMEM((GLA_HEADS, GLA_DK, GLA_DV), jnp.float32)],
        compiler_params=pltpu.CompilerParams(dimension_semantics=("parallel", "arbitrary"),
                                             vmem_limit_bytes=VMEM_LIMIT),
        name="gla_core",
    )(proj, proj, proj, proj, proj, wg, bg, ng, tri, s0)


def gla_mixer(xn, s0, w_in, w_gate_up, b_gate, norm_g, w_out):
    B, T, D = xn.shape
    proj = pmm(xn.reshape(B * T, D), w_in, keep_cols=True)
    assert proj.shape[1] >= GZ_COL + LANE
    Tp = -(-T // 16) * 16
    proj = jnp.pad(proj.reshape(B, T, -1), ((0, 0), (0, Tp - T), (0, 0)))
    o, s_new = gla_core(proj, s0.astype(jnp.float32), w_gate_up, b_gate, norm_g, T)
    return mm(o[:, :T], w_out), s_new.astype(s0.dtype)


WSUM_UNROLL = 8
PEER_CHUNK = 512


def _wsum_kernel(a_ref, b_ref, g_ref, o_ref, *, tb):
    sub = lax.broadcasted_iota(jnp.int32, (PEER_NKEYS, PEER_NKEYS), 0)

    def body(tt, c):
        t0 = pl.multiple_of(tt * WSUM_UNROLL, WSUM_UNROLL)
        a8 = a_ref[pl.ds(t0, WSUM_UNROLL), :]
        b8 = b_ref[pl.ds(t0, WSUM_UNROLL), :]
        g8 = g_ref[pl.ds(t0, WSUM_UNROLL), :]
        for u in range(WSUM_UNROLL):
            at = jnp.where(sub == a8[u:u + 1], 1.0, 0.0).astype(jnp.bfloat16)
            bt = jnp.where(sub == b8[u:u + 1], g8[u:u + 1], 0.0).astype(jnp.bfloat16)
            o_ref[t0 + u] = lax.dot_general(at, bt, (((1,), (1,)), ((), ())), preferred_element_type=jnp.float32)
        return c

    lax.fori_loop(0, tb // WSUM_UNROLL, body, 0)


def peer_wsum(i1, i2, gate, tb=64):
    n, K = i1.shape
    tb = math.gcd(n, tb)
    assert tb % WSUM_UNROLL == 0
    return pl.pallas_call(
        functools.partial(_wsum_kernel, tb=tb),
        grid=(n // tb,),
        in_specs=[pl.BlockSpec((tb, K), lambda i: (i, 0))] * 3,
        out_specs=pl.BlockSpec((tb, PEER_NKEYS, PEER_NKEYS), lambda i: (i, 0, 0)),
        out_shape=jax.ShapeDtypeStruct((n, PEER_NKEYS, PEER_NKEYS), jnp.float32),
        compiler_params=pltpu.CompilerParams(dimension_semantics=("parallel",), vmem_limit_bytes=VMEM_LIMIT),
        name="peer_wsum",
    )(i1, i2, gate)


def _peer_kernel(x_ref, u_ref, v_ref, w_ref, h_ref, o_ref, acc_ref, *, te):
    j = pl.program_id(1)

    @pl.when(j == 0)
    def _():
        acc_ref[...] = jnp.zeros_like(acc_ref)

    x = x_ref[...]
    acc = None
    for c in range(te // PEER_CHUNK):
        e0 = c * PEER_CHUNK
        hid = lax.dot_general(x, u_ref[e0:e0 + PEER_CHUNK, :], (((1,), (1,)), ((), ())),
                              preferred_element_type=jnp.float32)
        parts = []
        for r in range(PEER_CHUNK // PEER_NKEYS):
            hr = hid[:, r * PEER_NKEYS:(r + 1) * PEER_NKEYS]
            ar = 0.5 * hr * (1.0 + lax.erf(hr * (2.0 ** -0.5))) * w_ref[:, e0 // PEER_NKEYS + r, :]
            parts.append(ar.astype(jnp.bfloat16))
        d = jnp.dot(jnp.concatenate(parts, axis=1), v_ref[e0:e0 + PEER_CHUNK, :], preferred_element_type=jnp.float32)
        acc = d if acc is None else acc + d
    acc_ref[...] += acc

    @pl.when(j == pl.num_programs(1) - 1)
    def _():
        o_ref[...] = h_ref[...] + acc_ref[...]


def peer_dense(xn, u, v, wsum, h, tb=512, te=1024):
    n, D = xn.shape
    E = u.shape[0]
    tb = math.gcd(n, tb)
    return pl.pallas_call(
        functools.partial(_peer_kernel, te=te),
        grid=(n // tb, E // te),
        in_specs=[pl.BlockSpec((tb, D), lambda i, j: (i, 0)),
                  pl.BlockSpec((te, D), lambda i, j: (j, 0)),
                  pl.BlockSpec((te, D), lambda i, j: (j, 0)),
                  pl.BlockSpec((tb, te // PEER_NKEYS, PEER_NKEYS), lambda i, j: (i, j, 0)),
                  pl.BlockSpec((tb, D), lambda i, j: (i, 0))],
        out_specs=pl.BlockSpec((tb, D), lambda i, j: (i, 0)),
        out_shape=jax.ShapeDtypeStruct((n, D), jnp.float32),
        scratch_shapes=[pltpu.VMEM((tb, D), jnp.float32)],
        compiler_params=pltpu.CompilerParams(dimension_semantics=("parallel", "arbitrary"),
                                             vmem_limit_bytes=VMEM_LIMIT),
        name="peer_dense",
    )(xn, u, v, wsum, h)


RT = LANE
NHC = 2 * PEER_HEADS
_CAND_GROUPS = [(0, 0), (0, 8), (1, 0)] + [(a, 0) for a in range(2, 8)] + [(-1, 0)]
assert PEER_TOPK == 16 and PEER_KEY_DIM // 2 == LANE and PEER_NKEYS == LANE


def _top_rounds(s, n_rounds, extra=()):
    R = s.shape[0]
    iota = lax.broadcasted_iota(jnp.int32, s.shape, 0)
    vals, idxs, ex = [], [], [[] for _ in extra]
    for _ in range(n_rounds):
        m = jnp.max(s, axis=0, keepdims=True)
        idx = jnp.min(jnp.where(s == m, iota, R), axis=0, keepdims=True)
        sel = iota == idx
        for e, lst in zip(extra, ex):
            lst.append(jnp.max(jnp.where(sel, e, -1), axis=0, keepdims=True))
        s = jnp.where(sel, -jnp.inf, s)
        vals.append(m)
        idxs.append(idx)
    cat = lambda l: jnp.concatenate(l, axis=0)
    return cat(vals), cat(idxs), [cat(l) for l in ex]


def _route_kernel(q_ref, sub_ref, e1_ref, e2_ref, g_ref, v_scr, i_scr):
    K = PEER_TOPK

    def stage1(h, c):
        for hc in (2 * h, 2 * h + 1):
            off = pl.multiple_of(hc * LANE, LANE)
            qb = q_ref[:, pl.ds(off, LANE)].astype(jnp.bfloat16)
            s = lax.dot_general(sub_ref[hc], qb, (((1,), (1,)), ((), ())), preferred_element_type=jnp.float32)
            v, i, _ = _top_rounds(s, K)
            v_scr[hc] = v
            i_scr[hc] = i
        return c

    lax.fori_loop(0, PEER_HEADS, stage1, 0)

    row8 = lax.broadcasted_iota(jnp.int32, (8, RT), 0)

    def stage2(h):
        v1, v2 = v_scr[2 * h], v_scr[2 * h + 1]
        i1, i2 = i_scr[2 * h], i_scr[2 * h + 1]
        cand, c1, c2 = [], [], []
        for a, b0 in _CAND_GROUPS:
            if a >= 0:
                nb = K // (a + 1)
                sm = v1[a:a + 1] + v2[b0:b0 + 8]
                if nb - b0 < 8:
                    sm = jnp.where(row8 < nb - b0, sm, -jnp.inf)
                cand.append(sm)
                c1.append(jnp.broadcast_to(i1[a:a + 1], (8, RT)))
                c2.append(i2[b0:b0 + 8])
            else:
                cand.append(v1[8:16] + v2[0:1])
                c1.append(i1[8:16])
                c2.append(jnp.broadcast_to(i2[0:1], (8, RT)))
        cat = lambda l: jnp.concatenate(l, axis=0)
        top, _, (e12,) = _top_rounds(cat(cand), K, extra=(cat(c1) * PEER_NKEYS + cat(c2),))
        ex = jnp.exp(top - top[0:1])
        g = ex / jnp.sum(ex, axis=0, keepdims=True)
        r0 = pl.multiple_of(h * K, K)
        e1_ref[0, pl.ds(r0, K), :] = e12 // PEER_NKEYS
        e2_ref[0, pl.ds(r0, K), :] = e12 % PEER_NKEYS
        g_ref[0, pl.ds(r0, K), :] = g

    def stage2_pair(hp, c):
        stage2(2 * hp)
        stage2(2 * hp + 1)
        return c

    lax.fori_loop(0, PEER_HEADS // 2, stage2_pair, 0)


def peer_route(q, sub_bf):
    n = q.shape[0]
    assert n % RT == 0
    nb = n // RT
    slots = PEER_HEADS * PEER_TOPK
    out = jax.ShapeDtypeStruct((nb, slots, RT), jnp.int32)
    ospec = pl.BlockSpec((1, slots, RT), lambda i: (i, 0, 0))
    return pl.pallas_call(
        _route_kernel,
        grid=(nb,),
        in_specs=[pl.BlockSpec((RT, q.shape[1]), lambda i: (i, 0)),
                  pl.BlockSpec(sub_bf.shape, lambda i: (0, 0, 0))],
        out_specs=[ospec, ospec, ospec],
        out_shape=[out, out, jax.ShapeDtypeStruct((nb, slots, RT), jnp.float32)],
        scratch_shapes=[pltpu.VMEM((NHC, PEER_TOPK, RT), jnp.float32),
                        pltpu.VMEM((NHC, PEER_TOPK, RT), jnp.int32)],
        compiler_params=pltpu.CompilerParams(dimension_semantics=("parallel",), vmem_limit_bytes=VMEM_LIMIT),
        name="peer_route",
    )(q, sub_bf)


def peer_ffn(h, xn, w_q, subkeys, u_bf, v_bf):
    B, T, D = xn.shape
    n = B * T
    xt = xn.reshape(n, D)
    q = pmm(xt, w_q)
    sub_bf = subkeys.astype(jnp.bfloat16).reshape(NHC, PEER_NKEYS, PEER_KEY_DIM // 2)
    e1, e2, gate = peer_route(q, sub_bf)
    tok_major = lambda t: jnp.transpose(t, (0, 2, 1)).reshape(n, PEER_HEADS * PEER_TOPK)
    wsum = peer_wsum(tok_major(e1), tok_major(e2), tok_major(gate))
    out = peer_dense(xt.astype(jnp.bfloat16), u_bf, v_bf, wsum, h.reshape(n, D))
    return out.reshape(B, T, D)


KVG = 2 * NSA_GROUPS
ROW_W = KVG * NSA_HEAD_DIM
CPP = PAGE_SIZE // CMP_STRIDE
PPS = 8
assert CPP == 8 and PAGE_SIZE == LANE and NSA_HEAD_DIM == LANE


def _compress_kernel(pt_ref, *refs, n_steps):
    pages = refs[:PPS]
    perm_ref, w1_ref, c1_ref, w2_ref, b2_ref, kc_ref, vc_ref, xc_ref = refs[PPS:]
    s_idx = pl.program_id(1)
    for pair in range(PPS // 2):
        row0 = pl.multiple_of((s_idx * (PPS // 2) + pair) * 2 * CPP, 2 * CPP)
        for kvg in range(KVG):
            kv, g = divmod(kvg, NSA_GROUPS)
            x2 = jnp.concatenate([pages[2 * pair][0, :, kv, g, :], pages[2 * pair + 1][0, :, kv, g, :]],
                                 axis=0).astype(jnp.bfloat16)
            y = jnp.dot(perm_ref[...], x2, preferred_element_type=jnp.float32).astype(jnp.bfloat16)
            for s in range(CMP_STRIDE):
                xc_ref[kvg, pl.ds(row0, 2 * CPP), pl.ds(s * LANE, LANE)] = y[s * 2 * CPP:(s + 1) * 2 * CPP]

    @pl.when(s_idx == n_steps - 1)
    def _():
        nch = xc_ref.shape[1]
        for kv in range(2):
            out_ref = kc_ref if kv == 0 else vc_ref
            for g in range(NSA_GROUPS):
                hh = jnp.dot(xc_ref[kv * NSA_GROUPS + g], w1_ref[kv], preferred_element_type=jnp.float32)
                h1 = hh[:, :CMP_HIDDEN]
                h2 = pltpu.roll(hh[:, CMP_HIDDEN:], nch - 1, 0)
                x = h1 + h2 + c1_ref[kv]
                hid = 0.5 * x * (1.0 + lax.erf(x * (2.0 ** -0.5)))
                o = jnp.dot(hid.astype(jnp.bfloat16), w2_ref[kv], preferred_element_type=jnp.float32) + b2_ref[kv]
                out_ref[0, g] = o.astype(out_ref.dtype)


def compress_paged(pool, page_table, w1, c1, w2, b2):
    B, NP = page_table.shape
    assert NP % PPS == 0
    n_steps = NP // PPS
    nch = NP * CPP
    dh = NSA_HEAD_DIM
    r = jnp.arange(2 * PAGE_SIZE)
    s_, u_, c_ = r // (2 * CPP), (r // CPP) % 2, r % CPP
    perm = (r[None, :] == (u_ * PAGE_SIZE + c_ * CMP_STRIDE + s_)[:, None]).astype(jnp.bfloat16)

    def page_spec(u):
        return pl.BlockSpec((1, PAGE_SIZE, 2, NSA_GROUPS, NSA_HEAD_DIM),
                            lambda b, s, pt: (pt[b, s * PPS + u], 0, 0, 0, 0))

    full = lambda a: pl.BlockSpec(a.shape, lambda b, s, pt: (0,) * a.ndim)
    out_spec = pl.BlockSpec((1, NSA_GROUPS, nch, dh), lambda b, s, pt: (b, 0, 0, 0))
    grid_spec = pltpu.PrefetchScalarGridSpec(
        num_scalar_prefetch=1,
        grid=(B, n_steps),
        in_specs=[page_spec(u) for u in range(PPS)] + [full(perm), full(w1), full(c1), full(w2), full(b2)],
        out_specs=[out_spec, out_spec],
        scratch_shapes=[pltpu.VMEM((KVG, nch, CMP_STRIDE * dh), jnp.bfloat16)],
    )
    out = jax.ShapeDtypeStruct((B, NSA_GROUPS, nch, dh), jnp.bfloat16)
    return pl.pallas_call(
        functools.partial(_compress_kernel, n_steps=n_steps),
        grid_spec=grid_spec,
        out_shape=[out, out],
        compiler_params=pltpu.CompilerParams(dimension_semantics=("parallel", "arbitrary"),
                                             vmem_limit_bytes=VMEM_LIMIT),
        name="nsa_compress",
    )(page_table, *([pool] * PPS), perm, w1, c1, w2, b2)


def compress_params(P):
    dh, F = NSA_HEAD_DIM, CMP_HIDDEN
    w1s, c1s, w2s, b2s = [], [], [], []
    for n in ('k', 'v'):
        w1h = P['cmp_%s_w1' % n].reshape(2, CMP_STRIDE * dh, F)
        pe = P['cmp_%s_pe' % n].reshape(2, CMP_STRIDE * dh)
        w1s.append(jnp.concatenate([w1h[0], w1h[1]], axis=1))
        c1s.append(P['cmp_%s_b1' % n] + jnp.einsum('hk,hkf->f', pe, w1h, precision=lax.Precision.HIGHEST))
        w2s.append(P['cmp_%s_w2' % n])
        b2s.append(P['cmp_%s_b2' % n])
    return (jnp.stack(w1s).astype(jnp.bfloat16), jnp.stack(c1s)[:, None, :],
            jnp.stack(w2s).astype(jnp.bfloat16), jnp.stack(b2s)[:, None, :])


def _softmax_rows(s, valid):
    s = jnp.where(valid, s, NEG)
    m = jnp.max(s, axis=-1, keepdims=True)
    e = jnp.where(valid, jnp.exp(s - m), 0.0)
    return e / jnp.maximum(jnp.sum(e, axis=-1, keepdims=True), 1e-30)


HQ = NSA_HPG * TQ
GATE_ROWS = 8


def _lanes4(x):
    return jnp.concatenate([x] * NSA_HPG, axis=1)


def _nsa_cmp_sel_kernel(q_ref, kc_ref, vct_ref, bias_ref, ovt_ref, oct_ref, msel_ref, *, nc, ns):
    i = pl.program_id(2)
    ncp = kc_ref.shape[2]
    nsp = ovt_ref.shape[0]
    q_all = q_ref[0, 0].reshape(HQ, NSA_HEAD_DIM)
    cidx = lax.broadcasted_iota(jnp.int32, (ncp, TQ), 0)
    qpos = i * TQ + lax.broadcasted_iota(jnp.int32, (ncp, TQ), 1)
    valid = _lanes4((cidx * CMP_STRIDE + (CMP_BLOCK - 1) <= qpos) & (cidx < nc))
    s = lax.dot_general(kc_ref[0, 0], q_all, (((1,), (1,)), ((), ())), preferred_element_type=jnp.float32)
    s = jnp.where(valid, s + bias_ref[0, 0], NEG)
    m = jnp.max(s, axis=0, keepdims=True)
    e = jnp.where(valid, jnp.exp(s - m), 0.0)
    p = e / jnp.maximum(jnp.sum(e, axis=0, keepdims=True), 1e-30)
    oct_ref[0, 0, 0] = jnp.dot(vct_ref[0, 0], p.astype(jnp.bfloat16), preferred_element_type=jnp.float32)
    psum = p[:, 0:TQ]
    for hh in range(1, NSA_HPG):
        psum = psum + p[:, hh * TQ:(hh + 1) * TQ]
    imp = jnp.dot(ovt_ref[...], psum.astype(jnp.bfloat16), preferred_element_type=jnp.float32)
    blk = lax.broadcasted_iota(jnp.int32, (nsp, TQ), 0)
    qp = i * TQ + lax.broadcasted_iota(jnp.int32, (nsp, TQ), 1)
    cur = qp // SEL_BLOCK
    ok_blk = (blk * SEL_BLOCK <= qp) & (blk < ns)
    forced = (blk == 0) | (blk == cur) | (blk == cur - 1)
    score = jnp.where(ok_blk, jnp.where(forced, BIG, imp), -BIG)
    score = jnp.where(blk < ns, score, -jnp.inf)
    msel = jnp.zeros((nsp, TQ), jnp.float32)
    for _ in range(min(N_SEL, ns)):
        mx = jnp.max(score, axis=0, keepdims=True)
        idx = jnp.min(jnp.where(score == mx, blk, nsp), axis=0, keepdims=True)
        sel = blk == idx
        msel = jnp.where(sel & (mx > -0.5 * BIG), 1.0, msel)
        score = jnp.where(sel, -jnp.inf, score)
    msel_ref[0, 0] = msel


def nsa_cmp_sel(qh, kc, vct, bias_ct, ovt, nc, ns):
    B, G, HPG, T, dh = qh.shape
    ncp = kc.shape[2]
    nsp = ovt.shape[0]
    nqt = T // TQ
    return pl.pallas_call(
        functools.partial(_nsa_cmp_sel_kernel, nc=nc, ns=ns),
        grid=(B, G, nqt),
        in_specs=[pl.BlockSpec((1, 1, HPG, TQ, dh), lambda b, g, i: (b, g, 0, i, 0)),
                  pl.BlockSpec((1, 1, ncp, dh), lambda b, g, i: (b, g, 0, 0)),
                  pl.BlockSpec((1, 1, dh, ncp), lambda b, g, i: (b, g, 0, 0)),
                  pl.BlockSpec((1, 1, ncp, HQ), lambda b, g, i: (g, i, 0, 0)),
                  pl.BlockSpec((nsp, ncp), lambda b, g, i: (0, 0))],
        out_specs=[pl.BlockSpec((1, 1, 1, dh, HQ), lambda b, g, i: (b, g, i, 0, 0)),
                   pl.BlockSpec((1, 1, nsp, TQ), lambda b, g, i: (b, g, 0, i))],
        out_shape=[jax.ShapeDtypeStruct((B, G, nqt, dh, HQ), jnp.float32),
                   jax.ShapeDtypeStruct((B, G, nsp, T), jnp.float32)],
        compiler_params=pltpu.CompilerParams(dimension_semantics=("parallel", "parallel", "parallel"),
                                             vmem_limit_bytes=VMEM_LIMIT),
        name="nsa_cmp_sel",
    )(qh, kc, vct, bias_ct, ovt)


def _flash_tile_t(q_all, k, vt, bias, valid, m_ref, l_ref, acc_ref):
    s = lax.dot_general(k, q_all, (((1,), (1,)), ((), ())), preferred_element_type=jnp.float32)
    s = jnp.where(valid, s + bias, NEG)
    m_old = m_ref[...]
    m_new = jnp.maximum(m_old, jnp.max(s, axis=0, keepdims=True))
    p = jnp.where(valid, jnp.exp(s - m_new), 0.0)
    alpha = jnp.exp(m_old - m_new)
    l_ref[...] = alpha * l_ref[...] + jnp.sum(p, axis=0, keepdims=True)
    acc_ref[...] = alpha * acc_ref[...] + jnp.dot(vt, p.astype(jnp.bfloat16), preferred_element_type=jnp.float32)
    m_ref[...] = m_new


def _nsa_sw_kernel(q_ref, ks_ref, vst_ref, kw_ref, vwt_ref, msel_ref, expt_ref, bias_ref, selm_ref, winm_ref,
                   oct_ref, gate_ref, o_ref, msk_ref, m_ref, l_ref, acc_ref, *, nt):
    i = pl.program_id(2)
    q_all = q_ref[0, 0].reshape(HQ, NSA_HEAD_DIM)
    msel = msel_ref[0, 0].astype(jnp.bfloat16)
    for j in range(nt):
        msk_ref[j] = jnp.dot(expt_ref[j], msel, preferred_element_type=jnp.float32)

    def reset():
        m_ref[...] = jnp.full_like(m_ref, NEG)
        l_ref[...] = jnp.zeros_like(l_ref)
        acc_ref[...] = jnp.zeros_like(acc_ref)

    def result():
        return acc_ref[...] / jnp.maximum(l_ref[...], 1e-30)

    reset()

    def sel_body(j, c):
        dd = jnp.minimum(i - j, 2)
        valid = _lanes4((msk_ref[j] * selm_ref[dd]) > 0.5)
        _flash_tile_t(q_all, ks_ref[0, 0, j], vst_ref[0, 0, j], bias_ref[0, dd], valid, m_ref, l_ref, acc_ref)
        return c

    lax.fori_loop(0, i + 1, sel_body, 0)
    mix = gate_ref[0, 0, 0, 0:1, :] * oct_ref[0, 0, 0] + gate_ref[0, 0, 0, 1:2, :] * result()
    reset()

    def win_body(j, c):
        dd = i - j
        valid = _lanes4(winm_ref[dd] > 0.5)
        _flash_tile_t(q_all, kw_ref[0, 0, j], vwt_ref[0, 0, j], bias_ref[0, jnp.minimum(dd, 2)], valid,
                      m_ref, l_ref, acc_ref)
        return c

    lax.fori_loop(jnp.maximum(i - WINDOW // TQ, 0), i + 1, win_body, 0)
    mix = mix + gate_ref[0, 0, 0, 2:3, :] * result()
    for hh in range(NSA_HPG):
        o_ref[0, :, hh * NSA_HEAD_DIM:(hh + 1) * NSA_HEAD_DIM] = jnp.transpose(mix[:, hh * TQ:(hh + 1) * TQ])


def nsa_sel_win(qh, ks, vst, kw, vwt, msel_t, exp_t, bias_tt, selm_t, winm_t, oct, gates_t):
    B, G, HPG, T, dh = qh.shape
    nt = T // TQ
    nsp = msel_t.shape[2]
    k_spec = pl.BlockSpec((1, 1, nt, TQ, dh), lambda b, g, i: (b, g, 0, 0, 0))
    vt_spec = pl.BlockSpec((1, 1, nt, dh, TQ), lambda b, g, i: (b, g, 0, 0, 0))
    return pl.pallas_call(
        functools.partial(_nsa_sw_kernel, nt=nt),
        grid=(B, G, nt),
        in_specs=[pl.BlockSpec((1, 1, HPG, TQ, dh), lambda b, g, i: (b, g, 0, i, 0)),
                  k_spec, vt_spec, k_spec, vt_spec,
                  pl.BlockSpec((1, 1, nsp, TQ), lambda b, g, i: (b, g, 0, i)),
                  pl.BlockSpec((nt, TQ, nsp), lambda b, g, i: (0, 0, 0)),
                  pl.BlockSpec((1, 3, TQ, HQ), lambda b, g, i: (g, 0, 0, 0)),
                  pl.BlockSpec((3, TQ, TQ), lambda b, g, i: (0, 0, 0)),
                  pl.BlockSpec((WINDOW // TQ + 1, TQ, TQ), lambda b, g, i: (0, 0, 0)),
                  pl.BlockSpec((1, 1, 1, dh, HQ), lambda b, g, i: (b, g, i, 0, 0)),
                  pl.BlockSpec((1, 1, 1, GATE_ROWS, HQ), lambda b, g, i: (b, g, i, 0, 0))],
        out_specs=pl.BlockSpec((1, TQ, HPG * dh), lambda b, g, i: (b, i, g)),
        out_shape=jax.ShapeDtypeStruct((B, T, G * HPG * dh), jnp.float32),
        scratch_shapes=[pltpu.VMEM((nt, TQ, TQ), jnp.float32),
                        pltpu.VMEM((1, HQ), jnp.float32),
                        pltpu.VMEM((1, HQ), jnp.float32),
                        pltpu.VMEM((dh, HQ), jnp.float32)],
        compiler_params=pltpu.CompilerParams(dimension_semantics=("parallel", "parallel", "arbitrary"),
                                             vmem_limit_bytes=VMEM_LIMIT),
        name="nsa_sel_win",
    )(qh, ks, vst, kw, vwt, msel_t, exp_t, bias_tt, selm_t, winm_t, oct, gates_t)


def nsa_prompt(q, gates, kc, vc, ks, vs, kw_rows, vw_rows, table):
    B, T = q.shape[0], q.shape[1]
    G, HPG, dh = NSA_GROUPS, NSA_HPG, NSA_HEAD_DIM
    Nc = T // CMP_STRIDE - 1
    Ns = -(-T // SEL_BLOCK)
    ncp = kc.shape[2]
    assert ncp % LANE == 0 and ncp >= Nc
    nsp = -(-Ns // 16) * 16
    nqt = T // TQ
    bf = jnp.bfloat16
    qh = jnp.transpose(q.reshape(B, T, G, HPG, dh), (0, 2, 3, 1, 4)).astype(bf)
    qpos = jnp.arange(T)
    cidx = jnp.arange(ncp)
    bias_c = head_bias(table, qpos[:, None] - (cidx * CMP_STRIDE + (CMP_BLOCK - 1))[None, :])
    bias_ct = jnp.transpose(bias_c.reshape(G, HPG, nqt, TQ, ncp), (0, 2, 4, 1, 3)).reshape(G, nqt, ncp, HQ)
    sidx = jnp.arange(nsp)
    ovt = (((cidx * CMP_STRIDE)[None, :] < (sidx * SEL_BLOCK + SEL_BLOCK)[:, None])
           & ((cidx * CMP_STRIDE + CMP_BLOCK)[None, :] > (sidx * SEL_BLOCK)[:, None])
           & (cidx[None, :] < Nc) & (sidx[:, None] < Ns)).astype(bf)
    oct, msel_t = nsa_cmp_sel(qh, kc, jnp.transpose(vc, (0, 1, 3, 2)), bias_ct, ovt, Nc, Ns)
    exp_t = ((jnp.arange(T) // SEL_BLOCK)[:, None] == sidx[None, :]).astype(bf).reshape(nqt, TQ, nsp)
    r = jnp.arange(TQ)
    rel3 = (jnp.arange(3) * TQ)[:, None, None] + r[None, None, :] - r[None, :, None]
    bias_tt = jnp.transpose(table[t5_bucket(rel3)].astype(jnp.float32), (3, 0, 1, 2))
    bias_tt = jnp.transpose(bias_tt.reshape(G, HPG, 3, TQ, TQ), (0, 2, 3, 1, 4)).reshape(G, 3, TQ, HQ)
    selm_t = (rel3 >= 0).astype(jnp.float32)
    nw = WINDOW // TQ + 1
    relw = (jnp.arange(nw) * TQ)[:, None, None] + r[None, None, :] - r[None, :, None]
    winm_t = ((relw >= 0) & (relw < WINDOW)).astype(jnp.float32)

    def k_tiles(t):
        return jnp.transpose(t, (0, 2, 1, 3)).astype(bf).reshape(B, G, nqt, TQ, dh)

    def vt_tiles(t):
        return jnp.transpose(t.astype(bf).reshape(B, nqt, TQ, G, dh), (0, 3, 1, 4, 2))

    gates_t = jnp.transpose(gates.reshape(B, nqt, TQ, G, HPG, 3), (0, 3, 1, 5, 4, 2)).reshape(B, G, nqt, 3, HQ)
    gates_t = jnp.pad(gates_t, ((0, 0), (0, 0), (0, 0), (0, GATE_ROWS - 3), (0, 0)))
    return nsa_sel_win(qh, k_tiles(ks), vt_tiles(vs), k_tiles(kw_rows), vt_tiles(vw_rows), msel_t, exp_t, bias_tt,
                       selm_t, winm_t, oct, gates_t)


def _dec_cmp_kernel(q_ref, kc_ref, vc_ref, bias_ref, valid_ref, ov_ref, oc_ref, imp_ref, *, t):
    valid = valid_ref[...] > 0.5
    for g in range(NSA_GROUPS):
        s = lax.dot_general(q_ref[0, g], kc_ref[0, g], (((1,), (1,)), ((), ())), preferred_element_type=jnp.float32)
        p = _softmax_rows(s + bias_ref[g], valid)
        oc_ref[0, g] = jnp.dot(p.astype(jnp.bfloat16), vc_ref[0, g], preferred_element_type=jnp.float32)
        psum = p[0:t]
        for hh in range(1, NSA_HPG):
            psum = psum + p[hh * t:(hh + 1) * t]
        imp_ref[0, g] = jnp.dot(psum.astype(jnp.bfloat16), ov_ref[...], preferred_element_type=jnp.float32)


def dec_cmp(qg, kc, vc, bias_c, valid_c, overlap, t):
    B, G, R_, dh = qg.shape
    ncp, nsp = overlap.shape
    per_b = lambda a: pl.BlockSpec((1,) + a.shape[1:], lambda b: (b,) + (0,) * (a.ndim - 1))
    full = lambda a: pl.BlockSpec(a.shape, lambda b: (0,) * a.ndim)
    return pl.pallas_call(
        functools.partial(_dec_cmp_kernel, t=t),
        grid=(B,),
        in_specs=[per_b(qg), per_b(kc), per_b(vc), full(bias_c), full(valid_c), full(overlap)],
        out_specs=[pl.BlockSpec((1, G, R_, dh), lambda b: (b, 0, 0, 0)),
                   pl.BlockSpec((1, G, t, nsp), lambda b: (b, 0, 0, 0))],
        out_shape=[jax.ShapeDtypeStruct((B, G, R_, dh), jnp.float32),
                   jax.ShapeDtypeStruct((B, G, t, nsp), jnp.float32)],
        compiler_params=pltpu.CompilerParams(dimension_semantics=("parallel",), vmem_limit_bytes=VMEM_LIMIT),
        name="nsa_dec_cmp",
    )(qg, kc, vc, bias_c, valid_c, overlap)


def _flash_update(s, valid, v, m_ref, l_ref, acc_ref, g):
    s = jnp.where(valid, s, NEG)
    m_old = m_ref[g]
    m_new = jnp.maximum(m_old, jnp.max(s, axis=-1, keepdims=True))
    p = jnp.where(valid, jnp.exp(s - m_new), 0.0)
    alpha = jnp.exp(m_old - m_new)
    l_ref[g] = alpha * l_ref[g] + jnp.sum(p, axis=-1, keepdims=True)
    acc_ref[g] = alpha * acc_ref[g] + jnp.dot(p.astype(jnp.bfloat16), v, preferred_element_type=jnp.float32)
    m_ref[g] = m_new


def _dec_sel_kernel(pt_ref, q_ref, *refs, n_steps, t):
    pages = refs[:PPS]
    new_ref, bias_ref, mask_ref, biasn_ref, maskn_ref, o_ref, m_ref, l_ref, acc_ref = refs[PPS:]
    s_idx = pl.program_id(1)
    dh = NSA_HEAD_DIM

    @pl.when(s_idx == 0)
    def _():
        m_ref[...] = jnp.full_like(m_ref, NEG)
        l_ref[...] = jnp.zeros_like(l_ref)
        acc_ref[...] = jnp.zeros_like(acc_ref)

    def tile_mask(mref, g):
        mk = mref[0, g] > 0.5
        return jnp.concatenate([mk] * NSA_HPG, axis=0)

    for g in range(NSA_GROUPS):
        k = jnp.concatenate([pg[0, :, 0, g, :] for pg in pages], axis=0).astype(jnp.bfloat16)
        v = jnp.concatenate([pg[0, :, 1, g, :] for pg in pages], axis=0).astype(jnp.bfloat16)
        s = lax.dot_general(q_ref[0, g], k, (((1,), (1,)), ((), ())), preferred_element_type=jnp.float32)
        _flash_update(s + bias_ref[g], tile_mask(mask_ref, g), v, m_ref, l_ref, acc_ref, g)

    @pl.when(s_idx == n_steps - 1)
    def _():
        for g in range(NSA_GROUPS):
            k = new_ref[0, :, pl.ds(g * dh, dh)].astype(jnp.bfloat16)
            v = new_ref[0, :, pl.ds((NSA_GROUPS + g) * dh, dh)].astype(jnp.bfloat16)
            s = lax.dot_general(q_ref[0, g], k, (((1,), (1,)), ((), ())), preferred_element_type=jnp.float32)
            _flash_update(s + biasn_ref[g], tile_mask(maskn_ref, g), v, m_ref, l_ref, acc_ref, g)
            o_ref[0, g] = acc_ref[g] / jnp.maximum(l_ref[g], 1e-30)


def dec_sel(qg, pool, page_table, new_rows, bias_p, mask_p, bias_n, mask_n, t):
    B, G, R_, dh = qg.shape
    NP = page_table.shape[1]
    assert NP % PPS == 0
    n_steps = NP // PPS
    W = PPS * PAGE_SIZE

    def page_spec(u):
        return pl.BlockSpec((1, PAGE_SIZE, 2, NSA_GROUPS, NSA_HEAD_DIM),
                            lambda b, s, pt: (pt[b, s * PPS + u], 0, 0, 0, 0))

    grid_spec = pltpu.PrefetchScalarGridSpec(
        num_scalar_prefetch=1,
        grid=(B, n_steps),
        in_specs=[pl.BlockSpec((1, G, R_, dh), lambda b, s, pt: (b, 0, 0, 0))]
        + [page_spec(u) for u in range(PPS)]
        + [pl.BlockSpec((1, PAGE_SIZE, ROW_W), lambda b, s, pt: (b, 0, 0)),
           pl.BlockSpec((G, R_, W), lambda b, s, pt: (0, 0, s)),
           pl.BlockSpec((1, G, t, W), lambda b, s, pt: (b, 0, 0, s)),
           pl.BlockSpec((G, R_, PAGE_SIZE), lambda b, s, pt: (0, 0, 0)),
           pl.BlockSpec((1, G, t, PAGE_SIZE), lambda b, s, pt: (b, 0, 0, 0))],
        out_specs=pl.BlockSpec((1, G, R_, dh), lambda b, s, pt: (b, 0, 0, 0)),
        scratch_shapes=[pltpu.VMEM((G, R_, 1), jnp.float32), pltpu.VMEM((G, R_, 1), jnp.float32),
                        pltpu.VMEM((G, R_, dh), jnp.float32)],
    )
    return pl.pallas_call(
        functools.partial(_dec_sel_kernel, n_steps=n_steps, t=t),
        grid_spec=grid_spec,
        out_shape=jax.ShapeDtypeStruct((B, G, R_, dh), jnp.float32),
        compiler_params=pltpu.CompilerParams(dimension_semantics=("parallel", "arbitrary"),
                                             vmem_limit_bytes=VMEM_LIMIT),
        name="nsa_dec_sel",
    )(page_table, qg, *([pool] * PPS), new_rows, bias_p, mask_p, bias_n, mask_n)


def _dec_win_kernel(q_ref, win_ref, new_ref, bias_ref, valid_ref, o_ref):
    dh = NSA_HEAD_DIM
    valid = valid_ref[...] > 0.5
    for g in range(NSA_GROUPS):
        k = jnp.concatenate([win_ref[0, :, 0, g, :], new_ref[0, :, pl.ds(g * dh, dh)]], axis=0)
        v = jnp.concatenate([win_ref[0, :, 1, g, :], new_ref[0, :, pl.ds((NSA_GROUPS + g) * dh, dh)]], axis=0)
        s = lax.dot_general(q_ref[0, g], k.astype(jnp.bfloat16), (((1,), (1,)), ((), ())),
                            preferred_element_type=jnp.float32)
        p = _softmax_rows(s + bias_ref[g], valid)
        o_ref[0, g] = jnp.dot(p.astype(jnp.bfloat16), v.astype(jnp.bfloat16), preferred_element_type=jnp.float32)


def dec_win(qg, win_rows, new_rows, bias_w, valid_w):
    B, G, R_, dh = qg.shape
    per_b = lambda a: pl.BlockSpec((1,) + a.shape[1:], lambda b: (b,) + (0,) * (a.ndim - 1))
    full = lambda a: pl.BlockSpec(a.shape, lambda b: (0,) * a.ndim)
    return pl.pallas_call(
        _dec_win_kernel,
        grid=(B,),
        in_specs=[per_b(qg), per_b(win_rows), per_b(new_rows), full(bias_w), full(valid_w)],
        out_specs=pl.BlockSpec((1, G, R_, dh), lambda b: (b, 0, 0, 0)),
        out_shape=jax.ShapeDtypeStruct((B, G, R_, dh), jnp.float32),
        compiler_params=pltpu.CompilerParams(dimension_semantics=("parallel",), vmem_limit_bytes=VMEM_LIMIT),
        name="nsa_dec_win",
    )(qg, win_rows, new_rows, bias_w, valid_w)


def nsa_decode(q, gates, kc, vc, slc_pool, page_table, slc_new, win_buf, win_new_rows, table):
    B, t = q.shape[0], q.shape[1]
    G, HPG, dh = NSA_GROUPS, NSA_HPG, NSA_HEAD_DIM
    NP = page_table.shape[1]
    past = NP * PAGE_SIZE
    Wn = win_buf.shape[1]
    assert t <= PAGE_SIZE and Wn == WINDOW
    L = past + t
    nc = L // CMP_STRIDE - 1
    ncp = kc.shape[2]
    Ns = -(-L // SEL_BLOCK)
    nsp = -(-Ns // LANE) * LANE
    bf = jnp.bfloat16
    R_ = HPG * t
    qg = jnp.transpose(q.reshape(B, t, G, HPG, dh), (0, 2, 3, 1, 4)).reshape(B, G, R_, dh).astype(bf)
    qpos = past + jnp.arange(t)

    r_lo, r_hi = -(PAGE_SIZE + WINDOW), past + t
    lut_rev = table[t5_bucket(jnp.arange(r_hi, r_lo - 1, -1))].astype(jnp.float32)

    def rows_bias(p0, step, n):
        rows = []
        for i in range(t):
            j0 = r_hi - (past + i - p0)
            assert j0 >= 0 and j0 + step * (n - 1) < lut_rev.shape[0]
            rows.append(lax.slice(lut_rev, (j0, 0), (j0 + step * (n - 1) + 1, NSA_HEADS), (step, 1)))
        b = jnp.transpose(jnp.stack(rows), (2, 0, 1))
        return b.reshape(G, R_, n)

    rep = lambda m: jnp.tile(m, (HPG, 1))
    cidx = jnp.arange(ncp)
    cend = cidx * CMP_STRIDE + (CMP_BLOCK - 1)
    rel_c = qpos[:, None] - cend[None, :]
    valid_c = rep(((rel_c >= 0) & (cidx[None, :] < nc)).astype(jnp.float32))
    cstart = cidx[:, None] * CMP_STRIDE
    sstart = jnp.arange(nsp)[None, :] * SEL_BLOCK
    overlap = ((cstart < sstart + SEL_BLOCK) & (cstart + CMP_BLOCK > sstart)
               & (cidx[:, None] < nc) & (jnp.arange(nsp)[None, :] < Ns)).astype(bf)
    o_c, imp = dec_cmp(qg, kc, vc, rows_bias(CMP_BLOCK - 1, CMP_STRIDE, ncp), valid_c, overlap, t)
    imp = imp[..., :Ns]
    blk = jnp.arange(Ns)[None, :]
    cur = qpos[:, None] // SEL_BLOCK
    valid = blk * SEL_BLOCK <= qpos[:, None]
    forced = (blk == 0) | (blk == cur) | (blk == cur - 1)
    score = jnp.where(valid, jnp.where(forced, BIG, imp), -BIG)
    n_pick = min(N_SEL, Ns)
    top_s, idx = lax.top_k(score, n_pick)
    ok = top_s > -0.5 * BIG
    msel = jnp.sum(jax.nn.one_hot(idx, Ns, dtype=jnp.float32) * ok[..., None], axis=-2)
    kpos = jnp.arange(past + PAGE_SIZE)
    nblk = -(-(past + PAGE_SIZE) // SEL_BLOCK)
    mkey = jnp.repeat(jnp.pad(msel, ((0, 0), (0, 0), (0, 0), (0, nblk - Ns))), SEL_BLOCK, axis=-1)
    mkey = mkey[..., :past + PAGE_SIZE]
    mkey = mkey * ((kpos[None, :] <= qpos[:, None]) & (kpos[None, :] < L)).astype(jnp.float32)
    bias_s = rows_bias(0, 1, past + PAGE_SIZE)
    pad_new = lambda r: jnp.pad(r, ((0, 0), (0, PAGE_SIZE - t), (0, 0)))
    o_s = dec_sel(qg, slc_pool, page_table, pad_new(slc_new), bias_s[..., :past], mkey[..., :past],
                  bias_s[..., past:], mkey[..., past:], t)
    wpos = jnp.concatenate([past - Wn + jnp.arange(Wn), past + jnp.arange(PAGE_SIZE)])
    rel_w = qpos[:, None] - wpos[None, :]
    valid_w = rep(((rel_w >= 0) & (rel_w < WINDOW) & (wpos[None, :] >= 0) & (wpos[None, :] < L)).astype(jnp.float32))
    bias_w = jnp.concatenate([rows_bias(past - Wn, 1, Wn), rows_bias(past, 1, PAGE_SIZE)], axis=-1)
    o_w = dec_win(qg, win_buf, pad_new(win_new_rows), bias_w, valid_w)
    back = lambda o: jnp.transpose(o.reshape(B, G, HPG, t, dh), (0, 3, 1, 2, 4)).reshape(B, t, G * HPG * dh)
    gx = jnp.repeat(gates.reshape(B, t, G * HPG, 3), dh, axis=2)
    return gx[..., 0] * back(o_c) + gx[..., 1] * back(o_s) + gx[..., 2] * back(o_w)


def kv_rows(h, P):
    B, T, _ = h.shape
    return mm(rmsnorm(h, P['norm_kv']), P['kv_w']).reshape(B, T, 3, ROW_W)


def run_trunk(x, gla_s0, past, P):
    B, T, _ = x.shape
    h = x
    gla_states = []
    kv5 = lambda r: r.reshape(r.shape[0], r.shape[1], 2, NSA_GROUPS, NSA_HEAD_DIM)
    for layer in range(DEPTH):
        xn = rmsnorm(h, P['norm_mix'][layer])
        if layer < N_A_LAYERS:
            o, s = gla_mixer(xn, gla_s0[layer], P['gla_w_in'][layer], P['gla_w_gate_up'][layer],
                             P['gla_b_gate'][layer], P['gla_norm'][layer], P['gla_w_out'][layer])
            gla_states.append(s)
        else:
            j = layer - N_A_LAYERS
            qd = NSA_HEADS * NSA_HEAD_DIM
            proj = mm(xn, P['nsa_w_in'][j])
            q = proj[..., :qd].reshape(B, T, NSA_HEADS, NSA_HEAD_DIM) * NSA_HEAD_DIM ** -0.5
            gates = jax.nn.sigmoid(proj[..., qd:].astype(jnp.float32)).reshape(B, T, NSA_HEADS, 3).astype(x.dtype)
            if past is None:
                r5 = rows.reshape(B, T, 3, 2, NSA_GROUPS, NSA_HEAD_DIM)
                att = nsa_prompt(q, gates, kc, vc, r5[:, :, 1, 0], r5[:, :, 1, 1], r5[:, :, 2, 0], r5[:, :, 2, 1],
                                 P['rel_bias'])
            else:
                att = nsa_decode(q, gates, kc, vc, past['slc_pool'], past['page_table'], rows[:, :, 1],
                                 past['win_buf'], rows[:, :, 2], P['rel_bias'])
            o = mm(att, P['nsa_w_out'][j])
        h = h + o
        h = peer_ffn(h, rmsnorm(h, P['norm_ffn'][layer]), P['peer_w_q'][layer], P['peer_subkeys'][layer],
                     P['peer_u_bf'][layer], P['peer_v_bf'][layer])
        if layer == N_A_LAYERS - 1:
            rows = kv_rows(h, P)
            if past is None:
                assert T % (PAGE_SIZE * PPS) == 0 and T % TQ == 0
                npg = T // PAGE_SIZE
                pool = rows[:, :, 0].reshape(B * npg, PAGE_SIZE, 2, NSA_GROUPS, NSA_HEAD_DIM)
                table = jnp.arange(B * npg, dtype=jnp.int32).reshape(B, npg)
                win_new = kv5(rows[:, T - min(WINDOW, T):, 2])
            else:
                past_len = past['page_table'].shape[1] * PAGE_SIZE
                assert (past_len + T) // CMP_STRIDE == past_len // CMP_STRIDE
                pool, table = past['cmp_pool'], past['page_table']
                win_all = jnp.concatenate([past['win_buf'], kv5(rows[:, :, 2])], axis=1)
                win_new = win_all[:, win_all.shape[1] - min(WINDOW, win_all.shape[1]):]
            kc, vc = compress_paged(pool, table, *P['cmp_stack'])
    return (rmsnorm(h, P['norm_final']), jnp.stack(gla_states), kv5(rows[:, :, 0]), kv5(rows[:, :, 1]), win_new)


def kernel(x_prompt, x_sample, state_gla, cache_cmp_kv, cache_slc_kv, cache_win_kv, page_table,
           norm_mix, norm_ffn, norm_kv, norm_final, gla_w_in, gla_w_gate_up, gla_b_gate, gla_norm, gla_w_out,
           kv_w, cmp_k_w1, cmp_k_b1, cmp_k_w2, cmp_k_b2, cmp_k_pe, cmp_v_w1, cmp_v_b1, cmp_v_w2, cmp_v_b2, cmp_v_pe,
           nsa_w_in, nsa_w_out, rel_bias, peer_w_q, peer_subkeys, peer_u, peer_v):
    P = dict(norm_mix=norm_mix, norm_ffn=norm_ffn, norm_kv=norm_kv, norm_final=norm_final,
             gla_w_in=gla_w_in, gla_w_gate_up=gla_w_gate_up, gla_b_gate=gla_b_gate, gla_norm=gla_norm,
             gla_w_out=gla_w_out, kv_w=kv_w,
             cmp_k_w1=cmp_k_w1, cmp_k_b1=cmp_k_b1, cmp_k_w2=cmp_k_w2, cmp_k_b2=cmp_k_b2, cmp_k_pe=cmp_k_pe,
             cmp_v_w1=cmp_v_w1, cmp_v_b1=cmp_v_b1, cmp_v_w2=cmp_v_w2, cmp_v_b2=cmp_v_b2, cmp_v_pe=cmp_v_pe,
             nsa_w_in=nsa_w_in, nsa_w_out=nsa_w_out, rel_bias=rel_bias,
             peer_w_q=peer_w_q, peer_subkeys=peer_subkeys,
             peer_u_bf=peer_u.astype(jnp.bfloat16), peer_v_bf=peer_v.astype(jnp.bfloat16))
    P['cmp_stack'] = compress_params(P)
    past = dict(cmp_pool=cache_cmp_kv, slc_pool=cache_slc_kv, win_buf=cache_win_kv,
                page_table=page_table.astype(jnp.int32))
    gla_zero = jnp.zeros((N_A_LAYERS, BATCH, GLA_HEADS, GLA_DK, GLA_DV), x_prompt.dtype)
    y_prompt, gla_p, cmp_p, slc_p, win_p = run_trunk(x_prompt, gla_zero, None, P)
    y_sample, gla_s, cmp_s, slc_s, win_s = run_trunk(x_sample, state_gla, past, P)
    return (y_prompt, y_sample, gla_p, gla_s, cmp_p, cmp_s, slc_p, slc_s, win_p, win_s)
```

```python
import functools
import math

import jax
import jax.numpy as jnp
from jax import lax
from jax.experimental import pallas as pl
from jax.experimental.pallas import tpu as pltpu

D_MODEL = 2048
BATCH = 4
SEQ = 2048
DEPTH = 2
DEC_BATCH = 32
DEC_SEQ = 8
PAST_LEN = 8192
PAGE_SIZE = 128
N_A_LAYERS = DEPTH // 2
N_B_LAYERS = DEPTH - N_A_LAYERS
GLA_HEADS = 4
GLA_DK = D_MODEL // (2 * GLA_HEADS)
GLA_DV = D_MODEL // GLA_HEADS
GLA_GATE_RANK = 16
GLA_TAU = 16.0
GLA_CHUNK = 64
NSA_HEADS = 16
NSA_GROUPS = 4
NSA_HPG = NSA_HEADS // NSA_GROUPS
NSA_HEAD_DIM = D_MODEL // NSA_HEADS
CMP_STRIDE = 16
CMP_BLOCK = 2 * CMP_STRIDE
CMP_HIDDEN = 2 * NSA_HEAD_DIM
SEL_BLOCK = 64
N_SEL = 16
WINDOW = 512
Q_BLOCK = 32
REL_BUCKETS = 32
REL_MAX_DIST = 128
PEER_HEADS = 8
PEER_NKEYS = 128
PEER_EXPERTS = PEER_NKEYS * PEER_NKEYS
PEER_TOPK = 16
PEER_KEY_DIM = 256
PEER_TOK_BLOCK = 128
EPS = 1e-6
NEG = -1e30
BIG = 1e30

LANE = 128
VMEM_LIMIT = 56 * 1024 * 1024
TQ = 128
assert REL_MAX_DIST <= TQ and WINDOW % TQ == 0 and TQ % SEL_BLOCK == 0


def _mm_kernel(a_ref, b_ref, o_ref):
    a = a_ref[...].astype(jnp.bfloat16)
    b = b_ref[...].astype(jnp.bfloat16)
    o_ref[...] = jnp.dot(a, b, preferred_element_type=jnp.float32)


def pmm(a, b, tm=512, tn=512, keep_cols=False):
    M, K = a.shape
    N = b.shape[1]
    tm = min(tm, M)
    Mp = -(-M // tm) * tm
    Np = -(-N // LANE) * LANE
    wide = [d * LANE for d in range(1, 2 * tn // LANE + 1) if Np % (d * LANE) == 0]
    if max(wide) < tn:
        Np = -(-N // tn) * tn
        wide = [tn]
    tn = max(wide)
    if Mp != M:
        a = jnp.pad(a, ((0, Mp - M), (0, 0)))
    if Np != N:
        b = jnp.pad(b, ((0, 0), (0, Np - N)))
    out = pl.pallas_call(
        _mm_kernel,
        grid=(Mp // tm, Np // tn),
        in_specs=[pl.BlockSpec((tm, K), lambda i, j: (i, 0)),
                  pl.BlockSpec((K, tn), lambda i, j: (0, j))],
        out_specs=pl.BlockSpec((tm, tn), lambda i, j: (i, j)),
        out_shape=jax.ShapeDtypeStruct((Mp, Np), jnp.float32),
        compiler_params=pltpu.CompilerParams(
            dimension_semantics=("parallel", "parallel"), vmem_limit_bytes=VMEM_LIMIT),
        name="proj_matmul",
    )(a, b)
    return out[:M] if keep_cols else out[:M, :N]


def mm(x, w):
    lead = x.shape[:-1]
    return pmm(x.reshape(-1, x.shape[-1]), w).reshape(lead + (w.shape[1],))


def rmsnorm(x, g):
    xf = x.astype(jnp.float32)
    y = xf * lax.rsqrt(jnp.mean(xf * xf, axis=-1, keepdims=True) + EPS)
    return (y * g.astype(jnp.float32)).astype(x.dtype)


def masked_softmax(s, mask):
    s = jnp.where(mask, s, NEG)
    m = jnp.max(s, axis=-1, keepdims=True)
    e = jnp.where(mask, jnp.exp(s - m), 0.0)
    return e / jnp.maximum(jnp.sum(e, axis=-1, keepdims=True), 1e-30)


def t5_bucket(rel):
    n = jnp.maximum(rel, 0)
    exact = REL_BUCKETS // 2
    nf = jnp.maximum(n, exact).astype(jnp.float32)
    large = exact + (jnp.log(nf / exact) / math.log(REL_MAX_DIST / exact) * (REL_BUCKETS - exact)).astype(jnp.int32)
    large = jnp.minimum(large, REL_BUCKETS - 1)
    return jnp.where(n < exact, n, large)


def head_bias(table, rel):
    onehot = jax.nn.one_hot(t5_bucket(rel), REL_BUCKETS, dtype=jnp.float32)
    b = jnp.einsum('qkn,nh->hqk', onehot, table.astype(jnp.float32), precision=lax.Precision.HIGHEST)
    return b.reshape(NSA_GROUPS, NSA_HPG, rel.shape[0], rel.shape[1])


DK_ALL = GLA_HEADS * GLA_DK
DV_ALL = GLA_HEADS * GLA_DV
GZ_COL = 2 * DK_ALL + 2 * DV_ALL
assert GZ_COL % LANE == 0 and GLA_GATE_RANK <= LANE and DK_ALL % LANE == 0 and (2 * DK_ALL) % DV_ALL == 0


def _gla_kernel(q_ref, k_ref, v_ref, r_ref, gz_ref, wg_ref, bg_ref, ng_ref, tri_ref, s0_ref,
                o_ref, sfin_ref, s_scr, *, C, t_valid):
    c = pl.program_id(1)

    @pl.when(c == 0)
    def _():
        s_scr[...] = s0_ref[0]

    row = lax.broadcasted_iota(jnp.int32, (C, C), 0)
    col = lax.broadcasted_iota(jnp.int32, (C, C), 1)
    bf = jnp.bfloat16
    z_all = jnp.dot(gz_ref[0].astype(bf), wg_ref[...], preferred_element_type=jnp.float32) + bg_ref[...]
    for h in range(GLA_HEADS):
        dk = slice(h * GLA_DK, (h + 1) * GLA_DK)
        dv = slice(h * GLA_DV, (h + 1) * GLA_DV)
        z = z_all[:, dk]
        g = (jnp.minimum(z, 0.0) - jnp.log(1.0 + jnp.exp(-jnp.abs(z)))) * (1.0 / GLA_TAU)
        if t_valid is not None:
            g = jnp.where(lax.broadcasted_iota(jnp.int32, g.shape, 0) < t_valid, g, 0.0)
        b = jnp.dot(tri_ref[...], g, preferred_element_type=jnp.float32, precision=lax.Precision.HIGHEST)
        b_last = b[C - 1:C, :]
        q = q_ref[0, :, dk] * (GLA_DK ** -0.5)
        k = k_ref[0, :, dk]
        v = v_ref[0, :, dv].astype(bf)
        qe = (q * jnp.exp(b)).astype(bf)
        ke = (k * jnp.exp(-b)).astype(bf)
        a = lax.dot_general(qe, ke, (((1,), (1,)), ((), ())), preferred_element_type=jnp.float32)
        a = jnp.where(col <= row, a, 0.0)
        s = s_scr[h]
        o = (jnp.dot(a.astype(bf), v, preferred_element_type=jnp.float32)
             + jnp.dot(qe, s.astype(bf), preferred_element_type=jnp.float32))
        kd = (k * jnp.exp(b_last - b)).astype(bf)
        dcol = jnp.exp(jnp.transpose(jnp.broadcast_to(b_last, (LANE, GLA_DK))))
        s_new = (s * jnp.concatenate([dcol] * (GLA_DV // LANE), axis=1)
                 + lax.dot_general(kd, v, (((0,), (0,)), ((), ())), preferred_element_type=jnp.float32))
        s_scr[h] = s_new
        o = o * lax.rsqrt(jnp.mean(o * o, axis=-1, keepdims=True) + EPS) * ng_ref[...]
        r = r_ref[0, :, dv]
        o_ref[0, :, dv] = o * (r * (1.0 / (1.0 + jnp.exp(-r))))

    @pl.when(c == pl.num_programs(1) - 1)
    def _():
        sfin_ref[0] = s_scr[...]


def gla_core(proj, s0, w_gate_up, b_gate, norm_g, T):
    B, Tp, _ = proj.shape
    C = GLA_CHUNK if T % GLA_CHUNK == 0 else Tp
    assert Tp % C == 0 and C % 16 == 0 and (Tp == T or Tp == C)
    n = Tp // C
    wg = jnp.pad(w_gate_up, ((0, LANE - GLA_GATE_RANK), (0, 0))).astype(jnp.bfloat16)
    tri = (jnp.arange(C)[:, None] >= jnp.arange(C)[None, :]).astype(jnp.float32)
    col_blk = lambda w, i: pl.BlockSpec((1, C, w), lambda b, c: (b, c, i))
    full = lambda a: pl.BlockSpec(a.shape, lambda b, c: (0,) * a.ndim)
    bg = b_gate.reshape(1, DK_ALL)
    ng = norm_g.reshape(1, GLA_DV)
    st_spec = pl.BlockSpec((1, GLA_HEADS, GLA_DK, GLA_DV), lambda b, c: (b, 0, 0, 0))
    return pl.pallas_call(
        functools.partial(_gla_kernel, C=C, t_valid=None if Tp == T else T),
        grid=(B, n),
        in_specs=[col_blk(DK_ALL, 0), col_blk(DK_ALL, 1), col_blk(DV_ALL, 2 * DK_ALL // DV_ALL),
                  col_blk(DV_ALL, (2 * DK_ALL + DV_ALL) // DV_ALL), col_blk(LANE, GZ_COL // LANE),
                  full(wg), full(bg), full(ng), full(tri), st_spec],
        out_specs=[pl.BlockSpec((1, C, DV_ALL), lambda b, c: (b, c, 0)), st_spec],
        out_shape=[jax.ShapeDtypeStruct((B, Tp, DV_ALL), jnp.float32),
                   jax.ShapeDtypeStruct((B, GLA_HEADS, GLA_DK, GLA_DV), jnp.float32)],
        scratch_shapes=[pltpu.VMEM((GLA_HEADS, GLA_DK, GLA_DV), jnp.float32)],
        compiler_params=pltpu.CompilerParams(dimension_semantics=("parallel", "arbitrary"),
                                             vmem_limit_bytes=VMEM_LIMIT),
        name="gla_core",
    )(proj, proj, proj, proj, proj, wg, bg, ng, tri, s0)


def gla_mixer(xn, s0, w_in, w_gate_up, b_gate, norm_g, w_out):
    B, T, D = xn.shape
    proj = pmm(xn.reshape(B * T, D), w_in, keep_cols=True)
    assert proj.shape[1] >= GZ_COL + LANE
    Tp = -(-T // 16) * 16
    proj = jnp.pad(proj.reshape(B, T, -1), ((0, 0), (0, Tp - T), (0, 0)))
    o, s_new = gla_core(proj, s0.astype(jnp.float32), w_gate_up, b_gate, norm_g, T)
    return mm(o[:, :T], w_out), s_new.astype(s0.dtype)


WSUM_UNROLL = 8
PEER_CHUNK = 512


def _wsum_kernel(a_ref, b_ref, g_ref, o_ref, *, tb):
    sub = lax.broadcasted_iota(jnp.int32, (PEER_NKEYS, PEER_NKEYS), 0)

    def body(tt, c):
        t0 = pl.multiple_of(tt * WSUM_UNROLL, WSUM_UNROLL)
        a8 = a_ref[pl.ds(t0, WSUM_UNROLL), :]
        b8 = b_ref[pl.ds(t0, WSUM_UNROLL), :]
        g8 = g_ref[pl.ds(t0, WSUM_UNROLL), :]
        for u in range(WSUM_UNROLL):
            at = jnp.where(sub == a8[u:u + 1], 1.0, 0.0).astype(jnp.bfloat16)
            bt = jnp.where(sub == b8[u:u + 1], g8[u:u + 1], 0.0).astype(jnp.bfloat16)
            o_ref[t0 + u] = lax.dot_general(at, bt, (((1,), (1,)), ((), ())), preferred_element_type=jnp.float32)
        return c

    lax.fori_loop(0, tb // WSUM_UNROLL, body, 0)


def peer_wsum(i1, i2, gate, tb=64):
    n, K = i1.shape
    tb = math.gcd(n, tb)
    assert tb % WSUM_UNROLL == 0
    return pl.pallas_call(
        functools.partial(_wsum_kernel, tb=tb),
        grid=(n // tb,),
        in_specs=[pl.BlockSpec((tb, K), lambda i: (i, 0))] * 3,
        out_specs=pl.BlockSpec((tb, PEER_NKEYS, PEER_NKEYS), lambda i: (i, 0, 0)),
        out_shape=jax.ShapeDtypeStruct((n, PEER_NKEYS, PEER_NKEYS), jnp.float32),
        compiler_params=pltpu.CompilerParams(dimension_semantics=("parallel",), vmem_limit_bytes=VMEM_LIMIT),
        name="peer_wsum",
    )(i1, i2, gate)


def _peer_kernel(x_ref, u_ref, v_ref, w_ref, o_ref, *, te):
    j = pl.program_id(1)
    x = x_ref[...]
    acc = None
    for c in range(te // PEER_CHUNK):
        e0 = c * PEER_CHUNK
        hid = lax.dot_general(x, u_ref[e0:e0 + PEER_CHUNK, :], (((1,), (1,)), ((), ())),
                              preferred_element_type=jnp.float32)
        parts = []
        for r in range(PEER_CHUNK // PEER_NKEYS):
            hr = hid[:, r * PEER_NKEYS:(r + 1) * PEER_NKEYS]
            ar = 0.5 * hr * (1.0 + lax.erf(hr * (2.0 ** -0.5))) * w_ref[:, e0 // PEER_NKEYS + r, :]
            parts.append(ar.astype(jnp.bfloat16))
        d = jnp.dot(jnp.concatenate(parts, axis=1), v_ref[e0:e0 + PEER_CHUNK, :], preferred_element_type=jnp.float32)
        acc = d if acc is None else acc + d
    @pl.when(j == 0)
    def _():
        o_ref[...] = acc

    @pl.when(j > 0)
    def _():
        o_ref[...] += acc


def peer_dense(xn, u, v, wsum, tb=1024, te=1024):
    n, D = xn.shape
    E = u.shape[0]
    tb = math.gcd(n, tb)
    return pl.pallas_call(
        functools.partial(_peer_kernel, te=te),
        grid=(n // tb, E // te),
        in_specs=[pl.BlockSpec((tb, D), lambda i, j: (i, 0), pipeline_mode=pl.Buffered(1)),
                  pl.BlockSpec((te, D), lambda i, j: (j, 0)),
                  pl.BlockSpec((te, D), lambda i, j: (j, 0)),
                  pl.BlockSpec((tb, te // PEER_NKEYS, PEER_NKEYS), lambda i, j: (i, j, 0))],
        out_specs=pl.BlockSpec((tb, D), lambda i, j: (i, 0), pipeline_mode=pl.Buffered(1)),
        out_shape=jax.ShapeDtypeStruct((n, D), jnp.float32),
        compiler_params=pltpu.CompilerParams(dimension_semantics=("parallel", "arbitrary"),
                                             vmem_limit_bytes=VMEM_LIMIT),
        name="peer_dense",
    )(xn, u, v, wsum)


RT = LANE
NHC = 2 * PEER_HEADS
_CAND_GROUPS = [(0, 0), (0, 8), (1, 0)] + [(a, 0) for a in range(2, 8)] + [(-1, 0)]
assert PEER_TOPK == 16 and PEER_KEY_DIM // 2 == LANE and PEER_NKEYS == LANE


def _top_rounds(s, n_rounds, extra=()):
    R = s.shape[0]
    iota = lax.broadcasted_iota(jnp.int32, s.shape, 0)
    vals, idxs, ex = [], [], [[] for _ in extra]
    for _ in range(n_rounds):
        m = jnp.max(s, axis=0, keepdims=True)
        idx = jnp.min(jnp.where(s == m, iota, R), axis=0, keepdims=True)
        sel = iota == idx
        for e, lst in zip(extra, ex):
            lst.append(jnp.max(jnp.where(sel, e, -1), axis=0, keepdims=True))
        s = jnp.where(sel, -jnp.inf, s)
        vals.append(m)
        idxs.append(idx)
    cat = lambda l: jnp.concatenate(l, axis=0)
    return cat(vals), cat(idxs), [cat(l) for l in ex]


def _route_kernel(q_ref, sub_ref, e1_ref, e2_ref, g_ref, v_scr, i_scr):
    K = PEER_TOPK

    def stage1(h, c):
        for hc in (2 * h, 2 * h + 1):
            off = pl.multiple_of(hc * LANE, LANE)
            qb = q_ref[:, pl.ds(off, LANE)].astype(jnp.bfloat16)
            s = lax.dot_general(sub_ref[hc], qb, (((1,), (1,)), ((), ())), preferred_element_type=jnp.float32)
            v, i, _ = _top_rounds(s, K)
            v_scr[hc] = v
            i_scr[hc] = i
        return c

    lax.fori_loop(0, PEER_HEADS, stage1, 0)

    row8 = lax.broadcasted_iota(jnp.int32, (8, RT), 0)

    def stage2(h):
        v1, v2 = v_scr[2 * h], v_scr[2 * h + 1]
        i1, i2 = i_scr[2 * h], i_scr[2 * h + 1]
        cand, c1, c2 = [], [], []
        for a, b0 in _CAND_GROUPS:
            if a >= 0:
                nb = K // (a + 1)
                sm = v1[a:a + 1] + v2[b0:b0 + 8]
                if nb - b0 < 8:
                    sm = jnp.where(row8 < nb - b0, sm, -jnp.inf)
                cand.append(sm)
                c1.append(jnp.broadcast_to(i1[a:a + 1], (8, RT)))
                c2.append(i2[b0:b0 + 8])
            else:
                cand.append(v1[8:16] + v2[0:1])
                c1.append(i1[8:16])
                c2.append(jnp.broadcast_to(i2[0:1], (8, RT)))
        cat = lambda l: jnp.concatenate(l, axis=0)
        top, _, (e12,) = _top_rounds(cat(cand), K, extra=(cat(c1) * PEER_NKEYS + cat(c2),))
        ex = jnp.exp(top - top[0:1])
        g = ex / jnp.sum(ex, axis=0, keepdims=True)
        r0 = pl.multiple_of(h * K, K)
        e1_ref[0, pl.ds(r0, K), :] = e12 // PEER_NKEYS
        e2_ref[0, pl.ds(r0, K), :] = e12 % PEER_NKEYS
        g_ref[0, pl.ds(r0, K), :] = g

    def stage2_pair(hp, c):
        stage2(2 * hp)
        stage2(2 * hp + 1)
        return c

    lax.fori_loop(0, PEER_HEADS // 2, stage2_pair, 0)


def peer_route(q, sub_bf):
    n = q.shape[0]
    assert n % RT == 0
    nb = n // RT
    slots = PEER_HEADS * PEER_TOPK
    out = jax.ShapeDtypeStruct((nb, slots, RT), jnp.int32)
    ospec = pl.BlockSpec((1, slots, RT), lambda i: (i, 0, 0))
    return pl.pallas_call(
        _route_kernel,
        grid=(nb,),
        in_specs=[pl.BlockSpec((RT, q.shape[1]), lambda i: (i, 0)),
                  pl.BlockSpec(sub_bf.shape, lambda i: (0, 0, 0))],
        out_specs=[ospec, ospec, ospec],
        out_shape=[out, out, jax.ShapeDtypeStruct((nb, slots, RT), jnp.float32)],
        scratch_shapes=[pltpu.VMEM((NHC, PEER_TOPK, RT), jnp.float32),
                        pltpu.VMEM((NHC, PEER_TOPK, RT), jnp.int32)],
        compiler_params=pltpu.CompilerParams(dimension_semantics=("parallel",), vmem_limit_bytes=VMEM_LIMIT),
        name="peer_route",
    )(q, sub_bf)


def peer_ffn(h, xn, w_q, subkeys, u_bf, v_bf):
    B, T, D = xn.shape
    n = B * T
    xt = xn.reshape(n, D)
    q = pmm(xt, w_q)
    sub_bf = subkeys.astype(jnp.bfloat16).reshape(NHC, PEER_NKEYS, PEER_KEY_DIM // 2)
    e1, e2, gate = peer_route(q, sub_bf)
    tok_major = lambda t: jnp.transpose(t, (0, 2, 1)).reshape(n, PEER_HEADS * PEER_TOPK)
    wsum = peer_wsum(tok_major(e1), tok_major(e2), tok_major(gate))
    out = peer_dense(xt.astype(jnp.bfloat16), u_bf, v_bf, wsum)
    return h + out.reshape(B, T, D)


KVG = 2 * NSA_GROUPS
ROW_W = KVG * NSA_HEAD_DIM
CPP = PAGE_SIZE // CMP_STRIDE
PPS = 8
assert CPP == 8 and PAGE_SIZE == LANE and NSA_HEAD_DIM == LANE


def _compress_kernel(pt_ref, *refs, n_steps):
    pages = refs[:PPS]
    perm_ref, w1_ref, c1_ref, w2_ref, b2_ref, kc_ref, vc_ref, xc_ref = refs[PPS:]
    s_idx = pl.program_id(1)
    for pair in range(PPS // 2):
        row0 = pl.multiple_of((s_idx * (PPS // 2) + pair) * 2 * CPP, 2 * CPP)
        for kvg in range(KVG):
            kv, g = divmod(kvg, NSA_GROUPS)
            x2 = jnp.concatenate([pages[2 * pair][0, :, kv, g, :], pages[2 * pair + 1][0, :, kv, g, :]],
                                 axis=0).astype(jnp.bfloat16)
            y = jnp.dot(perm_ref[...], x2, preferred_element_type=jnp.float32).astype(jnp.bfloat16)
            for s in range(CMP_STRIDE):
                xc_ref[kvg, pl.ds(row0, 2 * CPP), pl.ds(s * LANE, LANE)] = y[s * 2 * CPP:(s + 1) * 2 * CPP]

    @pl.when(s_idx == n_steps - 1)
    def _():
        nch = xc_ref.shape[1]
        for kv in range(2):
            out_ref = kc_ref if kv == 0 else vc_ref
            for g in range(NSA_GROUPS):
                hh = jnp.dot(xc_ref[kv * NSA_GROUPS + g], w1_ref[kv], preferred_element_type=jnp.float32)
                h1 = hh[:, :CMP_HIDDEN]
                h2 = pltpu.roll(hh[:, CMP_HIDDEN:], nch - 1, 0)
                x = h1 + h2 + c1_ref[kv]
                hid = 0.5 * x * (1.0 + lax.erf(x * (2.0 ** -0.5)))
                o = jnp.dot(hid.astype(jnp.bfloat16), w2_ref[kv], preferred_element_type=jnp.float32) + b2_ref[kv]
                out_ref[0, g] = o.astype(out_ref.dtype)


def compress_paged(pool, page_table, w1, c1, w2, b2):
    B, NP = page_table.shape
    assert NP % PPS == 0
    n_steps = NP // PPS
    nch = NP * CPP
    dh = NSA_HEAD_DIM
    r = jnp.arange(2 * PAGE_SIZE)
    s_, u_, c_ = r // (2 * CPP), (r // CPP) % 2, r % CPP
    perm = (r[None, :] == (u_ * PAGE_SIZE + c_ * CMP_STRIDE + s_)[:, None]).astype(jnp.bfloat16)

    def page_spec(u):
        return pl.BlockSpec((1, PAGE_SIZE, 2, NSA_GROUPS, NSA_HEAD_DIM),
                            lambda b, s, pt: (pt[b, s * PPS + u], 0, 0, 0, 0))

    full = lambda a: pl.BlockSpec(a.shape, lambda b, s, pt: (0,) * a.ndim)
    out_spec = pl.BlockSpec((1, NSA_GROUPS, nch, dh), lambda b, s, pt: (b, 0, 0, 0))
    grid_spec = pltpu.PrefetchScalarGridSpec(
        num_scalar_prefetch=1,
        grid=(B, n_steps),
        in_specs=[page_spec(u) for u in range(PPS)] + [full(perm), full(w1), full(c1), full(w2), full(b2)],
        out_specs=[out_spec, out_spec],
        scratch_shapes=[pltpu.VMEM((KVG, nch, CMP_STRIDE * dh), jnp.bfloat16)],
    )
    out = jax.ShapeDtypeStruct((B, NSA_GROUPS, nch, dh), jnp.bfloat16)
    return pl.pallas_call(
        functools.partial(_compress_kernel, n_steps=n_steps),
        grid_spec=grid_spec,
        out_shape=[out, out],
        compiler_params=pltpu.CompilerParams(dimension_semantics=("parallel", "arbitrary"),
                                             vmem_limit_bytes=VMEM_LIMIT),
        name="nsa_compress",
    )(page_table, *([pool] * PPS), perm, w1, c1, w2, b2)


def compress_params(P):
    dh, F = NSA_HEAD_DIM, CMP_HIDDEN
    w1s, c1s, w2s, b2s = [], [], [], []
    for n in ('k', 'v'):
        w1h = P['cmp_%s_w1' % n].reshape(2, CMP_STRIDE * dh, F)
        pe = P['cmp_%s_pe' % n].reshape(2, CMP_STRIDE * dh)
        w1s.append(jnp.concatenate([w1h[0], w1h[1]], axis=1))
        c1s.append(P['cmp_%s_b1' % n] + jnp.einsum('hk,hkf->f', pe, w1h, precision=lax.Precision.HIGHEST))
        w2s.append(P['cmp_%s_w2' % n])
        b2s.append(P['cmp_%s_b2' % n])
    return (jnp.stack(w1s).astype(jnp.bfloat16), jnp.stack(c1s)[:, None, :],
            jnp.stack(w2s).astype(jnp.bfloat16), jnp.stack(b2s)[:, None, :])


def _softmax_rows(s, valid):
    s = jnp.where(valid, s, NEG)
    m = jnp.max(s, axis=-1, keepdims=True)
    e = jnp.where(valid, jnp.exp(s - m), 0.0)
    return e / jnp.maximum(jnp.sum(e, axis=-1, keepdims=True), 1e-30)


HQ = NSA_HPG * TQ
GATE_ROWS = 8


def _lanes4(x):
    return jnp.concatenate([x] * NSA_HPG, axis=1)


def _nsa_cmp_sel_kernel(q_ref, kc_ref, vct_ref, bias_ref, ovt_ref, oct_ref, msel_ref, *, nc, ns):
    i = pl.program_id(2)
    ncp = kc_ref.shape[2]
    nsp = ovt_ref.shape[0]
    q_all = q_ref[0, 0].reshape(HQ, NSA_HEAD_DIM)
    cidx = lax.broadcasted_iota(jnp.int32, (ncp, TQ), 0)
    qpos = i * TQ + lax.broadcasted_iota(jnp.int32, (ncp, TQ), 1)
    valid = _lanes4((cidx * CMP_STRIDE + (CMP_BLOCK - 1) <= qpos) & (cidx < nc))
    s = lax.dot_general(kc_ref[0, 0], q_all, (((1,), (1,)), ((), ())), preferred_element_type=jnp.float32)
    s = jnp.where(valid, s + bias_ref[0, 0], NEG)
    m = jnp.max(s, axis=0, keepdims=True)
    e = jnp.where(valid, jnp.exp(s - m), 0.0)
    p = e / jnp.maximum(jnp.sum(e, axis=0, keepdims=True), 1e-30)
    oct_ref[0, 0, 0] = jnp.dot(vct_ref[0, 0], p.astype(jnp.bfloat16), preferred_element_type=jnp.float32)
    psum = p[:, 0:TQ]
    for hh in range(1, NSA_HPG):
        psum = psum + p[:, hh * TQ:(hh + 1) * TQ]
    imp = jnp.dot(ovt_ref[...], psum.astype(jnp.bfloat16), preferred_element_type=jnp.float32)
    blk = lax.broadcasted_iota(jnp.int32, (nsp, TQ), 0)
    qp = i * TQ + lax.broadcasted_iota(jnp.int32, (nsp, TQ), 1)
    cur = qp // SEL_BLOCK
    ok_blk = (blk * SEL_BLOCK <= qp) & (blk < ns)
    forced = (blk == 0) | (blk == cur) | (blk == cur - 1)
    score = jnp.where(ok_blk, jnp.where(forced, BIG, imp), -BIG)
    score = jnp.where(blk < ns, score, -jnp.inf)
    msel = jnp.zeros((nsp, TQ), jnp.float32)
    for _ in range(min(N_SEL, ns)):
        mx = jnp.max(score, axis=0, keepdims=True)
        idx = jnp.min(jnp.where(score == mx, blk, nsp), axis=0, keepdims=True)
        sel = blk == idx
        msel = jnp.where(sel & (mx > -0.5 * BIG), 1.0, msel)
        score = jnp.where(sel, -jnp.inf, score)
    msel_ref[0, 0] = msel


def nsa_cmp_sel(qh, kc, vct, bias_ct, ovt, nc, ns):
    B, G, HPG, T, dh = qh.shape
    ncp = kc.shape[2]
    nsp = ovt.shape[0]
    nqt = T // TQ
    return pl.pallas_call(
        functools.partial(_nsa_cmp_sel_kernel, nc=nc, ns=ns),
        grid=(B, G, nqt),
        in_specs=[pl.BlockSpec((1, 1, HPG, TQ, dh), lambda b, g, i: (b, g, 0, i, 0)),
                  pl.BlockSpec((1, 1, ncp, dh), lambda b, g, i: (b, g, 0, 0)),
                  pl.BlockSpec((1, 1, dh, ncp), lambda b, g, i: (b, g, 0, 0)),
                  pl.BlockSpec((1, 1, ncp, HQ), lambda b, g, i: (g, i, 0, 0)),
                  pl.BlockSpec((nsp, ncp), lambda b, g, i: (0, 0))],
        out_specs=[pl.BlockSpec((1, 1, 1, dh, HQ), lambda b, g, i: (b, g, i, 0, 0)),
                   pl.BlockSpec((1, 1, nsp, TQ), lambda b, g, i: (b, g, 0, i))],
        out_shape=[jax.ShapeDtypeStruct((B, G, nqt, dh, HQ), jnp.float32),
                   jax.ShapeDtypeStruct((B, G, nsp, T), jnp.float32)],
        compiler_params=pltpu.CompilerParams(dimension_semantics=("parallel", "parallel", "parallel"),
                                             vmem_limit_bytes=VMEM_LIMIT),
        name="nsa_cmp_sel",
    )(qh, kc, vct, bias_ct, ovt)


def _flash_tile_t(q_all, k, vt, bias, valid, m_ref, l_ref, acc_ref):
    s = lax.dot_general(k, q_all, (((1,), (1,)), ((), ())), preferred_element_type=jnp.float32)
    s = jnp.where(valid, s + bias, NEG)
    m_old = m_ref[...]
    m_new = jnp.maximum(m_old, jnp.max(s, axis=0, keepdims=True))
    p = jnp.where(valid, jnp.exp(s - m_new), 0.0)
    alpha = jnp.exp(m_old - m_new)
    l_ref[...] = alpha * l_ref[...] + jnp.sum(p, axis=0, keepdims=True)
    acc_ref[...] = alpha * acc_ref[...] + jnp.dot(vt, p.astype(jnp.bfloat16), preferred_element_type=jnp.float32)
    m_ref[...] = m_new


def _nsa_sw_kernel(q_ref, ks_ref, vst_ref, kw_ref, vwt_ref, msel_ref, expt_ref, bias_ref, selm_ref, winm_ref,
                   oct_ref, gate_ref, o_ref, msk_ref, m_ref, l_ref, acc_ref, *, nt):
    i = pl.program_id(2)
    q_all = q_ref[0, 0].reshape(HQ, NSA_HEAD_DIM)
    msel = msel_ref[0, 0].astype(jnp.bfloat16)
    for j in range(nt):
        msk_ref[j] = jnp.dot(expt_ref[j], msel, preferred_element_type=jnp.float32)

    def reset():
        m_ref[...] = jnp.full_like(m_ref, NEG)
        l_ref[...] = jnp.zeros_like(l_ref)
        acc_ref[...] = jnp.zeros_like(acc_ref)

    def result():
        return acc_ref[...] / jnp.maximum(l_ref[...], 1e-30)

    def pair(k_ref, vt_ref, j1, dd1, valid_of):
        j0 = j1 - 1
        j0c = jnp.maximum(j0, 0)
        k2 = jnp.concatenate([k_ref[0, 0, j1], k_ref[0, 0, j0c]], axis=0)
        vt2 = jnp.concatenate([vt_ref[0, 0, j1], vt_ref[0, 0, j0c]], axis=1)
        bias2 = jnp.concatenate([bias_ref[0, jnp.minimum(dd1, 2)], bias_ref[0, jnp.minimum(dd1 + 1, 2)]], axis=0)
        valid2 = _lanes4(jnp.concatenate([valid_of(j1, dd1), valid_of(j0c, dd1 + 1) & (j0 >= 0)], axis=0))
        _flash_tile_t(q_all, k2, vt2, bias2, valid2, m_ref, l_ref, acc_ref)

    reset()

    def sel_valid(j, dd):
        return (msk_ref[j] * selm_ref[jnp.minimum(dd, 2)]) > 0.5

    def sel_body(jj, c):
        pair(ks_ref, vst_ref, i - 2 * jj, 2 * jj, sel_valid)
        return c

    lax.fori_loop(0, i // 2 + 1, sel_body, 0)
    mix = gate_ref[0, 0, 0, 0:1, :] * oct_ref[0, 0, 0] + gate_ref[0, 0, 0, 1:2, :] * result()
    reset()
    for pp in range(WINDOW // TQ // 2 + 1):
        j1 = i - 2 * pp
        pair(kw_ref, vwt_ref, jnp.maximum(j1, 0), 2 * pp, lambda j, dd: (winm_ref[dd] > 0.5) & (j1 >= 0))
    mix = mix + gate_ref[0, 0, 0, 2:3, :] * result()
    for hh in range(NSA_HPG):
        o_ref[0, :, hh * NSA_HEAD_DIM:(hh + 1) * NSA_HEAD_DIM] = jnp.transpose(mix[:, hh * TQ:(hh + 1) * TQ])


def nsa_sel_win(qh, ks, vst, kw, vwt, msel_t, exp_t, bias_tt, selm_t, winm_t, oct, gates_t):
    B, G, HPG, T, dh = qh.shape
    nt = T // TQ
    nsp = msel_t.shape[2]
    k_spec = pl.BlockSpec((1, 1, nt, TQ, dh), lambda b, g, i: (b, g, 0, 0, 0))
    vt_spec = pl.BlockSpec((1, 1, nt, dh, TQ), lambda b, g, i: (b, g, 0, 0, 0))
    return pl.pallas_call(
        functools.partial(_nsa_sw_kernel, nt=nt),
        grid=(B, G, nt),
        in_specs=[pl.BlockSpec((1, 1, HPG, TQ, dh), lambda b, g, i: (b, g, 0, i, 0)),
                  k_spec, vt_spec, k_spec, vt_spec,
                  pl.BlockSpec((1, 1, nsp, TQ), lambda b, g, i: (b, g, 0, i)),
                  pl.BlockSpec((nt, TQ, nsp), lambda b, g, i: (0, 0, 0)),
                  pl.BlockSpec((1, 3, TQ, HQ), lambda b, g, i: (g, 0, 0, 0)),
                  pl.BlockSpec((3, TQ, TQ), lambda b, g, i: (0, 0, 0)),
                  pl.BlockSpec(winm_t.shape, lambda b, g, i: (0, 0, 0)),
                  pl.BlockSpec((1, 1, 1, dh, HQ), lambda b, g, i: (b, g, i, 0, 0)),
                  pl.BlockSpec((1, 1, 1, GATE_ROWS, HQ), lambda b, g, i: (b, g, i, 0, 0))],
        out_specs=pl.BlockSpec((1, TQ, HPG * dh), lambda b, g, i: (b, i, g)),
        out_shape=jax.ShapeDtypeStruct((B, T, G * HPG * dh), jnp.float32),
        scratch_shapes=[pltpu.VMEM((nt, TQ, TQ), jnp.float32),
                        pltpu.VMEM((1, HQ), jnp.float32),
                        pltpu.VMEM((1, HQ), jnp.float32),
                        pltpu.VMEM((dh, HQ), jnp.float32)],
        compiler_params=pltpu.CompilerParams(dimension_semantics=("parallel", "parallel", "arbitrary"),
                                             vmem_limit_bytes=VMEM_LIMIT),
        name="nsa_sel_win",
    )(qh, ks, vst, kw, vwt, msel_t, exp_t, bias_tt, selm_t, winm_t, oct, gates_t)


def nsa_prompt(q, gates, kc, vc, ks, vs, kw_rows, vw_rows, table):
    B, T = q.shape[0], q.shape[1]
    G, HPG, dh = NSA_GROUPS, NSA_HPG, NSA_HEAD_DIM
    Nc = T // CMP_STRIDE - 1
    Ns = -(-T // SEL_BLOCK)
    ncp = kc.shape[2]
    assert ncp % LANE == 0 and ncp >= Nc
    nsp = -(-Ns // 16) * 16
    nqt = T // TQ
    bf = jnp.bfloat16
    qh = jnp.transpose(q.reshape(B, T, G, HPG, dh), (0, 2, 3, 1, 4)).astype(bf)
    qpos = jnp.arange(T)
    cidx = jnp.arange(ncp)
    bias_c = head_bias(table, qpos[:, None] - (cidx * CMP_STRIDE + (CMP_BLOCK - 1))[None, :])
    bias_ct = jnp.transpose(bias_c.reshape(G, HPG, nqt, TQ, ncp), (0, 2, 4, 1, 3)).reshape(G, nqt, ncp, HQ)
    sidx = jnp.arange(nsp)
    ovt = (((cidx * CMP_STRIDE)[None, :] < (sidx * SEL_BLOCK + SEL_BLOCK)[:, None])
           & ((cidx * CMP_STRIDE + CMP_BLOCK)[None, :] > (sidx * SEL_BLOCK)[:, None])
           & (cidx[None, :] < Nc) & (sidx[:, None] < Ns)).astype(bf)
    oct, msel_t = nsa_cmp_sel(qh, kc, jnp.transpose(vc, (0, 1, 3, 2)), bias_ct, ovt, Nc, Ns)
    exp_t = ((jnp.arange(T) // SEL_BLOCK)[:, None] == sidx[None, :]).astype(bf).reshape(nqt, TQ, nsp)
    r = jnp.arange(TQ)
    rel3 = (jnp.arange(3) * TQ)[:, None, None] + r[None, None, :] - r[None, :, None]
    bias_tt = jnp.transpose(table[t5_bucket(rel3)].astype(jnp.float32), (3, 0, 1, 2))
    bias_tt = jnp.transpose(bias_tt.reshape(G, HPG, 3, TQ, TQ), (0, 2, 3, 1, 4)).reshape(G, 3, TQ, HQ)
    selm_t = (rel3 >= 0).astype(jnp.float32)
    nw = 2 * (WINDOW // TQ // 2 + 1)
    relw = (jnp.arange(nw) * TQ)[:, None, None] + r[None, None, :] - r[None, :, None]
    winm_t = ((relw >= 0) & (relw < WINDOW)).astype(jnp.float32)

    def k_tiles(t):
        return jnp.transpose(t, (0, 2, 1, 3)).astype(bf).reshape(B, G, nqt, TQ, dh)

    def vt_tiles(t):
        return jnp.transpose(t.astype(bf).reshape(B, nqt, TQ, G, dh), (0, 3, 1, 4, 2))

    gates_t = jnp.transpose(gates.reshape(B, nqt, TQ, G, HPG, 3), (0, 3, 1, 5, 4, 2)).reshape(B, G, nqt, 3, HQ)
    gates_t = jnp.pad(gates_t, ((0, 0), (0, 0), (0, 0), (0, GATE_ROWS - 3), (0, 0)))
    return nsa_sel_win(qh, k_tiles(ks), vt_tiles(vs), k_tiles(kw_rows), vt_tiles(vw_rows), msel_t, exp_t, bias_tt,
                       selm_t, winm_t, oct, gates_t)


def _dec_cmp_kernel(q_ref, kc_ref, vc_ref, bias_ref, valid_ref, ov_ref, oc_ref, imp_ref, *, t):
    valid = valid_ref[...] > 0.5
    for g in range(NSA_GROUPS):
        s = lax.dot_general(q_ref[0, g], kc_ref[0, g], (((1,), (1,)), ((), ())), preferred_element_type=jnp.float32)
        p = _softmax_rows(s + bias_ref[g], valid)
        oc_ref[0, g] = jnp.dot(p.astype(jnp.bfloat16), vc_ref[0, g], preferred_element_type=jnp.float32)
        psum = p[0:t]
        for hh in range(1, NSA_HPG):
            psum = psum + p[hh * t:(hh + 1) * t]
        imp_ref[0, g] = jnp.dot(psum.astype(jnp.bfloat16), ov_ref[...], preferred_element_type=jnp.float32)


def dec_cmp(qg, kc, vc, bias_c, valid_c, overlap, t):
    B, G, R_, dh = qg.shape
    ncp, nsp = overlap.shape
    per_b = lambda a: pl.BlockSpec((1,) + a.shape[1:], lambda b: (b,) + (0,) * (a.ndim - 1))
    full = lambda a: pl.BlockSpec(a.shape, lambda b: (0,) * a.ndim)
    return pl.pallas_call(
        functools.partial(_dec_cmp_kernel, t=t),
        grid=(B,),
        in_specs=[per_b(qg), per_b(kc), per_b(vc), full(bias_c), full(valid_c), full(overlap)],
        out_specs=[pl.BlockSpec((1, G, R_, dh), lambda b: (b, 0, 0, 0)),
                   pl.BlockSpec((1, G, t, nsp), lambda b: (b, 0, 0, 0))],
        out_shape=[jax.ShapeDtypeStruct((B, G, R_, dh), jnp.float32),
                   jax.ShapeDtypeStruct((B, G, t, nsp), jnp.float32)],
        compiler_params=pltpu.CompilerParams(dimension_semantics=("parallel",), vmem_limit_bytes=VMEM_LIMIT),
        name="nsa_dec_cmp",
    )(qg, kc, vc, bias_c, valid_c, overlap)


def _flash_update(s, valid, v, m_ref, l_ref, acc_ref, g):
    s = jnp.where(valid, s, NEG)
    m_old = m_ref[g]
    m_new = jnp.maximum(m_old, jnp.max(s, axis=-1, keepdims=True))
    p = jnp.where(valid, jnp.exp(s - m_new), 0.0)
    alpha = jnp.exp(m_old - m_new)
    l_ref[g] = alpha * l_ref[g] + jnp.sum(p, axis=-1, keepdims=True)
    acc_ref[g] = alpha * acc_ref[g] + jnp.dot(p.astype(jnp.bfloat16), v, preferred_element_type=jnp.float32)
    m_ref[g] = m_new


def _dec_sel_kernel(pt_ref, q_ref, *refs, n_steps, t):
    pages = refs[:PPS]
    new_ref, bias_ref, mask_ref, biasn_ref, maskn_ref, o_ref, m_ref, l_ref, acc_ref = refs[PPS:]
    s_idx = pl.program_id(1)
    dh = NSA_HEAD_DIM

    @pl.when(s_idx == 0)
    def _():
        m_ref[...] = jnp.full_like(m_ref, NEG)
        l_ref[...] = jnp.zeros_like(l_ref)
        acc_ref[...] = jnp.zeros_like(acc_ref)

    def tile_mask(mref, g):
        mk = mref[0, g] > 0.5
        return jnp.concatenate([mk] * NSA_HPG, axis=0)

    for g in range(NSA_GROUPS):
        k = jnp.concatenate([pg[0, :, 0, g, :] for pg in pages], axis=0).astype(jnp.bfloat16)
        v = jnp.concatenate([pg[0, :, 1, g, :] for pg in pages], axis=0).astype(jnp.bfloat16)
        s = lax.dot_general(q_ref[0, g], k, (((1,), (1,)), ((), ())), preferred_element_type=jnp.float32)
        _flash_update(s + bias_ref[g], tile_mask(mask_ref, g), v, m_ref, l_ref, acc_ref, g)

    @pl.when(s_idx == n_steps - 1)
    def _():
        for g in range(NSA_GROUPS):
            k = new_ref[0, :, pl.ds(g * dh, dh)].astype(jnp.bfloat16)
            v = new_ref[0, :, pl.ds((NSA_GROUPS + g) * dh, dh)].astype(jnp.bfloat16)
            s = lax.dot_general(q_ref[0, g], k, (((1,), (1,)), ((), ())), preferred_element_type=jnp.float32)
            _flash_update(s + biasn_ref[g], tile_mask(maskn_ref, g), v, m_ref, l_ref, acc_ref, g)
            o_ref[0, g] = acc_ref[g] / jnp.maximum(l_ref[g], 1e-30)


def dec_sel(qg, pool, page_table, new_rows, bias_p, mask_p, bias_n, mask_n, t):
    B, G, R_, dh = qg.shape
    NP = page_table.shape[1]
    assert NP % PPS == 0
    n_steps = NP // PPS
    W = PPS * PAGE_SIZE

    def page_spec(u):
        return pl.BlockSpec((1, PAGE_SIZE, 2, NSA_GROUPS, NSA_HEAD_DIM),
                            lambda b, s, pt: (pt[b, s * PPS + u], 0, 0, 0, 0))

    grid_spec = pltpu.PrefetchScalarGridSpec(
        num_scalar_prefetch=1,
        grid=(B, n_steps),
        in_specs=[pl.BlockSpec((1, G, R_, dh), lambda b, s, pt: (b, 0, 0, 0))]
        + [page_spec(u) for u in range(PPS)]
        + [pl.BlockSpec((1, PAGE_SIZE, ROW_W), lambda b, s, pt: (b, 0, 0)),
           pl.BlockSpec((G, R_, W), lambda b, s, pt: (0, 0, s)),
           pl.BlockSpec((1, G, t, W), lambda b, s, pt: (b, 0, 0, s)),
           pl.BlockSpec((G, R_, PAGE_SIZE), lambda b, s, pt: (0, 0, 0)),
           pl.BlockSpec((1, G, t, PAGE_SIZE), lambda b, s, pt: (b, 0, 0, 0))],
        out_specs=pl.BlockSpec((1, G, R_, dh), lambda b, s, pt: (b, 0, 0, 0)),
        scratch_shapes=[pltpu.VMEM((G, R_, 1), jnp.float32), pltpu.VMEM((G, R_, 1), jnp.float32),
                        pltpu.VMEM((G, R_, dh), jnp.float32)],
    )
    return pl.pallas_call(
        functools.partial(_dec_sel_kernel, n_steps=n_steps, t=t),
        grid_spec=grid_spec,
        out_shape=jax.ShapeDtypeStruct((B, G, R_, dh), jnp.float32),
        compiler_params=pltpu.CompilerParams(dimension_semantics=("parallel", "arbitrary"),
                                             vmem_limit_bytes=VMEM_LIMIT),
        name="nsa_dec_sel",
    )(page_table, qg, *([pool] * PPS), new_rows, bias_p, mask_p, bias_n, mask_n)


def _dec_win_kernel(q_ref, win_ref, new_ref, bias_ref, valid_ref, o_ref):
    dh = NSA_HEAD_DIM
    valid = valid_ref[...] > 0.5
    for g in range(NSA_GROUPS):
        k = jnp.concatenate([win_ref[0, :, 0, g, :], new_ref[0, :, pl.ds(g * dh, dh)]], axis=0)
        v = jnp.concatenate([win_ref[0, :, 1, g, :], new_ref[0, :, pl.ds((NSA_GROUPS + g) * dh, dh)]], axis=0)
        s = lax.dot_general(q_ref[0, g], k.astype(jnp.bfloat16), (((1,), (1,)), ((), ())),
                            preferred_element_type=jnp.float32)
        p = _softmax_rows(s + bias_ref[g], valid)
        o_ref[0, g] = jnp.dot(p.astype(jnp.bfloat16), v.astype(jnp.bfloat16), preferred_element_type=jnp.float32)


def dec_win(qg, win_rows, new_rows, bias_w, valid_w):
    B, G, R_, dh = qg.shape
    per_b = lambda a: pl.BlockSpec((1,) + a.shape[1:], lambda b: (b,) + (0,) * (a.ndim - 1))
    full = lambda a: pl.BlockSpec(a.shape, lambda b: (0,) * a.ndim)
    return pl.pallas_call(
        _dec_win_kernel,
        grid=(B,),
        in_specs=[per_b(qg), per_b(win_rows), per_b(new_rows), full(bias_w), full(valid_w)],
        out_specs=pl.BlockSpec((1, G, R_, dh), lambda b: (b, 0, 0, 0)),
        out_shape=jax.ShapeDtypeStruct((B, G, R_, dh), jnp.float32),
        compiler_params=pltpu.CompilerParams(dimension_semantics=("parallel",), vmem_limit_bytes=VMEM_LIMIT),
        name="nsa_dec_win",
    )(qg, win_rows, new_rows, bias_w, valid_w)


def nsa_decode(q, gates, kc, vc, slc_pool, page_table, slc_new, win_buf, win_new_rows, table):
    B, t = q.shape[0], q.shape[1]
    G, HPG, dh = NSA_GROUPS, NSA_HPG, NSA_HEAD_DIM
    NP = page_table.shape[1]
    past = NP * PAGE_SIZE
    Wn = win_buf.shape[1]
    assert t <= PAGE_SIZE and Wn == WINDOW
    L = past + t
    nc = L // CMP_STRIDE - 1
    ncp = kc.shape[2]
    Ns = -(-L // SEL_BLOCK)
    nsp = -(-Ns // LANE) * LANE
    bf = jnp.bfloat16
    R_ = HPG * t
    qg = jnp.transpose(q.reshape(B, t, G, HPG, dh), (0, 2, 3, 1, 4)).reshape(B, G, R_, dh).astype(bf)
    qpos = past + jnp.arange(t)

    r_lo, r_hi = -(PAGE_SIZE + WINDOW), past + t
    lut_rev = table[t5_bucket(jnp.arange(r_hi, r_lo - 1, -1))].astype(jnp.float32)

    def rows_bias(p0, step, n):
        rows = []
        for i in range(t):
            j0 = r_hi - (past + i - p0)
            assert j0 >= 0 and j0 + step * (n - 1) < lut_rev.shape[0]
            rows.append(lax.slice(lut_rev, (j0, 0), (j0 + step * (n - 1) + 1, NSA_HEADS), (step, 1)))
        b = jnp.transpose(jnp.stack(rows), (2, 0, 1))
        return b.reshape(G, R_, n)

    rep = lambda m: jnp.tile(m, (HPG, 1))
    cidx = jnp.arange(ncp)
    cend = cidx * CMP_STRIDE + (CMP_BLOCK - 1)
    rel_c = qpos[:, None] - cend[None, :]
    valid_c = rep(((rel_c >= 0) & (cidx[None, :] < nc)).astype(jnp.float32))
    cstart = cidx[:, None] * CMP_STRIDE
    sstart = jnp.arange(nsp)[None, :] * SEL_BLOCK
    overlap = ((cstart < sstart + SEL_BLOCK) & (cstart + CMP_BLOCK > sstart)
               & (cidx[:, None] < nc) & (jnp.arange(nsp)[None, :] < Ns)).astype(bf)
    o_c, imp = dec_cmp(qg, kc, vc, rows_bias(CMP_BLOCK - 1, CMP_STRIDE, ncp), valid_c, overlap, t)
    imp = imp[..., :Ns]
    blk = jnp.arange(Ns)[None, :]
    cur = qpos[:, None] // SEL_BLOCK
    valid = blk * SEL_BLOCK <= qpos[:, None]
    forced = (blk == 0) | (blk == cur) | (blk == cur - 1)
    score = jnp.where(valid, jnp.where(forced, BIG, imp), -BIG)
    n_pick = min(N_SEL, Ns)
    top_s, idx = lax.top_k(score, n_pick)
    ok = top_s > -0.5 * BIG
    msel = jnp.sum(jax.nn.one_hot(idx, Ns, dtype=jnp.float32) * ok[..., None], axis=-2)
    kpos = jnp.arange(past + PAGE_SIZE)
    nblk = -(-(past + PAGE_SIZE) // SEL_BLOCK)
    mkey = jnp.repeat(jnp.pad(msel, ((0, 0), (0, 0), (0, 0), (0, nblk - Ns))), SEL_BLOCK, axis=-1)
    mkey = mkey[..., :past + PAGE_SIZE]
    mkey = mkey * ((kpos[None, :] <= qpos[:, None]) & (kpos[None, :] < L)).astype(jnp.float32)
    bias_s = rows_bias(0, 1, past + PAGE_SIZE)
    pad_new = lambda r: jnp.pad(r, ((0, 0), (0, PAGE_SIZE - t), (0, 0)))
    o_s = dec_sel(qg, slc_pool, page_table, pad_new(slc_new), bias_s[..., :past], mkey[..., :past],
                  bias_s[..., past:], mkey[..., past:], t)
    wpos = jnp.concatenate([past - Wn + jnp.arange(Wn), past + jnp.arange(PAGE_SIZE)])
    rel_w = qpos[:, None] - wpos[None, :]
    valid_w = rep(((rel_w >= 0) & (rel_w < WINDOW) & (wpos[None, :] >= 0) & (wpos[None, :] < L)).astype(jnp.float32))
    bias_w = jnp.concatenate([rows_bias(past - Wn, 1, Wn), rows_bias(past, 1, PAGE_SIZE)], axis=-1)
    o_w = dec_win(qg, win_buf, pad_new(win_new_rows), bias_w, valid_w)
    back = lambda o: jnp.transpose(o.reshape(B, G, HPG, t, dh), (0, 3, 1, 2, 4)).reshape(B, t, G * HPG * dh)
    gx = jnp.repeat(gates.reshape(B, t, G * HPG, 3), dh, axis=2)
    return gx[..., 0] * back(o_c) + gx[..., 1] * back(o_s) + gx[..., 2] * back(o_w)


def kv_rows(h, P):
    B, T, _ = h.shape
    return mm(rmsnorm(h, P['norm_kv']), P['kv_w']).reshape(B, T, 3, ROW_W)


def run_trunk(x, gla_s0, past, P):
    B, T, _ = x.shape
    h = x
    gla_states = []
    kv5 = lambda r: r.reshape(r.shape[0], r.shape[1], 2, NSA_GROUPS, NSA_HEAD_DIM)
    for layer in range(DEPTH):
        xn = rmsnorm(h, P['norm_mix'][layer])
        if layer < N_A_LAYERS:
            o, s = gla_mixer(xn, gla_s0[layer], P['gla_w_in'][layer], P['gla_w_gate_up'][layer],
                             P['gla_b_gate'][layer], P['gla_norm'][layer], P['gla_w_out'][layer])
            gla_states.append(s)
        else:
            j = layer - N_A_LAYERS
            qd = NSA_HEADS * NSA_HEAD_DIM
            proj = mm(xn, P['nsa_w_in'][j])
            q = proj[..., :qd].reshape(B, T, NSA_HEADS, NSA_HEAD_DIM) * NSA_HEAD_DIM ** -0.5
            gates = jax.nn.sigmoid(proj[..., qd:].astype(jnp.float32)).reshape(B, T, NSA_HEADS, 3).astype(x.dtype)
            if past is None:
                r5 = rows.reshape(B, T, 3, 2, NSA_GROUPS, NSA_HEAD_DIM)
                att = nsa_prompt(q, gates, kc, vc, r5[:, :, 1, 0], r5[:, :, 1, 1], r5[:, :, 2, 0], r5[:, :, 2, 1],
                                 P['rel_bias'])
            else:
                att = nsa_decode(q, gates, kc, vc, past['slc_pool'], past['page_table'], rows[:, :, 1],
                                 past['win_buf'], rows[:, :, 2], P['rel_bias'])
            o = mm(att, P['nsa_w_out'][j])
        h = h + o
        h = peer_ffn(h, rmsnorm(h, P['norm_ffn'][layer]), P['peer_w_q'][layer], P['peer_subkeys'][layer],
                     P['peer_u_bf'][layer], P['peer_v_bf'][layer])
        if layer == N_A_LAYERS - 1:
            rows = kv_rows(h, P)
            if past is None:
                assert T % (PAGE_SIZE * PPS) == 0 and T % TQ == 0
                npg = T // PAGE_SIZE
                pool = rows[:, :, 0].reshape(B * npg, PAGE_SIZE, 2, NSA_GROUPS, NSA_HEAD_DIM)
                table = jnp.arange(B * npg, dtype=jnp.int32).reshape(B, npg)
                win_new = kv5(rows[:, T - min(WINDOW, T):, 2])
            else:
                past_len = past['page_table'].shape[1] * PAGE_SIZE
                assert (past_len + T) // CMP_STRIDE == past_len // CMP_STRIDE
                pool, table = past['cmp_pool'], past['page_table']
                win_all = jnp.concatenate([past['win_buf'], kv5(rows[:, :, 2])], axis=1)
                win_new = win_all[:, win_all.shape[1] - min(WINDOW, win_all.shape[1]):]
            kc, vc = compress_paged(pool, table, *P['cmp_stack'])
    return (rmsnorm(h, P['norm_final']), jnp.stack(gla_states), kv5(rows[:, :, 0]), kv5(rows[:, :, 1]), win_new)


def kernel(x_prompt, x_sample, state_gla, cache_cmp_kv, cache_slc_kv, cache_win_kv, page_table,
           norm_mix, norm_ffn, norm_kv, norm_final, gla_w_in, gla_w_gate_up, gla_b_gate, gla_norm, gla_w_out,
           kv_w, cmp_k_w1, cmp_k_b1, cmp_k_w2, cmp_k_b2, cmp_k_pe, cmp_v_w1, cmp_v_b1, cmp_v_w2, cmp_v_b2, cmp_v_pe,
           nsa_w_in, nsa_w_out, rel_bias, peer_w_q, peer_subkeys, peer_u, peer_v):
    P = dict(norm_mix=norm_mix, norm_ffn=norm_ffn, norm_kv=norm_kv, norm_final=norm_final,
             gla_w_in=gla_w_in, gla_w_gate_up=gla_w_gate_up, gla_b_gate=gla_b_gate, gla_norm=gla_norm,
             gla_w_out=gla_w_out, kv_w=kv_w,
             cmp_k_w1=cmp_k_w1, cmp_k_b1=cmp_k_b1, cmp_k_w2=cmp_k_w2, cmp_k_b2=cmp_k_b2, cmp_k_pe=cmp_k_pe,
             cmp_v_w1=cmp_v_w1, cmp_v_b1=cmp_v_b1, cmp_v_w2=cmp_v_w2, cmp_v_b2=cmp_v_b2, cmp_v_pe=cmp_v_pe,
             nsa_w_in=nsa_w_in, nsa_w_out=nsa_w_out, rel_bias=rel_bias,
             peer_w_q=peer_w_q, peer_subkeys=peer_subkeys,
             peer_u_bf=peer_u.astype(jnp.bfloat16), peer_v_bf=peer_v.astype(jnp.bfloat16))
    P['cmp_stack'] = compress_params(P)
    past = dict(cmp_pool=cache_cmp_kv, slc_pool=cache_slc_kv, win_buf=cache_win_kv,
                page_table=page_table.astype(jnp.int32))
    gla_zero = jnp.zeros((N_A_LAYERS, BATCH, GLA_HEADS, GLA_DK, GLA_DV), x_prompt.dtype)
    y_prompt, gla_p, cmp_p, slc_p, win_p = run_trunk(x_prompt, gla_zero, None, P)
    y_sample, gla_s, cmp_s, slc_s, win_s = run_trunk(x_sample, state_gla, past, P)
    return (y_prompt, y_sample, gla_p, gla_s, cmp_p, cmp_s, slc_p, slc_s, win_p, win_s)
```

```python
import functools
import math

import jax
import jax.numpy as jnp
from jax import lax
from jax.experimental import pallas as pl
from jax.experimental.pallas import tpu as pltpu

D_MODEL = 2048
BATCH = 4
SEQ = 2048
DEPTH = 2
DEC_BATCH = 32
DEC_SEQ = 8
PAST_LEN = 8192
PAGE_SIZE = 128
N_A_LAYERS = DEPTH // 2
N_B_LAYERS = DEPTH - N_A_LAYERS
GLA_HEADS = 4
GLA_DK = D_MODEL // (2 * GLA_HEADS)
GLA_DV = D_MODEL // GLA_HEADS
GLA_GATE_RANK = 16
GLA_TAU = 16.0
GLA_CHUNK = 64
NSA_HEADS = 16
NSA_GROUPS = 4
NSA_HPG = NSA_HEADS // NSA_GROUPS
NSA_HEAD_DIM = D_MODEL // NSA_HEADS
CMP_STRIDE = 16
CMP_BLOCK = 2 * CMP_STRIDE
CMP_HIDDEN = 2 * NSA_HEAD_DIM
SEL_BLOCK = 64
N_SEL = 16
WINDOW = 512
Q_BLOCK = 32
REL_BUCKETS = 32
REL_MAX_DIST = 128
PEER_HEADS = 8
PEER_NKEYS = 128
PEER_EXPERTS = PEER_NKEYS * PEER_NKEYS
PEER_TOPK = 16
PEER_KEY_DIM = 256
PEER_TOK_BLOCK = 128
EPS = 1e-6
NEG = -1e30
BIG = 1e30

LANE = 128
VMEM_LIMIT = 56 * 1024 * 1024
TQ = 128
assert REL_MAX_DIST <= TQ and WINDOW % TQ == 0 and TQ % SEL_BLOCK == 0


def _mm_kernel(a_ref, b_ref, o_ref):
    a = a_ref[...].astype(jnp.bfloat16)
    b = b_ref[...].astype(jnp.bfloat16)
    o_ref[...] = jnp.dot(a, b, preferred_element_type=jnp.float32)


def pmm(a, b, tm=512, tn=512, keep_cols=False):
    M, K = a.shape
    N = b.shape[1]
    tm = min(tm, M)
    Mp = -(-M // tm) * tm
    Np = -(-N // LANE) * LANE
    wide = [d * LANE for d in range(1, 2 * tn // LANE + 1) if Np % (d * LANE) == 0]
    if max(wide) < tn:
        Np = -(-N // tn) * tn
        wide = [tn]
    tn = max(wide)
    if Mp != M:
        a = jnp.pad(a, ((0, Mp - M), (0, 0)))
    if Np != N:
        b = jnp.pad(b, ((0, 0), (0, Np - N)))
    out = pl.pallas_call(
        _mm_kernel,
        grid=(Mp // tm, Np // tn),
        in_specs=[pl.BlockSpec((tm, K), lambda i, j: (i, 0)),
                  pl.BlockSpec((K, tn), lambda i, j: (0, j))],
        out_specs=pl.BlockSpec((tm, tn), lambda i, j: (i, j)),
        out_shape=jax.ShapeDtypeStruct((Mp, Np), jnp.float32),
        compiler_params=pltpu.CompilerParams(
            dimension_semantics=("parallel", "parallel"), vmem_limit_bytes=VMEM_LIMIT),
        name="proj_matmul",
    )(a, b)
    return out[:M] if keep_cols else out[:M, :N]


def mm(x, w):
    lead = x.shape[:-1]
    return pmm(x.reshape(-1, x.shape[-1]), w).reshape(lead + (w.shape[1],))


def _norm_mm_kernel(a_ref, g_ref, b_ref, o_ref, xn_ref):
    @pl.when(pl.program_id(1) == 0)
    def _():
        x = a_ref[...]
        y = x * lax.rsqrt(jnp.mean(x * x, axis=-1, keepdims=True) + EPS) * g_ref[...]
        xn_ref[...] = y.astype(jnp.bfloat16)

    o_ref[...] = jnp.dot(xn_ref[...], b_ref[...].astype(jnp.bfloat16), preferred_element_type=jnp.float32)


def norm_mm(x, gain, b, tm=512, tn=512, keep_cols=False):
    M, K = x.shape
    N = b.shape[1]
    tm = math.gcd(M, tm)
    Np = -(-N // LANE) * LANE
    wide = [d * LANE for d in range(1, 2 * tn // LANE + 1) if Np % (d * LANE) == 0]
    if max(wide) < tn:
        Np = -(-N // tn) * tn
        wide = [tn]
    tn = max(wide)
    if Np != N:
        b = jnp.pad(b, ((0, 0), (0, Np - N)))
    out, xn = pl.pallas_call(
        _norm_mm_kernel,
        grid=(M // tm, Np // tn),
        in_specs=[pl.BlockSpec((tm, K), lambda i, j: (i, 0)),
                  pl.BlockSpec((1, K), lambda i, j: (0, 0)),
                  pl.BlockSpec((K, tn), lambda i, j: (0, j))],
        out_specs=[pl.BlockSpec((tm, tn), lambda i, j: (i, j)),
                   pl.BlockSpec((tm, K), lambda i, j: (i, 0))],
        out_shape=[jax.ShapeDtypeStruct((M, Np), jnp.float32), jax.ShapeDtypeStruct((M, K), jnp.bfloat16)],
        compiler_params=pltpu.CompilerParams(
            dimension_semantics=("parallel", "arbitrary"), vmem_limit_bytes=VMEM_LIMIT),
        name="norm_matmul",
    )(x, gain.reshape(1, K).astype(jnp.float32), b)
    return (out if keep_cols else out[:, :N]), xn


def rmsnorm(x, g):
    xf = x.astype(jnp.float32)
    y = xf * lax.rsqrt(jnp.mean(xf * xf, axis=-1, keepdims=True) + EPS)
    return (y * g.astype(jnp.float32)).astype(x.dtype)


def masked_softmax(s, mask):
    s = jnp.where(mask, s, NEG)
    m = jnp.max(s, axis=-1, keepdims=True)
    e = jnp.where(mask, jnp.exp(s - m), 0.0)
    return e / jnp.maximum(jnp.sum(e, axis=-1, keepdims=True), 1e-30)


def t5_bucket(rel):
    n = jnp.maximum(rel, 0)
    exact = REL_BUCKETS // 2
    nf = jnp.maximum(n, exact).astype(jnp.float32)
    large = exact + (jnp.log(nf / exact) / math.log(REL_MAX_DIST / exact) * (REL_BUCKETS - exact)).astype(jnp.int32)
    large = jnp.minimum(large, REL_BUCKETS - 1)
    return jnp.where(n < exact, n, large)


def head_bias(table, rel):
    onehot = jax.nn.one_hot(t5_bucket(rel), REL_BUCKETS, dtype=jnp.float32)
    b = jnp.einsum('qkn,nh->hqk', onehot, table.astype(jnp.float32), precision=lax.Precision.HIGHEST)
    return b.reshape(NSA_GROUPS, NSA_HPG, rel.shape[0], rel.shape[1])


DK_ALL = GLA_HEADS * GLA_DK
DV_ALL = GLA_HEADS * GLA_DV
GZ_COL = 2 * DK_ALL + 2 * DV_ALL
assert GZ_COL % LANE == 0 and GLA_GATE_RANK <= LANE and DK_ALL % LANE == 0 and (2 * DK_ALL) % DV_ALL == 0


def _gla_kernel(q_ref, k_ref, v_ref, r_ref, gz_ref, wg_ref, bg_ref, ng_ref, tri_ref, s0_ref,
                o_ref, sfin_ref, s_scr, *, C, t_valid):
    c = pl.program_id(1)

    @pl.when(c == 0)
    def _():
        s_scr[...] = s0_ref[0]

    row = lax.broadcasted_iota(jnp.int32, (C, C), 0)
    col = lax.broadcasted_iota(jnp.int32, (C, C), 1)
    bf = jnp.bfloat16
    z_all = jnp.dot(gz_ref[0].astype(bf), wg_ref[...], preferred_element_type=jnp.float32) + bg_ref[...]
    for h in range(GLA_HEADS):
        dk = slice(h * GLA_DK, (h + 1) * GLA_DK)
        dv = slice(h * GLA_DV, (h + 1) * GLA_DV)
        z = z_all[:, dk]
        g = (jnp.minimum(z, 0.0) - jnp.log(1.0 + jnp.exp(-jnp.abs(z)))) * (1.0 / GLA_TAU)
        if t_valid is not None:
            g = jnp.where(lax.broadcasted_iota(jnp.int32, g.shape, 0) < t_valid, g, 0.0)
        b = jnp.dot(tri_ref[...], g, preferred_element_type=jnp.float32, precision=lax.Precision.HIGHEST)
        b_last = b[C - 1:C, :]
        q = q_ref[0, :, dk] * (GLA_DK ** -0.5)
        k = k_ref[0, :, dk]
        v = v_ref[0, :, dv].astype(bf)
        qe = (q * jnp.exp(b)).astype(bf)
        ke = (k * jnp.exp(-b)).astype(bf)
        a = lax.dot_general(qe, ke, (((1,), (1,)), ((), ())), preferred_element_type=jnp.float32)
        a = jnp.where(col <= row, a, 0.0)
        s = s_scr[h]
        o = (jnp.dot(a.astype(bf), v, preferred_element_type=jnp.float32)
             + jnp.dot(qe, s.astype(bf), preferred_element_type=jnp.float32))
        kd = (k * jnp.exp(b_last - b)).astype(bf)
        dcol = jnp.exp(jnp.transpose(jnp.broadcast_to(b_last, (LANE, GLA_DK))))
        s_new = (s * jnp.concatenate([dcol] * (GLA_DV // LANE), axis=1)
                 + lax.dot_general(kd, v, (((0,), (0,)), ((), ())), preferred_element_type=jnp.float32))
        s_scr[h] = s_new
        o = o * lax.rsqrt(jnp.mean(o * o, axis=-1, keepdims=True) + EPS) * ng_ref[...]
        r = r_ref[0, :, dv]
        o_ref[0, :, dv] = o * (r * (1.0 / (1.0 + jnp.exp(-r))))

    @pl.when(c == pl.num_programs(1) - 1)
    def _():
        sfin_ref[0] = s_scr[...]


def gla_core(proj, s0, w_gate_up, b_gate, norm_g, T):
    B, Tp, _ = proj.shape
    C = GLA_CHUNK if T % GLA_CHUNK == 0 else Tp
    assert Tp % C == 0 and C % 16 == 0 and (Tp == T or Tp == C)
    n = Tp // C
    wg = jnp.pad(w_gate_up, ((0, LANE - GLA_GATE_RANK), (0, 0))).astype(jnp.bfloat16)
    tri = (jnp.arange(C)[:, None] >= jnp.arange(C)[None, :]).astype(jnp.float32)
    col_blk = lambda w, i: pl.BlockSpec((1, C, w), lambda b, c: (b, c, i))
    full = lambda a: pl.BlockSpec(a.shape, lambda b, c: (0,) * a.ndim)
    bg = b_gate.reshape(1, DK_ALL)
    ng = norm_g.reshape(1, GLA_DV)
    st_spec = pl.BlockSpec((1, GLA_HEADS, GLA_DK, GLA_DV), lambda b, c: (b, 0, 0, 0))
    return pl.pallas_call(
        functools.partial(_gla_kernel, C=C, t_valid=None if Tp == T else T),
        grid=(B, n),
        in_specs=[col_blk(DK_ALL, 0), col_blk(DK_ALL, 1), col_blk(DV_ALL, 2 * DK_ALL // DV_ALL),
                  col_blk(DV_ALL, (2 * DK_ALL + DV_ALL) // DV_ALL), col_blk(LANE, GZ_COL // LANE),
                  full(wg), full(bg), full(ng), full(tri), st_spec],
        out_specs=[pl.BlockSpec((1, C, DV_ALL), lambda b, c: (b, c, 0)), st_spec],
        out_shape=[jax.ShapeDtypeStruct((B, Tp, DV_ALL), jnp.float32),
                   jax.ShapeDtypeStruct((B, GLA_HEADS, GLA_DK, GLA_DV), jnp.float32)],
        scratch_shapes=[pltpu.VMEM((GLA_HEADS, GLA_DK, GLA_DV), jnp.float32)],
        compiler_params=pltpu.CompilerParams(dimension_semantics=("parallel", "arbitrary"),
                                             vmem_limit_bytes=VMEM_LIMIT),
        name="gla_core",
    )(proj, proj, proj, proj, proj, wg, bg, ng, tri, s0)


def gla_mixer(h, gain, s0, w_in, w_gate_up, b_gate, norm_g, w_out):
    B, T, D = h.shape
    proj, _ = norm_mm(h.reshape(B * T, D), gain, w_in, keep_cols=True)
    assert proj.shape[1] >= GZ_COL + LANE
    Tp = -(-T // 16) * 16
    proj = jnp.pad(proj.reshape(B, T, -1), ((0, 0), (0, Tp - T), (0, 0)))
    o, s_new = gla_core(proj, s0.astype(jnp.float32), w_gate_up, b_gate, norm_g, T)
    return mm(o[:, :T], w_out), s_new.astype(s0.dtype)


WSUM_UNROLL = 8
PEER_CHUNK = 512


def _wsum_kernel(a_ref, b_ref, g_ref, o_ref, *, tb):
    sub = lax.broadcasted_iota(jnp.int32, (PEER_NKEYS, PEER_NKEYS), 0)

    def body(tt, c):
        t0 = pl.multiple_of(tt * WSUM_UNROLL, WSUM_UNROLL)
        a8 = a_ref[pl.ds(t0, WSUM_UNROLL), :]
        b8 = b_ref[pl.ds(t0, WSUM_UNROLL), :]
        g8 = g_ref[pl.ds(t0, WSUM_UNROLL), :]
        for u in range(WSUM_UNROLL):
            at = jnp.where(sub == a8[u:u + 1], 1.0, 0.0).astype(jnp.bfloat16)
            bt = jnp.where(sub == b8[u:u + 1], g8[u:u + 1], 0.0).astype(jnp.bfloat16)
            o_ref[t0 + u] = lax.dot_general(at, bt, (((1,), (1,)), ((), ())), preferred_element_type=jnp.float32)
        return c

    lax.fori_loop(0, tb // WSUM_UNROLL, body, 0)


def peer_wsum(i1, i2, gate, tb=64):
    n, K = i1.shape
    tb = math.gcd(n, tb)
    assert tb % WSUM_UNROLL == 0
    return pl.pallas_call(
        functools.partial(_wsum_kernel, tb=tb),
        grid=(n // tb,),
        in_specs=[pl.BlockSpec((tb, K), lambda i: (i, 0))] * 3,
        out_specs=pl.BlockSpec((tb, PEER_NKEYS, PEER_NKEYS), lambda i: (i, 0, 0)),
        out_shape=jax.ShapeDtypeStruct((n, PEER_NKEYS, PEER_NKEYS), jnp.float32),
        compiler_params=pltpu.CompilerParams(dimension_semantics=("parallel",), vmem_limit_bytes=VMEM_LIMIT),
        name="peer_wsum",
    )(i1, i2, gate)


def _peer_kernel(x_ref, u_ref, v_ref, w_ref, o_ref, *, te):
    j = pl.program_id(1)
    x = x_ref[...]
    acc = None
    for c in range(te // PEER_CHUNK):
        e0 = c * PEER_CHUNK
        hid = lax.dot_general(x, u_ref[e0:e0 + PEER_CHUNK, :], (((1,), (1,)), ((), ())),
                              preferred_element_type=jnp.float32)
        parts = []
        for r in range(PEER_CHUNK // PEER_NKEYS):
            hr = hid[:, r * PEER_NKEYS:(r + 1) * PEER_NKEYS]
            ar = 0.5 * hr * (1.0 + lax.erf(hr * (2.0 ** -0.5))) * w_ref[:, e0 // PEER_NKEYS + r, :]
            parts.append(ar.astype(jnp.bfloat16))
        d = jnp.dot(jnp.concatenate(parts, axis=1), v_ref[e0:e0 + PEER_CHUNK, :], preferred_element_type=jnp.float32)
        acc = d if acc is None else acc + d
    @pl.when(j == 0)
    def _():
        o_ref[...] = acc

    @pl.when(j > 0)
    def _():
        o_ref[...] += acc


def peer_dense(xn, u, v, wsum, tb=1024, te=1024):
    n, D = xn.shape
    E = u.shape[0]
    tb = math.gcd(n, tb)
    return pl.pallas_call(
        functools.partial(_peer_kernel, te=te),
        grid=(n // tb, E // te),
        in_specs=[pl.BlockSpec((tb, D), lambda i, j: (i, 0), pipeline_mode=pl.Buffered(1)),
                  pl.BlockSpec((te, D), lambda i, j: (j, 0)),
                  pl.BlockSpec((te, D), lambda i, j: (j, 0)),
                  pl.BlockSpec((tb, te // PEER_NKEYS, PEER_NKEYS), lambda i, j: (i, j, 0))],
        out_specs=pl.BlockSpec((tb, D), lambda i, j: (i, 0), pipeline_mode=pl.Buffered(1)),
        out_shape=jax.ShapeDtypeStruct((n, D), jnp.float32),
        compiler_params=pltpu.CompilerParams(dimension_semantics=("parallel", "arbitrary"),
                                             vmem_limit_bytes=VMEM_LIMIT),
        name="peer_dense",
    )(xn, u, v, wsum)


RT = LANE
NHC = 2 * PEER_HEADS
_CAND_GROUPS = [(0, 0), (0, 8), (1, 0)] + [(a, 0) for a in range(2, 8)] + [(-1, 0)]
assert PEER_TOPK == 16 and PEER_KEY_DIM // 2 == LANE and PEER_NKEYS == LANE


def _top_rounds(s, n_rounds, extra=()):
    R = s.shape[0]
    iota = lax.broadcasted_iota(jnp.int32, s.shape, 0)
    vals, idxs, ex = [], [], [[] for _ in extra]
    for _ in range(n_rounds):
        m = jnp.max(s, axis=0, keepdims=True)
        idx = jnp.min(jnp.where(s == m, iota, R), axis=0, keepdims=True)
        sel = iota == idx
        for e, lst in zip(extra, ex):
            lst.append(jnp.max(jnp.where(sel, e, -1), axis=0, keepdims=True))
        s = jnp.where(sel, -jnp.inf, s)
        vals.append(m)
        idxs.append(idx)
    cat = lambda l: jnp.concatenate(l, axis=0)
    return cat(vals), cat(idxs), [cat(l) for l in ex]


def _route_kernel(q_ref, sub_ref, e1_ref, e2_ref, g_ref, v_scr, i_scr):
    K = PEER_TOPK

    def stage1(h, c):
        for hc in (2 * h, 2 * h + 1):
            off = pl.multiple_of(hc * LANE, LANE)
            qb = q_ref[:, pl.ds(off, LANE)].astype(jnp.bfloat16)
            s = lax.dot_general(sub_ref[hc], qb, (((1,), (1,)), ((), ())), preferred_element_type=jnp.float32)
            v, i, _ = _top_rounds(s, K)
            v_scr[hc] = v
            i_scr[hc] = i
        return c

    lax.fori_loop(0, PEER_HEADS, stage1, 0)

    row8 = lax.broadcasted_iota(jnp.int32, (8, RT), 0)

    def stage2(h):
        v1, v2 = v_scr[2 * h], v_scr[2 * h + 1]
        i1, i2 = i_scr[2 * h], i_scr[2 * h + 1]
        cand, c1, c2 = [], [], []
        for a, b0 in _CAND_GROUPS:
            if a >= 0:
                nb = K // (a + 1)
                sm = v1[a:a + 1] + v2[b0:b0 + 8]
                if nb - b0 < 8:
                    sm = jnp.where(row8 < nb - b0, sm, -jnp.inf)
                cand.append(sm)
                c1.append(jnp.broadcast_to(i1[a:a + 1], (8, RT)))
                c2.append(i2[b0:b0 + 8])
            else:
                cand.append(v1[8:16] + v2[0:1])
                c1.append(i1[8:16])
                c2.append(jnp.broadcast_to(i2[0:1], (8, RT)))
        cat = lambda l: jnp.concatenate(l, axis=0)
        top, _, (e12,) = _top_rounds(cat(cand), K, extra=(cat(c1) * PEER_NKEYS + cat(c2),))
        ex = jnp.exp(top - top[0:1])
        g = ex / jnp.sum(ex, axis=0, keepdims=True)
        r0 = pl.multiple_of(h * K, K)
        e1_ref[0, pl.ds(r0, K), :] = e12 // PEER_NKEYS
        e2_ref[0, pl.ds(r0, K), :] = e12 % PEER_NKEYS
        g_ref[0, pl.ds(r0, K), :] = g

    def stage2_pair(hp, c):
        stage2(2 * hp)
        stage2(2 * hp + 1)
        return c

    lax.fori_loop(0, PEER_HEADS // 2, stage2_pair, 0)


def peer_route(q, sub_bf):
    n = q.shape[0]
    assert n % RT == 0
    nb = n // RT
    slots = PEER_HEADS * PEER_TOPK
    out = jax.ShapeDtypeStruct((nb, slots, RT), jnp.int32)
    ospec = pl.BlockSpec((1, slots, RT), lambda i: (i, 0, 0))
    return pl.pallas_call(
        _route_kernel,
        grid=(nb,),
        in_specs=[pl.BlockSpec((RT, q.shape[1]), lambda i: (i, 0)),
                  pl.BlockSpec(sub_bf.shape, lambda i: (0, 0, 0))],
        out_specs=[ospec, ospec, ospec],
        out_shape=[out, out, jax.ShapeDtypeStruct((nb, slots, RT), jnp.float32)],
        scratch_shapes=[pltpu.VMEM((NHC, PEER_TOPK, RT), jnp.float32),
                        pltpu.VMEM((NHC, PEER_TOPK, RT), jnp.int32)],
        compiler_params=pltpu.CompilerParams(dimension_semantics=("parallel",), vmem_limit_bytes=VMEM_LIMIT),
        name="peer_route",
    )(q, sub_bf)


def peer_ffn(h, gain, w_q, subkeys, u_bf, v_bf):
    B, T, D = h.shape
    n = B * T
    q, xt = norm_mm(h.reshape(n, D), gain, w_q)
    sub_bf = subkeys.astype(jnp.bfloat16).reshape(NHC, PEER_NKEYS, PEER_KEY_DIM // 2)
    e1, e2, gate = peer_route(q, sub_bf)
    tok_major = lambda t: jnp.transpose(t, (0, 2, 1)).reshape(n, PEER_HEADS * PEER_TOPK)
    wsum = peer_wsum(tok_major(e1), tok_major(e2), tok_major(gate))
    out = peer_dense(xt, u_bf, v_bf, wsum)
    return h + out.reshape(B, T, D)


KVG = 2 * NSA_GROUPS
ROW_W = KVG * NSA_HEAD_DIM
CPP = PAGE_SIZE // CMP_STRIDE
PPS = 8
assert CPP == 8 and PAGE_SIZE == LANE and NSA_HEAD_DIM == LANE


def _compress_kernel(pt_ref, *refs, n_steps):
    pages = refs[:PPS]
    perm_ref, w1_ref, c1_ref, w2_ref, b2_ref, kc_ref, vc_ref, xc_ref = refs[PPS:]
    s_idx = pl.program_id(1)
    for pair in range(PPS // 2):
        row0 = pl.multiple_of((s_idx * (PPS // 2) + pair) * 2 * CPP, 2 * CPP)
        for kvg in range(KVG):
            kv, g = divmod(kvg, NSA_GROUPS)
            x2 = jnp.concatenate([pages[2 * pair][0, :, kv, g, :], pages[2 * pair + 1][0, :, kv, g, :]],
                                 axis=0).astype(jnp.bfloat16)
            y = jnp.dot(perm_ref[...], x2, preferred_element_type=jnp.float32).astype(jnp.bfloat16)
            for s in range(CMP_STRIDE):
                xc_ref[kvg, pl.ds(row0, 2 * CPP), pl.ds(s * LANE, LANE)] = y[s * 2 * CPP:(s + 1) * 2 * CPP]

    @pl.when(s_idx == n_steps - 1)
    def _():
        nch = xc_ref.shape[1]
        for kv in range(2):
            out_ref = kc_ref if kv == 0 else vc_ref
            for g in range(NSA_GROUPS):
                hh = jnp.dot(xc_ref[kv * NSA_GROUPS + g], w1_ref[kv], preferred_element_type=jnp.float32)
                h1 = hh[:, :CMP_HIDDEN]
                h2 = pltpu.roll(hh[:, CMP_HIDDEN:], nch - 1, 0)
                x = h1 + h2 + c1_ref[kv]
                hid = 0.5 * x * (1.0 + lax.erf(x * (2.0 ** -0.5)))
                o = jnp.dot(hid.astype(jnp.bfloat16), w2_ref[kv], preferred_element_type=jnp.float32) + b2_ref[kv]
                out_ref[0, g] = o.astype(out_ref.dtype)


def compress_paged(pool, page_table, w1, c1, w2, b2):
    B, NP = page_table.shape
    assert NP % PPS == 0
    n_steps = NP // PPS
    nch = NP * CPP
    dh = NSA_HEAD_DIM
    r = jnp.arange(2 * PAGE_SIZE)
    s_, u_, c_ = r // (2 * CPP), (r // CPP) % 2, r % CPP
    perm = (r[None, :] == (u_ * PAGE_SIZE + c_ * CMP_STRIDE + s_)[:, None]).astype(jnp.bfloat16)

    def page_spec(u):
        return pl.BlockSpec((1, PAGE_SIZE, 2, NSA_GROUPS, NSA_HEAD_DIM),
                            lambda b, s, pt: (pt[b, s * PPS + u], 0, 0, 0, 0))

    full = lambda a: pl.BlockSpec(a.shape, lambda b, s, pt: (0,) * a.ndim)
    out_spec = pl.BlockSpec((1, NSA_GROUPS, nch, dh), lambda b, s, pt: (b, 0, 0, 0))
    grid_spec = pltpu.PrefetchScalarGridSpec(
        num_scalar_prefetch=1,
        grid=(B, n_steps),
        in_specs=[page_spec(u) for u in range(PPS)] + [full(perm), full(w1), full(c1), full(w2), full(b2)],
        out_specs=[out_spec, out_spec],
        scratch_shapes=[pltpu.VMEM((KVG, nch, CMP_STRIDE * dh), jnp.bfloat16)],
    )
    out = jax.ShapeDtypeStruct((B, NSA_GROUPS, nch, dh), jnp.bfloat16)
    return pl.pallas_call(
        functools.partial(_compress_kernel, n_steps=n_steps),
        grid_spec=grid_spec,
        out_shape=[out, out],
        compiler_params=pltpu.CompilerParams(dimension_semantics=("parallel", "arbitrary"),
                                             vmem_limit_bytes=VMEM_LIMIT),
        name="nsa_compress",
    )(page_table, *([pool] * PPS), perm, w1, c1, w2, b2)


def compress_params(P):
    dh, F = NSA_HEAD_DIM, CMP_HIDDEN
    w1s, c1s, w2s, b2s = [], [], [], []
    for n in ('k', 'v'):
        w1h = P['cmp_%s_w1' % n].reshape(2, CMP_STRIDE * dh, F)
        pe = P['cmp_%s_pe' % n].reshape(2, CMP_STRIDE * dh)
        w1s.append(jnp.concatenate([w1h[0], w1h[1]], axis=1))
        c1s.append(P['cmp_%s_b1' % n] + jnp.einsum('hk,hkf->f', pe, w1h, precision=lax.Precision.HIGHEST))
        w2s.append(P['cmp_%s_w2' % n])
        b2s.append(P['cmp_%s_b2' % n])
    return (jnp.stack(w1s).astype(jnp.bfloat16), jnp.stack(c1s)[:, None, :],
            jnp.stack(w2s).astype(jnp.bfloat16), jnp.stack(b2s)[:, None, :])


def _softmax_rows(s, valid):
    s = jnp.where(valid, s, NEG)
    m = jnp.max(s, axis=-1, keepdims=True)
    e = jnp.where(valid, jnp.exp(s - m), 0.0)
    return e / jnp.maximum(jnp.sum(e, axis=-1, keepdims=True), 1e-30)


HQ = NSA_HPG * TQ
GATE_ROWS = 8


def _lanes4(x):
    return jnp.concatenate([x] * NSA_HPG, axis=1)


def _nsa_cmp_sel_kernel(q_ref, kc_ref, vct_ref, bias_ref, ovt_ref, oct_ref, msel_ref, *, nc, ns):
    i = pl.program_id(2)
    ncp = kc_ref.shape[2]
    nsp = ovt_ref.shape[0]
    q_all = q_ref[0, 0].reshape(HQ, NSA_HEAD_DIM)
    cidx = lax.broadcasted_iota(jnp.int32, (ncp, TQ), 0)
    qpos = i * TQ + lax.broadcasted_iota(jnp.int32, (ncp, TQ), 1)
    valid = _lanes4((cidx * CMP_STRIDE + (CMP_BLOCK - 1) <= qpos) & (cidx < nc))
    s = lax.dot_general(kc_ref[0, 0], q_all, (((1,), (1,)), ((), ())), preferred_element_type=jnp.float32)
    s = jnp.where(valid, s + bias_ref[0, 0], NEG)
    m = jnp.max(s, axis=0, keepdims=True)
    e = jnp.where(valid, jnp.exp(s - m), 0.0)
    p = e / jnp.maximum(jnp.sum(e, axis=0, keepdims=True), 1e-30)
    oct_ref[0, 0, 0] = jnp.dot(vct_ref[0, 0], p.astype(jnp.bfloat16), preferred_element_type=jnp.float32)
    psum = p[:, 0:TQ]
    for hh in range(1, NSA_HPG):
        psum = psum + p[:, hh * TQ:(hh + 1) * TQ]
    imp = jnp.dot(ovt_ref[...], psum.astype(jnp.bfloat16), preferred_element_type=jnp.float32)
    blk = lax.broadcasted_iota(jnp.int32, (nsp, TQ), 0)
    qp = i * TQ + lax.broadcasted_iota(jnp.int32, (nsp, TQ), 1)
    cur = qp // SEL_BLOCK
    ok_blk = (blk * SEL_BLOCK <= qp) & (blk < ns)
    forced = (blk == 0) | (blk == cur) | (blk == cur - 1)
    score = jnp.where(ok_blk, jnp.where(forced, BIG, imp), -BIG)
    score = jnp.where(blk < ns, score, -jnp.inf)
    msel = jnp.zeros((nsp, TQ), jnp.float32)
    for _ in range(min(N_SEL, ns)):
        mx = jnp.max(score, axis=0, keepdims=True)
        idx = jnp.min(jnp.where(score == mx, blk, nsp), axis=0, keepdims=True)
        sel = blk == idx
        msel = jnp.where(sel & (mx > -0.5 * BIG), 1.0, msel)
        score = jnp.where(sel, -jnp.inf, score)
    msel_ref[0, 0] = msel


def nsa_cmp_sel(qh, kc, vct, bias_ct, ovt, nc, ns):
    B, G, HPG, T, dh = qh.shape
    ncp = kc.shape[2]
    nsp = ovt.shape[0]
    nqt = T // TQ
    return pl.pallas_call(
        functools.partial(_nsa_cmp_sel_kernel, nc=nc, ns=ns),
        grid=(B, G, nqt),
        in_specs=[pl.BlockSpec((1, 1, HPG, TQ, dh), lambda b, g, i: (b, g, 0, i, 0)),
                  pl.BlockSpec((1, 1, ncp, dh), lambda b, g, i: (b, g, 0, 0)),
                  pl.BlockSpec((1, 1, dh, ncp), lambda b, g, i: (b, g, 0, 0)),
                  pl.BlockSpec((1, 1, ncp, HQ), lambda b, g, i: (g, i, 0, 0)),
                  pl.BlockSpec((nsp, ncp), lambda b, g, i: (0, 0))],
        out_specs=[pl.BlockSpec((1, 1, 1, dh, HQ), lambda b, g, i: (b, g, i, 0, 0)),
                   pl.BlockSpec((1, 1, nsp, TQ), lambda b, g, i: (b, g, 0, i))],
        out_shape=[jax.ShapeDtypeStruct((B, G, nqt, dh, HQ), jnp.float32),
                   jax.ShapeDtypeStruct((B, G, nsp, T), jnp.float32)],
        compiler_params=pltpu.CompilerParams(dimension_semantics=("parallel", "parallel", "parallel"),
                                             vmem_limit_bytes=VMEM_LIMIT),
        name="nsa_cmp_sel",
    )(qh, kc, vct, bias_ct, ovt)


def _flash_tile_t(q_all, k, vt, bias, valid, m_ref, l_ref, acc_ref):
    s = lax.dot_general(k, q_all, (((1,), (1,)), ((), ())), preferred_element_type=jnp.float32)
    s = jnp.where(valid, s + bias, NEG)
    m_old = m_ref[...]
    m_new = jnp.maximum(m_old, jnp.max(s, axis=0, keepdims=True))
    p = jnp.where(valid, jnp.exp(s - m_new), 0.0)
    alpha = jnp.exp(m_old - m_new)
    l_ref[...] = alpha * l_ref[...] + jnp.sum(p, axis=0, keepdims=True)
    acc_ref[...] = alpha * acc_ref[...] + jnp.dot(vt, p.astype(jnp.bfloat16), preferred_element_type=jnp.float32)
    m_ref[...] = m_new


def _nsa_sw_kernel(q_ref, ks_ref, vst_ref, kw_ref, vwt_ref, msel_ref, expt_ref, bias_ref, selm_ref, winm_ref,
                   oct_ref, gate_ref, o_ref, msk_ref, m_ref, l_ref, acc_ref, *, nt):
    i = pl.program_id(2)
    q_all = q_ref[0, 0].reshape(HQ, NSA_HEAD_DIM)
    msel = msel_ref[0, 0].astype(jnp.bfloat16)
    for j in range(nt):
        msk_ref[j] = jnp.dot(expt_ref[j], msel, preferred_element_type=jnp.float32)

    def reset():
        m_ref[...] = jnp.full_like(m_ref, NEG)
        l_ref[...] = jnp.zeros_like(l_ref)
        acc_ref[...] = jnp.zeros_like(acc_ref)

    def result():
        return acc_ref[...] / jnp.maximum(l_ref[...], 1e-30)

    def pair(k_ref, vt_ref, j1, dd1, valid_of):
        j0 = j1 - 1
        j0c = jnp.maximum(j0, 0)
        k2 = jnp.concatenate([k_ref[0, 0, j1], k_ref[0, 0, j0c]], axis=0)
        vt2 = jnp.concatenate([vt_ref[0, 0, j1], vt_ref[0, 0, j0c]], axis=1)
        bias2 = jnp.concatenate([bias_ref[0, jnp.minimum(dd1, 2)], bias_ref[0, jnp.minimum(dd1 + 1, 2)]], axis=0)
        valid2 = _lanes4(jnp.concatenate([valid_of(j1, dd1), valid_of(j0c, dd1 + 1) & (j0 >= 0)], axis=0))
        _flash_tile_t(q_all, k2, vt2, bias2, valid2, m_ref, l_ref, acc_ref)

    reset()

    def sel_valid(j, dd):
        return (msk_ref[j] * selm_ref[jnp.minimum(dd, 2)]) > 0.5

    def sel_body(jj, c):
        pair(ks_ref, vst_ref, i - 2 * jj, 2 * jj, sel_valid)
        return c

    lax.fori_loop(0, i // 2 + 1, sel_body, 0)
    mix = gate_ref[0, 0, 0, 0:1, :] * oct_ref[0, 0, 0] + gate_ref[0, 0, 0, 1:2, :] * result()
    reset()
    for pp in range(WINDOW // TQ // 2 + 1):
        j1 = i - 2 * pp
        pair(kw_ref, vwt_ref, jnp.maximum(j1, 0), 2 * pp, lambda j, dd: (winm_ref[dd] > 0.5) & (j1 >= 0))
    mix = mix + gate_ref[0, 0, 0, 2:3, :] * result()
    for hh in range(NSA_HPG):
        o_ref[0, :, hh * NSA_HEAD_DIM:(hh + 1) * NSA_HEAD_DIM] = jnp.transpose(mix[:, hh * TQ:(hh + 1) * TQ])


def nsa_sel_win(qh, ks, vst, kw, vwt, msel_t, exp_t, bias_tt, selm_t, winm_t, oct, gates_t):
    B, G, HPG, T, dh = qh.shape
    nt = T // TQ
    nsp = msel_t.shape[2]
    k_spec = pl.BlockSpec((1, 1, nt, TQ, dh), lambda b, g, i: (b, g, 0, 0, 0))
    vt_spec = pl.BlockSpec((1, 1, nt, dh, TQ), lambda b, g, i: (b, g, 0, 0, 0))
    return pl.pallas_call(
        functools.partial(_nsa_sw_kernel, nt=nt),
        grid=(B, G, nt),
        in_specs=[pl.BlockSpec((1, 1, HPG, TQ, dh), lambda b, g, i: (b, g, 0, i, 0)),
                  k_spec, vt_spec, k_spec, vt_spec,
                  pl.BlockSpec((1, 1, nsp, TQ), lambda b, g, i: (b, g, 0, i)),
                  pl.BlockSpec((nt, TQ, nsp), lambda b, g, i: (0, 0, 0)),
                  pl.BlockSpec((1, 3, TQ, HQ), lambda b, g, i: (g, 0, 0, 0)),
                  pl.BlockSpec((3, TQ, TQ), lambda b, g, i: (0, 0, 0)),
                  pl.BlockSpec(winm_t.shape, lambda b, g, i: (0, 0, 0)),
                  pl.BlockSpec((1, 1, 1, dh, HQ), lambda b, g, i: (b, g, i, 0, 0)),
                  pl.BlockSpec((1, 1, 1, GATE_ROWS, HQ), lambda b, g, i: (b, g, i, 0, 0))],
        out_specs=pl.BlockSpec((1, TQ, HPG * dh), lambda b, g, i: (b, i, g)),
        out_shape=jax.ShapeDtypeStruct((B, T, G * HPG * dh), jnp.float32),
        scratch_shapes=[pltpu.VMEM((nt, TQ, TQ), jnp.float32),
                        pltpu.VMEM((1, HQ), jnp.float32),
                        pltpu.VMEM((1, HQ), jnp.float32),
                        pltpu.VMEM((dh, HQ), jnp.float32)],
        compiler_params=pltpu.CompilerParams(dimension_semantics=("parallel", "parallel", "arbitrary"),
                                             vmem_limit_bytes=VMEM_LIMIT),
        name="nsa_sel_win",
    )(qh, ks, vst, kw, vwt, msel_t, exp_t, bias_tt, selm_t, winm_t, oct, gates_t)


def nsa_prompt(q, gates, kc, vc, ks, vs, kw_rows, vw_rows, table):
    B, T = q.shape[0], q.shape[1]
    G, HPG, dh = NSA_GROUPS, NSA_HPG, NSA_HEAD_DIM
    Nc = T // CMP_STRIDE - 1
    Ns = -(-T // SEL_BLOCK)
    ncp = kc.shape[2]
    assert ncp % LANE == 0 and ncp >= Nc
    nsp = -(-Ns // 16) * 16
    nqt = T // TQ
    bf = jnp.bfloat16
    qh = jnp.transpose(q.reshape(B, T, G, HPG, dh), (0, 2, 3, 1, 4)).astype(bf)
    qpos = jnp.arange(T)
    cidx = jnp.arange(ncp)
    bias_c = head_bias(table, qpos[:, None] - (cidx * CMP_STRIDE + (CMP_BLOCK - 1))[None, :])
    bias_ct = jnp.transpose(bias_c.reshape(G, HPG, nqt, TQ, ncp), (0, 2, 4, 1, 3)).reshape(G, nqt, ncp, HQ)
    sidx = jnp.arange(nsp)
    ovt = (((cidx * CMP_STRIDE)[None, :] < (sidx * SEL_BLOCK + SEL_BLOCK)[:, None])
           & ((cidx * CMP_STRIDE + CMP_BLOCK)[None, :] > (sidx * SEL_BLOCK)[:, None])
           & (cidx[None, :] < Nc) & (sidx[:, None] < Ns)).astype(bf)
    oct, msel_t = nsa_cmp_sel(qh, kc, jnp.transpose(vc, (0, 1, 3, 2)), bias_ct, ovt, Nc, Ns)
    exp_t = ((jnp.arange(T) // SEL_BLOCK)[:, None] == sidx[None, :]).astype(bf).reshape(nqt, TQ, nsp)
    r = jnp.arange(TQ)
    rel3 = (jnp.arange(3) * TQ)[:, None, None] + r[None, None, :] - r[None, :, None]
    bias_tt = jnp.transpose(table[t5_bucket(rel3)].astype(jnp.float32), (3, 0, 1, 2))
    bias_tt = jnp.transpose(bias_tt.reshape(G, HPG, 3, TQ, TQ), (0, 2, 3, 1, 4)).reshape(G, 3, TQ, HQ)
    selm_t = (rel3 >= 0).astype(jnp.float32)
    nw = 2 * (WINDOW // TQ // 2 + 1)
    relw = (jnp.arange(nw) * TQ)[:, None, None] + r[None, None, :] - r[None, :, None]
    winm_t = ((relw >= 0) & (relw < WINDOW)).astype(jnp.float32)

    def k_tiles(t):
        return jnp.transpose(t, (0, 2, 1, 3)).astype(bf).reshape(B, G, nqt, TQ, dh)

    def vt_tiles(t):
        return jnp.transpose(t.astype(bf).reshape(B, nqt, TQ, G, dh), (0, 3, 1, 4, 2))

    gates_t = jnp.transpose(gates.reshape(B, nqt, TQ, G, HPG, 3), (0, 3, 1, 5, 4, 2)).reshape(B, G, nqt, 3, HQ)
    gates_t = jnp.pad(gates_t, ((0, 0), (0, 0), (0, 0), (0, GATE_ROWS - 3), (0, 0)))
    return nsa_sel_win(qh, k_tiles(ks), vt_tiles(vs), k_tiles(kw_rows), vt_tiles(vw_rows), msel_t, exp_t, bias_tt,
                       selm_t, winm_t, oct, gates_t)


def _dec_cmp_kernel(q_ref, kc_ref, vc_ref, bias_ref, valid_ref, ov_ref, oc_ref, msel_ref, *, t, q0, ns):
    valid = valid_ref[...] > 0.5
    nsp = ov_ref.shape[1]
    imps = []
    for g in range(NSA_GROUPS):
        s = lax.dot_general(q_ref[0, g], kc_ref[0, g], (((1,), (1,)), ((), ())), preferred_element_type=jnp.float32)
        p = _softmax_rows(s + bias_ref[g], valid)
        oc_ref[0, g] = jnp.dot(p.astype(jnp.bfloat16), vc_ref[0, g], preferred_element_type=jnp.float32)
        psum = p[0:t]
        for hh in range(1, NSA_HPG):
            psum = psum + p[hh * t:(hh + 1) * t]
        imps.append(jnp.dot(psum.astype(jnp.bfloat16), ov_ref[...], preferred_element_type=jnp.float32))
    imp = jnp.concatenate(imps, axis=0)
    blk = lax.broadcasted_iota(jnp.int32, (NSA_GROUPS * t, nsp), 1)
    qp = q0 + jnp.concatenate([lax.broadcasted_iota(jnp.int32, (t, nsp), 0)] * NSA_GROUPS, axis=0)
    cur = qp // SEL_BLOCK
    ok_blk = (blk * SEL_BLOCK <= qp) & (blk < ns)
    forced = (blk == 0) | (blk == cur) | (blk == cur - 1)
    score = jnp.where(ok_blk, jnp.where(forced, BIG, imp), -BIG)
    score = jnp.where(blk < ns, score, -jnp.inf)
    msel = jnp.zeros((NSA_GROUPS * t, nsp), jnp.float32)
    for _ in range(min(N_SEL, ns)):
        mx = jnp.max(score, axis=1, keepdims=True)
        idx = jnp.min(jnp.where(score == mx, blk, nsp), axis=1, keepdims=True)
        sel = blk == idx
        msel = jnp.where(sel & (mx > -0.5 * BIG), 1.0, msel)
        score = jnp.where(sel, -jnp.inf, score)
    for g in range(NSA_GROUPS):
        msel_ref[0, g] = msel[g * t:(g + 1) * t]


def dec_cmp(qg, kc, vc, bias_c, valid_c, overlap, t, q0, ns):
    B, G, R_, dh = qg.shape
    ncp, nsp = overlap.shape
    per_b = lambda a: pl.BlockSpec((1,) + a.shape[1:], lambda b: (b,) + (0,) * (a.ndim - 1))
    full = lambda a: pl.BlockSpec(a.shape, lambda b: (0,) * a.ndim)
    return pl.pallas_call(
        functools.partial(_dec_cmp_kernel, t=t, q0=q0, ns=ns),
        grid=(B,),
        in_specs=[per_b(qg), per_b(kc), per_b(vc), full(bias_c), full(valid_c), full(overlap)],
        out_specs=[pl.BlockSpec((1, G, R_, dh), lambda b: (b, 0, 0, 0)),
                   pl.BlockSpec((1, G, t, nsp), lambda b: (b, 0, 0, 0))],
        out_shape=[jax.ShapeDtypeStruct((B, G, R_, dh), jnp.float32),
                   jax.ShapeDtypeStruct((B, G, t, nsp), jnp.float32)],
        compiler_params=pltpu.CompilerParams(dimension_semantics=("parallel",), vmem_limit_bytes=VMEM_LIMIT),
        name="nsa_dec_cmp",
    )(qg, kc, vc, bias_c, valid_c, overlap)


def _flash_update(s, valid, v, m_ref, l_ref, acc_ref, g):
    s = jnp.where(valid, s, NEG)
    m_old = m_ref[g]
    m_new = jnp.maximum(m_old, jnp.max(s, axis=-1, keepdims=True))
    p = jnp.where(valid, jnp.exp(s - m_new), 0.0)
    alpha = jnp.exp(m_old - m_new)
    l_ref[g] = alpha * l_ref[g] + jnp.sum(p, axis=-1, keepdims=True)
    acc_ref[g] = alpha * acc_ref[g] + jnp.dot(p.astype(jnp.bfloat16), v, preferred_element_type=jnp.float32)
    m_ref[g] = m_new


def _dec_sel_kernel(pt_ref, q_ref, *refs, n_steps, t):
    pages = refs[:PPS]
    new_ref, bias_ref, mask_ref, biasn_ref, maskn_ref, o_ref, m_ref, l_ref, acc_ref = refs[PPS:]
    s_idx = pl.program_id(1)
    dh = NSA_HEAD_DIM

    @pl.when(s_idx == 0)
    def _():
        m_ref[...] = jnp.full_like(m_ref, NEG)
        l_ref[...] = jnp.zeros_like(l_ref)
        acc_ref[...] = jnp.zeros_like(acc_ref)

    def tile_mask(mref, g):
        mk = mref[0, g] > 0.5
        return jnp.concatenate([mk] * NSA_HPG, axis=0)

    for g in range(NSA_GROUPS):
        k = jnp.concatenate([pg[0, :, 0, g, :] for pg in pages], axis=0).astype(jnp.bfloat16)
        v = jnp.concatenate([pg[0, :, 1, g, :] for pg in pages], axis=0).astype(jnp.bfloat16)
        s = lax.dot_general(q_ref[0, g], k, (((1,), (1,)), ((), ())), preferred_element_type=jnp.float32)
        _flash_update(s + bias_ref[g], tile_mask(mask_ref, g), v, m_ref, l_ref, acc_ref, g)

    @pl.when(s_idx == n_steps - 1)
    def _():
        for g in range(NSA_GROUPS):
            k = new_ref[0, :, pl.ds(g * dh, dh)].astype(jnp.bfloat16)
            v = new_ref[0, :, pl.ds((NSA_GROUPS + g) * dh, dh)].astype(jnp.bfloat16)
            s = lax.dot_general(q_ref[0, g], k, (((1,), (1,)), ((), ())), preferred_element_type=jnp.float32)
            _flash_update(s + biasn_ref[g], tile_mask(maskn_ref, g), v, m_ref, l_ref, acc_ref, g)
            o_ref[0, g] = acc_ref[g] / jnp.maximum(l_ref[g], 1e-30)


def dec_sel(qg, pool, page_table, new_rows, bias_p, mask_p, bias_n, mask_n, t):
    B, G, R_, dh = qg.shape
    NP = page_table.shape[1]
    assert NP % PPS == 0
    n_steps = NP // PPS
    W = PPS * PAGE_SIZE

    def page_spec(u):
        return pl.BlockSpec((1, PAGE_SIZE, 2, NSA_GROUPS, NSA_HEAD_DIM),
                            lambda b, s, pt: (pt[b, s * PPS + u], 0, 0, 0, 0))

    grid_spec = pltpu.PrefetchScalarGridSpec(
        num_scalar_prefetch=1,
        grid=(B, n_steps),
        in_specs=[pl.BlockSpec((1, G, R_, dh), lambda b, s, pt: (b, 0, 0, 0))]
        + [page_spec(u) for u in range(PPS)]
        + [pl.BlockSpec((1, PAGE_SIZE, ROW_W), lambda b, s, pt: (b, 0, 0)),
           pl.BlockSpec((G, R_, W), lambda b, s, pt: (0, 0, s)),
           pl.BlockSpec((1, G, t, W), lambda b, s, pt: (b, 0, 0, s)),
           pl.BlockSpec((G, R_, PAGE_SIZE), lambda b, s, pt: (0, 0, 0)),
           pl.BlockSpec((1, G, t, PAGE_SIZE), lambda b, s, pt: (b, 0, 0, 0))],
        out_specs=pl.BlockSpec((1, G, R_, dh), lambda b, s, pt: (b, 0, 0, 0)),
        scratch_shapes=[pltpu.VMEM((G, R_, 1), jnp.float32), pltpu.VMEM((G, R_, 1), jnp.float32),
                        pltpu.VMEM((G, R_, dh), jnp.float32)],
    )
    return pl.pallas_call(
        functools.partial(_dec_sel_kernel, n_steps=n_steps, t=t),
        grid_spec=grid_spec,
        out_shape=jax.ShapeDtypeStruct((B, G, R_, dh), jnp.float32),
        compiler_params=pltpu.CompilerParams(dimension_semantics=("parallel", "arbitrary"),
                                             vmem_limit_bytes=VMEM_LIMIT),
        name="nsa_dec_sel",
    )(page_table, qg, *([pool] * PPS), new_rows, bias_p, mask_p, bias_n, mask_n)


def _dec_win_kernel(q_ref, win_ref, new_ref, bias_ref, valid_ref, o_ref):
    dh = NSA_HEAD_DIM
    valid = valid_ref[...] > 0.5
    for g in range(NSA_GROUPS):
        k = jnp.concatenate([win_ref[0, :, 0, g, :], new_ref[0, :, pl.ds(g * dh, dh)]], axis=0)
        v = jnp.concatenate([win_ref[0, :, 1, g, :], new_ref[0, :, pl.ds((NSA_GROUPS + g) * dh, dh)]], axis=0)
        s = lax.dot_general(q_ref[0, g], k.astype(jnp.bfloat16), (((1,), (1,)), ((), ())),
                            preferred_element_type=jnp.float32)
        p = _softmax_rows(s + bias_ref[g], valid)
        o_ref[0, g] = jnp.dot(p.astype(jnp.bfloat16), v.astype(jnp.bfloat16), preferred_element_type=jnp.float32)


def dec_win(qg, win_rows, new_rows, bias_w, valid_w):
    B, G, R_, dh = qg.shape
    per_b = lambda a: pl.BlockSpec((1,) + a.shape[1:], lambda b: (b,) + (0,) * (a.ndim - 1))
    full = lambda a: pl.BlockSpec(a.shape, lambda b: (0,) * a.ndim)
    return pl.pallas_call(
        _dec_win_kernel,
        grid=(B,),
        in_specs=[per_b(qg), per_b(win_rows), per_b(new_rows), full(bias_w), full(valid_w)],
        out_specs=pl.BlockSpec((1, G, R_, dh), lambda b: (b, 0, 0, 0)),
        out_shape=jax.ShapeDtypeStruct((B, G, R_, dh), jnp.float32),
        compiler_params=pltpu.CompilerParams(dimension_semantics=("parallel",), vmem_limit_bytes=VMEM_LIMIT),
        name="nsa_dec_win",
    )(qg, win_rows, new_rows, bias_w, valid_w)


def nsa_decode(q, gates, kc, vc, slc_pool, page_table, slc_new, win_buf, win_new_rows, table):
    B, t = q.shape[0], q.shape[1]
    G, HPG, dh = NSA_GROUPS, NSA_HPG, NSA_HEAD_DIM
    NP = page_table.shape[1]
    past = NP * PAGE_SIZE
    Wn = win_buf.shape[1]
    assert t <= PAGE_SIZE and Wn == WINDOW
    L = past + t
    nc = L // CMP_STRIDE - 1
    ncp = kc.shape[2]
    Ns = -(-L // SEL_BLOCK)
    nsp = -(-Ns // LANE) * LANE
    bf = jnp.bfloat16
    R_ = HPG * t
    qg = jnp.transpose(q.reshape(B, t, G, HPG, dh), (0, 2, 3, 1, 4)).reshape(B, G, R_, dh).astype(bf)
    qpos = past + jnp.arange(t)

    r_lo, r_hi = -(PAGE_SIZE + WINDOW), past + t
    lut_rev = table[t5_bucket(jnp.arange(r_hi, r_lo - 1, -1))].astype(jnp.float32)

    def rows_bias(p0, step, n):
        rows = []
        for i in range(t):
            j0 = r_hi - (past + i - p0)
            assert j0 >= 0 and j0 + step * (n - 1) < lut_rev.shape[0]
            rows.append(lax.slice(lut_rev, (j0, 0), (j0 + step * (n - 1) + 1, NSA_HEADS), (step, 1)))
        b = jnp.transpose(jnp.stack(rows), (2, 0, 1))
        return b.reshape(G, R_, n)

    rep = lambda m: jnp.tile(m, (HPG, 1))
    cidx = jnp.arange(ncp)
    cend = cidx * CMP_STRIDE + (CMP_BLOCK - 1)
    rel_c = qpos[:, None] - cend[None, :]
    valid_c = rep(((rel_c >= 0) & (cidx[None, :] < nc)).astype(jnp.float32))
    cstart = cidx[:, None] * CMP_STRIDE
    sstart = jnp.arange(nsp)[None, :] * SEL_BLOCK
    overlap = ((cstart < sstart + SEL_BLOCK) & (cstart + CMP_BLOCK > sstart)
               & (cidx[:, None] < nc) & (jnp.arange(nsp)[None, :] < Ns)).astype(bf)
    o_c, msel = dec_cmp(qg, kc, vc, rows_bias(CMP_BLOCK - 1, CMP_STRIDE, ncp), valid_c, overlap, t, past, Ns)
    msel = msel[..., :Ns]
    kpos = jnp.arange(past + PAGE_SIZE)
    nblk = -(-(past + PAGE_SIZE) // SEL_BLOCK)
    mkey = jnp.repeat(jnp.pad(msel, ((0, 0), (0, 0), (0, 0), (0, nblk - Ns))), SEL_BLOCK, axis=-1)
    mkey = mkey[..., :past + PAGE_SIZE]
    mkey = mkey * ((kpos[None, :] <= qpos[:, None]) & (kpos[None, :] < L)).astype(jnp.float32)
    bias_s = rows_bias(0, 1, past + PAGE_SIZE)
    pad_new = lambda r: jnp.pad(r, ((0, 0), (0, PAGE_SIZE - t), (0, 0)))
    o_s = dec_sel(qg, slc_pool, page_table, pad_new(slc_new), bias_s[..., :past], mkey[..., :past],
                  bias_s[..., past:], mkey[..., past:], t)
    wpos = jnp.concatenate([past - Wn + jnp.arange(Wn), past + jnp.arange(PAGE_SIZE)])
    rel_w = qpos[:, None] - wpos[None, :]
    valid_w = rep(((rel_w >= 0) & (rel_w < WINDOW) & (wpos[None, :] >= 0) & (wpos[None, :] < L)).astype(jnp.float32))
    bias_w = jnp.concatenate([rows_bias(past - Wn, 1, Wn), rows_bias(past, 1, PAGE_SIZE)], axis=-1)
    o_w = dec_win(qg, win_buf, pad_new(win_new_rows), bias_w, valid_w)
    back = lambda o: jnp.transpose(o.reshape(B, G, HPG, t, dh), (0, 3, 1, 2, 4)).reshape(B, t, G * HPG * dh)
    gx = jnp.repeat(gates.reshape(B, t, G * HPG, 3), dh, axis=2)
    return gx[..., 0] * back(o_c) + gx[..., 1] * back(o_s) + gx[..., 2] * back(o_w)


def kv_rows(h, P):
    B, T, D = h.shape
    return norm_mm(h.reshape(B * T, D), P['norm_kv'], P['kv_w'])[0].reshape(B, T, 3, ROW_W)


def run_trunk(x, gla_s0, past, P):
    B, T, _ = x.shape
    h = x
    gla_states = []
    kv5 = lambda r: r.reshape(r.shape[0], r.shape[1], 2, NSA_GROUPS, NSA_HEAD_DIM)
    for layer in range(DEPTH):
        if layer < N_A_LAYERS:
            o, s = gla_mixer(h, P['norm_mix'][layer], gla_s0[layer], P['gla_w_in'][layer],
                             P['gla_w_gate_up'][layer], P['gla_b_gate'][layer], P['gla_norm'][layer],
                             P['gla_w_out'][layer])
            gla_states.append(s)
        else:
            j = layer - N_A_LAYERS
            qd = NSA_HEADS * NSA_HEAD_DIM
            proj = norm_mm(h.reshape(B * T, -1), P['norm_mix'][layer], P['nsa_w_in'][j])[0].reshape(B, T, -1)
            q = proj[..., :qd].reshape(B, T, NSA_HEADS, NSA_HEAD_DIM) * NSA_HEAD_DIM ** -0.5
            gates = jax.nn.sigmoid(proj[..., qd:].astype(jnp.float32)).reshape(B, T, NSA_HEADS, 3).astype(x.dtype)
            if past is None:
                r5 = rows.reshape(B, T, 3, 2, NSA_GROUPS, NSA_HEAD_DIM)
                att = nsa_prompt(q, gates, kc, vc, r5[:, :, 1, 0], r5[:, :, 1, 1], r5[:, :, 2, 0], r5[:, :, 2, 1],
                                 P['rel_bias'])
            else:
                att = nsa_decode(q, gates, kc, vc, past['slc_pool'], past['page_table'], rows[:, :, 1],
                                 past['win_buf'], rows[:, :, 2], P['rel_bias'])
            o = mm(att, P['nsa_w_out'][j])
        h = h + o
        h = peer_ffn(h, P['norm_ffn'][layer], P['peer_w_q'][layer], P['peer_subkeys'][layer],
                     P['peer_u_bf'][layer], P['peer_v_bf'][layer])
        if layer == N_A_LAYERS - 1:
            rows = kv_rows(h, P)
            if past is None:
                assert T % (PAGE_SIZE * PPS) == 0 and T % TQ == 0
                npg = T // PAGE_SIZE
                pool = rows[:, :, 0].reshape(B * npg, PAGE_SIZE, 2, NSA_GROUPS, NSA_HEAD_DIM)
                table = jnp.arange(B * npg, dtype=jnp.int32).reshape(B, npg)
                win_new = kv5(rows[:, T - min(WINDOW, T):, 2])
            else:
                past_len = past['page_table'].shape[1] * PAGE_SIZE
                assert (past_len + T) // CMP_STRIDE == past_len // CMP_STRIDE
                pool, table = past['cmp_pool'], past['page_table']
                win_all = jnp.concatenate([past['win_buf'], kv5(rows[:, :, 2])], axis=1)
                win_new = win_all[:, win_all.shape[1] - min(WINDOW, win_all.shape[1]):]
            kc, vc = compress_paged(pool, table, *P['cmp_stack'])
    return (rmsnorm(h, P['norm_final']), jnp.stack(gla_states), kv5(rows[:, :, 0]), kv5(rows[:, :, 1]), win_new)


def kernel(x_prompt, x_sample, state_gla, cache_cmp_kv, cache_slc_kv, cache_win_kv, page_table,
           norm_mix, norm_ffn, norm_kv, norm_final, gla_w_in, gla_w_gate_up, gla_b_gate, gla_norm, gla_w_out,
           kv_w, cmp_k_w1, cmp_k_b1, cmp_k_w2, cmp_k_b2, cmp_k_pe, cmp_v_w1, cmp_v_b1, cmp_v_w2, cmp_v_b2, cmp_v_pe,
           nsa_w_in, nsa_w_out, rel_bias, peer_w_q, peer_subkeys, peer_u, peer_v):
    P = dict(norm_mix=norm_mix, norm_ffn=norm_ffn, norm_kv=norm_kv, norm_final=norm_final,
             gla_w_in=gla_w_in, gla_w_gate_up=gla_w_gate_up, gla_b_gate=gla_b_gate, gla_norm=gla_norm,
             gla_w_out=gla_w_out, kv_w=kv_w,
             cmp_k_w1=cmp_k_w1, cmp_k_b1=cmp_k_b1, cmp_k_w2=cmp_k_w2, cmp_k_b2=cmp_k_b2, cmp_k_pe=cmp_k_pe,
             cmp_v_w1=cmp_v_w1, cmp_v_b1=cmp_v_b1, cmp_v_w2=cmp_v_w2, cmp_v_b2=cmp_v_b2, cmp_v_pe=cmp_v_pe,
             nsa_w_in=nsa_w_in, nsa_w_out=nsa_w_out, rel_bias=rel_bias,
             peer_w_q=peer_w_q, peer_subkeys=peer_subkeys,
             peer_u_bf=peer_u.astype(jnp.bfloat16), peer_v_bf=peer_v.astype(jnp.bfloat16))
    P['cmp_stack'] = compress_params(P)
    past = dict(cmp_pool=cache_cmp_kv, slc_pool=cache_slc_kv, win_buf=cache_win_kv,
                page_table=page_table.astype(jnp.int32))
    gla_zero = jnp.zeros((N_A_LAYERS, BATCH, GLA_HEADS, GLA_DK, GLA_DV), x_prompt.dtype)
    y_prompt, gla_p, cmp_p, slc_p, win_p = run_trunk(x_prompt, gla_zero, None, P)
    y_sample, gla_s, cmp_s, slc_s, win_s = run_trunk(x_sample, state_gla, past, P)
    return (y_prompt, y_sample, gla_p, gla_s, cmp_p, cmp_s, slc_p, slc_s, win_p, win_s)
```

```python
import functools
import math

import jax
import jax.numpy as jnp
from jax import lax
from jax.experimental import pallas as pl
from jax.experimental.pallas import tpu as pltpu

D_MODEL = 2048
BATCH = 4
SEQ = 2048
DEPTH = 2
DEC_BATCH = 32
DEC_SEQ = 8
PAST_LEN = 8192
PAGE_SIZE = 128
N_A_LAYERS = DEPTH // 2
N_B_LAYERS = DEPTH - N_A_LAYERS
GLA_HEADS = 4
GLA_DK = D_MODEL // (2 * GLA_HEADS)
GLA_DV = D_MODEL // GLA_HEADS
GLA_GATE_RANK = 16
GLA_TAU = 16.0
GLA_CHUNK = 64
NSA_HEADS = 16
NSA_GROUPS = 4
NSA_HPG = NSA_HEADS // NSA_GROUPS
NSA_HEAD_DIM = D_MODEL // NSA_HEADS
CMP_STRIDE = 16
CMP_BLOCK = 2 * CMP_STRIDE
CMP_HIDDEN = 2 * NSA_HEAD_DIM
SEL_BLOCK = 64
N_SEL = 16
WINDOW = 512
Q_BLOCK = 32
REL_BUCKETS = 32
REL_MAX_DIST = 128
PEER_HEADS = 8
PEER_NKEYS = 128
PEER_EXPERTS = PEER_NKEYS * PEER_NKEYS
PEER_TOPK = 16
PEER_KEY_DIM = 256
PEER_TOK_BLOCK = 128
EPS = 1e-6
NEG = -1e30
BIG = 1e30

LANE = 128
VMEM_LIMIT = 56 * 1024 * 1024
TQ = 128
assert REL_MAX_DIST <= TQ and WINDOW % TQ == 0 and TQ % SEL_BLOCK == 0


def _mm_kernel(a_ref, b_ref, o_ref):
    a = a_ref[...].astype(jnp.bfloat16)
    b = b_ref[...].astype(jnp.bfloat16)
    o_ref[...] = jnp.dot(a, b, preferred_element_type=jnp.float32)


def pmm(a, b, tm=512, tn=512, keep_cols=False):
    M, K = a.shape
    N = b.shape[1]
    tm = min(tm, M)
    Mp = -(-M // tm) * tm
    Np = -(-N // LANE) * LANE
    wide = [d * LANE for d in range(1, 2 * tn // LANE + 1) if Np % (d * LANE) == 0]
    if max(wide) < tn:
        Np = -(-N // tn) * tn
        wide = [tn]
    tn = max(wide)
    if Mp != M:
        a = jnp.pad(a, ((0, Mp - M), (0, 0)))
    if Np != N:
        b = jnp.pad(b, ((0, 0), (0, Np - N)))
    out = pl.pallas_call(
        _mm_kernel,
        grid=(Mp // tm, Np // tn),
        in_specs=[pl.BlockSpec((tm, K), lambda i, j: (i, 0)),
                  pl.BlockSpec((K, tn), lambda i, j: (0, j))],
        out_specs=pl.BlockSpec((tm, tn), lambda i, j: (i, j)),
        out_shape=jax.ShapeDtypeStruct((Mp, Np), jnp.float32),
        compiler_params=pltpu.CompilerParams(
            dimension_semantics=("parallel", "parallel"), vmem_limit_bytes=VMEM_LIMIT),
        name="proj_matmul",
    )(a, b)
    return out[:M] if keep_cols else out[:M, :N]


def mm(x, w):
    lead = x.shape[:-1]
    return pmm(x.reshape(-1, x.shape[-1]), w).reshape(lead + (w.shape[1],))


def _norm_mm_kernel(a_ref, g_ref, b_ref, o_ref, xn_ref):
    @pl.when(pl.program_id(1) == 0)
    def _():
        x = a_ref[...]
        y = x * lax.rsqrt(jnp.mean(x * x, axis=-1, keepdims=True) + EPS) * g_ref[...]
        xn_ref[...] = y.astype(jnp.bfloat16)

    o_ref[...] = jnp.dot(xn_ref[...], b_ref[...].astype(jnp.bfloat16), preferred_element_type=jnp.float32)


def norm_mm(x, gain, b, tm=512, tn=512, keep_cols=False):
    M, K = x.shape
    N = b.shape[1]
    tm = math.gcd(M, tm)
    Np = -(-N // LANE) * LANE
    wide = [d * LANE for d in range(1, 2 * tn // LANE + 1) if Np % (d * LANE) == 0]
    if max(wide) < tn:
        Np = -(-N // tn) * tn
        wide = [tn]
    tn = max(wide)
    if Np != N:
        b = jnp.pad(b, ((0, 0), (0, Np - N)))
    out, xn = pl.pallas_call(
        _norm_mm_kernel,
        grid=(M // tm, Np // tn),
        in_specs=[pl.BlockSpec((tm, K), lambda i, j: (i, 0)),
                  pl.BlockSpec((1, K), lambda i, j: (0, 0)),
                  pl.BlockSpec((K, tn), lambda i, j: (0, j))],
        out_specs=[pl.BlockSpec((tm, tn), lambda i, j: (i, j)),
                   pl.BlockSpec((tm, K), lambda i, j: (i, 0))],
        out_shape=[jax.ShapeDtypeStruct((M, Np), jnp.float32), jax.ShapeDtypeStruct((M, K), jnp.bfloat16)],
        compiler_params=pltpu.CompilerParams(
            dimension_semantics=("parallel", "arbitrary"), vmem_limit_bytes=VMEM_LIMIT),
        name="norm_matmul",
    )(x, gain.reshape(1, K).astype(jnp.float32), b)
    return (out if keep_cols else out[:, :N]), xn


def rmsnorm(x, g):
    xf = x.astype(jnp.float32)
    y = xf * lax.rsqrt(jnp.mean(xf * xf, axis=-1, keepdims=True) + EPS)
    return (y * g.astype(jnp.float32)).astype(x.dtype)


def masked_softmax(s, mask):
    s = jnp.where(mask, s, NEG)
    m = jnp.max(s, axis=-1, keepdims=True)
    e = jnp.where(mask, jnp.exp(s - m), 0.0)
    return e / jnp.maximum(jnp.sum(e, axis=-1, keepdims=True), 1e-30)


def t5_bucket(rel):
    n = jnp.maximum(rel, 0)
    exact = REL_BUCKETS // 2
    nf = jnp.maximum(n, exact).astype(jnp.float32)
    large = exact + (jnp.log(nf / exact) / math.log(REL_MAX_DIST / exact) * (REL_BUCKETS - exact)).astype(jnp.int32)
    large = jnp.minimum(large, REL_BUCKETS - 1)
    return jnp.where(n < exact, n, large)


def head_bias(table, rel):
    onehot = jax.nn.one_hot(t5_bucket(rel), REL_BUCKETS, dtype=jnp.float32)
    b = jnp.einsum('qkn,nh->hqk', onehot, table.astype(jnp.float32), precision=lax.Precision.HIGHEST)
    return b.reshape(NSA_GROUPS, NSA_HPG, rel.shape[0], rel.shape[1])


DK_ALL = GLA_HEADS * GLA_DK
DV_ALL = GLA_HEADS * GLA_DV
GZ_COL = 2 * DK_ALL + 2 * DV_ALL
assert GZ_COL % LANE == 0 and GLA_GATE_RANK <= LANE and DK_ALL % LANE == 0 and (2 * DK_ALL) % DV_ALL == 0


def _gla_kernel(q_ref, k_ref, v_ref, r_ref, gz_ref, wg_ref, bg_ref, ng_ref, tri_ref, s0_ref,
                o_ref, sfin_ref, s_scr, *, C, t_valid):
    c = pl.program_id(1)

    @pl.when(c == 0)
    def _():
        s_scr[...] = s0_ref[0]

    row = lax.broadcasted_iota(jnp.int32, (C, C), 0)
    col = lax.broadcasted_iota(jnp.int32, (C, C), 1)
    bf = jnp.bfloat16
    z_all = jnp.dot(gz_ref[0].astype(bf), wg_ref[...], preferred_element_type=jnp.float32) + bg_ref[...]
    for h in range(GLA_HEADS):
        dk = slice(h * GLA_DK, (h + 1) * GLA_DK)
        dv = slice(h * GLA_DV, (h + 1) * GLA_DV)
        z = z_all[:, dk]
        g = (jnp.minimum(z, 0.0) - jnp.log(1.0 + jnp.exp(-jnp.abs(z)))) * (1.0 / GLA_TAU)
        if t_valid is not None:
            g = jnp.where(lax.broadcasted_iota(jnp.int32, g.shape, 0) < t_valid, g, 0.0)
        b = jnp.dot(tri_ref[...], g, preferred_element_type=jnp.float32, precision=lax.Precision.HIGHEST)
        b_last = b[C - 1:C, :]
        q = q_ref[0, :, dk] * (GLA_DK ** -0.5)
        k = k_ref[0, :, dk]
        v = v_ref[0, :, dv].astype(bf)
        qe = (q * jnp.exp(b)).astype(bf)
        ke = (k * jnp.exp(-b)).astype(bf)
        a = lax.dot_general(qe, ke, (((1,), (1,)), ((), ())), preferred_element_type=jnp.float32)
        a = jnp.where(col <= row, a, 0.0)
        s = s_scr[h]
        o = (jnp.dot(a.astype(bf), v, preferred_element_type=jnp.float32)
             + jnp.dot(qe, s.astype(bf), preferred_element_type=jnp.float32))
        kd = (k * jnp.exp(b_last - b)).astype(bf)
        dcol = jnp.exp(jnp.transpose(jnp.broadcast_to(b_last, (LANE, GLA_DK))))
        s_new = (s * jnp.concatenate([dcol] * (GLA_DV // LANE), axis=1)
                 + lax.dot_general(kd, v, (((0,), (0,)), ((), ())), preferred_element_type=jnp.float32))
        s_scr[h] = s_new
        o = o * lax.rsqrt(jnp.mean(o * o, axis=-1, keepdims=True) + EPS) * ng_ref[...]
        r = r_ref[0, :, dv]
        o_ref[0, :, dv] = o * (r * (1.0 / (1.0 + jnp.exp(-r))))

    @pl.when(c == pl.num_programs(1) - 1)
    def _():
        sfin_ref[0] = s_scr[...]


def gla_core(proj, s0, w_gate_up, b_gate, norm_g, T):
    B, Tp, _ = proj.shape
    C = GLA_CHUNK if T % GLA_CHUNK == 0 else Tp
    assert Tp % C == 0 and C % 16 == 0 and (Tp == T or Tp == C)
    n = Tp // C
    wg = jnp.pad(w_gate_up, ((0, LANE - GLA_GATE_RANK), (0, 0))).astype(jnp.bfloat16)
    tri = (jnp.arange(C)[:, None] >= jnp.arange(C)[None, :]).astype(jnp.float32)
    col_blk = lambda w, i: pl.BlockSpec((1, C, w), lambda b, c: (b, c, i))
    full = lambda a: pl.BlockSpec(a.shape, lambda b, c: (0,) * a.ndim)
    bg = b_gate.reshape(1, DK_ALL)
    ng = norm_g.reshape(1, GLA_DV)
    st_spec = pl.BlockSpec((1, GLA_HEADS, GLA_DK, GLA_DV), lambda b, c: (b, 0, 0, 0))
    return pl.pallas_call(
        functools.partial(_gla_kernel, C=C, t_valid=None if Tp == T else T),
        grid=(B, n),
        in_specs=[col_blk(DK_ALL, 0), col_blk(DK_ALL, 1), col_blk(DV_ALL, 2 * DK_ALL // DV_ALL),
                  col_blk(DV_ALL, (2 * DK_ALL + DV_ALL) // DV_ALL), col_blk(LANE, GZ_COL // LANE),
                  full(wg), full(bg), full(ng), full(tri), st_spec],
        out_specs=[pl.BlockSpec((1, C, DV_ALL), lambda b, c: (b, c, 0)), st_spec],
        out_shape=[jax.ShapeDtypeStruct((B, Tp, DV_ALL), jnp.float32),
                   jax.ShapeDtypeStruct((B, GLA_HEADS, GLA_DK, GLA_DV), jnp.float32)],
        scratch_shapes=[pltpu.VMEM((GLA_HEADS, GLA_DK, GLA_DV), jnp.float32)],
        compiler_params=pltpu.CompilerParams(dimension_semantics=("parallel", "arbitrary"),
                                             vmem_limit_bytes=VMEM_LIMIT),
        name="gla_core",
    )(proj, proj, proj, proj, proj, wg, bg, ng, tri, s0)


def gla_mixer(h, gain, s0, w_in, w_gate_up, b_gate, norm_g, w_out):
    B, T, D = h.shape
    proj, _ = norm_mm(h.reshape(B * T, D), gain, w_in, keep_cols=True)
    assert proj.shape[1] >= GZ_COL + LANE
    Tp = -(-T // 16) * 16
    proj = jnp.pad(proj.reshape(B, T, -1), ((0, 0), (0, Tp - T), (0, 0)))
    o, s_new = gla_core(proj, s0.astype(jnp.float32), w_gate_up, b_gate, norm_g, T)
    return mm(o[:, :T], w_out), s_new.astype(s0.dtype)


WSUM_UNROLL = 8
PEER_CHUNK = 512


def _wsum_kernel(a_ref, b_ref, g_ref, o_ref, *, tb):
    sub = lax.broadcasted_iota(jnp.int32, (PEER_NKEYS, PEER_NKEYS), 0)

    def body(tt, c):
        t0 = pl.multiple_of(tt * WSUM_UNROLL, WSUM_UNROLL)
        a8 = a_ref[pl.ds(t0, WSUM_UNROLL), :]
        b8 = b_ref[pl.ds(t0, WSUM_UNROLL), :]
        g8 = g_ref[pl.ds(t0, WSUM_UNROLL), :]
        for u in range(WSUM_UNROLL):
            at = jnp.where(sub == a8[u:u + 1], 1.0, 0.0).astype(jnp.bfloat16)
            bt = jnp.where(sub == b8[u:u + 1], g8[u:u + 1], 0.0).astype(jnp.bfloat16)
            o_ref[t0 + u] = lax.dot_general(at, bt, (((1,), (1,)), ((), ())), preferred_element_type=jnp.float32)
        return c

    lax.fori_loop(0, tb // WSUM_UNROLL, body, 0)


def peer_wsum(i1, i2, gate, tb=64):
    n, K = i1.shape
    tb = math.gcd(n, tb)
    assert tb % WSUM_UNROLL == 0
    return pl.pallas_call(
        functools.partial(_wsum_kernel, tb=tb),
        grid=(n // tb,),
        in_specs=[pl.BlockSpec((tb, K), lambda i: (i, 0))] * 3,
        out_specs=pl.BlockSpec((tb, PEER_NKEYS, PEER_NKEYS), lambda i: (i, 0, 0)),
        out_shape=jax.ShapeDtypeStruct((n, PEER_NKEYS, PEER_NKEYS), jnp.float32),
        compiler_params=pltpu.CompilerParams(dimension_semantics=("parallel",), vmem_limit_bytes=VMEM_LIMIT),
        name="peer_wsum",
    )(i1, i2, gate)


def _peer_kernel(x_ref, u_ref, v_ref, w_ref, o_ref, *, te):
    j = pl.program_id(1)
    x = x_ref[...]
    acc = None
    for c in range(te // PEER_CHUNK):
        e0 = c * PEER_CHUNK
        hid = lax.dot_general(x, u_ref[e0:e0 + PEER_CHUNK, :], (((1,), (1,)), ((), ())),
                              preferred_element_type=jnp.float32)
        parts = []
        for r in range(PEER_CHUNK // PEER_NKEYS):
            hr = hid[:, r * PEER_NKEYS:(r + 1) * PEER_NKEYS]
            ar = 0.5 * hr * (1.0 + lax.erf(hr * (2.0 ** -0.5))) * w_ref[:, e0 // PEER_NKEYS + r, :]
            parts.append(ar.astype(jnp.bfloat16))
        d = jnp.dot(jnp.concatenate(parts, axis=1), v_ref[e0:e0 + PEER_CHUNK, :], preferred_element_type=jnp.float32)
        acc = d if acc is None else acc + d
    @pl.when(j == 0)
    def _():
        o_ref[...] = acc

    @pl.when(j > 0)
    def _():
        o_ref[...] += acc


def peer_dense(xn, u, v, layer, wsum, tb=1024, te=1024):
    n, D = xn.shape
    E = u.shape[1]
    tb = math.gcd(n, tb)
    return pl.pallas_call(
        functools.partial(_peer_kernel, te=te),
        grid=(n // tb, E // te),
        in_specs=[pl.BlockSpec((tb, D), lambda i, j: (i, 0), pipeline_mode=pl.Buffered(1)),
                  pl.BlockSpec((None, te, D), lambda i, j: (layer, j, 0)),
                  pl.BlockSpec((None, te, D), lambda i, j: (layer, j, 0)),
                  pl.BlockSpec((tb, te // PEER_NKEYS, PEER_NKEYS), lambda i, j: (i, j, 0))],
        out_specs=pl.BlockSpec((tb, D), lambda i, j: (i, 0), pipeline_mode=pl.Buffered(1)),
        out_shape=jax.ShapeDtypeStruct((n, D), jnp.float32),
        compiler_params=pltpu.CompilerParams(dimension_semantics=("parallel", "arbitrary"),
                                             vmem_limit_bytes=VMEM_LIMIT),
        name="peer_dense",
    )(xn, u, v, wsum)


RT = LANE
NHC = 2 * PEER_HEADS
_CAND_GROUPS = [(0, 0), (0, 8), (1, 0)] + [(a, 0) for a in range(2, 8)] + [(-1, 0)]
assert PEER_TOPK == 16 and PEER_KEY_DIM // 2 == LANE and PEER_NKEYS == LANE


def _top_rounds(s, n_rounds, extra=()):
    R = s.shape[0]
    iota = lax.broadcasted_iota(jnp.int32, s.shape, 0)
    vals, idxs, ex = [], [], [[] for _ in extra]
    for _ in range(n_rounds):
        m = jnp.max(s, axis=0, keepdims=True)
        idx = jnp.min(jnp.where(s == m, iota, R), axis=0, keepdims=True)
        sel = iota == idx
        for e, lst in zip(extra, ex):
            lst.append(jnp.max(jnp.where(sel, e, -1), axis=0, keepdims=True))
        s = jnp.where(sel, -jnp.inf, s)
        vals.append(m)
        idxs.append(idx)
    cat = lambda l: jnp.concatenate(l, axis=0)
    return cat(vals), cat(idxs), [cat(l) for l in ex]


def _route_kernel(q_ref, sub_ref, e1_ref, e2_ref, g_ref, v_scr, i_scr):
    K = PEER_TOPK

    def stage1(h, c):
        for hc in (2 * h, 2 * h + 1):
            off = pl.multiple_of(hc * LANE, LANE)
            qb = q_ref[:, pl.ds(off, LANE)].astype(jnp.bfloat16)
            s = lax.dot_general(sub_ref[hc], qb, (((1,), (1,)), ((), ())), preferred_element_type=jnp.float32)
            v, i, _ = _top_rounds(s, K)
            v_scr[hc] = v
            i_scr[hc] = i
        return c

    lax.fori_loop(0, PEER_HEADS, stage1, 0)

    row8 = lax.broadcasted_iota(jnp.int32, (8, RT), 0)

    def stage2(h):
        v1, v2 = v_scr[2 * h], v_scr[2 * h + 1]
        i1, i2 = i_scr[2 * h], i_scr[2 * h + 1]
        cand, c1, c2 = [], [], []
        for a, b0 in _CAND_GROUPS:
            if a >= 0:
                nb = K // (a + 1)
                sm = v1[a:a + 1] + v2[b0:b0 + 8]
                if nb - b0 < 8:
                    sm = jnp.where(row8 < nb - b0, sm, -jnp.inf)
                cand.append(sm)
                c1.append(jnp.broadcast_to(i1[a:a + 1], (8, RT)))
                c2.append(i2[b0:b0 + 8])
            else:
                cand.append(v1[8:16] + v2[0:1])
                c1.append(i1[8:16])
                c2.append(jnp.broadcast_to(i2[0:1], (8, RT)))
        cat = lambda l: jnp.concatenate(l, axis=0)
        top, _, (e12,) = _top_rounds(cat(cand), K, extra=(cat(c1) * PEER_NKEYS + cat(c2),))
        ex = jnp.exp(top - top[0:1])
        g = ex / jnp.sum(ex, axis=0, keepdims=True)
        r0 = pl.multiple_of(h * K, K)
        e1_ref[0, pl.ds(r0, K), :] = e12 // PEER_NKEYS
        e2_ref[0, pl.ds(r0, K), :] = e12 % PEER_NKEYS
        g_ref[0, pl.ds(r0, K), :] = g

    def stage2_pair(hp, c):
        stage2(2 * hp)
        stage2(2 * hp + 1)
        return c

    lax.fori_loop(0, PEER_HEADS // 2, stage2_pair, 0)


def peer_route(q, sub_bf):
    n = q.shape[0]
    assert n % RT == 0
    nb = n // RT
    slots = PEER_HEADS * PEER_TOPK
    out = jax.ShapeDtypeStruct((nb, slots, RT), jnp.int32)
    ospec = pl.BlockSpec((1, slots, RT), lambda i: (i, 0, 0))
    return pl.pallas_call(
        _route_kernel,
        grid=(nb,),
        in_specs=[pl.BlockSpec((RT, q.shape[1]), lambda i: (i, 0)),
                  pl.BlockSpec(sub_bf.shape, lambda i: (0, 0, 0))],
        out_specs=[ospec, ospec, ospec],
        out_shape=[out, out, jax.ShapeDtypeStruct((nb, slots, RT), jnp.float32)],
        scratch_shapes=[pltpu.VMEM((NHC, PEER_TOPK, RT), jnp.float32),
                        pltpu.VMEM((NHC, PEER_TOPK, RT), jnp.int32)],
        compiler_params=pltpu.CompilerParams(dimension_semantics=("parallel",), vmem_limit_bytes=VMEM_LIMIT),
        name="peer_route",
    )(q, sub_bf)


def peer_ffn(h, gain, w_q, subkeys, u_bf, v_bf, layer):
    B, T, D = h.shape
    n = B * T
    q, xt = norm_mm(h.reshape(n, D), gain, w_q)
    sub_bf = subkeys.astype(jnp.bfloat16).reshape(NHC, PEER_NKEYS, PEER_KEY_DIM // 2)
    e1, e2, gate = peer_route(q, sub_bf)
    tok_major = lambda t: jnp.transpose(t, (0, 2, 1)).reshape(n, PEER_HEADS * PEER_TOPK)
    wsum = peer_wsum(tok_major(e1), tok_major(e2), tok_major(gate))
    out = peer_dense(xt, u_bf, v_bf, layer, wsum)
    return h + out.reshape(B, T, D)


KVG = 2 * NSA_GROUPS
ROW_W = KVG * NSA_HEAD_DIM
CPP = PAGE_SIZE // CMP_STRIDE
PPS = 8
assert CPP == 8 and PAGE_SIZE == LANE and NSA_HEAD_DIM == LANE


def _compress_kernel(pt_ref, *refs, n_steps):
    pages = refs[:PPS]
    w1_ref, c1_ref, w2_ref, b2_ref, kc_ref, vc_ref, xc_ref = refs[PPS:]
    s_idx = pl.program_id(1)
    for pair in range(PPS // 2):
        row0 = pl.multiple_of((s_idx * (PPS // 2) + pair) * 2 * CPP, 2 * CPP)
        for kvg in range(KVG):
            for s in range(CMP_STRIDE):
                lo = pages[2 * pair][0, pl.ds(s * KVG + kvg, CPP, stride=KVG * CMP_STRIDE), :]
                hi = pages[2 * pair + 1][0, pl.ds(s * KVG + kvg, CPP, stride=KVG * CMP_STRIDE), :]
                xc_ref[kvg, pl.ds(row0, 2 * CPP), pl.ds(s * LANE, LANE)] = (
                    jnp.concatenate([lo, hi], axis=0).astype(jnp.bfloat16))

    @pl.when(s_idx == n_steps - 1)
    def _():
        nch = xc_ref.shape[1]
        for kv in range(2):
            out_ref = kc_ref if kv == 0 else vc_ref
            for g in range(NSA_GROUPS):
                hh = jnp.dot(xc_ref[kv * NSA_GROUPS + g], w1_ref[kv], preferred_element_type=jnp.float32)
                h1 = hh[:, :CMP_HIDDEN]
                h2 = pltpu.roll(hh[:, CMP_HIDDEN:], nch - 1, 0)
                x = h1 + h2 + c1_ref[kv]
                hid = 0.5 * x * (1.0 + lax.erf(x * (2.0 ** -0.5)))
                o = jnp.dot(hid.astype(jnp.bfloat16), w2_ref[kv], preferred_element_type=jnp.float32) + b2_ref[kv]
                out_ref[0, g] = o.astype(out_ref.dtype)


def compress_paged(pool, page_table, w1, c1, w2, b2):
    B, NP = page_table.shape
    assert NP % PPS == 0
    n_steps = NP // PPS
    nch = NP * CPP
    dh = NSA_HEAD_DIM
    pool = pool.reshape(pool.shape[0], PAGE_SIZE * KVG, dh)

    def page_spec(u):
        return pl.BlockSpec((1, PAGE_SIZE * KVG, dh), lambda b, s, pt: (pt[b, s * PPS + u], 0, 0))

    full = lambda a: pl.BlockSpec(a.shape, lambda b, s, pt: (0,) * a.ndim)
    out_spec = pl.BlockSpec((1, NSA_GROUPS, nch, dh), lambda b, s, pt: (b, 0, 0, 0))
    grid_spec = pltpu.PrefetchScalarGridSpec(
        num_scalar_prefetch=1,
        grid=(B, n_steps),
        in_specs=[page_spec(u) for u in range(PPS)] + [full(w1), full(c1), full(w2), full(b2)],
        out_specs=[out_spec, out_spec],
        scratch_shapes=[pltpu.VMEM((KVG, nch, CMP_STRIDE * dh), jnp.bfloat16)],
    )
    out = jax.ShapeDtypeStruct((B, NSA_GROUPS, nch, dh), jnp.bfloat16)
    return pl.pallas_call(
        functools.partial(_compress_kernel, n_steps=n_steps),
        grid_spec=grid_spec,
        out_shape=[out, out],
        compiler_params=pltpu.CompilerParams(dimension_semantics=("parallel", "arbitrary"),
                                             vmem_limit_bytes=VMEM_LIMIT),
        name="nsa_compress",
    )(page_table, *([pool] * PPS), w1, c1, w2, b2)


def compress_params(P):
    dh, F = NSA_HEAD_DIM, CMP_HIDDEN
    w1s, c1s, w2s, b2s = [], [], [], []
    for n in ('k', 'v'):
        w1h = P['cmp_%s_w1' % n].reshape(2, CMP_STRIDE * dh, F)
        pe = P['cmp_%s_pe' % n].reshape(2, CMP_STRIDE * dh)
        w1s.append(jnp.concatenate([w1h[0], w1h[1]], axis=1))
        c1s.append(P['cmp_%s_b1' % n] + jnp.einsum('hk,hkf->f', pe, w1h, precision=lax.Precision.HIGHEST))
        w2s.append(P['cmp_%s_w2' % n])
        b2s.append(P['cmp_%s_b2' % n])
    return (jnp.stack(w1s).astype(jnp.bfloat16), jnp.stack(c1s)[:, None, :],
            jnp.stack(w2s).astype(jnp.bfloat16), jnp.stack(b2s)[:, None, :])


def _softmax_rows(s, valid):
    s = jnp.where(valid, s, NEG)
    m = jnp.max(s, axis=-1, keepdims=True)
    e = jnp.where(valid, jnp.exp(s - m), 0.0)
    return e / jnp.maximum(jnp.sum(e, axis=-1, keepdims=True), 1e-30)


HQ = NSA_HPG * TQ
GATE_ROWS = 8


def _lanes4(x):
    return jnp.concatenate([x] * NSA_HPG, axis=1)


def _nsa_cmp_sel_kernel(q_ref, kc_ref, vct_ref, bias_ref, ovt_ref, oct_ref, msel_ref, *, nc, ns):
    i = pl.program_id(2)
    ncp = kc_ref.shape[2]
    nsp = ovt_ref.shape[0]
    q_all = q_ref[0, 0].reshape(HQ, NSA_HEAD_DIM)
    cidx = lax.broadcasted_iota(jnp.int32, (ncp, TQ), 0)
    qpos = i * TQ + lax.broadcasted_iota(jnp.int32, (ncp, TQ), 1)
    valid = _lanes4((cidx * CMP_STRIDE + (CMP_BLOCK - 1) <= qpos) & (cidx < nc))
    s = lax.dot_general(kc_ref[0, 0], q_all, (((1,), (1,)), ((), ())), preferred_element_type=jnp.float32)
    s = jnp.where(valid, s + bias_ref[0, 0], NEG)
    m = jnp.max(s, axis=0, keepdims=True)
    e = jnp.where(valid, jnp.exp(s - m), 0.0)
    p = e / jnp.maximum(jnp.sum(e, axis=0, keepdims=True), 1e-30)
    oct_ref[0, 0, 0] = jnp.dot(vct_ref[0, 0], p.astype(jnp.bfloat16), preferred_element_type=jnp.float32)
    psum = p[:, 0:TQ]
    for hh in range(1, NSA_HPG):
        psum = psum + p[:, hh * TQ:(hh + 1) * TQ]
    imp = jnp.dot(ovt_ref[...], psum.astype(jnp.bfloat16), preferred_element_type=jnp.float32)
    blk = lax.broadcasted_iota(jnp.int32, (nsp, TQ), 0)
    qp = i * TQ + lax.broadcasted_iota(jnp.int32, (nsp, TQ), 1)
    cur = qp // SEL_BLOCK
    ok_blk = (blk * SEL_BLOCK <= qp) & (blk < ns)
    forced = (blk == 0) | (blk == cur) | (blk == cur - 1)
    score = jnp.where(ok_blk, jnp.where(forced, BIG, imp), -BIG)
    score = jnp.where(blk < ns, score, -jnp.inf)
    msel = jnp.zeros((nsp, TQ), jnp.float32)
    for _ in range(min(N_SEL, ns)):
        mx = jnp.max(score, axis=0, keepdims=True)
        idx = jnp.min(jnp.where(score == mx, blk, nsp), axis=0, keepdims=True)
        sel = blk == idx
        msel = jnp.where(sel & (mx > -0.5 * BIG), 1.0, msel)
        score = jnp.where(sel, -jnp.inf, score)
    msel_ref[0, 0] = msel


def nsa_cmp_sel(qh, kc, vct, bias_ct, ovt, nc, ns):
    B, G, HPG, T, dh = qh.shape
    ncp = kc.shape[2]
    nsp = ovt.shape[0]
    nqt = T // TQ
    return pl.pallas_call(
        functools.partial(_nsa_cmp_sel_kernel, nc=nc, ns=ns),
        grid=(B, G, nqt),
        in_specs=[pl.BlockSpec((1, 1, HPG, TQ, dh), lambda b, g, i: (b, g, 0, i, 0)),
                  pl.BlockSpec((1, 1, ncp, dh), lambda b, g, i: (b, g, 0, 0)),
                  pl.BlockSpec((1, 1, dh, ncp), lambda b, g, i: (b, g, 0, 0)),
                  pl.BlockSpec((1, 1, ncp, HQ), lambda b, g, i: (g, i, 0, 0)),
                  pl.BlockSpec((nsp, ncp), lambda b, g, i: (0, 0))],
        out_specs=[pl.BlockSpec((1, 1, 1, dh, HQ), lambda b, g, i: (b, g, i, 0, 0)),
                   pl.BlockSpec((1, 1, nsp, TQ), lambda b, g, i: (b, g, 0, i))],
        out_shape=[jax.ShapeDtypeStruct((B, G, nqt, dh, HQ), jnp.float32),
                   jax.ShapeDtypeStruct((B, G, nsp, T), jnp.float32)],
        compiler_params=pltpu.CompilerParams(dimension_semantics=("parallel", "parallel", "parallel"),
                                             vmem_limit_bytes=VMEM_LIMIT),
        name="nsa_cmp_sel",
    )(qh, kc, vct, bias_ct, ovt)


def _flash_tile_t(q_all, k, vt, bias, valid, m_ref, l_ref, acc_ref):
    s = lax.dot_general(k, q_all, (((1,), (1,)), ((), ())), preferred_element_type=jnp.float32)
    s = jnp.where(valid, s + bias, NEG)
    m_old = m_ref[...]
    m_new = jnp.maximum(m_old, jnp.max(s, axis=0, keepdims=True))
    p = jnp.where(valid, jnp.exp(s - m_new), 0.0)
    alpha = jnp.exp(m_old - m_new)
    l_ref[...] = alpha * l_ref[...] + jnp.sum(p, axis=0, keepdims=True)
    acc_ref[...] = alpha * acc_ref[...] + jnp.dot(vt, p.astype(jnp.bfloat16), preferred_element_type=jnp.float32)
    m_ref[...] = m_new


def _nsa_sw_kernel(q_ref, ks_ref, vst_ref, kw_ref, vwt_ref, msel_ref, expt_ref, bias_ref, selm_ref, winm_ref,
                   oct_ref, gate_ref, o_ref, msk_ref, m_ref, l_ref, acc_ref, *, nt):
    i = pl.program_id(2)
    q_all = q_ref[0, 0].reshape(HQ, NSA_HEAD_DIM)
    msel = msel_ref[0, 0].astype(jnp.bfloat16)
    for j in range(nt):
        msk_ref[j] = jnp.dot(expt_ref[j], msel, preferred_element_type=jnp.float32)

    def reset():
        m_ref[...] = jnp.full_like(m_ref, NEG)
        l_ref[...] = jnp.zeros_like(l_ref)
        acc_ref[...] = jnp.zeros_like(acc_ref)

    def result():
        return acc_ref[...] / jnp.maximum(l_ref[...], 1e-30)

    def pair(k_ref, vt_ref, j1, dd1, valid_of):
        j0 = j1 - 1
        j0c = jnp.maximum(j0, 0)
        k2 = jnp.concatenate([k_ref[0, 0, j1], k_ref[0, 0, j0c]], axis=0)
        vt2 = jnp.concatenate([vt_ref[0, 0, j1], vt_ref[0, 0, j0c]], axis=1)
        bias2 = jnp.concatenate([bias_ref[0, jnp.minimum(dd1, 2)], bias_ref[0, jnp.minimum(dd1 + 1, 2)]], axis=0)
        valid2 = _lanes4(jnp.concatenate([valid_of(j1, dd1), valid_of(j0c, dd1 + 1) & (j0 >= 0)], axis=0))
        _flash_tile_t(q_all, k2, vt2, bias2, valid2, m_ref, l_ref, acc_ref)

    reset()

    def sel_valid(j, dd):
        return (msk_ref[j] * selm_ref[jnp.minimum(dd, 2)]) > 0.5

    def sel_body(jj, c):
        pair(ks_ref, vst_ref, i - 2 * jj, 2 * jj, sel_valid)
        return c

    lax.fori_loop(0, i // 2 + 1, sel_body, 0)
    mix = gate_ref[0, 0, 0, 0:1, :] * oct_ref[0, 0, 0] + gate_ref[0, 0, 0, 1:2, :] * result()
    reset()
    for pp in range(WINDOW // TQ // 2 + 1):
        j1 = i - 2 * pp
        pair(kw_ref, vwt_ref, jnp.maximum(j1, 0), 2 * pp, lambda j, dd: (winm_ref[dd] > 0.5) & (j1 >= 0))
    mix = mix + gate_ref[0, 0, 0, 2:3, :] * result()
    for hh in range(NSA_HPG):
        o_ref[0, :, hh * NSA_HEAD_DIM:(hh + 1) * NSA_HEAD_DIM] = jnp.transpose(mix[:, hh * TQ:(hh + 1) * TQ])


def nsa_sel_win(qh, ks, vst, kw, vwt, msel_t, exp_t, bias_tt, selm_t, winm_t, oct, gates_t):
    B, G, HPG, T, dh = qh.shape
    nt = T // TQ
    nsp = msel_t.shape[2]
    k_spec = pl.BlockSpec((1, 1, nt, TQ, dh), lambda b, g, i: (b, g, 0, 0, 0))
    vt_spec = pl.BlockSpec((1, 1, nt, dh, TQ), lambda b, g, i: (b, g, 0, 0, 0))
    return pl.pallas_call(
        functools.partial(_nsa_sw_kernel, nt=nt),
        grid=(B, G, nt),
        in_specs=[pl.BlockSpec((1, 1, HPG, TQ, dh), lambda b, g, i: (b, g, 0, i, 0)),
                  k_spec, vt_spec, k_spec, vt_spec,
                  pl.BlockSpec((1, 1, nsp, TQ), lambda b, g, i: (b, g, 0, i)),
                  pl.BlockSpec((nt, TQ, nsp), lambda b, g, i: (0, 0, 0)),
                  pl.BlockSpec((1, 3, TQ, HQ), lambda b, g, i: (g, 0, 0, 0)),
                  pl.BlockSpec((3, TQ, TQ), lambda b, g, i: (0, 0, 0)),
                  pl.BlockSpec(winm_t.shape, lambda b, g, i: (0, 0, 0)),
                  pl.BlockSpec((1, 1, 1, dh, HQ), lambda b, g, i: (b, g, i, 0, 0)),
                  pl.BlockSpec((1, 1, 1, GATE_ROWS, HQ), lambda b, g, i: (b, g, i, 0, 0))],
        out_specs=pl.BlockSpec((1, TQ, HPG * dh), lambda b, g, i: (b, i, g)),
        out_shape=jax.ShapeDtypeStruct((B, T, G * HPG * dh), jnp.float32),
        scratch_shapes=[pltpu.VMEM((nt, TQ, TQ), jnp.float32),
                        pltpu.VMEM((1, HQ), jnp.float32),
                        pltpu.VMEM((1, HQ), jnp.float32),
                        pltpu.VMEM((dh, HQ), jnp.float32)],
        compiler_params=pltpu.CompilerParams(dimension_semantics=("parallel", "parallel", "arbitrary"),
                                             vmem_limit_bytes=VMEM_LIMIT),
        name="nsa_sel_win",
    )(qh, ks, vst, kw, vwt, msel_t, exp_t, bias_tt, selm_t, winm_t, oct, gates_t)


def nsa_prompt(q, gates, kc, vc, ks, vs, kw_rows, vw_rows, table):
    B, T = q.shape[0], q.shape[1]
    G, HPG, dh = NSA_GROUPS, NSA_HPG, NSA_HEAD_DIM
    Nc = T // CMP_STRIDE - 1
    Ns = -(-T // SEL_BLOCK)
    ncp = kc.shape[2]
    assert ncp % LANE == 0 and ncp >= Nc
    nsp = -(-Ns // 16) * 16
    nqt = T // TQ
    bf = jnp.bfloat16
    qh = jnp.transpose(q.reshape(B, T, G, HPG, dh), (0, 2, 3, 1, 4)).astype(bf)
    qpos = jnp.arange(T)
    cidx = jnp.arange(ncp)
    bias_c = head_bias(table, qpos[:, None] - (cidx * CMP_STRIDE + (CMP_BLOCK - 1))[None, :])
    bias_ct = jnp.transpose(bias_c.reshape(G, HPG, nqt, TQ, ncp), (0, 2, 4, 1, 3)).reshape(G, nqt, ncp, HQ)
    sidx = jnp.arange(nsp)
    ovt = (((cidx * CMP_STRIDE)[None, :] < (sidx * SEL_BLOCK + SEL_BLOCK)[:, None])
           & ((cidx * CMP_STRIDE + CMP_BLOCK)[None, :] > (sidx * SEL_BLOCK)[:, None])
           & (cidx[None, :] < Nc) & (sidx[:, None] < Ns)).astype(bf)
    oct, msel_t = nsa_cmp_sel(qh, kc, jnp.transpose(vc, (0, 1, 3, 2)), bias_ct, ovt, Nc, Ns)
    exp_t = ((jnp.arange(T) // SEL_BLOCK)[:, None] == sidx[None, :]).astype(bf).reshape(nqt, TQ, nsp)
    r = jnp.arange(TQ)
    rel3 = (jnp.arange(3) * TQ)[:, None, None] + r[None, None, :] - r[None, :, None]
    bias_tt = jnp.einsum('dkqn,nh->hdkq', jax.nn.one_hot(t5_bucket(rel3), REL_BUCKETS, dtype=jnp.float32),
                         table.astype(jnp.float32), precision=lax.Precision.HIGHEST)
    bias_tt = jnp.transpose(bias_tt.reshape(G, HPG, 3, TQ, TQ), (0, 2, 3, 1, 4)).reshape(G, 3, TQ, HQ)
    selm_t = (rel3 >= 0).astype(jnp.float32)
    nw = 2 * (WINDOW // TQ // 2 + 1)
    relw = (jnp.arange(nw) * TQ)[:, None, None] + r[None, None, :] - r[None, :, None]
    winm_t = ((relw >= 0) & (relw < WINDOW)).astype(jnp.float32)

    def k_tiles(t):
        return jnp.transpose(t, (0, 2, 1, 3)).astype(bf).reshape(B, G, nqt, TQ, dh)

    def vt_tiles(t):
        return jnp.transpose(t.astype(bf).reshape(B, nqt, TQ, G, dh), (0, 3, 1, 4, 2))

    gates_t = jnp.transpose(gates.reshape(B, nqt, TQ, G, HPG, 3), (0, 3, 1, 5, 4, 2)).reshape(B, G, nqt, 3, HQ)
    gates_t = jnp.pad(gates_t, ((0, 0), (0, 0), (0, 0), (0, GATE_ROWS - 3), (0, 0)))
    return nsa_sel_win(qh, k_tiles(ks), vt_tiles(vs), k_tiles(kw_rows), vt_tiles(vw_rows), msel_t, exp_t, bias_tt,
                       selm_t, winm_t, oct, gates_t)


def _dec_cmp_kernel(q_ref, kc_ref, vc_ref, bias_ref, valid_ref, ov_ref, oc_ref, msel_ref, *, t, q0, ns):
    valid = valid_ref[...] > 0.5
    nsp = ov_ref.shape[1]
    imps = []
    for g in range(NSA_GROUPS):
        s = lax.dot_general(q_ref[0, g], kc_ref[0, g], (((1,), (1,)), ((), ())), preferred_element_type=jnp.float32)
        p = _softmax_rows(s + bias_ref[g], valid)
        oc_ref[0, g] = jnp.dot(p.astype(jnp.bfloat16), vc_ref[0, g], preferred_element_type=jnp.float32)
        psum = p[0:t]
        for hh in range(1, NSA_HPG):
            psum = psum + p[hh * t:(hh + 1) * t]
        imps.append(jnp.dot(psum.astype(jnp.bfloat16), ov_ref[...], preferred_element_type=jnp.float32))
    imp = jnp.concatenate(imps, axis=0)
    blk = lax.broadcasted_iota(jnp.int32, (NSA_GROUPS * t, nsp), 1)
    qp = q0 + jnp.concatenate([lax.broadcasted_iota(jnp.int32, (t, nsp), 0)] * NSA_GROUPS, axis=0)
    cur = qp // SEL_BLOCK
    ok_blk = (blk * SEL_BLOCK <= qp) & (blk < ns)
    forced = (blk == 0) | (blk == cur) | (blk == cur - 1)
    score = jnp.where(ok_blk, jnp.where(forced, BIG, imp), -BIG)
    score = jnp.where(blk < ns, score, -jnp.inf)
    msel = jnp.zeros((NSA_GROUPS * t, nsp), jnp.float32)
    for _ in range(min(N_SEL, ns)):
        mx = jnp.max(score, axis=1, keepdims=True)
        idx = jnp.min(jnp.where(score == mx, blk, nsp), axis=1, keepdims=True)
        sel = blk == idx
        msel = jnp.where(sel & (mx > -0.5 * BIG), 1.0, msel)
        score = jnp.where(sel, -jnp.inf, score)
    for g in range(NSA_GROUPS):
        msel_ref[0, g] = msel[g * t:(g + 1) * t]


def dec_cmp(qg, kc, vc, bias_c, valid_c, overlap, t, q0, ns):
    B, G, R_, dh = qg.shape
    ncp, nsp = overlap.shape
    per_b = lambda a: pl.BlockSpec((1,) + a.shape[1:], lambda b: (b,) + (0,) * (a.ndim - 1))
    full = lambda a: pl.BlockSpec(a.shape, lambda b: (0,) * a.ndim)
    return pl.pallas_call(
        functools.partial(_dec_cmp_kernel, t=t, q0=q0, ns=ns),
        grid=(B,),
        in_specs=[per_b(qg), per_b(kc), per_b(vc), full(bias_c), full(valid_c), full(overlap)],
        out_specs=[pl.BlockSpec((1, G, R_, dh), lambda b: (b, 0, 0, 0)),
                   pl.BlockSpec((1, G, t, nsp), lambda b: (b, 0, 0, 0))],
        out_shape=[jax.ShapeDtypeStruct((B, G, R_, dh), jnp.float32),
                   jax.ShapeDtypeStruct((B, G, t, nsp), jnp.float32)],
        compiler_params=pltpu.CompilerParams(dimension_semantics=("parallel",), vmem_limit_bytes=VMEM_LIMIT),
        name="nsa_dec_cmp",
    )(qg, kc, vc, bias_c, valid_c, overlap)


def _flash_update(s, valid, v, m_ref, l_ref, acc_ref, g):
    s = jnp.where(valid, s, NEG)
    m_old = m_ref[g]
    m_new = jnp.maximum(m_old, jnp.max(s, axis=-1, keepdims=True))
    p = jnp.where(valid, jnp.exp(s - m_new), 0.0)
    alpha = jnp.exp(m_old - m_new)
    l_ref[g] = alpha * l_ref[g] + jnp.sum(p, axis=-1, keepdims=True)
    acc_ref[g] = alpha * acc_ref[g] + jnp.dot(p.astype(jnp.bfloat16), v, preferred_element_type=jnp.float32)
    m_ref[g] = m_new


def _dec_sel_kernel(pt_ref, q_ref, *refs, n_steps, t):
    pages = refs[:PPS]
    new_ref, bias_ref, mask_ref, biasn_ref, maskn_ref, o_ref, m_ref, l_ref, acc_ref = refs[PPS:]
    s_idx = pl.program_id(1)
    dh = NSA_HEAD_DIM

    @pl.when(s_idx == 0)
    def _():
        m_ref[...] = jnp.full_like(m_ref, NEG)
        l_ref[...] = jnp.zeros_like(l_ref)
        acc_ref[...] = jnp.zeros_like(acc_ref)

    def tile_mask(mref, g):
        mk = mref[0, g] > 0.5
        return jnp.concatenate([mk] * NSA_HPG, axis=0)

    for g in range(NSA_GROUPS):
        rows_of = lambda pg, kvg: pg[0, pl.ds(kvg, PAGE_SIZE, stride=KVG), :]
        k = jnp.concatenate([rows_of(pg, g) for pg in pages], axis=0).astype(jnp.bfloat16)
        v = jnp.concatenate([rows_of(pg, NSA_GROUPS + g) for pg in pages], axis=0).astype(jnp.bfloat16)
        s = lax.dot_general(q_ref[0, g], k, (((1,), (1,)), ((), ())), preferred_element_type=jnp.float32)
        _flash_update(s + bias_ref[g], tile_mask(mask_ref, g), v, m_ref, l_ref, acc_ref, g)

    @pl.when(s_idx == n_steps - 1)
    def _():
        for g in range(NSA_GROUPS):
            k = new_ref[0, :, pl.ds(g * dh, dh)].astype(jnp.bfloat16)
            v = new_ref[0, :, pl.ds((NSA_GROUPS + g) * dh, dh)].astype(jnp.bfloat16)
            s = lax.dot_general(q_ref[0, g], k, (((1,), (1,)), ((), ())), preferred_element_type=jnp.float32)
            _flash_update(s + biasn_ref[g], tile_mask(maskn_ref, g), v, m_ref, l_ref, acc_ref, g)
            o_ref[0, g] = acc_ref[g] / jnp.maximum(l_ref[g], 1e-30)


def dec_sel(qg, pool, page_table, new_rows, bias_p, mask_p, bias_n, mask_n, t):
    B, G, R_, dh = qg.shape
    NP = page_table.shape[1]
    assert NP % PPS == 0
    n_steps = NP // PPS
    W = PPS * PAGE_SIZE
    pool = pool.reshape(pool.shape[0], PAGE_SIZE * KVG, dh)

    def page_spec(u):
        return pl.BlockSpec((1, PAGE_SIZE * KVG, dh), lambda b, s, pt: (pt[b, s * PPS + u], 0, 0))

    grid_spec = pltpu.PrefetchScalarGridSpec(
        num_scalar_prefetch=1,
        grid=(B, n_steps),
        in_specs=[pl.BlockSpec((1, G, R_, dh), lambda b, s, pt: (b, 0, 0, 0))]
        + [page_spec(u) for u in range(PPS)]
        + [pl.BlockSpec((1, PAGE_SIZE, ROW_W), lambda b, s, pt: (b, 0, 0)),
           pl.BlockSpec((G, R_, W), lambda b, s, pt: (0, 0, s)),
           pl.BlockSpec((1, G, t, W), lambda b, s, pt: (b, 0, 0, s)),
           pl.BlockSpec((G, R_, PAGE_SIZE), lambda b, s, pt: (0, 0, 0)),
           pl.BlockSpec((1, G, t, PAGE_SIZE), lambda b, s, pt: (b, 0, 0, 0))],
        out_specs=pl.BlockSpec((1, G, R_, dh), lambda b, s, pt: (b, 0, 0, 0)),
        scratch_shapes=[pltpu.VMEM((G, R_, 1), jnp.float32), pltpu.VMEM((G, R_, 1), jnp.float32),
                        pltpu.VMEM((G, R_, dh), jnp.float32)],
    )
    return pl.pallas_call(
        functools.partial(_dec_sel_kernel, n_steps=n_steps, t=t),
        grid_spec=grid_spec,
        out_shape=jax.ShapeDtypeStruct((B, G, R_, dh), jnp.float32),
        compiler_params=pltpu.CompilerParams(dimension_semantics=("parallel", "arbitrary"),
                                             vmem_limit_bytes=VMEM_LIMIT),
        name="nsa_dec_sel",
    )(page_table, qg, *([pool] * PPS), new_rows, bias_p, mask_p, bias_n, mask_n)


def _dec_win_kernel(q_ref, win_ref, new_ref, bias_ref, valid_ref, o_ref):
    dh = NSA_HEAD_DIM
    valid = valid_ref[...] > 0.5
    for g in range(NSA_GROUPS):
        wn = win_ref.shape[1] // KVG
        k = jnp.concatenate([win_ref[0, pl.ds(g, wn, stride=KVG), :], new_ref[0, :, pl.ds(g * dh, dh)]], axis=0)
        v = jnp.concatenate([win_ref[0, pl.ds(NSA_GROUPS + g, wn, stride=KVG), :],
                             new_ref[0, :, pl.ds((NSA_GROUPS + g) * dh, dh)]], axis=0)
        s = lax.dot_general(q_ref[0, g], k.astype(jnp.bfloat16), (((1,), (1,)), ((), ())),
                            preferred_element_type=jnp.float32)
        p = _softmax_rows(s + bias_ref[g], valid)
        o_ref[0, g] = jnp.dot(p.astype(jnp.bfloat16), v.astype(jnp.bfloat16), preferred_element_type=jnp.float32)


def dec_win(qg, win_rows, new_rows, bias_w, valid_w):
    B, G, R_, dh = qg.shape
    per_b = lambda a: pl.BlockSpec((1,) + a.shape[1:], lambda b: (b,) + (0,) * (a.ndim - 1))
    full = lambda a: pl.BlockSpec(a.shape, lambda b: (0,) * a.ndim)
    return pl.pallas_call(
        _dec_win_kernel,
        grid=(B,),
        in_specs=[per_b(qg), per_b(win_rows), per_b(new_rows), full(bias_w), full(valid_w)],
        out_specs=pl.BlockSpec((1, G, R_, dh), lambda b: (b, 0, 0, 0)),
        out_shape=jax.ShapeDtypeStruct((B, G, R_, dh), jnp.float32),
        compiler_params=pltpu.CompilerParams(dimension_semantics=("parallel",), vmem_limit_bytes=VMEM_LIMIT),
        name="nsa_dec_win",
    )(qg, win_rows, new_rows, bias_w, valid_w)


def nsa_decode(q, gates, kc, vc, slc_pool, page_table, slc_new, win_buf, win_new_rows, table):
    B, t = q.shape[0], q.shape[1]
    G, HPG, dh = NSA_GROUPS, NSA_HPG, NSA_HEAD_DIM
    NP = page_table.shape[1]
    past = NP * PAGE_SIZE
    Wn = win_buf.shape[1]
    assert t <= PAGE_SIZE and Wn == WINDOW
    L = past + t
    nc = L // CMP_STRIDE - 1
    ncp = kc.shape[2]
    Ns = -(-L // SEL_BLOCK)
    nsp = -(-Ns // LANE) * LANE
    bf = jnp.bfloat16
    R_ = HPG * t
    qg = jnp.transpose(q.reshape(B, t, G, HPG, dh), (0, 2, 3, 1, 4)).reshape(B, G, R_, dh).astype(bf)
    qpos = past + jnp.arange(t)

    r_lo, r_hi = -(PAGE_SIZE + WINDOW), past + t
    lut_rev = table[t5_bucket(jnp.arange(r_hi, r_lo - 1, -1))].astype(jnp.float32)

    def rows_bias(p0, step, n):
        rows = []
        for i in range(t):
            j0 = r_hi - (past + i - p0)
            assert j0 >= 0 and j0 + step * (n - 1) < lut_rev.shape[0]
            rows.append(lax.slice(lut_rev, (j0, 0), (j0 + step * (n - 1) + 1, NSA_HEADS), (step, 1)))
        b = jnp.transpose(jnp.stack(rows), (2, 0, 1))
        return b.reshape(G, R_, n)

    rep = lambda m: jnp.tile(m, (HPG, 1))
    cidx = jnp.arange(ncp)
    cend = cidx * CMP_STRIDE + (CMP_BLOCK - 1)
    rel_c = qpos[:, None] - cend[None, :]
    valid_c = rep(((rel_c >= 0) & (cidx[None, :] < nc)).astype(jnp.float32))
    cstart = cidx[:, None] * CMP_STRIDE
    sstart = jnp.arange(nsp)[None, :] * SEL_BLOCK
    overlap = ((cstart < sstart + SEL_BLOCK) & (cstart + CMP_BLOCK > sstart)
               & (cidx[:, None] < nc) & (jnp.arange(nsp)[None, :] < Ns)).astype(bf)
    o_c, msel = dec_cmp(qg, kc, vc, rows_bias(CMP_BLOCK - 1, CMP_STRIDE, ncp), valid_c, overlap, t, past, Ns)
    msel = msel[..., :Ns]
    kpos = jnp.arange(past + PAGE_SIZE)
    nblk = -(-(past + PAGE_SIZE) // SEL_BLOCK)
    mkey = jnp.repeat(jnp.pad(msel, ((0, 0), (0, 0), (0, 0), (0, nblk - Ns))), SEL_BLOCK, axis=-1)
    mkey = mkey[..., :past + PAGE_SIZE]
    mkey = mkey * ((kpos[None, :] <= qpos[:, None]) & (kpos[None, :] < L)).astype(jnp.float32)
    bias_s = rows_bias(0, 1, past + PAGE_SIZE)
    pad_new = lambda r: jnp.pad(r, ((0, 0), (0, PAGE_SIZE - t), (0, 0)))
    o_s = dec_sel(qg, slc_pool, page_table, pad_new(slc_new), bias_s[..., :past], mkey[..., :past],
                  bias_s[..., past:], mkey[..., past:], t)
    wpos = jnp.concatenate([past - Wn + jnp.arange(Wn), past + jnp.arange(PAGE_SIZE)])
    rel_w = qpos[:, None] - wpos[None, :]
    valid_w = rep(((rel_w >= 0) & (rel_w < WINDOW) & (wpos[None, :] >= 0) & (wpos[None, :] < L)).astype(jnp.float32))
    bias_w = jnp.concatenate([rows_bias(past - Wn, 1, Wn), rows_bias(past, 1, PAGE_SIZE)], axis=-1)
    o_w = dec_win(qg, win_buf.reshape(B, Wn * KVG, dh), pad_new(win_new_rows), bias_w, valid_w)
    back = lambda o: jnp.transpose(o.reshape(B, G, HPG, t, dh), (0, 3, 1, 2, 4)).reshape(B, t, G * HPG * dh)
    gx = jnp.repeat(gates.reshape(B, t, G * HPG, 3), dh, axis=2)
    return gx[..., 0] * back(o_c) + gx[..., 1] * back(o_s) + gx[..., 2] * back(o_w)


def kv_rows(h, P):
    B, T, D = h.shape
    return norm_mm(h.reshape(B * T, D), P['norm_kv'], P['kv_w'])[0].reshape(B, T, 3, ROW_W)


def run_trunk(x, gla_s0, past, P):
    B, T, _ = x.shape
    h = x
    gla_states = []
    kv5 = lambda r: r.reshape(r.shape[0], r.shape[1], 2, NSA_GROUPS, NSA_HEAD_DIM)
    for layer in range(DEPTH):
        if layer < N_A_LAYERS:
            o, s = gla_mixer(h, P['norm_mix'][layer], gla_s0[layer], P['gla_w_in'][layer],
                             P['gla_w_gate_up'][layer], P['gla_b_gate'][layer], P['gla_norm'][layer],
                             P['gla_w_out'][layer])
            gla_states.append(s)
        else:
            j = layer - N_A_LAYERS
            qd = NSA_HEADS * NSA_HEAD_DIM
            proj = norm_mm(h.reshape(B * T, -1), P['norm_mix'][layer], P['nsa_w_in'][j])[0].reshape(B, T, -1)
            q = proj[..., :qd].reshape(B, T, NSA_HEADS, NSA_HEAD_DIM) * NSA_HEAD_DIM ** -0.5
            gates = jax.nn.sigmoid(proj[..., qd:].astype(jnp.float32)).reshape(B, T, NSA_HEADS, 3).astype(x.dtype)
            if past is None:
                r5 = rows.reshape(B, T, 3, 2, NSA_GROUPS, NSA_HEAD_DIM)
                att = nsa_prompt(q, gates, kc, vc, r5[:, :, 1, 0], r5[:, :, 1, 1], r5[:, :, 2, 0], r5[:, :, 2, 1],
                                 P['rel_bias'])
            else:
                att = nsa_decode(q, gates, kc, vc, past['slc_pool'], past['page_table'], rows[:, :, 1],
                                 past['win_buf'], rows[:, :, 2], P['rel_bias'])
            o = mm(att, P['nsa_w_out'][j])
        h = h + o
        h = peer_ffn(h, P['norm_ffn'][layer], P['peer_w_q'][layer], P['peer_subkeys'][layer],
                     P['peer_u_bf'], P['peer_v_bf'], layer)
        if layer == N_A_LAYERS - 1:
            rows = kv_rows(h, P)
            if past is None:
                assert T % (PAGE_SIZE * PPS) == 0 and T % TQ == 0
                npg = T // PAGE_SIZE
                pool = rows[:, :, 0].reshape(B * npg, PAGE_SIZE, 2, NSA_GROUPS, NSA_HEAD_DIM)
                table = jnp.arange(B * npg, dtype=jnp.int32).reshape(B, npg)
                win_new = kv5(rows[:, T - min(WINDOW, T):, 2])
            else:
                past_len = past['page_table'].shape[1] * PAGE_SIZE
                assert (past_len + T) // CMP_STRIDE == past_len // CMP_STRIDE
                pool, table = past['cmp_pool'], past['page_table']
                win_all = jnp.concatenate([past['win_buf'], kv5(rows[:, :, 2])], axis=1)
                win_new = win_all[:, win_all.shape[1] - min(WINDOW, win_all.shape[1]):]
            kc, vc = compress_paged(pool, table, *P['cmp_stack'])
    return (rmsnorm(h, P['norm_final']), jnp.stack(gla_states), kv5(rows[:, :, 0]), kv5(rows[:, :, 1]), win_new)


def kernel(x_prompt, x_sample, state_gla, cache_cmp_kv, cache_slc_kv, cache_win_kv, page_table,
           norm_mix, norm_ffn, norm_kv, norm_final, gla_w_in, gla_w_gate_up, gla_b_gate, gla_norm, gla_w_out,
           kv_w, cmp_k_w1, cmp_k_b1, cmp_k_w2, cmp_k_b2, cmp_k_pe, cmp_v_w1, cmp_v_b1, cmp_v_w2, cmp_v_b2, cmp_v_pe,
           nsa_w_in, nsa_w_out, rel_bias, peer_w_q, peer_subkeys, peer_u, peer_v):
    P = dict(norm_mix=norm_mix, norm_ffn=norm_ffn, norm_kv=norm_kv, norm_final=norm_final,
             gla_w_in=gla_w_in, gla_w_gate_up=gla_w_gate_up, gla_b_gate=gla_b_gate, gla_norm=gla_norm,
             gla_w_out=gla_w_out, kv_w=kv_w,
             cmp_k_w1=cmp_k_w1, cmp_k_b1=cmp_k_b1, cmp_k_w2=cmp_k_w2, cmp_k_b2=cmp_k_b2, cmp_k_pe=cmp_k_pe,
             cmp_v_w1=cmp_v_w1, cmp_v_b1=cmp_v_b1, cmp_v_w2=cmp_v_w2, cmp_v_b2=cmp_v_b2, cmp_v_pe=cmp_v_pe,
             nsa_w_in=nsa_w_in, nsa_w_out=nsa_w_out, rel_bias=rel_bias,
             peer_w_q=peer_w_q, peer_subkeys=peer_subkeys,
             peer_u_bf=peer_u.astype(jnp.bfloat16), peer_v_bf=peer_v.astype(jnp.bfloat16))
    P['cmp_stack'] = compress_params(P)
    past = dict(cmp_pool=cache_cmp_kv, slc_pool=cache_slc_kv, win_buf=cache_win_kv,
                page_table=page_table.astype(jnp.int32))
    gla_zero = jnp.zeros((N_A_LAYERS, BATCH, GLA_HEADS, GLA_DK, GLA_DV), x_prompt.dtype)
    y_prompt, gla_p, cmp_p, slc_p, win_p = run_trunk(x_prompt, gla_zero, None, P)
    y_sample, gla_s, cmp_s, slc_s, win_s = run_trunk(x_sample, state_gla, past, P)
    return (y_prompt, y_sample, gla_p, gla_s, cmp_p, cmp_s, slc_p, slc_s, win_p, win_s)
```

```python
import functools
import math

import jax
import jax.numpy as jnp
from jax import lax
from jax.experimental import pallas as pl
from jax.experimental.pallas import tpu as pltpu

D_MODEL = 2048
BATCH = 4
SEQ = 2048
DEPTH = 2
DEC_BATCH = 32
DEC_SEQ = 8
PAST_LEN = 8192
PAGE_SIZE = 128
N_A_LAYERS = DEPTH // 2
N_B_LAYERS = DEPTH - N_A_LAYERS
GLA_HEADS = 4
GLA_DK = D_MODEL // (2 * GLA_HEADS)
GLA_DV = D_MODEL // GLA_HEADS
GLA_GATE_RANK = 16
GLA_TAU = 16.0
GLA_CHUNK = 64
NSA_HEADS = 16
NSA_GROUPS = 4
NSA_HPG = NSA_HEADS // NSA_GROUPS
NSA_HEAD_DIM = D_MODEL // NSA_HEADS
CMP_STRIDE = 16
CMP_BLOCK = 2 * CMP_STRIDE
CMP_HIDDEN = 2 * NSA_HEAD_DIM
SEL_BLOCK = 64
N_SEL = 16
WINDOW = 512
Q_BLOCK = 32
REL_BUCKETS = 32
REL_MAX_DIST = 128
PEER_HEADS = 8
PEER_NKEYS = 128
PEER_EXPERTS = PEER_NKEYS * PEER_NKEYS
PEER_TOPK = 16
PEER_KEY_DIM = 256
PEER_TOK_BLOCK = 128
EPS = 1e-6
NEG = -1e30
BIG = 1e30

LANE = 128
VMEM_LIMIT = 56 * 1024 * 1024
TQ = 128
assert REL_MAX_DIST <= TQ and WINDOW % TQ == 0 and TQ % SEL_BLOCK == 0


def _mm_kernel(a_ref, b_ref, o_ref):
    a = a_ref[...].astype(jnp.bfloat16)
    b = b_ref[...].astype(jnp.bfloat16)
    o_ref[...] = jnp.dot(a, b, preferred_element_type=jnp.float32)


def pmm(a, b, tm=512, tn=512, keep_cols=False):
    M, K = a.shape
    N = b.shape[1]
    tm = min(tm, M)
    Mp = -(-M // tm) * tm
    Np = -(-N // LANE) * LANE
    wide = [d * LANE for d in range(1, 2 * tn // LANE + 1) if Np % (d * LANE) == 0]
    if max(wide) < tn:
        Np = -(-N // tn) * tn
        wide = [tn]
    tn = max(wide)
    if Mp != M:
        a = jnp.pad(a, ((0, Mp - M), (0, 0)))
    if Np != N:
        b = jnp.pad(b, ((0, 0), (0, Np - N)))
    out = pl.pallas_call(
        _mm_kernel,
        grid=(Mp // tm, Np // tn),
        in_specs=[pl.BlockSpec((tm, K), lambda i, j: (i, 0)),
                  pl.BlockSpec((K, tn), lambda i, j: (0, j))],
        out_specs=pl.BlockSpec((tm, tn), lambda i, j: (i, j)),
        out_shape=jax.ShapeDtypeStruct((Mp, Np), jnp.float32),
        compiler_params=pltpu.CompilerParams(
            dimension_semantics=("parallel", "parallel"), vmem_limit_bytes=VMEM_LIMIT),
        name="proj_matmul",
    )(a, b)
    return out[:M] if keep_cols else out[:M, :N]


def mm(x, w):
    lead = x.shape[:-1]
    return pmm(x.reshape(-1, x.shape[-1]), w).reshape(lead + (w.shape[1],))


def _norm_mm_kernel(a_ref, g_ref, b_ref, o_ref, xn_ref):
    @pl.when(pl.program_id(1) == 0)
    def _():
        x = a_ref[...]
        y = x * lax.rsqrt(jnp.mean(x * x, axis=-1, keepdims=True) + EPS) * g_ref[...]
        xn_ref[...] = y.astype(jnp.bfloat16)

    o_ref[...] = jnp.dot(xn_ref[...], b_ref[...].astype(jnp.bfloat16), preferred_element_type=jnp.float32)


def norm_mm(x, gain, b, tm=1024, tn=512, keep_cols=False):
    M, K = x.shape
    N = b.shape[1]
    tm = math.gcd(M, tm)
    Np = -(-N // LANE) * LANE
    wide = [d * LANE for d in range(1, 2 * tn // LANE + 1) if Np % (d * LANE) == 0]
    if max(wide) < tn:
        Np = -(-N // tn) * tn
        wide = [tn]
    tn = max(wide)
    if Np != N:
        b = jnp.pad(b, ((0, 0), (0, Np - N)))
    out, xn = pl.pallas_call(
        _norm_mm_kernel,
        grid=(M // tm, Np // tn),
        in_specs=[pl.BlockSpec((tm, K), lambda i, j: (i, 0)),
                  pl.BlockSpec((1, K), lambda i, j: (0, 0)),
                  pl.BlockSpec((K, tn), lambda i, j: (0, j))],
        out_specs=[pl.BlockSpec((tm, tn), lambda i, j: (i, j)),
                   pl.BlockSpec((tm, K), lambda i, j: (i, 0))],
        out_shape=[jax.ShapeDtypeStruct((M, Np), jnp.float32), jax.ShapeDtypeStruct((M, K), jnp.bfloat16)],
        compiler_params=pltpu.CompilerParams(
            dimension_semantics=("parallel", "arbitrary"), vmem_limit_bytes=VMEM_LIMIT),
        name="norm_matmul",
    )(x, gain.reshape(1, K).astype(jnp.float32), b)
    return (out if keep_cols else out[:, :N]), xn


def rmsnorm(x, g):
    xf = x.astype(jnp.float32)
    y = xf * lax.rsqrt(jnp.mean(xf * xf, axis=-1, keepdims=True) + EPS)
    return (y * g.astype(jnp.float32)).astype(x.dtype)


def masked_softmax(s, mask):
    s = jnp.where(mask, s, NEG)
    m = jnp.max(s, axis=-1, keepdims=True)
    e = jnp.where(mask, jnp.exp(s - m), 0.0)
    return e / jnp.maximum(jnp.sum(e, axis=-1, keepdims=True), 1e-30)


def t5_bucket(rel):
    n = jnp.maximum(rel, 0)
    exact = REL_BUCKETS // 2
    nf = jnp.maximum(n, exact).astype(jnp.float32)
    large = exact + (jnp.log(nf / exact) / math.log(REL_MAX_DIST / exact) * (REL_BUCKETS - exact)).astype(jnp.int32)
    large = jnp.minimum(large, REL_BUCKETS - 1)
    return jnp.where(n < exact, n, large)


def head_bias(table, rel):
    onehot = jax.nn.one_hot(t5_bucket(rel), REL_BUCKETS, dtype=jnp.float32)
    b = jnp.einsum('qkn,nh->hqk', onehot, table.astype(jnp.float32), precision=lax.Precision.HIGHEST)
    return b.reshape(NSA_GROUPS, NSA_HPG, rel.shape[0], rel.shape[1])


DK_ALL = GLA_HEADS * GLA_DK
DV_ALL = GLA_HEADS * GLA_DV
GZ_COL = 2 * DK_ALL + 2 * DV_ALL
assert GZ_COL % LANE == 0 and GLA_GATE_RANK <= LANE and DK_ALL % LANE == 0 and (2 * DK_ALL) % DV_ALL == 0


def _gla_kernel(q_ref, k_ref, v_ref, r_ref, gz_ref, wg_ref, bg_ref, ng_ref, tri_ref, s0_ref,
                o_ref, sfin_ref, s_scr, *, C, t_valid):
    c = pl.program_id(1)

    @pl.when(c == 0)
    def _():
        s_scr[...] = s0_ref[0]

    row = lax.broadcasted_iota(jnp.int32, (C, C), 0)
    col = lax.broadcasted_iota(jnp.int32, (C, C), 1)
    bf = jnp.bfloat16
    z_all = jnp.dot(gz_ref[0].astype(bf), wg_ref[...], preferred_element_type=jnp.float32) + bg_ref[...]
    for h in range(GLA_HEADS):
        dk = slice(h * GLA_DK, (h + 1) * GLA_DK)
        dv = slice(h * GLA_DV, (h + 1) * GLA_DV)
        z = z_all[:, dk]
        g = (jnp.minimum(z, 0.0) - jnp.log(1.0 + jnp.exp(-jnp.abs(z)))) * (1.0 / GLA_TAU)
        if t_valid is not None:
            g = jnp.where(lax.broadcasted_iota(jnp.int32, g.shape, 0) < t_valid, g, 0.0)
        b = jnp.dot(tri_ref[...], g, preferred_element_type=jnp.float32, precision=lax.Precision.HIGHEST)
        b_last = b[C - 1:C, :]
        q = q_ref[0, :, dk] * (GLA_DK ** -0.5)
        k = k_ref[0, :, dk]
        v = v_ref[0, :, dv].astype(bf)
        qe = (q * jnp.exp(b)).astype(bf)
        ke = (k * jnp.exp(-b)).astype(bf)
        a = lax.dot_general(qe, ke, (((1,), (1,)), ((), ())), preferred_element_type=jnp.float32)
        a = jnp.where(col <= row, a, 0.0)
        s = s_scr[h]
        o = (jnp.dot(a.astype(bf), v, preferred_element_type=jnp.float32)
             + jnp.dot(qe, s.astype(bf), preferred_element_type=jnp.float32))
        kd = (k * jnp.exp(b_last - b)).astype(bf)
        dcol = jnp.exp(jnp.transpose(jnp.broadcast_to(b_last, (LANE, GLA_DK))))
        s_new = (s * jnp.concatenate([dcol] * (GLA_DV // LANE), axis=1)
                 + lax.dot_general(kd, v, (((0,), (0,)), ((), ())), preferred_element_type=jnp.float32))
        s_scr[h] = s_new
        o = o * lax.rsqrt(jnp.mean(o * o, axis=-1, keepdims=True) + EPS) * ng_ref[...]
        r = r_ref[0, :, dv]
        o_ref[0, :, dv] = o * (r * (1.0 / (1.0 + jnp.exp(-r))))

    @pl.when(c == pl.num_programs(1) - 1)
    def _():
        sfin_ref[0] = s_scr[...]


def gla_core(proj, s0, w_gate_up, b_gate, norm_g, T):
    B, Tp, _ = proj.shape
    C = GLA_CHUNK if T % GLA_CHUNK == 0 else Tp
    assert Tp % C == 0 and C % 16 == 0 and (Tp == T or Tp == C)
    n = Tp // C
    wg = jnp.pad(w_gate_up, ((0, LANE - GLA_GATE_RANK), (0, 0))).astype(jnp.bfloat16)
    tri = (jnp.arange(C)[:, None] >= jnp.arange(C)[None, :]).astype(jnp.float32)
    col_blk = lambda w, i: pl.BlockSpec((1, C, w), lambda b, c: (b, c, i))
    full = lambda a: pl.BlockSpec(a.shape, lambda b, c: (0,) * a.ndim)
    bg = b_gate.reshape(1, DK_ALL)
    ng = norm_g.reshape(1, GLA_DV)
    st_spec = pl.BlockSpec((1, GLA_HEADS, GLA_DK, GLA_DV), lambda b, c: (b, 0, 0, 0))
    return pl.pallas_call(
        functools.partial(_gla_kernel, C=C, t_valid=None if Tp == T else T),
        grid=(B, n),
        in_specs=[col_blk(DK_ALL, 0), col_blk(DK_ALL, 1), col_blk(DV_ALL, 2 * DK_ALL // DV_ALL),
                  col_blk(DV_ALL, (2 * DK_ALL + DV_ALL) // DV_ALL), col_blk(LANE, GZ_COL // LANE),
                  full(wg), full(bg), full(ng), full(tri), st_spec],
        out_specs=[pl.BlockSpec((1, C, DV_ALL), lambda b, c: (b, c, 0)), st_spec],
        out_shape=[jax.ShapeDtypeStruct((B, Tp, DV_ALL), jnp.float32),
                   jax.ShapeDtypeStruct((B, GLA_HEADS, GLA_DK, GLA_DV), jnp.float32)],
        scratch_shapes=[pltpu.VMEM((GLA_HEADS, GLA_DK, GLA_DV), jnp.float32)],
        compiler_params=pltpu.CompilerParams(dimension_semantics=("parallel", "arbitrary"),
                                             vmem_limit_bytes=VMEM_LIMIT),
        name="gla_core",
    )(proj, proj, proj, proj, proj, wg, bg, ng, tri, s0)


def gla_mixer(h, gain, s0, w_in, w_gate_up, b_gate, norm_g, w_out):
    B, T, D = h.shape
    proj, _ = norm_mm(h.reshape(B * T, D), gain, w_in, keep_cols=True)
    assert proj.shape[1] >= GZ_COL + LANE
    Tp = -(-T // 16) * 16
    proj = jnp.pad(proj.reshape(B, T, -1), ((0, 0), (0, Tp - T), (0, 0)))
    o, s_new = gla_core(proj, s0.astype(jnp.float32), w_gate_up, b_gate, norm_g, T)
    return mm(o[:, :T], w_out), s_new.astype(s0.dtype)


WSUM_UNROLL = 16
PEER_CHUNK = 512


def _wsum_kernel(a_ref, b_ref, g_ref, o_ref, *, tb):
    sub = lax.broadcasted_iota(jnp.int32, (PEER_NKEYS, PEER_NKEYS), 0)

    def body(tt, c):
        t0 = pl.multiple_of(tt * WSUM_UNROLL, WSUM_UNROLL)
        a8 = a_ref[pl.ds(t0, WSUM_UNROLL), :]
        b8 = b_ref[pl.ds(t0, WSUM_UNROLL), :]
        g8 = g_ref[pl.ds(t0, WSUM_UNROLL), :]
        for u in range(WSUM_UNROLL):
            at = jnp.where(sub == a8[u:u + 1], 1.0, 0.0).astype(jnp.bfloat16)
            bt = jnp.where(sub == b8[u:u + 1], g8[u:u + 1], 0.0).astype(jnp.bfloat16)
            o_ref[t0 + u] = lax.dot_general(at, bt, (((1,), (1,)), ((), ())), preferred_element_type=jnp.float32)
        return c

    lax.fori_loop(0, tb // WSUM_UNROLL, body, 0)


def peer_wsum(i1, i2, gate, tb=64):
    n, K = i1.shape
    tb = math.gcd(n, tb)
    assert tb % WSUM_UNROLL == 0
    return pl.pallas_call(
        functools.partial(_wsum_kernel, tb=tb),
        grid=(n // tb,),
        in_specs=[pl.BlockSpec((tb, K), lambda i: (i, 0))] * 3,
        out_specs=pl.BlockSpec((tb, PEER_NKEYS, PEER_NKEYS), lambda i: (i, 0, 0)),
        out_shape=jax.ShapeDtypeStruct((n, PEER_NKEYS, PEER_NKEYS), jnp.float32),
        compiler_params=pltpu.CompilerParams(dimension_semantics=("parallel",), vmem_limit_bytes=VMEM_LIMIT),
        name="peer_wsum",
    )(i1, i2, gate)


def _peer_kernel(x_ref, u_ref, v_ref, w_ref, o_ref, *, te):
    j = pl.program_id(1)
    x = x_ref[...]
    acc = None
    for c in range(te // PEER_CHUNK):
        e0 = c * PEER_CHUNK
        hid = lax.dot_general(x, u_ref[e0:e0 + PEER_CHUNK, :], (((1,), (1,)), ((), ())),
                              preferred_element_type=jnp.float32)
        parts = []
        for r in range(PEER_CHUNK // PEER_NKEYS):
            hr = hid[:, r * PEER_NKEYS:(r + 1) * PEER_NKEYS]
            ar = 0.5 * hr * (1.0 + lax.erf(hr * (2.0 ** -0.5))) * w_ref[:, e0 // PEER_NKEYS + r, :]
            parts.append(ar.astype(jnp.bfloat16))
        d = jnp.dot(jnp.concatenate(parts, axis=1), v_ref[e0:e0 + PEER_CHUNK, :], preferred_element_type=jnp.float32)
        acc = d if acc is None else acc + d
    @pl.when(j == 0)
    def _():
        o_ref[...] = acc

    @pl.when(j > 0)
    def _():
        o_ref[...] += acc


def peer_dense(xn, u, v, layer, wsum, tb=1024, te=1024):
    n, D = xn.shape
    E = u.shape[1]
    tb = math.gcd(n, tb)
    return pl.pallas_call(
        functools.partial(_peer_kernel, te=te),
        grid=(n // tb, E // te),
        in_specs=[pl.BlockSpec((tb, D), lambda i, j: (i, 0), pipeline_mode=pl.Buffered(1)),
                  pl.BlockSpec((None, te, D), lambda i, j: (layer, j, 0)),
                  pl.BlockSpec((None, te, D), lambda i, j: (layer, j, 0)),
                  pl.BlockSpec((tb, te // PEER_NKEYS, PEER_NKEYS), lambda i, j: (i, j, 0))],
        out_specs=pl.BlockSpec((tb, D), lambda i, j: (i, 0), pipeline_mode=pl.Buffered(1)),
        out_shape=jax.ShapeDtypeStruct((n, D), jnp.float32),
        compiler_params=pltpu.CompilerParams(dimension_semantics=("parallel", "arbitrary"),
                                             vmem_limit_bytes=VMEM_LIMIT),
        name="peer_dense",
    )(xn, u, v, wsum)


RT = LANE
NHC = 2 * PEER_HEADS
ROUTE_CHAINS = 4
_CAND_GROUPS = [(0, 0), (0, 8), (1, 0)] + [(a, 0) for a in range(2, 8)] + [(-1, 0)]
assert PEER_TOPK == 16 and PEER_KEY_DIM // 2 == LANE and PEER_NKEYS == LANE


def _top_rounds(s, n_rounds, extra=()):
    R = s.shape[0]
    iota = lax.broadcasted_iota(jnp.int32, s.shape, 0)
    vals, idxs, ex = [], [], [[] for _ in extra]
    for _ in range(n_rounds):
        m = jnp.max(s, axis=0, keepdims=True)
        idx = jnp.min(jnp.where(s == m, iota, R), axis=0, keepdims=True)
        sel = iota == idx
        for e, lst in zip(extra, ex):
            lst.append(jnp.max(jnp.where(sel, e, -1), axis=0, keepdims=True))
        s = jnp.where(sel, -jnp.inf, s)
        vals.append(m)
        idxs.append(idx)
    cat = lambda l: jnp.concatenate(l, axis=0)
    return cat(vals), cat(idxs), [cat(l) for l in ex]


def _route_kernel(q_ref, sub_ref, e1_ref, e2_ref, g_ref, v_scr, i_scr):
    K = PEER_TOPK

    def stage1(hp, c):
        for u in range(ROUTE_CHAINS):
            hc = ROUTE_CHAINS * hp + u
            off = pl.multiple_of(hc * LANE, LANE)
            qb = q_ref[:, pl.ds(off, LANE)].astype(jnp.bfloat16)
            s = lax.dot_general(sub_ref[hc], qb, (((1,), (1,)), ((), ())), preferred_element_type=jnp.float32)
            v, i, _ = _top_rounds(s, K)
            v_scr[hc] = v
            i_scr[hc] = i
        return c

    lax.fori_loop(0, NHC // ROUTE_CHAINS, stage1, 0)

    row8 = lax.broadcasted_iota(jnp.int32, (8, RT), 0)

    def stage2(h):
        v1, v2 = v_scr[2 * h], v_scr[2 * h + 1]
        i1, i2 = i_scr[2 * h], i_scr[2 * h + 1]
        cand, c1, c2 = [], [], []
        for a, b0 in _CAND_GROUPS:
            if a >= 0:
                nb = K // (a + 1)
                sm = v1[a:a + 1] + v2[b0:b0 + 8]
                if nb - b0 < 8:
                    sm = jnp.where(row8 < nb - b0, sm, -jnp.inf)
                cand.append(sm)
                c1.append(jnp.broadcast_to(i1[a:a + 1], (8, RT)))
                c2.append(i2[b0:b0 + 8])
            else:
                cand.append(v1[8:16] + v2[0:1])
                c1.append(i1[8:16])
                c2.append(jnp.broadcast_to(i2[0:1], (8, RT)))
        cat = lambda l: jnp.concatenate(l, axis=0)
        top, _, (e12,) = _top_rounds(cat(cand), K, extra=(cat(c1) * PEER_NKEYS + cat(c2),))
        ex = jnp.exp(top - top[0:1])
        g = ex / jnp.sum(ex, axis=0, keepdims=True)
        r0 = pl.multiple_of(h * K, K)
        e1_ref[0, pl.ds(r0, K), :] = e12 // PEER_NKEYS
        e2_ref[0, pl.ds(r0, K), :] = e12 % PEER_NKEYS
        g_ref[0, pl.ds(r0, K), :] = g

    def stage2_pair(hp, c):
        stage2(2 * hp)
        stage2(2 * hp + 1)
        return c

    lax.fori_loop(0, PEER_HEADS // 2, stage2_pair, 0)


def peer_route(q, sub_bf):
    n = q.shape[0]
    assert n % RT == 0
    nb = n // RT
    slots = PEER_HEADS * PEER_TOPK
    out = jax.ShapeDtypeStruct((nb, slots, RT), jnp.int32)
    ospec = pl.BlockSpec((1, slots, RT), lambda i: (i, 0, 0))
    return pl.pallas_call(
        _route_kernel,
        grid=(nb,),
        in_specs=[pl.BlockSpec((RT, q.shape[1]), lambda i: (i, 0)),
                  pl.BlockSpec(sub_bf.shape, lambda i: (0, 0, 0))],
        out_specs=[ospec, ospec, ospec],
        out_shape=[out, out, jax.ShapeDtypeStruct((nb, slots, RT), jnp.float32)],
        scratch_shapes=[pltpu.VMEM((NHC, PEER_TOPK, RT), jnp.float32),
                        pltpu.VMEM((NHC, PEER_TOPK, RT), jnp.int32)],
        compiler_params=pltpu.CompilerParams(dimension_semantics=("parallel",), vmem_limit_bytes=VMEM_LIMIT),
        name="peer_route",
    )(q, sub_bf)


def peer_ffn(h, gain, w_q, subkeys, u_bf, v_bf, layer):
    B, T, D = h.shape
    n = B * T
    q, xt = norm_mm(h.reshape(n, D), gain, w_q)
    sub_bf = subkeys.astype(jnp.bfloat16).reshape(NHC, PEER_NKEYS, PEER_KEY_DIM // 2)
    e1, e2, gate = peer_route(q, sub_bf)
    tok_major = lambda t: jnp.transpose(t, (0, 2, 1)).reshape(n, PEER_HEADS * PEER_TOPK)
    wsum = peer_wsum(tok_major(e1), tok_major(e2), tok_major(gate))
    out = peer_dense(xt, u_bf, v_bf, layer, wsum)
    return h + out.reshape(B, T, D)


KVG = 2 * NSA_GROUPS
ROW_W = KVG * NSA_HEAD_DIM
CPP = PAGE_SIZE // CMP_STRIDE
PPS = 8
assert CPP == 8 and PAGE_SIZE == LANE and NSA_HEAD_DIM == LANE


def _compress_kernel(pt_ref, *refs, n_steps):
    pages = refs[:PPS]
    w1_ref, c1_ref, w2_ref, b2_ref, kc_ref, vc_ref, xc_ref = refs[PPS:]
    s_idx = pl.program_id(1)
    for pair in range(PPS // 2):
        row0 = pl.multiple_of((s_idx * (PPS // 2) + pair) * 2 * CPP, 2 * CPP)
        for kvg in range(KVG):
            for s in range(CMP_STRIDE):
                lo = pages[2 * pair][0, pl.ds(s * KVG + kvg, CPP, stride=KVG * CMP_STRIDE), :]
                hi = pages[2 * pair + 1][0, pl.ds(s * KVG + kvg, CPP, stride=KVG * CMP_STRIDE), :]
                xc_ref[kvg, pl.ds(row0, 2 * CPP), pl.ds(s * LANE, LANE)] = (
                    jnp.concatenate([lo, hi], axis=0).astype(jnp.bfloat16))

    @pl.when(s_idx == n_steps - 1)
    def _():
        nch = xc_ref.shape[1]
        for kv in range(2):
            out_ref = kc_ref if kv == 0 else vc_ref
            for g in range(NSA_GROUPS):
                hh = jnp.dot(xc_ref[kv * NSA_GROUPS + g], w1_ref[kv], preferred_element_type=jnp.float32)
                h1 = hh[:, :CMP_HIDDEN]
                h2 = pltpu.roll(hh[:, CMP_HIDDEN:], nch - 1, 0)
                x = h1 + h2 + c1_ref[kv]
                hid = 0.5 * x * (1.0 + lax.erf(x * (2.0 ** -0.5)))
                o = jnp.dot(hid.astype(jnp.bfloat16), w2_ref[kv], preferred_element_type=jnp.float32) + b2_ref[kv]
                out_ref[0, g] = o.astype(out_ref.dtype)


def compress_paged(pool, page_table, w1, c1, w2, b2):
    B, NP = page_table.shape
    assert NP % PPS == 0
    n_steps = NP // PPS
    nch = NP * CPP
    dh = NSA_HEAD_DIM
    pool = pool.reshape(pool.shape[0], PAGE_SIZE * KVG, dh)

    def page_spec(u):
        return pl.BlockSpec((1, PAGE_SIZE * KVG, dh), lambda b, s, pt: (pt[b, s * PPS + u], 0, 0))

    full = lambda a: pl.BlockSpec(a.shape, lambda b, s, pt: (0,) * a.ndim)
    out_spec = pl.BlockSpec((1, NSA_GROUPS, nch, dh), lambda b, s, pt: (b, 0, 0, 0))
    grid_spec = pltpu.PrefetchScalarGridSpec(
        num_scalar_prefetch=1,
        grid=(B, n_steps),
        in_specs=[page_spec(u) for u in range(PPS)] + [full(w1), full(c1), full(w2), full(b2)],
        out_specs=[out_spec, out_spec],
        scratch_shapes=[pltpu.VMEM((KVG, nch, CMP_STRIDE * dh), jnp.bfloat16)],
    )
    out = jax.ShapeDtypeStruct((B, NSA_GROUPS, nch, dh), jnp.bfloat16)
    return pl.pallas_call(
        functools.partial(_compress_kernel, n_steps=n_steps),
        grid_spec=grid_spec,
        out_shape=[out, out],
        compiler_params=pltpu.CompilerParams(dimension_semantics=("parallel", "arbitrary"),
                                             vmem_limit_bytes=VMEM_LIMIT),
        name="nsa_compress",
    )(page_table, *([pool] * PPS), w1, c1, w2, b2)


def compress_params(P):
    dh, F = NSA_HEAD_DIM, CMP_HIDDEN
    w1s, c1s, w2s, b2s = [], [], [], []
    for n in ('k', 'v'):
        w1h = P['cmp_%s_w1' % n].reshape(2, CMP_STRIDE * dh, F)
        pe = P['cmp_%s_pe' % n].reshape(2, CMP_STRIDE * dh)
        w1s.append(jnp.concatenate([w1h[0], w1h[1]], axis=1))
        c1s.append(P['cmp_%s_b1' % n] + jnp.einsum('hk,hkf->f', pe, w1h, precision=lax.Precision.HIGHEST))
        w2s.append(P['cmp_%s_w2' % n])
        b2s.append(P['cmp_%s_b2' % n])
    return (jnp.stack(w1s).astype(jnp.bfloat16), jnp.stack(c1s)[:, None, :],
            jnp.stack(w2s).astype(jnp.bfloat16), jnp.stack(b2s)[:, None, :])


def _softmax_rows(s, valid):
    s = jnp.where(valid, s, NEG)
    m = jnp.max(s, axis=-1, keepdims=True)
    e = jnp.where(valid, jnp.exp(s - m), 0.0)
    return e / jnp.maximum(jnp.sum(e, axis=-1, keepdims=True), 1e-30)


HQ = NSA_HPG * TQ
GATE_ROWS = 8


def _lanes4(x):
    return jnp.concatenate([x] * NSA_HPG, axis=1)


def _nsa_cmp_sel_kernel(q_ref, kc_ref, vct_ref, bias_ref, ovt_ref, oct_ref, msel_ref, *, nc, ns):
    i = pl.program_id(2)
    ncp = kc_ref.shape[2]
    nsp = ovt_ref.shape[0]
    q_all = q_ref[0, 0].reshape(HQ, NSA_HEAD_DIM)
    cidx = lax.broadcasted_iota(jnp.int32, (ncp, TQ), 0)
    qpos = i * TQ + lax.broadcasted_iota(jnp.int32, (ncp, TQ), 1)
    valid = _lanes4((cidx * CMP_STRIDE + (CMP_BLOCK - 1) <= qpos) & (cidx < nc))
    s = lax.dot_general(kc_ref[0, 0], q_all, (((1,), (1,)), ((), ())), preferred_element_type=jnp.float32)
    s = jnp.where(valid, s + bias_ref[0, 0], NEG)
    m = jnp.max(s, axis=0, keepdims=True)
    e = jnp.where(valid, jnp.exp(s - m), 0.0)
    p = e / jnp.maximum(jnp.sum(e, axis=0, keepdims=True), 1e-30)
    oct_ref[0, 0, 0] = jnp.dot(vct_ref[0, 0], p.astype(jnp.bfloat16), preferred_element_type=jnp.float32)
    psum = p[:, 0:TQ]
    for hh in range(1, NSA_HPG):
        psum = psum + p[:, hh * TQ:(hh + 1) * TQ]
    imp = jnp.dot(ovt_ref[...], psum.astype(jnp.bfloat16), preferred_element_type=jnp.float32)
    blk = lax.broadcasted_iota(jnp.int32, (nsp, TQ), 0)
    qp = i * TQ + lax.broadcasted_iota(jnp.int32, (nsp, TQ), 1)
    cur = qp // SEL_BLOCK
    ok_blk = (blk * SEL_BLOCK <= qp) & (blk < ns)
    forced = (blk == 0) | (blk == cur) | (blk == cur - 1)
    score = jnp.where(ok_blk, jnp.where(forced, BIG, imp), -BIG)
    score = jnp.where(blk < ns, score, -jnp.inf)
    msel = jnp.zeros((nsp, TQ), jnp.float32)
    for _ in range(min(N_SEL, ns)):
        mx = jnp.max(score, axis=0, keepdims=True)
        idx = jnp.min(jnp.where(score == mx, blk, nsp), axis=0, keepdims=True)
        sel = blk == idx
        msel = jnp.where(sel & (mx > -0.5 * BIG), 1.0, msel)
        score = jnp.where(sel, -jnp.inf, score)
    msel_ref[0, 0] = msel


def nsa_cmp_sel(qh, kc, vct, bias_ct, ovt, nc, ns):
    B, G, HPG, T, dh = qh.shape
    ncp = kc.shape[2]
    nsp = ovt.shape[0]
    nqt = T // TQ
    return pl.pallas_call(
        functools.partial(_nsa_cmp_sel_kernel, nc=nc, ns=ns),
        grid=(B, G, nqt),
        in_specs=[pl.BlockSpec((1, 1, HPG, TQ, dh), lambda b, g, i: (b, g, 0, i, 0)),
                  pl.BlockSpec((1, 1, ncp, dh), lambda b, g, i: (b, g, 0, 0)),
                  pl.BlockSpec((1, 1, dh, ncp), lambda b, g, i: (b, g, 0, 0)),
                  pl.BlockSpec((1, 1, ncp, HQ), lambda b, g, i: (g, i, 0, 0)),
                  pl.BlockSpec((nsp, ncp), lambda b, g, i: (0, 0))],
        out_specs=[pl.BlockSpec((1, 1, 1, dh, HQ), lambda b, g, i: (b, g, i, 0, 0)),
                   pl.BlockSpec((1, 1, nsp, TQ), lambda b, g, i: (b, g, 0, i))],
        out_shape=[jax.ShapeDtypeStruct((B, G, nqt, dh, HQ), jnp.float32),
                   jax.ShapeDtypeStruct((B, G, nsp, T), jnp.float32)],
        compiler_params=pltpu.CompilerParams(dimension_semantics=("parallel", "parallel", "parallel"),
                                             vmem_limit_bytes=VMEM_LIMIT),
        name="nsa_cmp_sel",
    )(qh, kc, vct, bias_ct, ovt)


def _flash_tile_t(q_all, k, vt, bias, valid, m_ref, l_ref, acc_ref):
    s = lax.dot_general(k, q_all, (((1,), (1,)), ((), ())), preferred_element_type=jnp.float32)
    s = jnp.where(valid, s + bias, NEG)
    m_old = m_ref[...]
    m_new = jnp.maximum(m_old, jnp.max(s, axis=0, keepdims=True))
    p = jnp.where(valid, jnp.exp(s - m_new), 0.0)
    alpha = jnp.exp(m_old - m_new)
    l_ref[...] = alpha * l_ref[...] + jnp.sum(p, axis=0, keepdims=True)
    acc_ref[...] = alpha * acc_ref[...] + jnp.dot(vt, p.astype(jnp.bfloat16), preferred_element_type=jnp.float32)
    m_ref[...] = m_new


def _nsa_sw_kernel(q_ref, ks_ref, vst_ref, kw_ref, vwt_ref, msel_ref, expt_ref, bias_ref, selm_ref, winm_ref,
                   oct_ref, gate_ref, o_ref, msk_ref, m_ref, l_ref, acc_ref, *, nt):
    i = pl.program_id(2)
    q_all = q_ref[0, 0].reshape(HQ, NSA_HEAD_DIM)
    msel = msel_ref[0, 0].astype(jnp.bfloat16)
    for j in range(nt):
        msk_ref[j] = jnp.dot(expt_ref[j], msel, preferred_element_type=jnp.float32)

    def reset():
        m_ref[...] = jnp.full_like(m_ref, NEG)
        l_ref[...] = jnp.zeros_like(l_ref)
        acc_ref[...] = jnp.zeros_like(acc_ref)

    def result():
        return acc_ref[...] / jnp.maximum(l_ref[...], 1e-30)

    def pair(k_ref, vt_ref, j1, dd1, valid_of):
        j0 = j1 - 1
        j0c = jnp.maximum(j0, 0)
        k2 = jnp.concatenate([k_ref[0, 0, j1], k_ref[0, 0, j0c]], axis=0)
        vt2 = jnp.concatenate([vt_ref[0, 0, j1], vt_ref[0, 0, j0c]], axis=1)
        bias2 = jnp.concatenate([bias_ref[0, jnp.minimum(dd1, 2)], bias_ref[0, jnp.minimum(dd1 + 1, 2)]], axis=0)
        valid2 = _lanes4(jnp.concatenate([valid_of(j1, dd1), valid_of(j0c, dd1 + 1) & (j0 >= 0)], axis=0))
        _flash_tile_t(q_all, k2, vt2, bias2, valid2, m_ref, l_ref, acc_ref)

    reset()

    def sel_valid(j, dd):
        return (msk_ref[j] * selm_ref[jnp.minimum(dd, 2)]) > 0.5

    def sel_body(jj, c):
        pair(ks_ref, vst_ref, i - 2 * jj, 2 * jj, sel_valid)
        return c

    lax.fori_loop(0, i // 2 + 1, sel_body, 0)
    mix = gate_ref[0, 0, 0, 0:1, :] * oct_ref[0, 0, 0] + gate_ref[0, 0, 0, 1:2, :] * result()
    reset()
    for pp in range(WINDOW // TQ // 2 + 1):
        j1 = i - 2 * pp
        pair(kw_ref, vwt_ref, jnp.maximum(j1, 0), 2 * pp, lambda j, dd: (winm_ref[dd] > 0.5) & (j1 >= 0))
    mix = mix + gate_ref[0, 0, 0, 2:3, :] * result()
    for hh in range(NSA_HPG):
        o_ref[0, :, hh * NSA_HEAD_DIM:(hh + 1) * NSA_HEAD_DIM] = jnp.transpose(mix[:, hh * TQ:(hh + 1) * TQ])


def nsa_sel_win(qh, ks, vst, kw, vwt, msel_t, exp_t, bias_tt, selm_t, winm_t, oct, gates_t):
    B, G, HPG, T, dh = qh.shape
    nt = T // TQ
    nsp = msel_t.shape[2]
    k_spec = pl.BlockSpec((1, 1, nt, TQ, dh), lambda b, g, i: (b, g, 0, 0, 0))
    vt_spec = pl.BlockSpec((1, 1, nt, dh, TQ), lambda b, g, i: (b, g, 0, 0, 0))
    return pl.pallas_call(
        functools.partial(_nsa_sw_kernel, nt=nt),
        grid=(B, G, nt),
        in_specs=[pl.BlockSpec((1, 1, HPG, TQ, dh), lambda b, g, i: (b, g, 0, i, 0)),
                  k_spec, vt_spec, k_spec, vt_spec,
                  pl.BlockSpec((1, 1, nsp, TQ), lambda b, g, i: (b, g, 0, i)),
                  pl.BlockSpec((nt, TQ, nsp), lambda b, g, i: (0, 0, 0)),
                  pl.BlockSpec((1, 3, TQ, HQ), lambda b, g, i: (g, 0, 0, 0)),
                  pl.BlockSpec((3, TQ, TQ), lambda b, g, i: (0, 0, 0)),
                  pl.BlockSpec(winm_t.shape, lambda b, g, i: (0, 0, 0)),
                  pl.BlockSpec((1, 1, 1, dh, HQ), lambda b, g, i: (b, g, i, 0, 0)),
                  pl.BlockSpec((1, 1, 1, GATE_ROWS, HQ), lambda b, g, i: (b, g, i, 0, 0))],
        out_specs=pl.BlockSpec((1, TQ, HPG * dh), lambda b, g, i: (b, i, g)),
        out_shape=jax.ShapeDtypeStruct((B, T, G * HPG * dh), jnp.float32),
        scratch_shapes=[pltpu.VMEM((nt, TQ, TQ), jnp.float32),
                        pltpu.VMEM((1, HQ), jnp.float32),
                        pltpu.VMEM((1, HQ), jnp.float32),
                        pltpu.VMEM((dh, HQ), jnp.float32)],
        compiler_params=pltpu.CompilerParams(dimension_semantics=("parallel", "parallel", "arbitrary"),
                                             vmem_limit_bytes=VMEM_LIMIT),
        name="nsa_sel_win",
    )(qh, ks, vst, kw, vwt, msel_t, exp_t, bias_tt, selm_t, winm_t, oct, gates_t)


def nsa_prompt(q, gates, kc, vc, ks, vs, kw_rows, vw_rows, table):
    B, T = q.shape[0], q.shape[1]
    G, HPG, dh = NSA_GROUPS, NSA_HPG, NSA_HEAD_DIM
    Nc = T // CMP_STRIDE - 1
    Ns = -(-T // SEL_BLOCK)
    ncp = kc.shape[2]
    assert ncp % LANE == 0 and ncp >= Nc
    nsp = -(-Ns // 16) * 16
    nqt = T // TQ
    bf = jnp.bfloat16
    qh = jnp.transpose(q.reshape(B, T, G, HPG, dh), (0, 2, 3, 1, 4)).astype(bf)
    qpos = jnp.arange(T)
    cidx = jnp.arange(ncp)
    bias_c = head_bias(table, qpos[:, None] - (cidx * CMP_STRIDE + (CMP_BLOCK - 1))[None, :])
    bias_ct = jnp.transpose(bias_c.reshape(G, HPG, nqt, TQ, ncp), (0, 2, 4, 1, 3)).reshape(G, nqt, ncp, HQ)
    sidx = jnp.arange(nsp)
    ovt = (((cidx * CMP_STRIDE)[None, :] < (sidx * SEL_BLOCK + SEL_BLOCK)[:, None])
           & ((cidx * CMP_STRIDE + CMP_BLOCK)[None, :] > (sidx * SEL_BLOCK)[:, None])
           & (cidx[None, :] < Nc) & (sidx[:, None] < Ns)).astype(bf)
    oct, msel_t = nsa_cmp_sel(qh, kc, jnp.transpose(vc, (0, 1, 3, 2)), bias_ct, ovt, Nc, Ns)
    exp_t = ((jnp.arange(T) // SEL_BLOCK)[:, None] == sidx[None, :]).astype(bf).reshape(nqt, TQ, nsp)
    r = jnp.arange(TQ)
    rel3 = (jnp.arange(3) * TQ)[:, None, None] + r[None, None, :] - r[None, :, None]
    bias_tt = jnp.einsum('dkqn,nh->hdkq', jax.nn.one_hot(t5_bucket(rel3), REL_BUCKETS, dtype=jnp.float32),
                         table.astype(jnp.float32), precision=lax.Precision.HIGHEST)
    bias_tt = jnp.transpose(bias_tt.reshape(G, HPG, 3, TQ, TQ), (0, 2, 3, 1, 4)).reshape(G, 3, TQ, HQ)
    selm_t = (rel3 >= 0).astype(jnp.float32)
    nw = 2 * (WINDOW // TQ // 2 + 1)
    relw = (jnp.arange(nw) * TQ)[:, None, None] + r[None, None, :] - r[None, :, None]
    winm_t = ((relw >= 0) & (relw < WINDOW)).astype(jnp.float32)

    def k_tiles(t):
        return jnp.transpose(t, (0, 2, 1, 3)).astype(bf).reshape(B, G, nqt, TQ, dh)

    def vt_tiles(t):
        return jnp.transpose(t.astype(bf).reshape(B, nqt, TQ, G, dh), (0, 3, 1, 4, 2))

    gates_t = jnp.transpose(gates.reshape(B, nqt, TQ, G, HPG, 3), (0, 3, 1, 5, 4, 2)).reshape(B, G, nqt, 3, HQ)
    gates_t = jnp.pad(gates_t, ((0, 0), (0, 0), (0, 0), (0, GATE_ROWS - 3), (0, 0)))
    return nsa_sel_win(qh, k_tiles(ks), vt_tiles(vs), k_tiles(kw_rows), vt_tiles(vw_rows), msel_t, exp_t, bias_tt,
                       selm_t, winm_t, oct, gates_t)


def _dec_cmp_kernel(q_ref, kc_ref, vc_ref, bias_ref, valid_ref, ov_ref, oc_ref, msel_ref, *, t, q0, ns):
    valid = valid_ref[...] > 0.5
    nsp = ov_ref.shape[1]
    imps = []
    for g in range(NSA_GROUPS):
        s = lax.dot_general(q_ref[0, g], kc_ref[0, g], (((1,), (1,)), ((), ())), preferred_element_type=jnp.float32)
        p = _softmax_rows(s + bias_ref[g], valid)
        oc_ref[0, g] = jnp.dot(p.astype(jnp.bfloat16), vc_ref[0, g], preferred_element_type=jnp.float32)
        psum = p[0:t]
        for hh in range(1, NSA_HPG):
            psum = psum + p[hh * t:(hh + 1) * t]
        imps.append(jnp.dot(psum.astype(jnp.bfloat16), ov_ref[...], preferred_element_type=jnp.float32))
    imp = jnp.concatenate(imps, axis=0)
    blk = lax.broadcasted_iota(jnp.int32, (NSA_GROUPS * t, nsp), 1)
    qp = q0 + jnp.concatenate([lax.broadcasted_iota(jnp.int32, (t, nsp), 0)] * NSA_GROUPS, axis=0)
    cur = qp // SEL_BLOCK
    ok_blk = (blk * SEL_BLOCK <= qp) & (blk < ns)
    forced = (blk == 0) | (blk == cur) | (blk == cur - 1)
    score = jnp.where(ok_blk, jnp.where(forced, BIG, imp), -BIG)
    score = jnp.where(blk < ns, score, -jnp.inf)
    msel = jnp.zeros((NSA_GROUPS * t, nsp), jnp.float32)
    for _ in range(min(N_SEL, ns)):
        mx = jnp.max(score, axis=1, keepdims=True)
        idx = jnp.min(jnp.where(score == mx, blk, nsp), axis=1, keepdims=True)
        sel = blk == idx
        msel = jnp.where(sel & (mx > -0.5 * BIG), 1.0, msel)
        score = jnp.where(sel, -jnp.inf, score)
    for g in range(NSA_GROUPS):
        msel_ref[0, g] = msel[g * t:(g + 1) * t]


def dec_cmp(qg, kc, vc, bias_c, valid_c, overlap, t, q0, ns):
    B, G, R_, dh = qg.shape
    ncp, nsp = overlap.shape
    per_b = lambda a: pl.BlockSpec((1,) + a.shape[1:], lambda b: (b,) + (0,) * (a.ndim - 1))
    full = lambda a: pl.BlockSpec(a.shape, lambda b: (0,) * a.ndim)
    return pl.pallas_call(
        functools.partial(_dec_cmp_kernel, t=t, q0=q0, ns=ns),
        grid=(B,),
        in_specs=[per_b(qg), per_b(kc), per_b(vc), full(bias_c), full(valid_c), full(overlap)],
        out_specs=[pl.BlockSpec((1, G, R_, dh), lambda b: (b, 0, 0, 0)),
                   pl.BlockSpec((1, G, t, nsp), lambda b: (b, 0, 0, 0))],
        out_shape=[jax.ShapeDtypeStruct((B, G, R_, dh), jnp.float32),
                   jax.ShapeDtypeStruct((B, G, t, nsp), jnp.float32)],
        compiler_params=pltpu.CompilerParams(dimension_semantics=("parallel",), vmem_limit_bytes=VMEM_LIMIT),
        name="nsa_dec_cmp",
    )(qg, kc, vc, bias_c, valid_c, overlap)


def _flash_update(s, valid, v, m_ref, l_ref, acc_ref, g):
    s = jnp.where(valid, s, NEG)
    m_old = m_ref[g]
    m_new = jnp.maximum(m_old, jnp.max(s, axis=-1, keepdims=True))
    p = jnp.where(valid, jnp.exp(s - m_new), 0.0)
    alpha = jnp.exp(m_old - m_new)
    l_ref[g] = alpha * l_ref[g] + jnp.sum(p, axis=-1, keepdims=True)
    acc_ref[g] = alpha * acc_ref[g] + jnp.dot(p.astype(jnp.bfloat16), v, preferred_element_type=jnp.float32)
    m_ref[g] = m_new


def _dec_sel_kernel(pt_ref, q_ref, *refs, n_steps, t):
    pages = refs[:PPS]
    new_ref, bias_ref, mask_ref, biasn_ref, maskn_ref, o_ref, m_ref, l_ref, acc_ref = refs[PPS:]
    s_idx = pl.program_id(1)
    dh = NSA_HEAD_DIM

    @pl.when(s_idx == 0)
    def _():
        m_ref[...] = jnp.full_like(m_ref, NEG)
        l_ref[...] = jnp.zeros_like(l_ref)
        acc_ref[...] = jnp.zeros_like(acc_ref)

    def tile_mask(mref, g):
        mk = mref[0, g] > 0.5
        return jnp.concatenate([mk] * NSA_HPG, axis=0)

    for g in range(NSA_GROUPS):
        rows_of = lambda pg, kvg: pg[0, pl.ds(kvg, PAGE_SIZE, stride=KVG), :]
        k = jnp.concatenate([rows_of(pg, g) for pg in pages], axis=0).astype(jnp.bfloat16)
        v = jnp.concatenate([rows_of(pg, NSA_GROUPS + g) for pg in pages], axis=0).astype(jnp.bfloat16)
        s = lax.dot_general(q_ref[0, g], k, (((1,), (1,)), ((), ())), preferred_element_type=jnp.float32)
        _flash_update(s + bias_ref[g], tile_mask(mask_ref, g), v, m_ref, l_ref, acc_ref, g)

    @pl.when(s_idx == n_steps - 1)
    def _():
        for g in range(NSA_GROUPS):
            k = new_ref[0, :, pl.ds(g * dh, dh)].astype(jnp.bfloat16)
            v = new_ref[0, :, pl.ds((NSA_GROUPS + g) * dh, dh)].astype(jnp.bfloat16)
            s = lax.dot_general(q_ref[0, g], k, (((1,), (1,)), ((), ())), preferred_element_type=jnp.float32)
            _flash_update(s + biasn_ref[g], tile_mask(maskn_ref, g), v, m_ref, l_ref, acc_ref, g)
            o_ref[0, g] = acc_ref[g] / jnp.maximum(l_ref[g], 1e-30)


def dec_sel(qg, pool, page_table, new_rows, bias_p, mask_p, bias_n, mask_n, t):
    B, G, R_, dh = qg.shape
    NP = page_table.shape[1]
    assert NP % PPS == 0
    n_steps = NP // PPS
    W = PPS * PAGE_SIZE
    pool = pool.reshape(pool.shape[0], PAGE_SIZE * KVG, dh)

    def page_spec(u):
        return pl.BlockSpec((1, PAGE_SIZE * KVG, dh), lambda b, s, pt: (pt[b, s * PPS + u], 0, 0))

    grid_spec = pltpu.PrefetchScalarGridSpec(
        num_scalar_prefetch=1,
        grid=(B, n_steps),
        in_specs=[pl.BlockSpec((1, G, R_, dh), lambda b, s, pt: (b, 0, 0, 0))]
        + [page_spec(u) for u in range(PPS)]
        + [pl.BlockSpec((1, PAGE_SIZE, ROW_W), lambda b, s, pt: (b, 0, 0)),
           pl.BlockSpec((G, R_, W), lambda b, s, pt: (0, 0, s)),
           pl.BlockSpec((1, G, t, W), lambda b, s, pt: (b, 0, 0, s)),
           pl.BlockSpec((G, R_, PAGE_SIZE), lambda b, s, pt: (0, 0, 0)),
           pl.BlockSpec((1, G, t, PAGE_SIZE), lambda b, s, pt: (b, 0, 0, 0))],
        out_specs=pl.BlockSpec((1, G, R_, dh), lambda b, s, pt: (b, 0, 0, 0)),
        scratch_shapes=[pltpu.VMEM((G, R_, 1), jnp.float32), pltpu.VMEM((G, R_, 1), jnp.float32),
                        pltpu.VMEM((G, R_, dh), jnp.float32)],
    )
    return pl.pallas_call(
        functools.partial(_dec_sel_kernel, n_steps=n_steps, t=t),
        grid_spec=grid_spec,
        out_shape=jax.ShapeDtypeStruct((B, G, R_, dh), jnp.float32),
        compiler_params=pltpu.CompilerParams(dimension_semantics=("parallel", "arbitrary"),
                                             vmem_limit_bytes=VMEM_LIMIT),
        name="nsa_dec_sel",
    )(page_table, qg, *([pool] * PPS), new_rows, bias_p, mask_p, bias_n, mask_n)


def _dec_win_kernel(q_ref, win_ref, new_ref, bias_ref, valid_ref, o_ref):
    dh = NSA_HEAD_DIM
    valid = valid_ref[...] > 0.5
    for g in range(NSA_GROUPS):
        wn = win_ref.shape[1] // KVG
        k = jnp.concatenate([win_ref[0, pl.ds(g, wn, stride=KVG), :], new_ref[0, :, pl.ds(g * dh, dh)]], axis=0)
        v = jnp.concatenate([win_ref[0, pl.ds(NSA_GROUPS + g, wn, stride=KVG), :],
                             new_ref[0, :, pl.ds((NSA_GROUPS + g) * dh, dh)]], axis=0)
        s = lax.dot_general(q_ref[0, g], k.astype(jnp.bfloat16), (((1,), (1,)), ((), ())),
                            preferred_element_type=jnp.float32)
        p = _softmax_rows(s + bias_ref[g], valid)
        o_ref[0, g] = jnp.dot(p.astype(jnp.bfloat16), v.astype(jnp.bfloat16), preferred_element_type=jnp.float32)


def dec_win(qg, win_rows, new_rows, bias_w, valid_w):
    B, G, R_, dh = qg.shape
    per_b = lambda a: pl.BlockSpec((1,) + a.shape[1:], lambda b: (b,) + (0,) * (a.ndim - 1))
    full = lambda a: pl.BlockSpec(a.shape, lambda b: (0,) * a.ndim)
    return pl.pallas_call(
        _dec_win_kernel,
        grid=(B,),
        in_specs=[per_b(qg), per_b(win_rows), per_b(new_rows), full(bias_w), full(valid_w)],
        out_specs=pl.BlockSpec((1, G, R_, dh), lambda b: (b, 0, 0, 0)),
        out_shape=jax.ShapeDtypeStruct((B, G, R_, dh), jnp.float32),
        compiler_params=pltpu.CompilerParams(dimension_semantics=("parallel",), vmem_limit_bytes=VMEM_LIMIT),
        name="nsa_dec_win",
    )(qg, win_rows, new_rows, bias_w, valid_w)


def nsa_decode(q, gates, kc, vc, slc_pool, page_table, slc_new, win_buf, win_new_rows, table):
    B, t = q.shape[0], q.shape[1]
    G, HPG, dh = NSA_GROUPS, NSA_HPG, NSA_HEAD_DIM
    NP = page_table.shape[1]
    past = NP * PAGE_SIZE
    Wn = win_buf.shape[1]
    assert t <= PAGE_SIZE and Wn == WINDOW
    L = past + t
    nc = L // CMP_STRIDE - 1
    ncp = kc.shape[2]
    Ns = -(-L // SEL_BLOCK)
    nsp = -(-Ns // LANE) * LANE
    bf = jnp.bfloat16
    R_ = HPG * t
    qg = jnp.transpose(q.reshape(B, t, G, HPG, dh), (0, 2, 3, 1, 4)).reshape(B, G, R_, dh).astype(bf)
    qpos = past + jnp.arange(t)

    r_lo, r_hi = -(PAGE_SIZE + WINDOW), past + t
    lut_rev = table[t5_bucket(jnp.arange(r_hi, r_lo - 1, -1))].astype(jnp.float32)

    def rows_bias(p0, step, n):
        rows = []
        for i in range(t):
            j0 = r_hi - (past + i - p0)
            assert j0 >= 0 and j0 + step * (n - 1) < lut_rev.shape[0]
            rows.append(lax.slice(lut_rev, (j0, 0), (j0 + step * (n - 1) + 1, NSA_HEADS), (step, 1)))
        b = jnp.transpose(jnp.stack(rows), (2, 0, 1))
        return b.reshape(G, R_, n)

    rep = lambda m: jnp.tile(m, (HPG, 1))
    cidx = jnp.arange(ncp)
    cend = cidx * CMP_STRIDE + (CMP_BLOCK - 1)
    rel_c = qpos[:, None] - cend[None, :]
    valid_c = rep(((rel_c >= 0) & (cidx[None, :] < nc)).astype(jnp.float32))
    cstart = cidx[:, None] * CMP_STRIDE
    sstart = jnp.arange(nsp)[None, :] * SEL_BLOCK
    overlap = ((cstart < sstart + SEL_BLOCK) & (cstart + CMP_BLOCK > sstart)
               & (cidx[:, None] < nc) & (jnp.arange(nsp)[None, :] < Ns)).astype(bf)
    o_c, msel = dec_cmp(qg, kc, vc, rows_bias(CMP_BLOCK - 1, CMP_STRIDE, ncp), valid_c, overlap, t, past, Ns)
    msel = msel[..., :Ns]
    kpos = jnp.arange(past + PAGE_SIZE)
    nblk = -(-(past + PAGE_SIZE) // SEL_BLOCK)
    mkey = jnp.repeat(jnp.pad(msel, ((0, 0), (0, 0), (0, 0), (0, nblk - Ns))), SEL_BLOCK, axis=-1)
    mkey = mkey[..., :past + PAGE_SIZE]
    mkey = mkey * ((kpos[None, :] <= qpos[:, None]) & (kpos[None, :] < L)).astype(jnp.float32)
    bias_s = rows_bias(0, 1, past + PAGE_SIZE)
    pad_new = lambda r: jnp.pad(r, ((0, 0), (0, PAGE_SIZE - t), (0, 0)))
    o_s = dec_sel(qg, slc_pool, page_table, pad_new(slc_new), bias_s[..., :past], mkey[..., :past],
                  bias_s[..., past:], mkey[..., past:], t)
    wpos = jnp.concatenate([past - Wn + jnp.arange(Wn), past + jnp.arange(PAGE_SIZE)])
    rel_w = qpos[:, None] - wpos[None, :]
    valid_w = rep(((rel_w >= 0) & (rel_w < WINDOW) & (wpos[None, :] >= 0) & (wpos[None, :] < L)).astype(jnp.float32))
    bias_w = jnp.concatenate([rows_bias(past - Wn, 1, Wn), rows_bias(past, 1, PAGE_SIZE)], axis=-1)
    o_w = dec_win(qg, win_buf.reshape(B, Wn * KVG, dh), pad_new(win_new_rows), bias_w, valid_w)
    back = lambda o: jnp.transpose(o.reshape(B, G, HPG, t, dh), (0, 3, 1, 2, 4)).reshape(B, t, G * HPG * dh)
    gx = jnp.repeat(gates.reshape(B, t, G * HPG, 3), dh, axis=2)
    return gx[..., 0] * back(o_c) + gx[..., 1] * back(o_s) + gx[..., 2] * back(o_w)


def kv_rows(h, P):
    B, T, D = h.shape
    return norm_mm(h.reshape(B * T, D), P['norm_kv'], P['kv_w'])[0].reshape(B, T, 3, ROW_W)


def run_trunk(x, gla_s0, past, P):
    B, T, _ = x.shape
    h = x
    gla_states = []
    kv5 = lambda r: r.reshape(r.shape[0], r.shape[1], 2, NSA_GROUPS, NSA_HEAD_DIM)
    for layer in range(DEPTH):
        if layer < N_A_LAYERS:
            o, s = gla_mixer(h, P['norm_mix'][layer], gla_s0[layer], P['gla_w_in'][layer],
                             P['gla_w_gate_up'][layer], P['gla_b_gate'][layer], P['gla_norm'][layer],
                             P['gla_w_out'][layer])
            gla_states.append(s)
        else:
            j = layer - N_A_LAYERS
            qd = NSA_HEADS * NSA_HEAD_DIM
            proj = norm_mm(h.reshape(B * T, -1), P['norm_mix'][layer], P['nsa_w_in'][j])[0].reshape(B, T, -1)
            q = proj[..., :qd].reshape(B, T, NSA_HEADS, NSA_HEAD_DIM) * NSA_HEAD_DIM ** -0.5
            gates = jax.nn.sigmoid(proj[..., qd:].astype(jnp.float32)).reshape(B, T, NSA_HEADS, 3).astype(x.dtype)
            if past is None:
                r5 = rows.reshape(B, T, 3, 2, NSA_GROUPS, NSA_HEAD_DIM)
                att = nsa_prompt(q, gates, kc, vc, r5[:, :, 1, 0], r5[:, :, 1, 1], r5[:, :, 2, 0], r5[:, :, 2, 1],
                                 P['rel_bias'])
            else:
                att = nsa_decode(q, gates, kc, vc, past['slc_pool'], past['page_table'], rows[:, :, 1],
                                 past['win_buf'], rows[:, :, 2], P['rel_bias'])
            o = mm(att, P['nsa_w_out'][j])
        h = h + o
        h = peer_ffn(h, P['norm_ffn'][layer], P['peer_w_q'][layer], P['peer_subkeys'][layer],
                     P['peer_u_bf'], P['peer_v_bf'], layer)
        if layer == N_A_LAYERS - 1:
            rows = kv_rows(h, P)
            if past is None:
                assert T % (PAGE_SIZE * PPS) == 0 and T % TQ == 0
                npg = T // PAGE_SIZE
                pool = rows[:, :, 0].reshape(B * npg, PAGE_SIZE, 2, NSA_GROUPS, NSA_HEAD_DIM)
                table = jnp.arange(B * npg, dtype=jnp.int32).reshape(B, npg)
                win_new = kv5(rows[:, T - min(WINDOW, T):, 2])
            else:
                past_len = past['page_table'].shape[1] * PAGE_SIZE
                assert (past_len + T) // CMP_STRIDE == past_len // CMP_STRIDE
                pool, table = past['cmp_pool'], past['page_table']
                win_all = jnp.concatenate([past['win_buf'], kv5(rows[:, :, 2])], axis=1)
                win_new = win_all[:, win_all.shape[1] - min(WINDOW, win_all.shape[1]):]
            kc, vc = compress_paged(pool, table, *P['cmp_stack'])
    return (rmsnorm(h, P['norm_final']), jnp.stack(gla_states), kv5(rows[:, :, 0]), kv5(rows[:, :, 1]), win_new)


def kernel(x_prompt, x_sample, state_gla, cache_cmp_kv, cache_slc_kv, cache_win_kv, page_table,
           norm_mix, norm_ffn, norm_kv, norm_final, gla_w_in, gla_w_gate_up, gla_b_gate, gla_norm, gla_w_out,
           kv_w, cmp_k_w1, cmp_k_b1, cmp_k_w2, cmp_k_b2, cmp_k_pe, cmp_v_w1, cmp_v_b1, cmp_v_w2, cmp_v_b2, cmp_v_pe,
           nsa_w_in, nsa_w_out, rel_bias, peer_w_q, peer_subkeys, peer_u, peer_v):
    P = dict(norm_mix=norm_mix, norm_ffn=norm_ffn, norm_kv=norm_kv, norm_final=norm_final,
             gla_w_in=gla_w_in, gla_w_gate_up=gla_w_gate_up, gla_b_gate=gla_b_gate, gla_norm=gla_norm,
             gla_w_out=gla_w_out, kv_w=kv_w,
             cmp_k_w1=cmp_k_w1, cmp_k_b1=cmp_k_b1, cmp_k_w2=cmp_k_w2, cmp_k_b2=cmp_k_b2, cmp_k_pe=cmp_k_pe,
             cmp_v_w1=cmp_v_w1, cmp_v_b1=cmp_v_b1, cmp_v_w2=cmp_v_w2, cmp_v_b2=cmp_v_b2, cmp_v_pe=cmp_v_pe,
             nsa_w_in=nsa_w_in, nsa_w_out=nsa_w_out, rel_bias=rel_bias,
             peer_w_q=peer_w_q, peer_subkeys=peer_subkeys,
             peer_u_bf=peer_u.astype(jnp.bfloat16), peer_v_bf=peer_v.astype(jnp.bfloat16))
    P['cmp_stack'] = compress_params(P)
    past = dict(cmp_pool=cache_cmp_kv, slc_pool=cache_slc_kv, win_buf=cache_win_kv,
                page_table=page_table.astype(jnp.int32))
    gla_zero = jnp.zeros((N_A_LAYERS, BATCH, GLA_HEADS, GLA_DK, GLA_DV), x_prompt.dtype)
    y_prompt, gla_p, cmp_p, slc_p, win_p = run_trunk(x_prompt, gla_zero, None, P)
    y_sample, gla_s, cmp_s, slc_s, win_s = run_trunk(x_sample, state_gla, past, P)
    return (y_prompt, y_sample, gla_p, gla_s, cmp_p, cmp_s, slc_p, slc_s, win_p, win_s)
```

```python
import functools
import math

import jax
import jax.numpy as jnp
from jax import lax
from jax.experimental import pallas as pl
from jax.experimental.pallas import tpu as pltpu

D_MODEL = 2048
BATCH = 4
SEQ = 2048
DEPTH = 2
DEC_BATCH = 32
DEC_SEQ = 8
PAST_LEN = 8192
PAGE_SIZE = 128
N_A_LAYERS = DEPTH // 2
N_B_LAYERS = DEPTH - N_A_LAYERS
GLA_HEADS = 4
GLA_DK = D_MODEL // (2 * GLA_HEADS)
GLA_DV = D_MODEL // GLA_HEADS
GLA_GATE_RANK = 16
GLA_TAU = 16.0
GLA_CHUNK = 64
NSA_HEADS = 16
NSA_GROUPS = 4
NSA_HPG = NSA_HEADS // NSA_GROUPS
NSA_HEAD_DIM = D_MODEL // NSA_HEADS
CMP_STRIDE = 16
CMP_BLOCK = 2 * CMP_STRIDE
CMP_HIDDEN = 2 * NSA_HEAD_DIM
SEL_BLOCK = 64
N_SEL = 16
WINDOW = 512
Q_BLOCK = 32
REL_BUCKETS = 32
REL_MAX_DIST = 128
PEER_HEADS = 8
PEER_NKEYS = 128
PEER_EXPERTS = PEER_NKEYS * PEER_NKEYS
PEER_TOPK = 16
PEER_KEY_DIM = 256
PEER_TOK_BLOCK = 128
EPS = 1e-6
NEG = -1e30
BIG = 1e30

LANE = 128
VMEM_LIMIT = 56 * 1024 * 1024
TQ = 128
assert REL_MAX_DIST <= TQ and WINDOW % TQ == 0 and TQ % SEL_BLOCK == 0


def _mm_res_kernel(a_ref, b_ref, r_ref, o_ref):
    a = a_ref[...].astype(jnp.bfloat16)
    b = b_ref[...].astype(jnp.bfloat16)
    o_ref[...] = r_ref[...] + jnp.dot(a, b, preferred_element_type=jnp.float32)


def mm_res(x, w, res, tm=1024, tn=1024):
    lead = x.shape[:-1]
    K, N = w.shape
    a = x.reshape(-1, K)
    M = a.shape[0]
    tm, tn = math.gcd(M, tm), math.gcd(N, tn)
    assert tn % LANE == 0 and tm % 8 == 0
    out = pl.pallas_call(
        _mm_res_kernel,
        grid=(M // tm, N // tn),
        in_specs=[pl.BlockSpec((tm, K), lambda i, j: (i, 0)),
                  pl.BlockSpec((K, tn), lambda i, j: (0, j)),
                  pl.BlockSpec((tm, tn), lambda i, j: (i, j))],
        out_specs=pl.BlockSpec((tm, tn), lambda i, j: (i, j)),
        out_shape=jax.ShapeDtypeStruct((M, N), jnp.float32),
        compiler_params=pltpu.CompilerParams(
            dimension_semantics=("parallel", "parallel"), vmem_limit_bytes=VMEM_LIMIT),
        name="out_proj_matmul",
    )(a, w, res.reshape(M, N))
    return out.reshape(lead + (N,))


def _norm_mm_kernel(a_ref, g_ref, b_ref, o_ref, xn_ref):
    @pl.when(pl.program_id(1) == 0)
    def _():
        x = a_ref[...]
        y = x * lax.rsqrt(jnp.mean(x * x, axis=-1, keepdims=True) + EPS) * g_ref[...]
        xn_ref[...] = y.astype(jnp.bfloat16)

    o_ref[...] = jnp.dot(xn_ref[...], b_ref[...].astype(jnp.bfloat16), preferred_element_type=jnp.float32)


def norm_mm(x, gain, b, tm=1024, tn=512, keep_cols=False):
    M, K = x.shape
    N = b.shape[1]
    tm = math.gcd(M, tm)
    Np = -(-N // LANE) * LANE
    wide = [d * LANE for d in range(1, 2 * tn // LANE + 1) if Np % (d * LANE) == 0]
    if max(wide) < tn:
        Np = -(-N // tn) * tn
        wide = [tn]
    tn = max(wide)
    if Np != N:
        b = jnp.pad(b, ((0, 0), (0, Np - N)))
    out, xn = pl.pallas_call(
        _norm_mm_kernel,
        grid=(M // tm, Np // tn),
        in_specs=[pl.BlockSpec((tm, K), lambda i, j: (i, 0)),
                  pl.BlockSpec((1, K), lambda i, j: (0, 0)),
                  pl.BlockSpec((K, tn), lambda i, j: (0, j))],
        out_specs=[pl.BlockSpec((tm, tn), lambda i, j: (i, j)),
                   pl.BlockSpec((tm, K), lambda i, j: (i, 0))],
        out_shape=[jax.ShapeDtypeStruct((M, Np), jnp.float32), jax.ShapeDtypeStruct((M, K), jnp.bfloat16)],
        compiler_params=pltpu.CompilerParams(
            dimension_semantics=("parallel", "arbitrary"), vmem_limit_bytes=VMEM_LIMIT),
        name="norm_matmul",
    )(x, gain.reshape(1, K).astype(jnp.float32), b)
    return (out if keep_cols else out[:, :N]), xn


def rmsnorm(x, g):
    xf = x.astype(jnp.float32)
    y = xf * lax.rsqrt(jnp.mean(xf * xf, axis=-1, keepdims=True) + EPS)
    return (y * g.astype(jnp.float32)).astype(x.dtype)


def t5_bucket(rel):
    n = jnp.maximum(rel, 0)
    exact = REL_BUCKETS // 2
    nf = jnp.maximum(n, exact).astype(jnp.float32)
    large = exact + (jnp.log(nf / exact) / math.log(REL_MAX_DIST / exact) * (REL_BUCKETS - exact)).astype(jnp.int32)
    large = jnp.minimum(large, REL_BUCKETS - 1)
    return jnp.where(n < exact, n, large)


def head_bias(table, rel):
    onehot = jax.nn.one_hot(t5_bucket(rel), REL_BUCKETS, dtype=jnp.float32)
    b = jnp.einsum('qkn,nh->hqk', onehot, table.astype(jnp.float32), precision=lax.Precision.HIGHEST)
    return b.reshape(NSA_GROUPS, NSA_HPG, rel.shape[0], rel.shape[1])


DK_ALL = GLA_HEADS * GLA_DK
DV_ALL = GLA_HEADS * GLA_DV
GZ_COL = 2 * DK_ALL + 2 * DV_ALL
assert GZ_COL % LANE == 0 and GLA_GATE_RANK <= LANE and DK_ALL % LANE == 0 and (2 * DK_ALL) % DV_ALL == 0


def _gla_kernel(q_ref, k_ref, v_ref, r_ref, gz_ref, wg_ref, bg_ref, ng_ref, tri_ref, s0_ref,
                o_ref, sfin_ref, s_scr, *, C, t_valid):
    c = pl.program_id(1)

    @pl.when(c == 0)
    def _():
        s_scr[...] = s0_ref[0]

    row = lax.broadcasted_iota(jnp.int32, (C, C), 0)
    col = lax.broadcasted_iota(jnp.int32, (C, C), 1)
    bf = jnp.bfloat16
    z_all = jnp.dot(gz_ref[0].astype(bf), wg_ref[...], preferred_element_type=jnp.float32) + bg_ref[...]
    for h in range(GLA_HEADS):
        dk = slice(h * GLA_DK, (h + 1) * GLA_DK)
        dv = slice(h * GLA_DV, (h + 1) * GLA_DV)
        z = z_all[:, dk]
        g = (jnp.minimum(z, 0.0) - jnp.log(1.0 + jnp.exp(-jnp.abs(z)))) * (1.0 / GLA_TAU)
        if t_valid is not None:
            g = jnp.where(lax.broadcasted_iota(jnp.int32, g.shape, 0) < t_valid, g, 0.0)
        b = jnp.dot(tri_ref[...], g, preferred_element_type=jnp.float32, precision=lax.Precision.HIGHEST)
        b_last = b[C - 1:C, :]
        q = q_ref[0, :, dk] * (GLA_DK ** -0.5)
        k = k_ref[0, :, dk]
        v = v_ref[0, :, dv].astype(bf)
        qe = (q * jnp.exp(b)).astype(bf)
        ke = (k * jnp.exp(-b)).astype(bf)
        a = lax.dot_general(qe, ke, (((1,), (1,)), ((), ())), preferred_element_type=jnp.float32)
        a = jnp.where(col <= row, a, 0.0)
        s = s_scr[h]
        o = (jnp.dot(a.astype(bf), v, preferred_element_type=jnp.float32)
             + jnp.dot(qe, s.astype(bf), preferred_element_type=jnp.float32))
        kd = (k * jnp.exp(b_last - b)).astype(bf)
        dcol = jnp.exp(jnp.transpose(jnp.broadcast_to(b_last, (LANE, GLA_DK))))
        s_new = (s * jnp.concatenate([dcol] * (GLA_DV // LANE), axis=1)
                 + lax.dot_general(kd, v, (((0,), (0,)), ((), ())), preferred_element_type=jnp.float32))
        s_scr[h] = s_new
        o = o * lax.rsqrt(jnp.mean(o * o, axis=-1, keepdims=True) + EPS) * ng_ref[...]
        r = r_ref[0, :, dv]
        o_ref[0, :, dv] = o * (r * (1.0 / (1.0 + jnp.exp(-r))))

    @pl.when(c == pl.num_programs(1) - 1)
    def _():
        sfin_ref[0] = s_scr[...]


def gla_core(proj, s0, w_gate_up, b_gate, norm_g, T):
    B, Tp, _ = proj.shape
    C = GLA_CHUNK if T % GLA_CHUNK == 0 else Tp
    assert Tp % C == 0 and C % 16 == 0 and (Tp == T or Tp == C)
    n = Tp // C
    wg = jnp.pad(w_gate_up, ((0, LANE - GLA_GATE_RANK), (0, 0))).astype(jnp.bfloat16)
    tri = (jnp.arange(C)[:, None] >= jnp.arange(C)[None, :]).astype(jnp.float32)
    col_blk = lambda w, i: pl.BlockSpec((1, C, w), lambda b, c: (b, c, i))
    full = lambda a: pl.BlockSpec(a.shape, lambda b, c: (0,) * a.ndim)
    bg = b_gate.reshape(1, DK_ALL)
    ng = norm_g.reshape(1, GLA_DV)
    st_spec = pl.BlockSpec((1, GLA_HEADS, GLA_DK, GLA_DV), lambda b, c: (b, 0, 0, 0))
    return pl.pallas_call(
        functools.partial(_gla_kernel, C=C, t_valid=None if Tp == T else T),
        grid=(B, n),
        in_specs=[col_blk(DK_ALL, 0), col_blk(DK_ALL, 1), col_blk(DV_ALL, 2 * DK_ALL // DV_ALL),
                  col_blk(DV_ALL, (2 * DK_ALL + DV_ALL) // DV_ALL), col_blk(LANE, GZ_COL // LANE),
                  full(wg), full(bg), full(ng), full(tri), st_spec],
        out_specs=[pl.BlockSpec((1, C, DV_ALL), lambda b, c: (b, c, 0)), st_spec],
        out_shape=[jax.ShapeDtypeStruct((B, Tp, DV_ALL), jnp.float32),
                   jax.ShapeDtypeStruct((B, GLA_HEADS, GLA_DK, GLA_DV), jnp.float32)],
        scratch_shapes=[pltpu.VMEM((GLA_HEADS, GLA_DK, GLA_DV), jnp.float32)],
        compiler_params=pltpu.CompilerParams(dimension_semantics=("parallel", "arbitrary"),
                                             vmem_limit_bytes=VMEM_LIMIT),
        name="gla_core",
    )(proj, proj, proj, proj, proj, wg, bg, ng, tri, s0)


def gla_mixer(h, gain, s0, w_in, w_gate_up, b_gate, norm_g, w_out):
    B, T, D = h.shape
    proj, _ = norm_mm(h.reshape(B * T, D), gain, w_in, keep_cols=True)
    assert proj.shape[1] >= GZ_COL + LANE
    Tp = -(-T // 16) * 16
    proj = jnp.pad(proj.reshape(B, T, -1), ((0, 0), (0, Tp - T), (0, 0)))
    o, s_new = gla_core(proj, s0.astype(jnp.float32), w_gate_up, b_gate, norm_g, T)
    return mm_res(o[:, :T], w_out, h), s_new.astype(s0.dtype)


WSUM_UNROLL = 16
PEER_CHUNK = 1024


def _wsum_kernel(a_ref, b_ref, g_ref, o_ref, *, tb):
    sub = lax.broadcasted_iota(jnp.int32, (PEER_NKEYS, PEER_NKEYS), 0)

    def body(tt, c):
        t0 = pl.multiple_of(tt * WSUM_UNROLL, WSUM_UNROLL)
        a8 = a_ref[pl.ds(t0, WSUM_UNROLL), :]
        b8 = b_ref[pl.ds(t0, WSUM_UNROLL), :]
        g8 = g_ref[pl.ds(t0, WSUM_UNROLL), :]
        for u in range(WSUM_UNROLL):
            at = jnp.where(sub == a8[u:u + 1], 1.0, 0.0).astype(jnp.bfloat16)
            bt = jnp.where(sub == b8[u:u + 1], g8[u:u + 1], 0.0).astype(jnp.bfloat16)
            o_ref[t0 + u] = lax.dot_general(at, bt, (((1,), (1,)), ((), ())), preferred_element_type=jnp.float32)
        return c

    lax.fori_loop(0, tb // WSUM_UNROLL, body, 0)


def peer_wsum(i1, i2, gate, tb=64):
    n, K = i1.shape
    tb = math.gcd(n, tb)
    assert tb % WSUM_UNROLL == 0
    return pl.pallas_call(
        functools.partial(_wsum_kernel, tb=tb),
        grid=(n // tb,),
        in_specs=[pl.BlockSpec((tb, K), lambda i: (i, 0))] * 3,
        out_specs=pl.BlockSpec((tb, PEER_NKEYS, PEER_NKEYS), lambda i: (i, 0, 0)),
        out_shape=jax.ShapeDtypeStruct((n, PEER_NKEYS, PEER_NKEYS), jnp.float32),
        compiler_params=pltpu.CompilerParams(dimension_semantics=("parallel",), vmem_limit_bytes=VMEM_LIMIT),
        name="peer_wsum",
    )(i1, i2, gate)


def _peer_kernel(x_ref, u_ref, v_ref, w_ref, o_ref, *, te):
    j = pl.program_id(1)
    x = x_ref[...]
    acc = None
    for c in range(te // PEER_CHUNK):
        e0 = c * PEER_CHUNK
        hid = lax.dot_general(x, u_ref[e0:e0 + PEER_CHUNK, :], (((1,), (1,)), ((), ())),
                              preferred_element_type=jnp.float32)
        parts = []
        for r in range(PEER_CHUNK // PEER_NKEYS):
            hr = hid[:, r * PEER_NKEYS:(r + 1) * PEER_NKEYS]
            ar = 0.5 * hr * (1.0 + lax.erf(hr * (2.0 ** -0.5))) * w_ref[:, e0 // PEER_NKEYS + r, :]
            parts.append(ar.astype(jnp.bfloat16))
        d = jnp.dot(jnp.concatenate(parts, axis=1), v_ref[e0:e0 + PEER_CHUNK, :], preferred_element_type=jnp.float32)
        acc = d if acc is None else acc + d
    @pl.when(j == 0)
    def _():
        o_ref[...] = acc

    @pl.when(j > 0)
    def _():
        o_ref[...] += acc


def peer_dense(xn, u, v, layer, wsum, tb=1024, te=1024):
    n, D = xn.shape
    E = u.shape[1]
    tb = math.gcd(n, tb)
    return pl.pallas_call(
        functools.partial(_peer_kernel, te=te),
        grid=(n // tb, E // te),
        in_specs=[pl.BlockSpec((tb, D), lambda i, j: (i, 0), pipeline_mode=pl.Buffered(1)),
                  pl.BlockSpec((None, te, D), lambda i, j: (layer, j, 0)),
                  pl.BlockSpec((None, te, D), lambda i, j: (layer, j, 0)),
                  pl.BlockSpec((tb, te // PEER_NKEYS, PEER_NKEYS), lambda i, j: (i, j, 0))],
        out_specs=pl.BlockSpec((tb, D), lambda i, j: (i, 0), pipeline_mode=pl.Buffered(1)),
        out_shape=jax.ShapeDtypeStruct((n, D), jnp.float32),
        compiler_params=pltpu.CompilerParams(dimension_semantics=("parallel", "arbitrary"),
                                             vmem_limit_bytes=VMEM_LIMIT),
        name="peer_dense",
    )(xn, u, v, wsum)


RT = LANE
NHC = 2 * PEER_HEADS
ROUTE_CHAINS = 4
_CAND_GROUPS = [(0, 0), (0, 8), (1, 0)] + [(a, 0) for a in range(2, 8)] + [(-1, 0)]
assert PEER_TOPK == 16 and PEER_KEY_DIM // 2 == LANE and PEER_NKEYS == LANE


def _top_rounds(s, n_rounds, extra=()):
    R = s.shape[0]
    iota = lax.broadcasted_iota(jnp.int32, s.shape, 0)
    vals, idxs, ex = [], [], [[] for _ in extra]
    for _ in range(n_rounds):
        m = jnp.max(s, axis=0, keepdims=True)
        idx = jnp.min(jnp.where(s == m, iota, R), axis=0, keepdims=True)
        sel = iota == idx
        for e, lst in zip(extra, ex):
            lst.append(jnp.max(jnp.where(sel, e, -1), axis=0, keepdims=True))
        s = jnp.where(sel, -jnp.inf, s)
        vals.append(m)
        idxs.append(idx)
    cat = lambda l: jnp.concatenate(l, axis=0)
    return cat(vals), cat(idxs), [cat(l) for l in ex]


def _route_kernel(q_ref, sub_ref, e1_ref, e2_ref, g_ref, v_scr, i_scr):
    K = PEER_TOPK

    def stage1(hp, c):
        for u in range(ROUTE_CHAINS):
            hc = ROUTE_CHAINS * hp + u
            off = pl.multiple_of(hc * LANE, LANE)
            qb = q_ref[:, pl.ds(off, LANE)].astype(jnp.bfloat16)
            s = lax.dot_general(sub_ref[hc], qb, (((1,), (1,)), ((), ())), preferred_element_type=jnp.float32)
            v, i, _ = _top_rounds(s, K)
            v_scr[hc] = v
            i_scr[hc] = i
        return c

    lax.fori_loop(0, NHC // ROUTE_CHAINS, stage1, 0)

    row8 = lax.broadcasted_iota(jnp.int32, (8, RT), 0)

    def stage2(h):
        v1, v2 = v_scr[2 * h], v_scr[2 * h + 1]
        i1, i2 = i_scr[2 * h], i_scr[2 * h + 1]
        cand, c1, c2 = [], [], []
        for a, b0 in _CAND_GROUPS:
            if a >= 0:
                nb = K // (a + 1)
                sm = v1[a:a + 1] + v2[b0:b0 + 8]
                if nb - b0 < 8:
                    sm = jnp.where(row8 < nb - b0, sm, -jnp.inf)
                cand.append(sm)
                c1.append(jnp.broadcast_to(i1[a:a + 1], (8, RT)))
                c2.append(i2[b0:b0 + 8])
            else:
                cand.append(v1[8:16] + v2[0:1])
                c1.append(i1[8:16])
                c2.append(jnp.broadcast_to(i2[0:1], (8, RT)))
        cat = lambda l: jnp.concatenate(l, axis=0)
        top, _, (e12,) = _top_rounds(cat(cand), K, extra=(cat(c1) * PEER_NKEYS + cat(c2),))
        ex = jnp.exp(top - top[0:1])
        g = ex / jnp.sum(ex, axis=0, keepdims=True)
        r0 = pl.multiple_of(h * K, K)
        e1_ref[0, pl.ds(r0, K), :] = e12 // PEER_NKEYS
        e2_ref[0, pl.ds(r0, K), :] = e12 % PEER_NKEYS
        g_ref[0, pl.ds(r0, K), :] = g

    def stage2_pair(hp, c):
        stage2(2 * hp)
        stage2(2 * hp + 1)
        return c

    lax.fori_loop(0, PEER_HEADS // 2, stage2_pair, 0)


def peer_route(q, sub_bf):
    n = q.shape[0]
    assert n % RT == 0
    nb = n // RT
    slots = PEER_HEADS * PEER_TOPK
    out = jax.ShapeDtypeStruct((nb, slots, RT), jnp.int32)
    ospec = pl.BlockSpec((1, slots, RT), lambda i: (i, 0, 0))
    return pl.pallas_call(
        _route_kernel,
        grid=(nb,),
        in_specs=[pl.BlockSpec((RT, q.shape[1]), lambda i: (i, 0)),
                  pl.BlockSpec(sub_bf.shape, lambda i: (0, 0, 0))],
        out_specs=[ospec, ospec, ospec],
        out_shape=[out, out, jax.ShapeDtypeStruct((nb, slots, RT), jnp.float32)],
        scratch_shapes=[pltpu.VMEM((NHC, PEER_TOPK, RT), jnp.float32),
                        pltpu.VMEM((NHC, PEER_TOPK, RT), jnp.int32)],
        compiler_params=pltpu.CompilerParams(dimension_semantics=("parallel",), vmem_limit_bytes=VMEM_LIMIT),
        name="peer_route",
    )(q, sub_bf)


def peer_ffn(h, gain, w_q, subkeys, u_bf, v_bf, layer):
    B, T, D = h.shape
    n = B * T
    q, xt = norm_mm(h.reshape(n, D), gain, w_q)
    sub_bf = subkeys.astype(jnp.bfloat16).reshape(NHC, PEER_NKEYS, PEER_KEY_DIM // 2)
    e1, e2, gate = peer_route(q, sub_bf)
    tok_major = lambda t: jnp.transpose(t, (0, 2, 1)).reshape(n, PEER_HEADS * PEER_TOPK)
    wsum = peer_wsum(tok_major(e1), tok_major(e2), tok_major(gate))
    out = peer_dense(xt, u_bf, v_bf, layer, wsum)
    return h + out.reshape(B, T, D)


KVG = 2 * NSA_GROUPS
ROW_W = KVG * NSA_HEAD_DIM
CPP = PAGE_SIZE // CMP_STRIDE
PPS = 8
assert CPP == 8 and PAGE_SIZE == LANE and NSA_HEAD_DIM == LANE


def _compress_kernel(pt_ref, *refs, n_steps):
    pages = refs[:PPS]
    w1_ref, c1_ref, w2_ref, b2_ref, kc_ref, vc_ref, xc_ref = refs[PPS:]
    s_idx = pl.program_id(1)
    for pair in range(PPS // 2):
        row0 = pl.multiple_of((s_idx * (PPS // 2) + pair) * 2 * CPP, 2 * CPP)
        for kvg in range(KVG):
            for s in range(CMP_STRIDE):
                lo = pages[2 * pair][0, pl.ds(s * KVG + kvg, CPP, stride=KVG * CMP_STRIDE), :]
                hi = pages[2 * pair + 1][0, pl.ds(s * KVG + kvg, CPP, stride=KVG * CMP_STRIDE), :]
                xc_ref[kvg, pl.ds(row0, 2 * CPP), pl.ds(s * LANE, LANE)] = (
                    jnp.concatenate([lo, hi], axis=0).astype(jnp.bfloat16))

    @pl.when(s_idx == n_steps - 1)
    def _():
        nch = xc_ref.shape[1]
        for kv in range(2):
            out_ref = kc_ref if kv == 0 else vc_ref
            for g in range(NSA_GROUPS):
                hh = jnp.dot(xc_ref[kv * NSA_GROUPS + g], w1_ref[kv], preferred_element_type=jnp.float32)
                h1 = hh[:, :CMP_HIDDEN]
                h2 = pltpu.roll(hh[:, CMP_HIDDEN:], nch - 1, 0)
                x = h1 + h2 + c1_ref[kv]
                hid = 0.5 * x * (1.0 + lax.erf(x * (2.0 ** -0.5)))
                o = jnp.dot(hid.astype(jnp.bfloat16), w2_ref[kv], preferred_element_type=jnp.float32) + b2_ref[kv]
                out_ref[0, g] = o.astype(out_ref.dtype)


def compress_paged(pool, page_table, w1, c1, w2, b2):
    B, NP = page_table.shape
    assert NP % PPS == 0
    n_steps = NP // PPS
    nch = NP * CPP
    dh = NSA_HEAD_DIM
    pool = pool.reshape(pool.shape[0], PAGE_SIZE * KVG, dh)

    def page_spec(u):
        return pl.BlockSpec((1, PAGE_SIZE * KVG, dh), lambda b, s, pt: (pt[b, s * PPS + u], 0, 0))

    full = lambda a: pl.BlockSpec(a.shape, lambda b, s, pt: (0,) * a.ndim)
    out_spec = pl.BlockSpec((1, NSA_GROUPS, nch, dh), lambda b, s, pt: (b, 0, 0, 0))
    grid_spec = pltpu.PrefetchScalarGridSpec(
        num_scalar_prefetch=1,
        grid=(B, n_steps),
        in_specs=[page_spec(u) for u in range(PPS)] + [full(w1), full(c1), full(w2), full(b2)],
        out_specs=[out_spec, out_spec],
        scratch_shapes=[pltpu.VMEM((KVG, nch, CMP_STRIDE * dh), jnp.bfloat16)],
    )
    out = jax.ShapeDtypeStruct((B, NSA_GROUPS, nch, dh), jnp.bfloat16)
    return pl.pallas_call(
        functools.partial(_compress_kernel, n_steps=n_steps),
        grid_spec=grid_spec,
        out_shape=[out, out],
        compiler_params=pltpu.CompilerParams(dimension_semantics=("parallel", "arbitrary"),
                                             vmem_limit_bytes=VMEM_LIMIT),
        name="nsa_compress",
    )(page_table, *([pool] * PPS), w1, c1, w2, b2)


def compress_params(P):
    dh, F = NSA_HEAD_DIM, CMP_HIDDEN
    w1s, c1s, w2s, b2s = [], [], [], []
    for n in ('k', 'v'):
        w1h = P['cmp_%s_w1' % n].reshape(2, CMP_STRIDE * dh, F)
        pe = P['cmp_%s_pe' % n].reshape(2, CMP_STRIDE * dh)
        w1s.append(jnp.concatenate([w1h[0], w1h[1]], axis=1))
        c1s.append(P['cmp_%s_b1' % n] + jnp.einsum('hk,hkf->f', pe, w1h, precision=lax.Precision.HIGHEST))
        w2s.append(P['cmp_%s_w2' % n])
        b2s.append(P['cmp_%s_b2' % n])
    return (jnp.stack(w1s).astype(jnp.bfloat16), jnp.stack(c1s)[:, None, :],
            jnp.stack(w2s).astype(jnp.bfloat16), jnp.stack(b2s)[:, None, :])


def _softmax_rows(s, valid):
    s = jnp.where(valid, s, NEG)
    m = jnp.max(s, axis=-1, keepdims=True)
    e = jnp.where(valid, jnp.exp(s - m), 0.0)
    return e / jnp.maximum(jnp.sum(e, axis=-1, keepdims=True), 1e-30)


HQ = NSA_HPG * TQ
GATE_ROWS = 8


def _lanes4(x):
    return jnp.concatenate([x] * NSA_HPG, axis=1)


def _nsa_cmp_sel_kernel(q_ref, kc_ref, vct_ref, bias_ref, ovt_ref, oct_ref, msel_ref, *, nc, ns):
    i = pl.program_id(2)
    ncp = kc_ref.shape[2]
    nsp = ovt_ref.shape[0]
    q_all = q_ref[0, 0].reshape(HQ, NSA_HEAD_DIM)
    cidx = lax.broadcasted_iota(jnp.int32, (ncp, TQ), 0)
    qpos = i * TQ + lax.broadcasted_iota(jnp.int32, (ncp, TQ), 1)
    valid = _lanes4((cidx * CMP_STRIDE + (CMP_BLOCK - 1) <= qpos) & (cidx < nc))
    s = lax.dot_general(kc_ref[0, 0], q_all, (((1,), (1,)), ((), ())), preferred_element_type=jnp.float32)
    s = jnp.where(valid, s + bias_ref[0, 0], NEG)
    m = jnp.max(s, axis=0, keepdims=True)
    e = jnp.where(valid, jnp.exp(s - m), 0.0)
    p = e / jnp.maximum(jnp.sum(e, axis=0, keepdims=True), 1e-30)
    oct_ref[0, 0, 0] = jnp.dot(vct_ref[0, 0], p.astype(jnp.bfloat16), preferred_element_type=jnp.float32)
    psum = p[:, 0:TQ]
    for hh in range(1, NSA_HPG):
        psum = psum + p[:, hh * TQ:(hh + 1) * TQ]
    imp = jnp.dot(ovt_ref[...], psum.astype(jnp.bfloat16), preferred_element_type=jnp.float32)
    blk = lax.broadcasted_iota(jnp.int32, (nsp, TQ), 0)
    qp = i * TQ + lax.broadcasted_iota(jnp.int32, (nsp, TQ), 1)
    cur = qp // SEL_BLOCK
    ok_blk = (blk * SEL_BLOCK <= qp) & (blk < ns)
    forced = (blk == 0) | (blk == cur) | (blk == cur - 1)
    score = jnp.where(ok_blk, jnp.where(forced, BIG, imp), -BIG)
    score = jnp.where(blk < ns, score, -jnp.inf)
    msel = jnp.zeros((nsp, TQ), jnp.float32)
    for _ in range(min(N_SEL, ns)):
        mx = jnp.max(score, axis=0, keepdims=True)
        idx = jnp.min(jnp.where(score == mx, blk, nsp), axis=0, keepdims=True)
        sel = blk == idx
        msel = jnp.where(sel & (mx > -0.5 * BIG), 1.0, msel)
        score = jnp.where(sel, -jnp.inf, score)
    msel_ref[0, 0] = msel


def nsa_cmp_sel(qh, kc, vct, bias_ct, ovt, nc, ns):
    B, G, HPG, T, dh = qh.shape
    ncp = kc.shape[2]
    nsp = ovt.shape[0]
    nqt = T // TQ
    return pl.pallas_call(
        functools.partial(_nsa_cmp_sel_kernel, nc=nc, ns=ns),
        grid=(B, G, nqt),
        in_specs=[pl.BlockSpec((1, 1, HPG, TQ, dh), lambda b, g, i: (b, g, 0, i, 0)),
                  pl.BlockSpec((1, 1, ncp, dh), lambda b, g, i: (b, g, 0, 0)),
                  pl.BlockSpec((1, 1, dh, ncp), lambda b, g, i: (b, g, 0, 0)),
                  pl.BlockSpec((1, 1, ncp, HQ), lambda b, g, i: (g, i, 0, 0)),
                  pl.BlockSpec((nsp, ncp), lambda b, g, i: (0, 0))],
        out_specs=[pl.BlockSpec((1, 1, 1, dh, HQ), lambda b, g, i: (b, g, i, 0, 0)),
                   pl.BlockSpec((1, 1, nsp, TQ), lambda b, g, i: (b, g, 0, i))],
        out_shape=[jax.ShapeDtypeStruct((B, G, nqt, dh, HQ), jnp.float32),
                   jax.ShapeDtypeStruct((B, G, nsp, T), jnp.float32)],
        compiler_params=pltpu.CompilerParams(dimension_semantics=("parallel", "parallel", "parallel"),
                                             vmem_limit_bytes=VMEM_LIMIT),
        name="nsa_cmp_sel",
    )(qh, kc, vct, bias_ct, ovt)


def _flash_tile_t(q_all, k, vt, bias, valid, m_ref, l_ref, acc_ref):
    s = lax.dot_general(k, q_all, (((1,), (1,)), ((), ())), preferred_element_type=jnp.float32)
    s = jnp.where(valid, s + bias, NEG)
    m_old = m_ref[...]
    m_new = jnp.maximum(m_old, jnp.max(s, axis=0, keepdims=True))
    p = jnp.where(valid, jnp.exp(s - m_new), 0.0)
    alpha = jnp.exp(m_old - m_new)
    l_ref[...] = alpha * l_ref[...] + jnp.sum(p, axis=0, keepdims=True)
    acc_ref[...] = alpha * acc_ref[...] + jnp.dot(vt, p.astype(jnp.bfloat16), preferred_element_type=jnp.float32)
    m_ref[...] = m_new


def _nsa_sw_kernel(q_ref, ks_ref, vst_ref, kw_ref, vwt_ref, msel_ref, expt_ref, bias_ref, selm_ref, winm_ref,
                   oct_ref, gate_ref, o_ref, msk_ref, m_ref, l_ref, acc_ref, *, nt):
    i = pl.program_id(2)
    q_all = q_ref[0, 0].reshape(HQ, NSA_HEAD_DIM)
    msel = msel_ref[0, 0].astype(jnp.bfloat16)
    for j in range(nt):
        msk_ref[j] = jnp.dot(expt_ref[j], msel, preferred_element_type=jnp.float32)

    def reset():
        m_ref[...] = jnp.full_like(m_ref, NEG)
        l_ref[...] = jnp.zeros_like(l_ref)
        acc_ref[...] = jnp.zeros_like(acc_ref)

    def result():
        return acc_ref[...] / jnp.maximum(l_ref[...], 1e-30)

    def pair(k_ref, vt_ref, j1, dd1, valid_of):
        j0 = j1 - 1
        j0c = jnp.maximum(j0, 0)
        k2 = jnp.concatenate([k_ref[0, 0, j1], k_ref[0, 0, j0c]], axis=0)
        vt2 = jnp.concatenate([vt_ref[0, 0, j1], vt_ref[0, 0, j0c]], axis=1)
        bias2 = jnp.concatenate([bias_ref[0, jnp.minimum(dd1, 2)], bias_ref[0, jnp.minimum(dd1 + 1, 2)]], axis=0)
        valid2 = _lanes4(jnp.concatenate([valid_of(j1, dd1), valid_of(j0c, dd1 + 1) & (j0 >= 0)], axis=0))
        _flash_tile_t(q_all, k2, vt2, bias2, valid2, m_ref, l_ref, acc_ref)

    reset()

    def sel_valid(j, dd):
        return (msk_ref[j] * selm_ref[jnp.minimum(dd, 2)]) > 0.5

    def sel_body(jj, c):
        pair(ks_ref, vst_ref, i - 2 * jj, 2 * jj, sel_valid)
        return c

    lax.fori_loop(0, i // 2 + 1, sel_body, 0)
    mix = gate_ref[0, 0, 0, 0:1, :] * oct_ref[0, 0, 0] + gate_ref[0, 0, 0, 1:2, :] * result()
    reset()
    for pp in range(WINDOW // TQ // 2 + 1):
        j1 = i - 2 * pp
        pair(kw_ref, vwt_ref, jnp.maximum(j1, 0), 2 * pp, lambda j, dd: (winm_ref[dd] > 0.5) & (j1 >= 0))
    mix = mix + gate_ref[0, 0, 0, 2:3, :] * result()
    for hh in range(NSA_HPG):
        o_ref[0, :, hh * NSA_HEAD_DIM:(hh + 1) * NSA_HEAD_DIM] = jnp.transpose(mix[:, hh * TQ:(hh + 1) * TQ])


def nsa_sel_win(qh, ks, vst, kw, vwt, msel_t, exp_t, bias_tt, selm_t, winm_t, oct, gates_t):
    B, G, HPG, T, dh = qh.shape
    nt = T // TQ
    nsp = msel_t.shape[2]
    k_spec = pl.BlockSpec((1, 1, nt, TQ, dh), lambda b, g, i: (b, g, 0, 0, 0))
    vt_spec = pl.BlockSpec((1, 1, nt, dh, TQ), lambda b, g, i: (b, g, 0, 0, 0))
    return pl.pallas_call(
        functools.partial(_nsa_sw_kernel, nt=nt),
        grid=(B, G, nt),
        in_specs=[pl.BlockSpec((1, 1, HPG, TQ, dh), lambda b, g, i: (b, g, 0, i, 0)),
                  k_spec, vt_spec, k_spec, vt_spec,
                  pl.BlockSpec((1, 1, nsp, TQ), lambda b, g, i: (b, g, 0, i)),
                  pl.BlockSpec((nt, TQ, nsp), lambda b, g, i: (0, 0, 0)),
                  pl.BlockSpec((1, 3, TQ, HQ), lambda b, g, i: (g, 0, 0, 0)),
                  pl.BlockSpec((3, TQ, TQ), lambda b, g, i: (0, 0, 0)),
                  pl.BlockSpec(winm_t.shape, lambda b, g, i: (0, 0, 0)),
                  pl.BlockSpec((1, 1, 1, dh, HQ), lambda b, g, i: (b, g, i, 0, 0)),
                  pl.BlockSpec((1, 1, 1, GATE_ROWS, HQ), lambda b, g, i: (b, g, i, 0, 0))],
        out_specs=pl.BlockSpec((1, TQ, HPG * dh), lambda b, g, i: (b, i, g)),
        out_shape=jax.ShapeDtypeStruct((B, T, G * HPG * dh), jnp.float32),
        scratch_shapes=[pltpu.VMEM((nt, TQ, TQ), jnp.float32),
                        pltpu.VMEM((1, HQ), jnp.float32),
                        pltpu.VMEM((1, HQ), jnp.float32),
                        pltpu.VMEM((dh, HQ), jnp.float32)],
        compiler_params=pltpu.CompilerParams(dimension_semantics=("parallel", "parallel", "arbitrary"),
                                             vmem_limit_bytes=VMEM_LIMIT),
        name="nsa_sel_win",
    )(qh, ks, vst, kw, vwt, msel_t, exp_t, bias_tt, selm_t, winm_t, oct, gates_t)


def nsa_prompt(q, gates, kc, vc, ks, vs, kw_rows, vw_rows, table):
    B, T = q.shape[0], q.shape[1]
    G, HPG, dh = NSA_GROUPS, NSA_HPG, NSA_HEAD_DIM
    Nc = T // CMP_STRIDE - 1
    Ns = -(-T // SEL_BLOCK)
    ncp = kc.shape[2]
    assert ncp % LANE == 0 and ncp >= Nc
    nsp = -(-Ns // 16) * 16
    nqt = T // TQ
    bf = jnp.bfloat16
    qh = jnp.transpose(q.reshape(B, T, G, HPG, dh), (0, 2, 3, 1, 4)).astype(bf)
    qpos = jnp.arange(T)
    cidx = jnp.arange(ncp)
    bias_c = head_bias(table, qpos[:, None] - (cidx * CMP_STRIDE + (CMP_BLOCK - 1))[None, :])
    bias_ct = jnp.transpose(bias_c.reshape(G, HPG, nqt, TQ, ncp), (0, 2, 4, 1, 3)).reshape(G, nqt, ncp, HQ)
    sidx = jnp.arange(nsp)
    ovt = (((cidx * CMP_STRIDE)[None, :] < (sidx * SEL_BLOCK + SEL_BLOCK)[:, None])
           & ((cidx * CMP_STRIDE + CMP_BLOCK)[None, :] > (sidx * SEL_BLOCK)[:, None])
           & (cidx[None, :] < Nc) & (sidx[:, None] < Ns)).astype(bf)
    oct, msel_t = nsa_cmp_sel(qh, kc, jnp.transpose(vc, (0, 1, 3, 2)), bias_ct, ovt, Nc, Ns)
    exp_t = ((jnp.arange(T) // SEL_BLOCK)[:, None] == sidx[None, :]).astype(bf).reshape(nqt, TQ, nsp)
    r = jnp.arange(TQ)
    rel3 = (jnp.arange(3) * TQ)[:, None, None] + r[None, None, :] - r[None, :, None]
    bias_tt = jnp.einsum('dkqn,nh->hdkq', jax.nn.one_hot(t5_bucket(rel3), REL_BUCKETS, dtype=jnp.float32),
                         table.astype(jnp.float32), precision=lax.Precision.HIGHEST)
    bias_tt = jnp.transpose(bias_tt.reshape(G, HPG, 3, TQ, TQ), (0, 2, 3, 1, 4)).reshape(G, 3, TQ, HQ)
    selm_t = (rel3 >= 0).astype(jnp.float32)
    nw = 2 * (WINDOW // TQ // 2 + 1)
    relw = (jnp.arange(nw) * TQ)[:, None, None] + r[None, None, :] - r[None, :, None]
    winm_t = ((relw >= 0) & (relw < WINDOW)).astype(jnp.float32)

    def k_tiles(t):
        return jnp.transpose(t, (0, 2, 1, 3)).astype(bf).reshape(B, G, nqt, TQ, dh)

    def vt_tiles(t):
        return jnp.transpose(t.astype(bf).reshape(B, nqt, TQ, G, dh), (0, 3, 1, 4, 2))

    gates_t = jnp.transpose(gates.reshape(B, nqt, TQ, G, HPG, 3), (0, 3, 1, 5, 4, 2)).reshape(B, G, nqt, 3, HQ)
    gates_t = jnp.pad(gates_t, ((0, 0), (0, 0), (0, 0), (0, GATE_ROWS - 3), (0, 0)))
    return nsa_sel_win(qh, k_tiles(ks), vt_tiles(vs), k_tiles(kw_rows), vt_tiles(vw_rows), msel_t, exp_t, bias_tt,
                       selm_t, winm_t, oct, gates_t)


def _dec_cmp_kernel(q_ref, kc_ref, vc_ref, bias_ref, valid_ref, ov_ref, oc_ref, msel_ref, *, t, q0, ns):
    valid = valid_ref[...] > 0.5
    nsp = ov_ref.shape[1]
    imps = []
    for g in range(NSA_GROUPS):
        s = lax.dot_general(q_ref[0, g], kc_ref[0, g], (((1,), (1,)), ((), ())), preferred_element_type=jnp.float32)
        p = _softmax_rows(s + bias_ref[g], valid)
        oc_ref[0, g] = jnp.dot(p.astype(jnp.bfloat16), vc_ref[0, g], preferred_element_type=jnp.float32)
        psum = p[0:t]
        for hh in range(1, NSA_HPG):
            psum = psum + p[hh * t:(hh + 1) * t]
        imps.append(jnp.dot(psum.astype(jnp.bfloat16), ov_ref[...], preferred_element_type=jnp.float32))
    imp = jnp.concatenate(imps, axis=0)
    blk = lax.broadcasted_iota(jnp.int32, (NSA_GROUPS * t, nsp), 1)
    qp = q0 + jnp.concatenate([lax.broadcasted_iota(jnp.int32, (t, nsp), 0)] * NSA_GROUPS, axis=0)
    cur = qp // SEL_BLOCK
    ok_blk = (blk * SEL_BLOCK <= qp) & (blk < ns)
    forced = (blk == 0) | (blk == cur) | (blk == cur - 1)
    score = jnp.where(ok_blk, jnp.where(forced, BIG, imp), -BIG)
    score = jnp.where(blk < ns, score, -jnp.inf)
    msel = jnp.zeros((NSA_GROUPS * t, nsp), jnp.float32)
    for _ in range(min(N_SEL, ns)):
        mx = jnp.max(score, axis=1, keepdims=True)
        idx = jnp.min(jnp.where(score == mx, blk, nsp), axis=1, keepdims=True)
        sel = blk == idx
        msel = jnp.where(sel & (mx > -0.5 * BIG), 1.0, msel)
        score = jnp.where(sel, -jnp.inf, score)
    for g in range(NSA_GROUPS):
        msel_ref[0, g] = msel[g * t:(g + 1) * t]


def dec_cmp(qg, kc, vc, bias_c, valid_c, overlap, t, q0, ns):
    B, G, R_, dh = qg.shape
    ncp, nsp = overlap.shape
    per_b = lambda a: pl.BlockSpec((1,) + a.shape[1:], lambda b: (b,) + (0,) * (a.ndim - 1))
    full = lambda a: pl.BlockSpec(a.shape, lambda b: (0,) * a.ndim)
    return pl.pallas_call(
        functools.partial(_dec_cmp_kernel, t=t, q0=q0, ns=ns),
        grid=(B,),
        in_specs=[per_b(qg), per_b(kc), per_b(vc), full(bias_c), full(valid_c), full(overlap)],
        out_specs=[pl.BlockSpec((1, G, R_, dh), lambda b: (b, 0, 0, 0)),
                   pl.BlockSpec((1, G, t, nsp), lambda b: (b, 0, 0, 0))],
        out_shape=[jax.ShapeDtypeStruct((B, G, R_, dh), jnp.float32),
                   jax.ShapeDtypeStruct((B, G, t, nsp), jnp.float32)],
        compiler_params=pltpu.CompilerParams(dimension_semantics=("parallel",), vmem_limit_bytes=VMEM_LIMIT),
        name="nsa_dec_cmp",
    )(qg, kc, vc, bias_c, valid_c, overlap)


def _flash_update(s, valid, v, m_ref, l_ref, acc_ref, g):
    s = jnp.where(valid, s, NEG)
    m_old = m_ref[g]
    m_new = jnp.maximum(m_old, jnp.max(s, axis=-1, keepdims=True))
    p = jnp.where(valid, jnp.exp(s - m_new), 0.0)
    alpha = jnp.exp(m_old - m_new)
    l_ref[g] = alpha * l_ref[g] + jnp.sum(p, axis=-1, keepdims=True)
    acc_ref[g] = alpha * acc_ref[g] + jnp.dot(p.astype(jnp.bfloat16), v, preferred_element_type=jnp.float32)
    m_ref[g] = m_new


def _dec_sel_kernel(pt_ref, q_ref, *refs, n_steps, t):
    pages = refs[:PPS]
    new_ref, bias_ref, mask_ref, biasn_ref, maskn_ref, o_ref, m_ref, l_ref, acc_ref = refs[PPS:]
    s_idx = pl.program_id(1)
    dh = NSA_HEAD_DIM

    @pl.when(s_idx == 0)
    def _():
        m_ref[...] = jnp.full_like(m_ref, NEG)
        l_ref[...] = jnp.zeros_like(l_ref)
        acc_ref[...] = jnp.zeros_like(acc_ref)

    def tile_mask(mref, g):
        mk = mref[0, g] > 0.5
        return jnp.concatenate([mk] * NSA_HPG, axis=0)

    for g in range(NSA_GROUPS):
        rows_of = lambda pg, kvg: pg[0, pl.ds(kvg, PAGE_SIZE, stride=KVG), :]
        k = jnp.concatenate([rows_of(pg, g) for pg in pages], axis=0).astype(jnp.bfloat16)
        v = jnp.concatenate([rows_of(pg, NSA_GROUPS + g) for pg in pages], axis=0).astype(jnp.bfloat16)
        s = lax.dot_general(q_ref[0, g], k, (((1,), (1,)), ((), ())), preferred_element_type=jnp.float32)
        _flash_update(s + bias_ref[g], tile_mask(mask_ref, g), v, m_ref, l_ref, acc_ref, g)

    @pl.when(s_idx == n_steps - 1)
    def _():
        for g in range(NSA_GROUPS):
            k = new_ref[0, :, pl.ds(g * dh, dh)].astype(jnp.bfloat16)
            v = new_ref[0, :, pl.ds((NSA_GROUPS + g) * dh, dh)].astype(jnp.bfloat16)
            s = lax.dot_general(q_ref[0, g], k, (((1,), (1,)), ((), ())), preferred_element_type=jnp.float32)
            _flash_update(s + biasn_ref[g], tile_mask(maskn_ref, g), v, m_ref, l_ref, acc_ref, g)
            o_ref[0, g] = acc_ref[g] / jnp.maximum(l_ref[g], 1e-30)


def dec_sel(qg, pool, page_table, new_rows, bias_p, mask_p, bias_n, mask_n, t):
    B, G, R_, dh = qg.shape
    NP = page_table.shape[1]
    assert NP % PPS == 0
    n_steps = NP // PPS
    W = PPS * PAGE_SIZE
    pool = pool.reshape(pool.shape[0], PAGE_SIZE * KVG, dh)

    def page_spec(u):
        return pl.BlockSpec((1, PAGE_SIZE * KVG, dh), lambda b, s, pt: (pt[b, s * PPS + u], 0, 0))

    grid_spec = pltpu.PrefetchScalarGridSpec(
        num_scalar_prefetch=1,
        grid=(B, n_steps),
        in_specs=[pl.BlockSpec((1, G, R_, dh), lambda b, s, pt: (b, 0, 0, 0))]
        + [page_spec(u) for u in range(PPS)]
        + [pl.BlockSpec((1, PAGE_SIZE, ROW_W), lambda b, s, pt: (b, 0, 0)),
           pl.BlockSpec((G, R_, W), lambda b, s, pt: (0, 0, s)),
           pl.BlockSpec((1, G, t, W), lambda b, s, pt: (b, 0, 0, s)),
           pl.BlockSpec((G, R_, PAGE_SIZE), lambda b, s, pt: (0, 0, 0)),
           pl.BlockSpec((1, G, t, PAGE_SIZE), lambda b, s, pt: (b, 0, 0, 0))],
        out_specs=pl.BlockSpec((1, G, R_, dh), lambda b, s, pt: (b, 0, 0, 0)),
        scratch_shapes=[pltpu.VMEM((G, R_, 1), jnp.float32), pltpu.VMEM((G, R_, 1), jnp.float32),
                        pltpu.VMEM((G, R_, dh), jnp.float32)],
    )
    return pl.pallas_call(
        functools.partial(_dec_sel_kernel, n_steps=n_steps, t=t),
        grid_spec=grid_spec,
        out_shape=jax.ShapeDtypeStruct((B, G, R_, dh), jnp.float32),
        compiler_params=pltpu.CompilerParams(dimension_semantics=("parallel", "arbitrary"),
                                             vmem_limit_bytes=VMEM_LIMIT),
        name="nsa_dec_sel",
    )(page_table, qg, *([pool] * PPS), new_rows, bias_p, mask_p, bias_n, mask_n)


def _dec_win_kernel(q_ref, win_ref, new_ref, bias_ref, valid_ref, o_ref):
    dh = NSA_HEAD_DIM
    valid = valid_ref[...] > 0.5
    for g in range(NSA_GROUPS):
        wn = win_ref.shape[1] // KVG
        k = jnp.concatenate([win_ref[0, pl.ds(g, wn, stride=KVG), :], new_ref[0, :, pl.ds(g * dh, dh)]], axis=0)
        v = jnp.concatenate([win_ref[0, pl.ds(NSA_GROUPS + g, wn, stride=KVG), :],
                             new_ref[0, :, pl.ds((NSA_GROUPS + g) * dh, dh)]], axis=0)
        s = lax.dot_general(q_ref[0, g], k.astype(jnp.bfloat16), (((1,), (1,)), ((), ())),
                            preferred_element_type=jnp.float32)
        p = _softmax_rows(s + bias_ref[g], valid)
        o_ref[0, g] = jnp.dot(p.astype(jnp.bfloat16), v.astype(jnp.bfloat16), preferred_element_type=jnp.float32)


def dec_win(qg, win_rows, new_rows, bias_w, valid_w):
    B, G, R_, dh = qg.shape
    per_b = lambda a: pl.BlockSpec((1,) + a.shape[1:], lambda b: (b,) + (0,) * (a.ndim - 1))
    full = lambda a: pl.BlockSpec(a.shape, lambda b: (0,) * a.ndim)
    return pl.pallas_call(
        _dec_win_kernel,
        grid=(B,),
        in_specs=[per_b(qg), per_b(win_rows), per_b(new_rows), full(bias_w), full(valid_w)],
        out_specs=pl.BlockSpec((1, G, R_, dh), lambda b: (b, 0, 0, 0)),
        out_shape=jax.ShapeDtypeStruct((B, G, R_, dh), jnp.float32),
        compiler_params=pltpu.CompilerParams(dimension_semantics=("parallel",), vmem_limit_bytes=VMEM_LIMIT),
        name="nsa_dec_win",
    )(qg, win_rows, new_rows, bias_w, valid_w)


def nsa_decode(q, gates, kc, vc, slc_pool, page_table, slc_new, win_buf, win_new_rows, table):
    B, t = q.shape[0], q.shape[1]
    G, HPG, dh = NSA_GROUPS, NSA_HPG, NSA_HEAD_DIM
    NP = page_table.shape[1]
    past = NP * PAGE_SIZE
    Wn = win_buf.shape[1]
    assert t <= PAGE_SIZE and Wn == WINDOW
    L = past + t
    nc = L // CMP_STRIDE - 1
    ncp = kc.shape[2]
    Ns = -(-L // SEL_BLOCK)
    nsp = -(-Ns // LANE) * LANE
    bf = jnp.bfloat16
    R_ = HPG * t
    qg = jnp.transpose(q.reshape(B, t, G, HPG, dh), (0, 2, 3, 1, 4)).reshape(B, G, R_, dh).astype(bf)
    qpos = past + jnp.arange(t)

    r_lo, r_hi = -(PAGE_SIZE + WINDOW), past + t
    lut_rev = table[t5_bucket(jnp.arange(r_hi, r_lo - 1, -1))].astype(jnp.float32)

    def rows_bias(p0, step, n):
        rows = []
        for i in range(t):
            j0 = r_hi - (past + i - p0)
            assert j0 >= 0 and j0 + step * (n - 1) < lut_rev.shape[0]
            rows.append(lax.slice(lut_rev, (j0, 0), (j0 + step * (n - 1) + 1, NSA_HEADS), (step, 1)))
        b = jnp.transpose(jnp.stack(rows), (2, 0, 1))
        return b.reshape(G, R_, n)

    rep = lambda m: jnp.tile(m, (HPG, 1))
    cidx = jnp.arange(ncp)
    cend = cidx * CMP_STRIDE + (CMP_BLOCK - 1)
    rel_c = qpos[:, None] - cend[None, :]
    valid_c = rep(((rel_c >= 0) & (cidx[None, :] < nc)).astype(jnp.float32))
    cstart = cidx[:, None] * CMP_STRIDE
    sstart = jnp.arange(nsp)[None, :] * SEL_BLOCK
    overlap = ((cstart < sstart + SEL_BLOCK) & (cstart + CMP_BLOCK > sstart)
               & (cidx[:, None] < nc) & (jnp.arange(nsp)[None, :] < Ns)).astype(bf)
    o_c, msel = dec_cmp(qg, kc, vc, rows_bias(CMP_BLOCK - 1, CMP_STRIDE, ncp), valid_c, overlap, t, past, Ns)
    msel = msel[..., :Ns]
    kpos = jnp.arange(past + PAGE_SIZE)
    nblk = -(-(past + PAGE_SIZE) // SEL_BLOCK)
    mkey = jnp.repeat(jnp.pad(msel, ((0, 0), (0, 0), (0, 0), (0, nblk - Ns))), SEL_BLOCK, axis=-1)
    mkey = mkey[..., :past + PAGE_SIZE]
    mkey = mkey * ((kpos[None, :] <= qpos[:, None]) & (kpos[None, :] < L)).astype(jnp.float32)
    bias_s = rows_bias(0, 1, past + PAGE_SIZE)
    pad_new = lambda r: jnp.pad(r, ((0, 0), (0, PAGE_SIZE - t), (0, 0)))
    o_s = dec_sel(qg, slc_pool, page_table, pad_new(slc_new), bias_s[..., :past], mkey[..., :past],
                  bias_s[..., past:], mkey[..., past:], t)
    wpos = jnp.concatenate([past - Wn + jnp.arange(Wn), past + jnp.arange(PAGE_SIZE)])
    rel_w = qpos[:, None] - wpos[None, :]
    valid_w = rep(((rel_w >= 0) & (rel_w < WINDOW) & (wpos[None, :] >= 0) & (wpos[None, :] < L)).astype(jnp.float32))
    bias_w = jnp.concatenate([rows_bias(past - Wn, 1, Wn), rows_bias(past, 1, PAGE_SIZE)], axis=-1)
    o_w = dec_win(qg, win_buf.reshape(B, Wn * KVG, dh), pad_new(win_new_rows), bias_w, valid_w)
    back = lambda o: jnp.transpose(o.reshape(B, G, HPG, t, dh), (0, 3, 1, 2, 4)).reshape(B, t, G * HPG * dh)
    gx = jnp.repeat(gates.reshape(B, t, G * HPG, 3), dh, axis=2)
    return gx[..., 0] * back(o_c) + gx[..., 1] * back(o_s) + gx[..., 2] * back(o_w)


def kv_rows(h, P):
    B, T, D = h.shape
    return norm_mm(h.reshape(B * T, D), P['norm_kv'], P['kv_w'])[0].reshape(B, T, 3, ROW_W)


def run_trunk(x, gla_s0, past, P):
    B, T, _ = x.shape
    h = x
    gla_states = []
    kv5 = lambda r: r.reshape(r.shape[0], r.shape[1], 2, NSA_GROUPS, NSA_HEAD_DIM)
    for layer in range(DEPTH):
        if layer < N_A_LAYERS:
            h, s = gla_mixer(h, P['norm_mix'][layer], gla_s0[layer], P['gla_w_in'][layer],
                             P['gla_w_gate_up'][layer], P['gla_b_gate'][layer], P['gla_norm'][layer],
                             P['gla_w_out'][layer])
            gla_states.append(s)
        else:
            j = layer - N_A_LAYERS
            qd = NSA_HEADS * NSA_HEAD_DIM
            proj = norm_mm(h.reshape(B * T, -1), P['norm_mix'][layer], P['nsa_w_in'][j])[0].reshape(B, T, -1)
            q = proj[..., :qd].reshape(B, T, NSA_HEADS, NSA_HEAD_DIM) * NSA_HEAD_DIM ** -0.5
            gates = jax.nn.sigmoid(proj[..., qd:].astype(jnp.float32)).reshape(B, T, NSA_HEADS, 3).astype(x.dtype)
            if past is None:
                r5 = rows.reshape(B, T, 3, 2, NSA_GROUPS, NSA_HEAD_DIM)
                att = nsa_prompt(q, gates, kc, vc, r5[:, :, 1, 0], r5[:, :, 1, 1], r5[:, :, 2, 0], r5[:, :, 2, 1],
                                 P['rel_bias'])
            else:
                att = nsa_decode(q, gates, kc, vc, past['slc_pool'], past['page_table'], rows[:, :, 1],
                                 past['win_buf'], rows[:, :, 2], P['rel_bias'])
            h = mm_res(att, P['nsa_w_out'][j], h)
        h = peer_ffn(h, P['norm_ffn'][layer], P['peer_w_q'][layer], P['peer_subkeys'][layer],
                     P['peer_u_bf'], P['peer_v_bf'], layer)
        if layer == N_A_LAYERS - 1:
            rows = kv_rows(h, P)
            if past is None:
                assert T % (PAGE_SIZE * PPS) == 0 and T % TQ == 0
                npg = T // PAGE_SIZE
                pool = rows[:, :, 0].reshape(B * npg, PAGE_SIZE, 2, NSA_GROUPS, NSA_HEAD_DIM)
                table = jnp.arange(B * npg, dtype=jnp.int32).reshape(B, npg)
                win_new = kv5(rows[:, T - min(WINDOW, T):, 2])
            else:
                past_len = past['page_table'].shape[1] * PAGE_SIZE
                assert (past_len + T) // CMP_STRIDE == past_len // CMP_STRIDE
                pool, table = past['cmp_pool'], past['page_table']
                win_all = jnp.concatenate([past['win_buf'], kv5(rows[:, :, 2])], axis=1)
                win_new = win_all[:, win_all.shape[1] - min(WINDOW, win_all.shape[1]):]
            kc, vc = compress_paged(pool, table, *P['cmp_stack'])
    return (rmsnorm(h, P['norm_final']), jnp.stack(gla_states), kv5(rows[:, :, 0]), kv5(rows[:, :, 1]), win_new)


def kernel(x_prompt, x_sample, state_gla, cache_cmp_kv, cache_slc_kv, cache_win_kv, page_table,
           norm_mix, norm_ffn, norm_kv, norm_final, gla_w_in, gla_w_gate_up, gla_b_gate, gla_norm, gla_w_out,
           kv_w, cmp_k_w1, cmp_k_b1, cmp_k_w2, cmp_k_b2, cmp_k_pe, cmp_v_w1, cmp_v_b1, cmp_v_w2, cmp_v_b2, cmp_v_pe,
           nsa_w_in, nsa_w_out, rel_bias, peer_w_q, peer_subkeys, peer_u, peer_v):
    P = dict(norm_mix=norm_mix, norm_ffn=norm_ffn, norm_kv=norm_kv, norm_final=norm_final,
             gla_w_in=gla_w_in, gla_w_gate_up=gla_w_gate_up, gla_b_gate=gla_b_gate, gla_norm=gla_norm,
             gla_w_out=gla_w_out, kv_w=kv_w,
             cmp_k_w1=cmp_k_w1, cmp_k_b1=cmp_k_b1, cmp_k_w2=cmp_k_w2, cmp_k_b2=cmp_k_b2, cmp_k_pe=cmp_k_pe,
             cmp_v_w1=cmp_v_w1, cmp_v_b1=cmp_v_b1, cmp_v_w2=cmp_v_w2, cmp_v_b2=cmp_v_b2, cmp_v_pe=cmp_v_pe,
             nsa_w_in=nsa_w_in, nsa_w_out=nsa_w_out, rel_bias=rel_bias,
             peer_w_q=peer_w_q, peer_subkeys=peer_subkeys,
             peer_u_bf=peer_u.astype(jnp.bfloat16), peer_v_bf=peer_v.astype(jnp.bfloat16))
    P['cmp_stack'] = compress_params(P)
    past = dict(cmp_pool=cache_cmp_kv, slc_pool=cache_slc_kv, win_buf=cache_win_kv,
                page_table=page_table.astype(jnp.int32))
    gla_zero = jnp.zeros((N_A_LAYERS, BATCH, GLA_HEADS, GLA_DK, GLA_DV), x_prompt.dtype)
    y_prompt, gla_p, cmp_p, slc_p, win_p = run_trunk(x_prompt, gla_zero, None, P)
    y_sample, gla_s, cmp_s, slc_s, win_s = run_trunk(x_sample, state_gla, past, P)
    return (y_prompt, y_sample, gla_p, gla_s, cmp_p, cmp_s, slc_p, slc_s, win_p, win_s)
```

```python
import functools
import math

import jax
import jax.numpy as jnp
from jax import lax
from jax.experimental import pallas as pl
from jax.experimental.pallas import tpu as pltpu

D_MODEL = 2048
BATCH = 4
SEQ = 2048
DEPTH = 2
DEC_BATCH = 32
DEC_SEQ = 8
PAST_LEN = 8192
PAGE_SIZE = 128
N_A_LAYERS = DEPTH // 2
N_B_LAYERS = DEPTH - N_A_LAYERS
GLA_HEADS = 4
GLA_DK = D_MODEL // (2 * GLA_HEADS)
GLA_DV = D_MODEL // GLA_HEADS
GLA_GATE_RANK = 16
GLA_TAU = 16.0
GLA_CHUNK = 64
NSA_HEADS = 16
NSA_GROUPS = 4
NSA_HPG = NSA_HEADS // NSA_GROUPS
NSA_HEAD_DIM = D_MODEL // NSA_HEADS
CMP_STRIDE = 16
CMP_BLOCK = 2 * CMP_STRIDE
CMP_HIDDEN = 2 * NSA_HEAD_DIM
SEL_BLOCK = 64
N_SEL = 16
WINDOW = 512
Q_BLOCK = 32
REL_BUCKETS = 32
REL_MAX_DIST = 128
PEER_HEADS = 8
PEER_NKEYS = 128
PEER_EXPERTS = PEER_NKEYS * PEER_NKEYS
PEER_TOPK = 16
PEER_KEY_DIM = 256
PEER_TOK_BLOCK = 128
EPS = 1e-6
NEG = -1e30
BIG = 1e30

LANE = 128
VMEM_LIMIT = 56 * 1024 * 1024
TQ = 128
assert REL_MAX_DIST <= TQ and WINDOW % TQ == 0 and TQ % SEL_BLOCK == 0


def _mm_res_kernel(a_ref, b_ref, r_ref, o_ref):
    a = a_ref[...].astype(jnp.bfloat16)
    b = b_ref[...].astype(jnp.bfloat16)
    o_ref[...] = r_ref[...] + jnp.dot(a, b, preferred_element_type=jnp.float32)


def mm_res(x, w, res, tm=1024, tn=1024):
    lead = x.shape[:-1]
    K, N = w.shape
    a = x.reshape(-1, K)
    M = a.shape[0]
    tm, tn = math.gcd(M, tm), math.gcd(N, tn)
    assert tn % LANE == 0 and tm % 8 == 0
    out = pl.pallas_call(
        _mm_res_kernel,
        grid=(M // tm, N // tn),
        in_specs=[pl.BlockSpec((tm, K), lambda i, j: (i, 0)),
                  pl.BlockSpec((K, tn), lambda i, j: (0, j)),
                  pl.BlockSpec((tm, tn), lambda i, j: (i, j))],
        out_specs=pl.BlockSpec((tm, tn), lambda i, j: (i, j)),
        out_shape=jax.ShapeDtypeStruct((M, N), jnp.float32),
        compiler_params=pltpu.CompilerParams(
            dimension_semantics=("parallel", "parallel"), vmem_limit_bytes=VMEM_LIMIT),
        name="out_proj_matmul",
    )(a, w, res.reshape(M, N))
    return out.reshape(lead + (N,))


def _norm_mm_kernel(a_ref, g_ref, b_ref, o_ref, xn_ref):
    @pl.when(pl.program_id(1) == 0)
    def _():
        x = a_ref[...]
        y = x * lax.rsqrt(jnp.mean(x * x, axis=-1, keepdims=True) + EPS) * g_ref[...]
        xn_ref[...] = y.astype(jnp.bfloat16)

    o_ref[...] = jnp.dot(xn_ref[...], b_ref[...].astype(jnp.bfloat16), preferred_element_type=jnp.float32)


def norm_mm(x, gain, b, tm=1024, tn=512, keep_cols=False):
    M, K = x.shape
    N = b.shape[1]
    tm = math.gcd(M, tm)
    Np = -(-N // LANE) * LANE
    wide = [d * LANE for d in range(1, 2 * tn // LANE + 1) if Np % (d * LANE) == 0]
    if max(wide) < tn:
        Np = -(-N // tn) * tn
        wide = [tn]
    tn = max(wide)
    if Np != N:
        b = jnp.pad(b, ((0, 0), (0, Np - N)))
    out, xn = pl.pallas_call(
        _norm_mm_kernel,
        grid=(M // tm, Np // tn),
        in_specs=[pl.BlockSpec((tm, K), lambda i, j: (i, 0)),
                  pl.BlockSpec((1, K), lambda i, j: (0, 0)),
                  pl.BlockSpec((K, tn), lambda i, j: (0, j))],
        out_specs=[pl.BlockSpec((tm, tn), lambda i, j: (i, j)),
                   pl.BlockSpec((tm, K), lambda i, j: (i, 0))],
        out_shape=[jax.ShapeDtypeStruct((M, Np), jnp.float32), jax.ShapeDtypeStruct((M, K), jnp.bfloat16)],
        compiler_params=pltpu.CompilerParams(
            dimension_semantics=("parallel", "arbitrary"), vmem_limit_bytes=VMEM_LIMIT),
        name="norm_matmul",
    )(x, gain.reshape(1, K).astype(jnp.float32), b)
    return (out if keep_cols else out[:, :N]), xn


def rmsnorm(x, g):
    xf = x.astype(jnp.float32)
    y = xf * lax.rsqrt(jnp.mean(xf * xf, axis=-1, keepdims=True) + EPS)
    return (y * g.astype(jnp.float32)).astype(x.dtype)


def t5_bucket(rel):
    n = jnp.maximum(rel, 0)
    exact = REL_BUCKETS // 2
    nf = jnp.maximum(n, exact).astype(jnp.float32)
    large = exact + (jnp.log(nf / exact) / math.log(REL_MAX_DIST / exact) * (REL_BUCKETS - exact)).astype(jnp.int32)
    large = jnp.minimum(large, REL_BUCKETS - 1)
    return jnp.where(n < exact, n, large)


def head_bias(table, rel):
    onehot = jax.nn.one_hot(t5_bucket(rel), REL_BUCKETS, dtype=jnp.float32)
    b = jnp.einsum('qkn,nh->hqk', onehot, table.astype(jnp.float32), precision=lax.Precision.HIGHEST)
    return b.reshape(NSA_GROUPS, NSA_HPG, rel.shape[0], rel.shape[1])


DK_ALL = GLA_HEADS * GLA_DK
DV_ALL = GLA_HEADS * GLA_DV
GZ_COL = 2 * DK_ALL + 2 * DV_ALL
assert GZ_COL % LANE == 0 and GLA_GATE_RANK <= LANE and DK_ALL % LANE == 0 and (2 * DK_ALL) % DV_ALL == 0


def _gla_kernel(q_ref, k_ref, v_ref, r_ref, gz_ref, wg_ref, bg_ref, ng_ref, tri_ref, s0_ref,
                o_ref, sfin_ref, s_scr, *, C, t_valid):
    c = pl.program_id(1)

    @pl.when(c == 0)
    def _():
        s_scr[...] = s0_ref[0]

    row = lax.broadcasted_iota(jnp.int32, (C, C), 0)
    col = lax.broadcasted_iota(jnp.int32, (C, C), 1)
    bf = jnp.bfloat16
    z_all = jnp.dot(gz_ref[0].astype(bf), wg_ref[...], preferred_element_type=jnp.float32) + bg_ref[...]
    for h in range(GLA_HEADS):
        dk = slice(h * GLA_DK, (h + 1) * GLA_DK)
        dv = slice(h * GLA_DV, (h + 1) * GLA_DV)
        z = z_all[:, dk]
        g = (jnp.minimum(z, 0.0) - jnp.log(1.0 + jnp.exp(-jnp.abs(z)))) * (1.0 / GLA_TAU)
        if t_valid is not None:
            g = jnp.where(lax.broadcasted_iota(jnp.int32, g.shape, 0) < t_valid, g, 0.0)
        b = jnp.dot(tri_ref[...], g, preferred_element_type=jnp.float32, precision=lax.Precision.HIGHEST)
        b_last = b[C - 1:C, :]
        q = q_ref[0, :, dk] * (GLA_DK ** -0.5)
        k = k_ref[0, :, dk]
        v = v_ref[0, :, dv].astype(bf)
        qe = (q * jnp.exp(b)).astype(bf)
        ke = (k * jnp.exp(-b)).astype(bf)
        a = lax.dot_general(qe, ke, (((1,), (1,)), ((), ())), preferred_element_type=jnp.float32)
        a = jnp.where(col <= row, a, 0.0)
        s = s_scr[h]
        o = (jnp.dot(a.astype(bf), v, preferred_element_type=jnp.float32)
             + jnp.dot(qe, s.astype(bf), preferred_element_type=jnp.float32))
        kd = (k * jnp.exp(b_last - b)).astype(bf)
        dcol = jnp.exp(jnp.transpose(jnp.broadcast_to(b_last, (LANE, GLA_DK))))
        s_new = (s * jnp.concatenate([dcol] * (GLA_DV // LANE), axis=1)
                 + lax.dot_general(kd, v, (((0,), (0,)), ((), ())), preferred_element_type=jnp.float32))
        s_scr[h] = s_new
        o = o * lax.rsqrt(jnp.mean(o * o, axis=-1, keepdims=True) + EPS) * ng_ref[...]
        r = r_ref[0, :, dv]
        o_ref[0, :, dv] = o * (r * (1.0 / (1.0 + jnp.exp(-r))))

    @pl.when(c == pl.num_programs(1) - 1)
    def _():
        sfin_ref[0] = s_scr[...]


def gla_core(proj, s0, w_gate_up, b_gate, norm_g, T):
    B, Tp, _ = proj.shape
    C = GLA_CHUNK if T % GLA_CHUNK == 0 else Tp
    assert Tp % C == 0 and C % 16 == 0 and (Tp == T or Tp == C)
    n = Tp // C
    wg = jnp.pad(w_gate_up, ((0, LANE - GLA_GATE_RANK), (0, 0))).astype(jnp.bfloat16)
    tri = (jnp.arange(C)[:, None] >= jnp.arange(C)[None, :]).astype(jnp.float32)
    col_blk = lambda w, i: pl.BlockSpec((1, C, w), lambda b, c: (b, c, i))
    full = lambda a: pl.BlockSpec(a.shape, lambda b, c: (0,) * a.ndim)
    bg = b_gate.reshape(1, DK_ALL)
    ng = norm_g.reshape(1, GLA_DV)
    st_spec = pl.BlockSpec((1, GLA_HEADS, GLA_DK, GLA_DV), lambda b, c: (b, 0, 0, 0))
    return pl.pallas_call(
        functools.partial(_gla_kernel, C=C, t_valid=None if Tp == T else T),
        grid=(B, n),
        in_specs=[col_blk(DK_ALL, 0), col_blk(DK_ALL, 1), col_blk(DV_ALL, 2 * DK_ALL // DV_ALL),
                  col_blk(DV_ALL, (2 * DK_ALL + DV_ALL) // DV_ALL), col_blk(LANE, GZ_COL // LANE),
                  full(wg), full(bg), full(ng), full(tri), st_spec],
        out_specs=[pl.BlockSpec((1, C, DV_ALL), lambda b, c: (b, c, 0)), st_spec],
        out_shape=[jax.ShapeDtypeStruct((B, Tp, DV_ALL), jnp.float32),
                   jax.ShapeDtypeStruct((B, GLA_HEADS, GLA_DK, GLA_DV), jnp.float32)],
        scratch_shapes=[pltpu.VMEM((GLA_HEADS, GLA_DK, GLA_DV), jnp.float32)],
        compiler_params=pltpu.CompilerParams(dimension_semantics=("parallel", "arbitrary"),
                                             vmem_limit_bytes=VMEM_LIMIT),
        name="gla_core",
    )(proj, proj, proj, proj, proj, wg, bg, ng, tri, s0)


def gla_mixer(h, gain, s0, w_in, w_gate_up, b_gate, norm_g, w_out):
    B, T, D = h.shape
    proj, _ = norm_mm(h.reshape(B * T, D), gain, w_in, keep_cols=True)
    assert proj.shape[1] >= GZ_COL + LANE
    Tp = -(-T // 16) * 16
    proj = jnp.pad(proj.reshape(B, T, -1), ((0, 0), (0, Tp - T), (0, 0)))
    o, s_new = gla_core(proj, s0.astype(jnp.float32), w_gate_up, b_gate, norm_g, T)
    return mm_res(o[:, :T], w_out, h), s_new.astype(s0.dtype)


WSUM_UNROLL = 16
PEER_CHUNK = 1024


def _wsum_kernel(a_ref, b_ref, g_ref, o_ref, *, tb):
    sub = lax.broadcasted_iota(jnp.int32, (PEER_NKEYS, PEER_NKEYS), 0)

    def body(tt, c):
        t0 = pl.multiple_of(tt * WSUM_UNROLL, WSUM_UNROLL)
        a8 = a_ref[pl.ds(t0, WSUM_UNROLL), :]
        b8 = b_ref[pl.ds(t0, WSUM_UNROLL), :]
        g8 = g_ref[pl.ds(t0, WSUM_UNROLL), :]
        for u in range(WSUM_UNROLL):
            at = jnp.where(sub == a8[u:u + 1], 1.0, 0.0).astype(jnp.bfloat16)
            bt = jnp.where(sub == b8[u:u + 1], g8[u:u + 1], 0.0).astype(jnp.bfloat16)
            o_ref[t0 + u] = lax.dot_general(at, bt, (((1,), (1,)), ((), ())), preferred_element_type=jnp.float32)
        return c

    lax.fori_loop(0, tb // WSUM_UNROLL, body, 0)


def peer_wsum(i1, i2, gate, tb=64):
    n, K = i1.shape
    tb = math.gcd(n, tb)
    assert tb % WSUM_UNROLL == 0
    return pl.pallas_call(
        functools.partial(_wsum_kernel, tb=tb),
        grid=(n // tb,),
        in_specs=[pl.BlockSpec((tb, K), lambda i: (i, 0))] * 3,
        out_specs=pl.BlockSpec((tb, PEER_NKEYS, PEER_NKEYS), lambda i: (i, 0, 0)),
        out_shape=jax.ShapeDtypeStruct((n, PEER_NKEYS, PEER_NKEYS), jnp.float32),
        compiler_params=pltpu.CompilerParams(dimension_semantics=("parallel",), vmem_limit_bytes=VMEM_LIMIT),
        name="peer_wsum",
    )(i1, i2, gate)


def _peer_kernel(x_ref, u_ref, v_ref, w_ref, o_ref, *, te):
    j = pl.program_id(1)
    x = x_ref[...]
    acc = None
    for c in range(te // PEER_CHUNK):
        e0 = c * PEER_CHUNK
        hid = lax.dot_general(x, u_ref[e0:e0 + PEER_CHUNK, :], (((1,), (1,)), ((), ())),
                              preferred_element_type=jnp.float32)
        parts = []
        for r in range(PEER_CHUNK // PEER_NKEYS):
            hr = hid[:, r * PEER_NKEYS:(r + 1) * PEER_NKEYS]
            ar = 0.5 * hr * (1.0 + lax.erf(hr * (2.0 ** -0.5))) * w_ref[:, e0 // PEER_NKEYS + r, :]
            parts.append(ar.astype(jnp.bfloat16))
        d = jnp.dot(jnp.concatenate(parts, axis=1), v_ref[e0:e0 + PEER_CHUNK, :], preferred_element_type=jnp.float32)
        acc = d if acc is None else acc + d
    @pl.when(j == 0)
    def _():
        o_ref[...] = acc

    @pl.when(j > 0)
    def _():
        o_ref[...] += acc


def peer_dense(xn, u, v, layer, wsum, tb=1024, te=1024):
    n, D = xn.shape
    E = u.shape[1]
    tb = math.gcd(n, tb)
    return pl.pallas_call(
        functools.partial(_peer_kernel, te=te),
        grid=(n // tb, E // te),
        in_specs=[pl.BlockSpec((tb, D), lambda i, j: (i, 0), pipeline_mode=pl.Buffered(1)),
                  pl.BlockSpec((None, te, D), lambda i, j: (layer, j, 0)),
                  pl.BlockSpec((None, te, D), lambda i, j: (layer, j, 0)),
                  pl.BlockSpec((tb, te // PEER_NKEYS, PEER_NKEYS), lambda i, j: (i, j, 0))],
        out_specs=pl.BlockSpec((tb, D), lambda i, j: (i, 0), pipeline_mode=pl.Buffered(1)),
        out_shape=jax.ShapeDtypeStruct((n, D), jnp.float32),
        compiler_params=pltpu.CompilerParams(dimension_semantics=("parallel", "arbitrary"),
                                             vmem_limit_bytes=VMEM_LIMIT),
        name="peer_dense",
    )(xn, u, v, wsum)


RT = LANE
NHC = 2 * PEER_HEADS
ROUTE_CHAINS = 4
_CAND_GROUPS = [(0, 0), (0, 8), (1, 0)] + [(a, 0) for a in range(2, 8)] + [(-1, 0)]
assert PEER_TOPK == 16 and PEER_KEY_DIM // 2 == LANE and PEER_NKEYS == LANE


def _top_rounds(s, n_rounds, extra=()):
    R = s.shape[0]
    iota = lax.broadcasted_iota(jnp.int32, s.shape, 0)
    vals, idxs, ex = [], [], [[] for _ in extra]
    for _ in range(n_rounds):
        m = jnp.max(s, axis=0, keepdims=True)
        idx = jnp.min(jnp.where(s == m, iota, R), axis=0, keepdims=True)
        sel = iota == idx
        for e, lst in zip(extra, ex):
            lst.append(jnp.max(jnp.where(sel, e, -1), axis=0, keepdims=True))
        s = jnp.where(sel, -jnp.inf, s)
        vals.append(m)
        idxs.append(idx)
    cat = lambda l: jnp.concatenate(l, axis=0)
    return cat(vals), cat(idxs), [cat(l) for l in ex]


def _route_kernel(q_ref, sub_ref, e1_ref, e2_ref, g_ref, v_scr, i_scr):
    K = PEER_TOPK

    def stage1(hp, c):
        for u in range(ROUTE_CHAINS):
            hc = ROUTE_CHAINS * hp + u
            off = pl.multiple_of(hc * LANE, LANE)
            qb = q_ref[:, pl.ds(off, LANE)].astype(jnp.bfloat16)
            s = lax.dot_general(sub_ref[hc], qb, (((1,), (1,)), ((), ())), preferred_element_type=jnp.float32)
            v, i, _ = _top_rounds(s, K)
            v_scr[hc] = v
            i_scr[hc] = i
        return c

    lax.fori_loop(0, NHC // ROUTE_CHAINS, stage1, 0)

    row8 = lax.broadcasted_iota(jnp.int32, (8, RT), 0)

    def stage2(h):
        v1, v2 = v_scr[2 * h], v_scr[2 * h + 1]
        i1, i2 = i_scr[2 * h], i_scr[2 * h + 1]
        cand, c1, c2 = [], [], []
        for a, b0 in _CAND_GROUPS:
            if a >= 0:
                nb = K // (a + 1)
                sm = v1[a:a + 1] + v2[b0:b0 + 8]
                if nb - b0 < 8:
                    sm = jnp.where(row8 < nb - b0, sm, -jnp.inf)
                cand.append(sm)
                c1.append(jnp.broadcast_to(i1[a:a + 1], (8, RT)))
                c2.append(i2[b0:b0 + 8])
            else:
                cand.append(v1[8:16] + v2[0:1])
                c1.append(i1[8:16])
                c2.append(jnp.broadcast_to(i2[0:1], (8, RT)))
        cat = lambda l: jnp.concatenate(l, axis=0)
        top, _, (e12,) = _top_rounds(cat(cand), K, extra=(cat(c1) * PEER_NKEYS + cat(c2),))
        ex = jnp.exp(top - top[0:1])
        g = ex / jnp.sum(ex, axis=0, keepdims=True)
        r0 = pl.multiple_of(h * K, K)
        e1_ref[0, pl.ds(r0, K), :] = e12 // PEER_NKEYS
        e2_ref[0, pl.ds(r0, K), :] = e12 % PEER_NKEYS
        g_ref[0, pl.ds(r0, K), :] = g

    def stage2_pair(hp, c):
        stage2(2 * hp)
        stage2(2 * hp + 1)
        return c

    lax.fori_loop(0, PEER_HEADS // 2, stage2_pair, 0)


def peer_route(q, sub_bf):
    n = q.shape[0]
    assert n % RT == 0
    nb = n // RT
    slots = PEER_HEADS * PEER_TOPK
    out = jax.ShapeDtypeStruct((nb, slots, RT), jnp.int32)
    ospec = pl.BlockSpec((1, slots, RT), lambda i: (i, 0, 0))
    return pl.pallas_call(
        _route_kernel,
        grid=(nb,),
        in_specs=[pl.BlockSpec((RT, q.shape[1]), lambda i: (i, 0)),
                  pl.BlockSpec(sub_bf.shape, lambda i: (0, 0, 0))],
        out_specs=[ospec, ospec, ospec],
        out_shape=[out, out, jax.ShapeDtypeStruct((nb, slots, RT), jnp.float32)],
        scratch_shapes=[pltpu.VMEM((NHC, PEER_TOPK, RT), jnp.float32),
                        pltpu.VMEM((NHC, PEER_TOPK, RT), jnp.int32)],
        compiler_params=pltpu.CompilerParams(dimension_semantics=("parallel",), vmem_limit_bytes=VMEM_LIMIT),
        name="peer_route",
    )(q, sub_bf)


def peer_ffn(h, gain, w_q, subkeys, u_bf, v_bf, layer):
    B, T, D = h.shape
    n = B * T
    q, xt = norm_mm(h.reshape(n, D), gain, w_q)
    sub_bf = subkeys.astype(jnp.bfloat16).reshape(NHC, PEER_NKEYS, PEER_KEY_DIM // 2)
    e1, e2, gate = peer_route(q, sub_bf)
    tok_major = lambda t: jnp.transpose(t, (0, 2, 1)).reshape(n, PEER_HEADS * PEER_TOPK)
    wsum = peer_wsum(tok_major(e1), tok_major(e2), tok_major(gate))
    out = peer_dense(xt, u_bf, v_bf, layer, wsum)
    return h + out.reshape(B, T, D)


KVG = 2 * NSA_GROUPS
ROW_W = KVG * NSA_HEAD_DIM
CPP = PAGE_SIZE // CMP_STRIDE
PPS = 8
assert CPP == 8 and PAGE_SIZE == LANE and NSA_HEAD_DIM == LANE


def _compress_kernel(pt_ref, *refs, n_steps):
    pages = refs[:PPS]
    w1_ref, c1_ref, w2_ref, b2_ref, kc_ref, vc_ref, xc_ref = refs[PPS:]
    s_idx = pl.program_id(1)
    for pair in range(PPS // 2):
        row0 = pl.multiple_of((s_idx * (PPS // 2) + pair) * 2 * CPP, 2 * CPP)
        for kvg in range(KVG):
            for s in range(CMP_STRIDE):
                lo = pages[2 * pair][0, pl.ds(s * KVG + kvg, CPP, stride=KVG * CMP_STRIDE), :]
                hi = pages[2 * pair + 1][0, pl.ds(s * KVG + kvg, CPP, stride=KVG * CMP_STRIDE), :]
                xc_ref[kvg, pl.ds(row0, 2 * CPP), pl.ds(s * LANE, LANE)] = (
                    jnp.concatenate([lo, hi], axis=0).astype(jnp.bfloat16))

    @pl.when(s_idx == n_steps - 1)
    def _():
        nch = xc_ref.shape[1]
        for kv in range(2):
            out_ref = kc_ref if kv == 0 else vc_ref
            for g in range(NSA_GROUPS):
                hh = jnp.dot(xc_ref[kv * NSA_GROUPS + g], w1_ref[kv], preferred_element_type=jnp.float32)
                h1 = hh[:, :CMP_HIDDEN]
                h2 = pltpu.roll(hh[:, CMP_HIDDEN:], nch - 1, 0)
                x = h1 + h2 + c1_ref[kv]
                hid = 0.5 * x * (1.0 + lax.erf(x * (2.0 ** -0.5)))
                o = jnp.dot(hid.astype(jnp.bfloat16), w2_ref[kv], preferred_element_type=jnp.float32) + b2_ref[kv]
                out_ref[0, g] = o.astype(out_ref.dtype)


def compress_paged(pool, page_table, w1, c1, w2, b2):
    B, NP = page_table.shape
    assert NP % PPS == 0
    n_steps = NP // PPS
    nch = NP * CPP
    dh = NSA_HEAD_DIM
    pool = pool.reshape(pool.shape[0], PAGE_SIZE * KVG, dh)

    def page_spec(u):
        return pl.BlockSpec((1, PAGE_SIZE * KVG, dh), lambda b, s, pt: (pt[b, s * PPS + u], 0, 0))

    full = lambda a: pl.BlockSpec(a.shape, lambda b, s, pt: (0,) * a.ndim)
    out_spec = pl.BlockSpec((1, NSA_GROUPS, nch, dh), lambda b, s, pt: (b, 0, 0, 0))
    grid_spec = pltpu.PrefetchScalarGridSpec(
        num_scalar_prefetch=1,
        grid=(B, n_steps),
        in_specs=[page_spec(u) for u in range(PPS)] + [full(w1), full(c1), full(w2), full(b2)],
        out_specs=[out_spec, out_spec],
        scratch_shapes=[pltpu.VMEM((KVG, nch, CMP_STRIDE * dh), jnp.bfloat16)],
    )
    out = jax.ShapeDtypeStruct((B, NSA_GROUPS, nch, dh), jnp.bfloat16)
    return pl.pallas_call(
        functools.partial(_compress_kernel, n_steps=n_steps),
        grid_spec=grid_spec,
        out_shape=[out, out],
        compiler_params=pltpu.CompilerParams(dimension_semantics=("parallel", "arbitrary"),
                                             vmem_limit_bytes=VMEM_LIMIT),
        name="nsa_compress",
    )(page_table, *([pool] * PPS), w1, c1, w2, b2)


def compress_params(P):
    dh, F = NSA_HEAD_DIM, CMP_HIDDEN
    w1s, c1s, w2s, b2s = [], [], [], []
    for n in ('k', 'v'):
        w1h = P['cmp_%s_w1' % n].reshape(2, CMP_STRIDE * dh, F)
        pe = P['cmp_%s_pe' % n].reshape(2, CMP_STRIDE * dh)
        w1s.append(jnp.concatenate([w1h[0], w1h[1]], axis=1))
        c1s.append(P['cmp_%s_b1' % n] + jnp.einsum('hk,hkf->f', pe, w1h, precision=lax.Precision.HIGHEST))
        w2s.append(P['cmp_%s_w2' % n])
        b2s.append(P['cmp_%s_b2' % n])
    return (jnp.stack(w1s).astype(jnp.bfloat16), jnp.stack(c1s)[:, None, :],
            jnp.stack(w2s).astype(jnp.bfloat16), jnp.stack(b2s)[:, None, :])


def _softmax_rows(s, valid):
    s = jnp.where(valid, s, NEG)
    m = jnp.max(s, axis=-1, keepdims=True)
    e = jnp.where(valid, jnp.exp(s - m), 0.0)
    return e / jnp.maximum(jnp.sum(e, axis=-1, keepdims=True), 1e-30)


HQ = NSA_HPG * TQ
GATE_ROWS = 8


def _lanes4(x):
    return jnp.concatenate([x] * NSA_HPG, axis=1)


def _nsa_cmp_sel_kernel(q_ref, kc_ref, vct_ref, bias_ref, ovt_ref, oct_ref, msel_ref, *, nc, ns):
    i = pl.program_id(2)
    ncp = kc_ref.shape[2]
    nsp = ovt_ref.shape[0]
    q_all = q_ref[0, 0].reshape(HQ, NSA_HEAD_DIM)
    cidx = lax.broadcasted_iota(jnp.int32, (ncp, TQ), 0)
    qpos = i * TQ + lax.broadcasted_iota(jnp.int32, (ncp, TQ), 1)
    valid = _lanes4((cidx * CMP_STRIDE + (CMP_BLOCK - 1) <= qpos) & (cidx < nc))
    s = lax.dot_general(kc_ref[0, 0], q_all, (((1,), (1,)), ((), ())), preferred_element_type=jnp.float32)
    s = jnp.where(valid, s + bias_ref[0, 0], NEG)
    m = jnp.max(s, axis=0, keepdims=True)
    e = jnp.where(valid, jnp.exp(s - m), 0.0)
    p = e / jnp.maximum(jnp.sum(e, axis=0, keepdims=True), 1e-30)
    oct_ref[0, 0, 0] = jnp.dot(vct_ref[0, 0], p.astype(jnp.bfloat16), preferred_element_type=jnp.float32)
    psum = p[:, 0:TQ]
    for hh in range(1, NSA_HPG):
        psum = psum + p[:, hh * TQ:(hh + 1) * TQ]
    imp = jnp.dot(ovt_ref[...], psum.astype(jnp.bfloat16), preferred_element_type=jnp.float32)
    blk = lax.broadcasted_iota(jnp.int32, (nsp, TQ), 0)
    qp = i * TQ + lax.broadcasted_iota(jnp.int32, (nsp, TQ), 1)
    cur = qp // SEL_BLOCK
    ok_blk = (blk * SEL_BLOCK <= qp) & (blk < ns)
    forced = (blk == 0) | (blk == cur) | (blk == cur - 1)
    score = jnp.where(ok_blk, jnp.where(forced, BIG, imp), -BIG)
    score = jnp.where(blk < ns, score, -jnp.inf)
    msel = jnp.zeros((nsp, TQ), jnp.float32)
    for _ in range(min(N_SEL, ns)):
        mx = jnp.max(score, axis=0, keepdims=True)
        idx = jnp.min(jnp.where(score == mx, blk, nsp), axis=0, keepdims=True)
        sel = blk == idx
        msel = jnp.where(sel & (mx > -0.5 * BIG), 1.0, msel)
        score = jnp.where(sel, -jnp.inf, score)
    msel_ref[0, 0] = msel


def nsa_cmp_sel(qh, kc, vct, bias_ct, ovt, nc, ns):
    B, G, HPG, T, dh = qh.shape
    ncp = kc.shape[2]
    nsp = ovt.shape[0]
    nqt = T // TQ
    return pl.pallas_call(
        functools.partial(_nsa_cmp_sel_kernel, nc=nc, ns=ns),
        grid=(B, G, nqt),
        in_specs=[pl.BlockSpec((1, 1, HPG, TQ, dh), lambda b, g, i: (b, g, 0, i, 0)),
                  pl.BlockSpec((1, 1, ncp, dh), lambda b, g, i: (b, g, 0, 0)),
                  pl.BlockSpec((1, 1, dh, ncp), lambda b, g, i: (b, g, 0, 0)),
                  pl.BlockSpec((1, 1, ncp, HQ), lambda b, g, i: (g, i, 0, 0)),
                  pl.BlockSpec((nsp, ncp), lambda b, g, i: (0, 0))],
        out_specs=[pl.BlockSpec((1, 1, 1, dh, HQ), lambda b, g, i: (b, g, i, 0, 0)),
                   pl.BlockSpec((1, 1, nsp, TQ), lambda b, g, i: (b, g, 0, i))],
        out_shape=[jax.ShapeDtypeStruct((B, G, nqt, dh, HQ), jnp.float32),
                   jax.ShapeDtypeStruct((B, G, nsp, T), jnp.float32)],
        compiler_params=pltpu.CompilerParams(dimension_semantics=("parallel", "parallel", "parallel"),
                                             vmem_limit_bytes=VMEM_LIMIT),
        name="nsa_cmp_sel",
    )(qh, kc, vct, bias_ct, ovt)


def _flash_tile_t(q_all, k, vt, bias, valid, m_ref, l_ref, acc_ref):
    s = lax.dot_general(k, q_all, (((1,), (1,)), ((), ())), preferred_element_type=jnp.float32)
    s = jnp.where(valid, s + bias, NEG)
    m_old = m_ref[...]
    m_new = jnp.maximum(m_old, jnp.max(s, axis=0, keepdims=True))
    p = jnp.where(valid, jnp.exp(s - m_new), 0.0)
    alpha = jnp.exp(m_old - m_new)
    l_ref[...] = alpha * l_ref[...] + jnp.sum(p, axis=0, keepdims=True)
    acc_ref[...] = alpha * acc_ref[...] + jnp.dot(vt, p.astype(jnp.bfloat16), preferred_element_type=jnp.float32)
    m_ref[...] = m_new


def _nsa_sw_kernel(q_ref, ks_ref, vst_ref, kw_ref, vwt_ref, msel_ref, expt_ref, bias_ref, selm_ref, winm_ref,
                   oct_ref, gate_ref, o_ref, m_ref, l_ref, acc_ref):
    i = pl.program_id(2)
    q_all = q_ref[0, 0].reshape(HQ, NSA_HEAD_DIM)
    msel = msel_ref[0, 0].astype(jnp.bfloat16)

    def reset():
        m_ref[...] = jnp.full_like(m_ref, NEG)
        l_ref[...] = jnp.zeros_like(l_ref)
        acc_ref[...] = jnp.zeros_like(acc_ref)

    def result():
        return acc_ref[...] / jnp.maximum(l_ref[...], 1e-30)

    def pair(k_ref, vt_ref, j1, dd1, valid_of):
        j0 = j1 - 1
        j0c = jnp.maximum(j0, 0)
        k2 = jnp.concatenate([k_ref[0, 0, j1], k_ref[0, 0, j0c]], axis=0)
        vt2 = jnp.concatenate([vt_ref[0, 0, j1], vt_ref[0, 0, j0c]], axis=1)
        bias2 = jnp.concatenate([bias_ref[0, jnp.minimum(dd1, 2)], bias_ref[0, jnp.minimum(dd1 + 1, 2)]], axis=0)
        valid2 = _lanes4(jnp.concatenate([valid_of(j1, dd1), valid_of(j0c, dd1 + 1) & (j0 >= 0)], axis=0))
        _flash_tile_t(q_all, k2, vt2, bias2, valid2, m_ref, l_ref, acc_ref)

    reset()

    def sel_valid(j, dd):
        keys = jnp.dot(expt_ref[j], msel, preferred_element_type=jnp.float32)
        return (keys * selm_ref[jnp.minimum(dd, 2)]) > 0.5

    def sel_body(jj, c):
        pair(ks_ref, vst_ref, i - 2 * jj, 2 * jj, sel_valid)
        return c

    lax.fori_loop(0, i // 2 + 1, sel_body, 0)
    mix = gate_ref[0, 0, 0, 0:1, :] * oct_ref[0, 0, 0] + gate_ref[0, 0, 0, 1:2, :] * result()
    reset()
    for pp in range(WINDOW // TQ // 2 + 1):
        j1 = i - 2 * pp
        pair(kw_ref, vwt_ref, jnp.maximum(j1, 0), 2 * pp, lambda j, dd: (winm_ref[dd] > 0.5) & (j1 >= 0))
    mix = mix + gate_ref[0, 0, 0, 2:3, :] * result()
    for hh in range(NSA_HPG):
        o_ref[0, :, hh * NSA_HEAD_DIM:(hh + 1) * NSA_HEAD_DIM] = jnp.transpose(mix[:, hh * TQ:(hh + 1) * TQ])


def nsa_sel_win(qh, ks, vst, kw, vwt, msel_t, exp_t, bias_tt, selm_t, winm_t, oct, gates_t):
    B, G, HPG, T, dh = qh.shape
    nt = T // TQ
    nsp = msel_t.shape[2]
    k_spec = pl.BlockSpec((1, 1, nt, TQ, dh), lambda b, g, i: (b, g, 0, 0, 0))
    vt_spec = pl.BlockSpec((1, 1, nt, dh, TQ), lambda b, g, i: (b, g, 0, 0, 0))
    return pl.pallas_call(
        _nsa_sw_kernel,
        grid=(B, G, nt),
        in_specs=[pl.BlockSpec((1, 1, HPG, TQ, dh), lambda b, g, i: (b, g, 0, i, 0)),
                  k_spec, vt_spec, k_spec, vt_spec,
                  pl.BlockSpec((1, 1, nsp, TQ), lambda b, g, i: (b, g, 0, i)),
                  pl.BlockSpec((nt, TQ, nsp), lambda b, g, i: (0, 0, 0)),
                  pl.BlockSpec((1, 3, TQ, HQ), lambda b, g, i: (g, 0, 0, 0)),
                  pl.BlockSpec((3, TQ, TQ), lambda b, g, i: (0, 0, 0)),
                  pl.BlockSpec(winm_t.shape, lambda b, g, i: (0, 0, 0)),
                  pl.BlockSpec((1, 1, 1, dh, HQ), lambda b, g, i: (b, g, i, 0, 0)),
                  pl.BlockSpec((1, 1, 1, GATE_ROWS, HQ), lambda b, g, i: (b, g, i, 0, 0))],
        out_specs=pl.BlockSpec((1, TQ, HPG * dh), lambda b, g, i: (b, i, g)),
        out_shape=jax.ShapeDtypeStruct((B, T, G * HPG * dh), jnp.float32),
        scratch_shapes=[pltpu.VMEM((1, HQ), jnp.float32),
                        pltpu.VMEM((1, HQ), jnp.float32),
                        pltpu.VMEM((dh, HQ), jnp.float32)],
        compiler_params=pltpu.CompilerParams(dimension_semantics=("parallel", "parallel", "arbitrary"),
                                             vmem_limit_bytes=VMEM_LIMIT),
        name="nsa_sel_win",
    )(qh, ks, vst, kw, vwt, msel_t, exp_t, bias_tt, selm_t, winm_t, oct, gates_t)


def nsa_prompt(q, gates, kc, vc, ks, vs, kw_rows, vw_rows, table):
    B, T = q.shape[0], q.shape[1]
    G, HPG, dh = NSA_GROUPS, NSA_HPG, NSA_HEAD_DIM
    Nc = T // CMP_STRIDE - 1
    Ns = -(-T // SEL_BLOCK)
    ncp = kc.shape[2]
    assert ncp % LANE == 0 and ncp >= Nc
    nsp = -(-Ns // 16) * 16
    nqt = T // TQ
    bf = jnp.bfloat16
    qh = jnp.transpose(q.reshape(B, T, G, HPG, dh), (0, 2, 3, 1, 4)).astype(bf)
    qpos = jnp.arange(T)
    cidx = jnp.arange(ncp)
    bias_c = head_bias(table, qpos[:, None] - (cidx * CMP_STRIDE + (CMP_BLOCK - 1))[None, :])
    bias_ct = jnp.transpose(bias_c.reshape(G, HPG, nqt, TQ, ncp), (0, 2, 4, 1, 3)).reshape(G, nqt, ncp, HQ)
    sidx = jnp.arange(nsp)
    ovt = (((cidx * CMP_STRIDE)[None, :] < (sidx * SEL_BLOCK + SEL_BLOCK)[:, None])
           & ((cidx * CMP_STRIDE + CMP_BLOCK)[None, :] > (sidx * SEL_BLOCK)[:, None])
           & (cidx[None, :] < Nc) & (sidx[:, None] < Ns)).astype(bf)
    oct, msel_t = nsa_cmp_sel(qh, kc, jnp.transpose(vc, (0, 1, 3, 2)), bias_ct, ovt, Nc, Ns)
    exp_t = ((jnp.arange(T) // SEL_BLOCK)[:, None] == sidx[None, :]).astype(bf).reshape(nqt, TQ, nsp)
    r = jnp.arange(TQ)
    rel3 = (jnp.arange(3) * TQ)[:, None, None] + r[None, None, :] - r[None, :, None]
    bias_tt = jnp.einsum('dkqn,nh->hdkq', jax.nn.one_hot(t5_bucket(rel3), REL_BUCKETS, dtype=jnp.float32),
                         table.astype(jnp.float32), precision=lax.Precision.HIGHEST)
    bias_tt = jnp.transpose(bias_tt.reshape(G, HPG, 3, TQ, TQ), (0, 2, 3, 1, 4)).reshape(G, 3, TQ, HQ)
    selm_t = (rel3 >= 0).astype(jnp.float32)
    nw = 2 * (WINDOW // TQ // 2 + 1)
    relw = (jnp.arange(nw) * TQ)[:, None, None] + r[None, None, :] - r[None, :, None]
    winm_t = ((relw >= 0) & (relw < WINDOW)).astype(jnp.float32)

    def k_tiles(t):
        return jnp.transpose(t, (0, 2, 1, 3)).astype(bf).reshape(B, G, nqt, TQ, dh)

    def vt_tiles(t):
        return jnp.transpose(t.astype(bf).reshape(B, nqt, TQ, G, dh), (0, 3, 1, 4, 2))

    gates_t = jnp.transpose(gates.reshape(B, nqt, TQ, G, HPG, 3), (0, 3, 1, 5, 4, 2)).reshape(B, G, nqt, 3, HQ)
    gates_t = jnp.pad(gates_t, ((0, 0), (0, 0), (0, 0), (0, GATE_ROWS - 3), (0, 0)))
    return nsa_sel_win(qh, k_tiles(ks), vt_tiles(vs), k_tiles(kw_rows), vt_tiles(vw_rows), msel_t, exp_t, bias_tt,
                       selm_t, winm_t, oct, gates_t)


def _dec_cmp_kernel(q_ref, kc_ref, vc_ref, bias_ref, valid_ref, ov_ref, oc_ref, msel_ref, *, t, q0, ns):
    valid = valid_ref[...] > 0.5
    nsp = ov_ref.shape[1]
    imps = []
    for g in range(NSA_GROUPS):
        s = lax.dot_general(q_ref[0, g], kc_ref[0, g], (((1,), (1,)), ((), ())), preferred_element_type=jnp.float32)
        p = _softmax_rows(s + bias_ref[g], valid)
        oc_ref[0, g] = jnp.dot(p.astype(jnp.bfloat16), vc_ref[0, g], preferred_element_type=jnp.float32)
        psum = p[0:t]
        for hh in range(1, NSA_HPG):
            psum = psum + p[hh * t:(hh + 1) * t]
        imps.append(jnp.dot(psum.astype(jnp.bfloat16), ov_ref[...], preferred_element_type=jnp.float32))
    imp = jnp.concatenate(imps, axis=0)
    blk = lax.broadcasted_iota(jnp.int32, (NSA_GROUPS * t, nsp), 1)
    qp = q0 + jnp.concatenate([lax.broadcasted_iota(jnp.int32, (t, nsp), 0)] * NSA_GROUPS, axis=0)
    cur = qp // SEL_BLOCK
    ok_blk = (blk * SEL_BLOCK <= qp) & (blk < ns)
    forced = (blk == 0) | (blk == cur) | (blk == cur - 1)
    score = jnp.where(ok_blk, jnp.where(forced, BIG, imp), -BIG)
    score = jnp.where(blk < ns, score, -jnp.inf)
    msel = jnp.zeros((NSA_GROUPS * t, nsp), jnp.float32)
    for _ in range(min(N_SEL, ns)):
        mx = jnp.max(score, axis=1, keepdims=True)
        idx = jnp.min(jnp.where(score == mx, blk, nsp), axis=1, keepdims=True)
        sel = blk == idx
        msel = jnp.where(sel & (mx > -0.5 * BIG), 1.0, msel)
        score = jnp.where(sel, -jnp.inf, score)
    for g in range(NSA_GROUPS):
        msel_ref[0, g] = msel[g * t:(g + 1) * t]


def dec_cmp(qg, kc, vc, bias_c, valid_c, overlap, t, q0, ns):
    B, G, R_, dh = qg.shape
    ncp, nsp = overlap.shape
    per_b = lambda a: pl.BlockSpec((1,) + a.shape[1:], lambda b: (b,) + (0,) * (a.ndim - 1))
    full = lambda a: pl.BlockSpec(a.shape, lambda b: (0,) * a.ndim)
    return pl.pallas_call(
        functools.partial(_dec_cmp_kernel, t=t, q0=q0, ns=ns),
        grid=(B,),
        in_specs=[per_b(qg), per_b(kc), per_b(vc), full(bias_c), full(valid_c), full(overlap)],
        out_specs=[pl.BlockSpec((1, G, R_, dh), lambda b: (b, 0, 0, 0)),
                   pl.BlockSpec((1, G, t, nsp), lambda b: (b, 0, 0, 0))],
        out_shape=[jax.ShapeDtypeStruct((B, G, R_, dh), jnp.float32),
                   jax.ShapeDtypeStruct((B, G, t, nsp), jnp.float32)],
        compiler_params=pltpu.CompilerParams(dimension_semantics=("parallel",), vmem_limit_bytes=VMEM_LIMIT),
        name="nsa_dec_cmp",
    )(qg, kc, vc, bias_c, valid_c, overlap)


def _flash_update(s, valid, v, m_ref, l_ref, acc_ref, g):
    s = jnp.where(valid, s, NEG)
    m_old = m_ref[g]
    m_new = jnp.maximum(m_old, jnp.max(s, axis=-1, keepdims=True))
    p = jnp.where(valid, jnp.exp(s - m_new), 0.0)
    alpha = jnp.exp(m_old - m_new)
    l_ref[g] = alpha * l_ref[g] + jnp.sum(p, axis=-1, keepdims=True)
    acc_ref[g] = alpha * acc_ref[g] + jnp.dot(p.astype(jnp.bfloat16), v, preferred_element_type=jnp.float32)
    m_ref[g] = m_new


def _dec_sel_kernel(pt_ref, q_ref, *refs, n_steps, t):
    pages = refs[:PPS]
    new_ref, bias_ref, mask_ref, biasn_ref, maskn_ref, o_ref, m_ref, l_ref, acc_ref = refs[PPS:]
    s_idx = pl.program_id(1)
    dh = NSA_HEAD_DIM

    @pl.when(s_idx == 0)
    def _():
        m_ref[...] = jnp.full_like(m_ref, NEG)
        l_ref[...] = jnp.zeros_like(l_ref)
        acc_ref[...] = jnp.zeros_like(acc_ref)

    def tile_mask(mref, g):
        mk = mref[0, g] > 0.5
        return jnp.concatenate([mk] * NSA_HPG, axis=0)

    for g in range(NSA_GROUPS):
        rows_of = lambda pg, kvg: pg[0, pl.ds(kvg, PAGE_SIZE, stride=KVG), :]
        k = jnp.concatenate([rows_of(pg, g) for pg in pages], axis=0).astype(jnp.bfloat16)
        v = jnp.concatenate([rows_of(pg, NSA_GROUPS + g) for pg in pages], axis=0).astype(jnp.bfloat16)
        s = lax.dot_general(q_ref[0, g], k, (((1,), (1,)), ((), ())), preferred_element_type=jnp.float32)
        _flash_update(s + bias_ref[g], tile_mask(mask_ref, g), v, m_ref, l_ref, acc_ref, g)

    @pl.when(s_idx == n_steps - 1)
    def _():
        for g in range(NSA_GROUPS):
            k = new_ref[0, :, pl.ds(g * dh, dh)].astype(jnp.bfloat16)
            v = new_ref[0, :, pl.ds((NSA_GROUPS + g) * dh, dh)].astype(jnp.bfloat16)
            s = lax.dot_general(q_ref[0, g], k, (((1,), (1,)), ((), ())), preferred_element_type=jnp.float32)
            _flash_update(s + biasn_ref[g], tile_mask(maskn_ref, g), v, m_ref, l_ref, acc_ref, g)
            o_ref[0, g] = acc_ref[g] / jnp.maximum(l_ref[g], 1e-30)


def dec_sel(qg, pool, page_table, new_rows, bias_p, mask_p, bias_n, mask_n, t):
    B, G, R_, dh = qg.shape
    NP = page_table.shape[1]
    assert NP % PPS == 0
    n_steps = NP // PPS
    W = PPS * PAGE_SIZE
    pool = pool.reshape(pool.shape[0], PAGE_SIZE * KVG, dh)

    def page_spec(u):
        return pl.BlockSpec((1, PAGE_SIZE * KVG, dh), lambda b, s, pt: (pt[b, s * PPS + u], 0, 0))

    grid_spec = pltpu.PrefetchScalarGridSpec(
        num_scalar_prefetch=1,
        grid=(B, n_steps),
        in_specs=[pl.BlockSpec((1, G, R_, dh), lambda b, s, pt: (b, 0, 0, 0))]
        + [page_spec(u) for u in range(PPS)]
        + [pl.BlockSpec((1, PAGE_SIZE, ROW_W), lambda b, s, pt: (b, 0, 0)),
           pl.BlockSpec((G, R_, W), lambda b, s, pt: (0, 0, s)),
           pl.BlockSpec((1, G, t, W), lambda b, s, pt: (b, 0, 0, s)),
           pl.BlockSpec((G, R_, PAGE_SIZE), lambda b, s, pt: (0, 0, 0)),
           pl.BlockSpec((1, G, t, PAGE_SIZE), lambda b, s, pt: (b, 0, 0, 0))],
        out_specs=pl.BlockSpec((1, G, R_, dh), lambda b, s, pt: (b, 0, 0, 0)),
        scratch_shapes=[pltpu.VMEM((G, R_, 1), jnp.float32), pltpu.VMEM((G, R_, 1), jnp.float32),
                        pltpu.VMEM((G, R_, dh), jnp.float32)],
    )
    return pl.pallas_call(
        functools.partial(_dec_sel_kernel, n_steps=n_steps, t=t),
        grid_spec=grid_spec,
        out_shape=jax.ShapeDtypeStruct((B, G, R_, dh), jnp.float32),
        compiler_params=pltpu.CompilerParams(dimension_semantics=("parallel", "arbitrary"),
                                             vmem_limit_bytes=VMEM_LIMIT),
        name="nsa_dec_sel",
    )(page_table, qg, *([pool] * PPS), new_rows, bias_p, mask_p, bias_n, mask_n)


def _dec_win_kernel(q_ref, win_ref, new_ref, bias_ref, valid_ref, o_ref):
    dh = NSA_HEAD_DIM
    valid = valid_ref[...] > 0.5
    for g in range(NSA_GROUPS):
        wn = win_ref.shape[1] // KVG
        k = jnp.concatenate([win_ref[0, pl.ds(g, wn, stride=KVG), :], new_ref[0, :, pl.ds(g * dh, dh)]], axis=0)
        v = jnp.concatenate([win_ref[0, pl.ds(NSA_GROUPS + g, wn, stride=KVG), :],
                             new_ref[0, :, pl.ds((NSA_GROUPS + g) * dh, dh)]], axis=0)
        s = lax.dot_general(q_ref[0, g], k.astype(jnp.bfloat16), (((1,), (1,)), ((), ())),
                            preferred_element_type=jnp.float32)
        p = _softmax_rows(s + bias_ref[g], valid)
        o_ref[0, g] = jnp.dot(p.astype(jnp.bfloat16), v.astype(jnp.bfloat16), preferred_element_type=jnp.float32)


def dec_win(qg, win_rows, new_rows, bias_w, valid_w):
    B, G, R_, dh = qg.shape
    per_b = lambda a: pl.BlockSpec((1,) + a.shape[1:], lambda b: (b,) + (0,) * (a.ndim - 1))
    full = lambda a: pl.BlockSpec(a.shape, lambda b: (0,) * a.ndim)
    return pl.pallas_call(
        _dec_win_kernel,
        grid=(B,),
        in_specs=[per_b(qg), per_b(win_rows), per_b(new_rows), full(bias_w), full(valid_w)],
        out_specs=pl.BlockSpec((1, G, R_, dh), lambda b: (b, 0, 0, 0)),
        out_shape=jax.ShapeDtypeStruct((B, G, R_, dh), jnp.float32),
        compiler_params=pltpu.CompilerParams(dimension_semantics=("parallel",), vmem_limit_bytes=VMEM_LIMIT),
        name="nsa_dec_win",
    )(qg, win_rows, new_rows, bias_w, valid_w)


def nsa_decode(q, gates, kc, vc, slc_pool, page_table, slc_new, win_buf, win_new_rows, table):
    B, t = q.shape[0], q.shape[1]
    G, HPG, dh = NSA_GROUPS, NSA_HPG, NSA_HEAD_DIM
    NP = page_table.shape[1]
    past = NP * PAGE_SIZE
    Wn = win_buf.shape[1]
    assert t <= PAGE_SIZE and Wn == WINDOW
    L = past + t
    nc = L // CMP_STRIDE - 1
    ncp = kc.shape[2]
    Ns = -(-L // SEL_BLOCK)
    nsp = -(-Ns // LANE) * LANE
    bf = jnp.bfloat16
    R_ = HPG * t
    qg = jnp.transpose(q.reshape(B, t, G, HPG, dh), (0, 2, 3, 1, 4)).reshape(B, G, R_, dh).astype(bf)
    qpos = past + jnp.arange(t)

    r_lo, r_hi = -(PAGE_SIZE + WINDOW), past + t
    lut_rev = table[t5_bucket(jnp.arange(r_hi, r_lo - 1, -1))].astype(jnp.float32)

    def rows_bias(p0, step, n):
        rows = []
        for i in range(t):
            j0 = r_hi - (past + i - p0)
            assert j0 >= 0 and j0 + step * (n - 1) < lut_rev.shape[0]
            rows.append(lax.slice(lut_rev, (j0, 0), (j0 + step * (n - 1) + 1, NSA_HEADS), (step, 1)))
        b = jnp.transpose(jnp.stack(rows), (2, 0, 1))
        return b.reshape(G, R_, n)

    rep = lambda m: jnp.tile(m, (HPG, 1))
    cidx = jnp.arange(ncp)
    cend = cidx * CMP_STRIDE + (CMP_BLOCK - 1)
    rel_c = qpos[:, None] - cend[None, :]
    valid_c = rep(((rel_c >= 0) & (cidx[None, :] < nc)).astype(jnp.float32))
    cstart = cidx[:, None] * CMP_STRIDE
    sstart = jnp.arange(nsp)[None, :] * SEL_BLOCK
    overlap = ((cstart < sstart + SEL_BLOCK) & (cstart + CMP_BLOCK > sstart)
               & (cidx[:, None] < nc) & (jnp.arange(nsp)[None, :] < Ns)).astype(bf)
    o_c, msel = dec_cmp(qg, kc, vc, rows_bias(CMP_BLOCK - 1, CMP_STRIDE, ncp), valid_c, overlap, t, past, Ns)
    msel = msel[..., :Ns]
    kpos = jnp.arange(past + PAGE_SIZE)
    nblk = -(-(past + PAGE_SIZE) // SEL_BLOCK)
    mkey = jnp.repeat(jnp.pad(msel, ((0, 0), (0, 0), (0, 0), (0, nblk - Ns))), SEL_BLOCK, axis=-1)
    mkey = mkey[..., :past + PAGE_SIZE]
    mkey = mkey * ((kpos[None, :] <= qpos[:, None]) & (kpos[None, :] < L)).astype(jnp.float32)
    bias_s = rows_bias(0, 1, past + PAGE_SIZE)
    pad_new = lambda r: jnp.pad(r, ((0, 0), (0, PAGE_SIZE - t), (0, 0)))
    o_s = dec_sel(qg, slc_pool, page_table, pad_new(slc_new), bias_s[..., :past], mkey[..., :past],
                  bias_s[..., past:], mkey[..., past:], t)
    wpos = jnp.concatenate([past - Wn + jnp.arange(Wn), past + jnp.arange(PAGE_SIZE)])
    rel_w = qpos[:, None] - wpos[None, :]
    valid_w = rep(((rel_w >= 0) & (rel_w < WINDOW) & (wpos[None, :] >= 0) & (wpos[None, :] < L)).astype(jnp.float32))
    bias_w = jnp.concatenate([rows_bias(past - Wn, 1, Wn), rows_bias(past, 1, PAGE_SIZE)], axis=-1)
    o_w = dec_win(qg, win_buf.reshape(B, Wn * KVG, dh), pad_new(win_new_rows), bias_w, valid_w)
    back = lambda o: jnp.transpose(o.reshape(B, G, HPG, t, dh), (0, 3, 1, 2, 4)).reshape(B, t, G * HPG * dh)
    gx = jnp.repeat(gates.reshape(B, t, G * HPG, 3), dh, axis=2)
    return gx[..., 0] * back(o_c) + gx[..., 1] * back(o_s) + gx[..., 2] * back(o_w)


def kv_rows(h, P):
    B, T, D = h.shape
    return norm_mm(h.reshape(B * T, D), P['norm_kv'], P['kv_w'])[0].reshape(B, T, 3, ROW_W)


def run_trunk(x, gla_s0, past, P):
    B, T, _ = x.shape
    h = x
    gla_states = []
    kv5 = lambda r: r.reshape(r.shape[0], r.shape[1], 2, NSA_GROUPS, NSA_HEAD_DIM)
    for layer in range(DEPTH):
        if layer < N_A_LAYERS:
            h, s = gla_mixer(h, P['norm_mix'][layer], gla_s0[layer], P['gla_w_in'][layer],
                             P['gla_w_gate_up'][layer], P['gla_b_gate'][layer], P['gla_norm'][layer],
                             P['gla_w_out'][layer])
            gla_states.append(s)
        else:
            j = layer - N_A_LAYERS
            qd = NSA_HEADS * NSA_HEAD_DIM
            proj = norm_mm(h.reshape(B * T, -1), P['norm_mix'][layer], P['nsa_w_in'][j])[0].reshape(B, T, -1)
            q = proj[..., :qd].reshape(B, T, NSA_HEADS, NSA_HEAD_DIM) * NSA_HEAD_DIM ** -0.5
            gates = jax.nn.sigmoid(proj[..., qd:].astype(jnp.float32)).reshape(B, T, NSA_HEADS, 3).astype(x.dtype)
            if past is None:
                r5 = rows.reshape(B, T, 3, 2, NSA_GROUPS, NSA_HEAD_DIM)
                att = nsa_prompt(q, gates, kc, vc, r5[:, :, 1, 0], r5[:, :, 1, 1], r5[:, :, 2, 0], r5[:, :, 2, 1],
                                 P['rel_bias'])
            else:
                att = nsa_decode(q, gates, kc, vc, past['slc_pool'], past['page_table'], rows[:, :, 1],
                                 past['win_buf'], rows[:, :, 2], P['rel_bias'])
            h = mm_res(att, P['nsa_w_out'][j], h)
        h = peer_ffn(h, P['norm_ffn'][layer], P['peer_w_q'][layer], P['peer_subkeys'][layer],
                     P['peer_u_bf'], P['peer_v_bf'], layer)
        if layer == N_A_LAYERS - 1:
            rows = kv_rows(h, P)
            if past is None:
                assert T % (PAGE_SIZE * PPS) == 0 and T % TQ == 0
                npg = T // PAGE_SIZE
                pool = rows[:, :, 0].reshape(B * npg, PAGE_SIZE, 2, NSA_GROUPS, NSA_HEAD_DIM)
                table = jnp.arange(B * npg, dtype=jnp.int32).reshape(B, npg)
                win_new = kv5(rows[:, T - min(WINDOW, T):, 2])
            else:
                past_len = past['page_table'].shape[1] * PAGE_SIZE
                assert (past_len + T) // CMP_STRIDE == past_len // CMP_STRIDE
                pool, table = past['cmp_pool'], past['page_table']
                win_all = jnp.concatenate([past['win_buf'], kv5(rows[:, :, 2])], axis=1)
                win_new = win_all[:, win_all.shape[1] - min(WINDOW, win_all.shape[1]):]
            kc, vc = compress_paged(pool, table, *P['cmp_stack'])
    return (rmsnorm(h, P['norm_final']), jnp.stack(gla_states), kv5(rows[:, :, 0]), kv5(rows[:, :, 1]), win_new)


def kernel(x_prompt, x_sample, state_gla, cache_cmp_kv, cache_slc_kv, cache_win_kv, page_table,
           norm_mix, norm_ffn, norm_kv, norm_final, gla_w_in, gla_w_gate_up, gla_b_gate, gla_norm, gla_w_out,
           kv_w, cmp_k_w1, cmp_k_b1, cmp_k_w2, cmp_k_b2, cmp_k_pe, cmp_v_w1, cmp_v_b1, cmp_v_w2, cmp_v_b2, cmp_v_pe,
           nsa_w_in, nsa_w_out, rel_bias, peer_w_q, peer_subkeys, peer_u, peer_v):
    P = dict(norm_mix=norm_mix, norm_ffn=norm_ffn, norm_kv=norm_kv, norm_final=norm_final,
             gla_w_in=gla_w_in, gla_w_gate_up=gla_w_gate_up, gla_b_gate=gla_b_gate, gla_norm=gla_norm,
             gla_w_out=gla_w_out, kv_w=kv_w,
             cmp_k_w1=cmp_k_w1, cmp_k_b1=cmp_k_b1, cmp_k_w2=cmp_k_w2, cmp_k_b2=cmp_k_b2, cmp_k_pe=cmp_k_pe,
             cmp_v_w1=cmp_v_w1, cmp_v_b1=cmp_v_b1, cmp_v_w2=cmp_v_w2, cmp_v_b2=cmp_v_b2, cmp_v_pe=cmp_v_pe,
             nsa_w_in=nsa_w_in, nsa_w_out=nsa_w_out, rel_bias=rel_bias,
             peer_w_q=peer_w_q, peer_subkeys=peer_subkeys,
             peer_u_bf=peer_u.astype(jnp.bfloat16), peer_v_bf=peer_v.astype(jnp.bfloat16))
    P['cmp_stack'] = compress_params(P)
    past = dict(cmp_pool=cache_cmp_kv, slc_pool=cache_slc_kv, win_buf=cache_win_kv,
                page_table=page_table.astype(jnp.int32))
    gla_zero = jnp.zeros((N_A_LAYERS, BATCH, GLA_HEADS, GLA_DK, GLA_DV), x_prompt.dtype)
    y_prompt, gla_p, cmp_p, slc_p, win_p = run_trunk(x_prompt, gla_zero, None, P)
    y_sample, gla_s, cmp_s, slc_s, win_s = run_trunk(x_sample, state_gla, past, P)
    return (y_prompt, y_sample, gla_p, gla_s, cmp_p, cmp_s, slc_p, slc_s, win_p, win_s)
```

```python
import functools
import math

import jax
import jax.numpy as jnp
from jax import lax
from jax.experimental import pallas as pl
from jax.experimental.pallas import tpu as pltpu

D_MODEL = 2048
BATCH = 4
SEQ = 2048
DEPTH = 2
DEC_BATCH = 32
DEC_SEQ = 8
PAST_LEN = 8192
PAGE_SIZE = 128
N_A_LAYERS = DEPTH // 2
N_B_LAYERS = DEPTH - N_A_LAYERS
GLA_HEADS = 4
GLA_DK = D_MODEL // (2 * GLA_HEADS)
GLA_DV = D_MODEL // GLA_HEADS
GLA_GATE_RANK = 16
GLA_TAU = 16.0
GLA_CHUNK = 64
NSA_HEADS = 16
NSA_GROUPS = 4
NSA_HPG = NSA_HEADS // NSA_GROUPS
NSA_HEAD_DIM = D_MODEL // NSA_HEADS
CMP_STRIDE = 16
CMP_BLOCK = 2 * CMP_STRIDE
CMP_HIDDEN = 2 * NSA_HEAD_DIM
SEL_BLOCK = 64
N_SEL = 16
WINDOW = 512
Q_BLOCK = 32
REL_BUCKETS = 32
REL_MAX_DIST = 128
PEER_HEADS = 8
PEER_NKEYS = 128
PEER_EXPERTS = PEER_NKEYS * PEER_NKEYS
PEER_TOPK = 16
PEER_KEY_DIM = 256
PEER_TOK_BLOCK = 128
EPS = 1e-6
NEG = -1e30
BIG = 1e30

LANE = 128
VMEM_LIMIT = 56 * 1024 * 1024
TQ = 128
assert REL_MAX_DIST <= TQ and WINDOW % TQ == 0 and TQ % SEL_BLOCK == 0


def _mm_res_kernel(a_ref, b_ref, r_ref, o_ref):
    a = a_ref[...].astype(jnp.bfloat16)
    b = b_ref[...].astype(jnp.bfloat16)
    o_ref[...] = r_ref[...] + jnp.dot(a, b, preferred_element_type=jnp.float32)


def mm_res(x, w, res, tm=1024, tn=1024):
    lead = x.shape[:-1]
    K, N = w.shape
    a = x.reshape(-1, K)
    M = a.shape[0]
    tm, tn = math.gcd(M, tm), math.gcd(N, tn)
    assert tn % LANE == 0 and tm % 8 == 0
    out = pl.pallas_call(
        _mm_res_kernel,
        grid=(M // tm, N // tn),
        in_specs=[pl.BlockSpec((tm, K), lambda i, j: (i, 0)),
                  pl.BlockSpec((K, tn), lambda i, j: (0, j)),
                  pl.BlockSpec((tm, tn), lambda i, j: (i, j))],
        out_specs=pl.BlockSpec((tm, tn), lambda i, j: (i, j)),
        out_shape=jax.ShapeDtypeStruct((M, N), jnp.float32),
        compiler_params=pltpu.CompilerParams(
            dimension_semantics=("parallel", "parallel"), vmem_limit_bytes=VMEM_LIMIT),
        name="out_proj_matmul",
    )(a, w, res.reshape(M, N))
    return out.reshape(lead + (N,))


def _norm_mm_kernel(a_ref, g_ref, b_ref, o_ref, xn_ref):
    @pl.when(pl.program_id(1) == 0)
    def _():
        x = a_ref[...]
        y = x * lax.rsqrt(jnp.mean(x * x, axis=-1, keepdims=True) + EPS) * g_ref[...]
        xn_ref[...] = y.astype(jnp.bfloat16)

    o_ref[...] = jnp.dot(xn_ref[...], b_ref[...].astype(jnp.bfloat16), preferred_element_type=jnp.float32)


def norm_mm(x, gain, b, tm=1024, tn=512, keep_cols=False):
    M, K = x.shape
    N = b.shape[1]
    tm = math.gcd(M, tm)
    Np = -(-N // LANE) * LANE
    wide = [d * LANE for d in range(1, 2 * tn // LANE + 1) if Np % (d * LANE) == 0]
    if max(wide) < tn:
        Np = -(-N // tn) * tn
        wide = [tn]
    tn = max(wide)
    if Np != N:
        b = jnp.pad(b, ((0, 0), (0, Np - N)))
    out, xn = pl.pallas_call(
        _norm_mm_kernel,
        grid=(M // tm, Np // tn),
        in_specs=[pl.BlockSpec((tm, K), lambda i, j: (i, 0)),
                  pl.BlockSpec((1, K), lambda i, j: (0, 0)),
                  pl.BlockSpec((K, tn), lambda i, j: (0, j))],
        out_specs=[pl.BlockSpec((tm, tn), lambda i, j: (i, j)),
                   pl.BlockSpec((tm, K), lambda i, j: (i, 0))],
        out_shape=[jax.ShapeDtypeStruct((M, Np), jnp.float32), jax.ShapeDtypeStruct((M, K), jnp.bfloat16)],
        compiler_params=pltpu.CompilerParams(
            dimension_semantics=("parallel", "arbitrary"), vmem_limit_bytes=VMEM_LIMIT),
        name="norm_matmul",
    )(x, gain.reshape(1, K).astype(jnp.float32), b)
    return (out if keep_cols else out[:, :N]), xn


def rmsnorm(x, g):
    xf = x.astype(jnp.float32)
    y = xf * lax.rsqrt(jnp.mean(xf * xf, axis=-1, keepdims=True) + EPS)
    return (y * g.astype(jnp.float32)).astype(x.dtype)


def t5_bucket(rel):
    n = jnp.maximum(rel, 0)
    exact = REL_BUCKETS // 2
    nf = jnp.maximum(n, exact).astype(jnp.float32)
    large = exact + (jnp.log(nf / exact) / math.log(REL_MAX_DIST / exact) * (REL_BUCKETS - exact)).astype(jnp.int32)
    large = jnp.minimum(large, REL_BUCKETS - 1)
    return jnp.where(n < exact, n, large)


def head_bias(table, rel):
    onehot = jax.nn.one_hot(t5_bucket(rel), REL_BUCKETS, dtype=jnp.float32)
    b = jnp.einsum('qkn,nh->hqk', onehot, table.astype(jnp.float32), precision=lax.Precision.HIGHEST)
    return b.reshape(NSA_GROUPS, NSA_HPG, rel.shape[0], rel.shape[1])


DK_ALL = GLA_HEADS * GLA_DK
DV_ALL = GLA_HEADS * GLA_DV
GZ_COL = 2 * DK_ALL + 2 * DV_ALL
assert GZ_COL % LANE == 0 and GLA_GATE_RANK <= LANE and DK_ALL % LANE == 0 and (2 * DK_ALL) % DV_ALL == 0


def _gla_kernel(q_ref, k_ref, v_ref, r_ref, gz_ref, wg_ref, bg_ref, ng_ref, tri_ref, s0_ref,
                o_ref, sfin_ref, s_scr, *, C, t_valid):
    c = pl.program_id(1)

    @pl.when(c == 0)
    def _():
        s_scr[...] = s0_ref[0]

    row = lax.broadcasted_iota(jnp.int32, (C, C), 0)
    col = lax.broadcasted_iota(jnp.int32, (C, C), 1)
    bf = jnp.bfloat16
    z_all = jnp.dot(gz_ref[0].astype(bf), wg_ref[...], preferred_element_type=jnp.float32) + bg_ref[...]
    for h in range(GLA_HEADS):
        dk = slice(h * GLA_DK, (h + 1) * GLA_DK)
        dv = slice(h * GLA_DV, (h + 1) * GLA_DV)
        z = z_all[:, dk]
        g = (jnp.minimum(z, 0.0) - jnp.log(1.0 + jnp.exp(-jnp.abs(z)))) * (1.0 / GLA_TAU)
        if t_valid is not None:
            g = jnp.where(lax.broadcasted_iota(jnp.int32, g.shape, 0) < t_valid, g, 0.0)
        b = jnp.dot(tri_ref[...], g, preferred_element_type=jnp.float32, precision=lax.Precision.HIGHEST)
        b_last = b[C - 1:C, :]
        q = q_ref[0, :, dk] * (GLA_DK ** -0.5)
        k = k_ref[0, :, dk]
        v = v_ref[0, :, dv].astype(bf)
        qe = (q * jnp.exp(b)).astype(bf)
        ke = (k * jnp.exp(-b)).astype(bf)
        a = lax.dot_general(qe, ke, (((1,), (1,)), ((), ())), preferred_element_type=jnp.float32)
        a = jnp.where(col <= row, a, 0.0)
        s = s_scr[h]
        o = (jnp.dot(a.astype(bf), v, preferred_element_type=jnp.float32)
             + jnp.dot(qe, s.astype(bf), preferred_element_type=jnp.float32))
        kd = (k * jnp.exp(b_last - b)).astype(bf)
        dcol = jnp.exp(jnp.transpose(jnp.broadcast_to(b_last, (LANE, GLA_DK))))
        s_new = (s * jnp.concatenate([dcol] * (GLA_DV // LANE), axis=1)
                 + lax.dot_general(kd, v, (((0,), (0,)), ((), ())), preferred_element_type=jnp.float32))
        s_scr[h] = s_new
        o = o * lax.rsqrt(jnp.mean(o * o, axis=-1, keepdims=True) + EPS) * ng_ref[...]
        r = r_ref[0, :, dv]
        o_ref[0, :, dv] = o * (r * (1.0 / (1.0 + jnp.exp(-r))))

    @pl.when(c == pl.num_programs(1) - 1)
    def _():
        sfin_ref[0] = s_scr[...]


def gla_core(proj, s0, w_gate_up, b_gate, norm_g, T):
    B, Tp, _ = proj.shape
    C = GLA_CHUNK if T % GLA_CHUNK == 0 else Tp
    assert Tp % C == 0 and C % 16 == 0 and (Tp == T or Tp == C)
    n = Tp // C
    wg = jnp.pad(w_gate_up, ((0, LANE - GLA_GATE_RANK), (0, 0))).astype(jnp.bfloat16)
    tri = (jnp.arange(C)[:, None] >= jnp.arange(C)[None, :]).astype(jnp.float32)
    col_blk = lambda w, i: pl.BlockSpec((1, C, w), lambda b, c: (b, c, i))
    full = lambda a: pl.BlockSpec(a.shape, lambda b, c: (0,) * a.ndim)
    bg = b_gate.reshape(1, DK_ALL)
    ng = norm_g.reshape(1, GLA_DV)
    st_spec = pl.BlockSpec((1, GLA_HEADS, GLA_DK, GLA_DV), lambda b, c: (b, 0, 0, 0))
    return pl.pallas_call(
        functools.partial(_gla_kernel, C=C, t_valid=None if Tp == T else T),
        grid=(B, n),
        in_specs=[col_blk(DK_ALL, 0), col_blk(DK_ALL, 1), col_blk(DV_ALL, 2 * DK_ALL // DV_ALL),
                  col_blk(DV_ALL, (2 * DK_ALL + DV_ALL) // DV_ALL), col_blk(LANE, GZ_COL // LANE),
                  full(wg), full(bg), full(ng), full(tri), st_spec],
        out_specs=[pl.BlockSpec((1, C, DV_ALL), lambda b, c: (b, c, 0)), st_spec],
        out_shape=[jax.ShapeDtypeStruct((B, Tp, DV_ALL), jnp.float32),
                   jax.ShapeDtypeStruct((B, GLA_HEADS, GLA_DK, GLA_DV), jnp.float32)],
        scratch_shapes=[pltpu.VMEM((GLA_HEADS, GLA_DK, GLA_DV), jnp.float32)],
        compiler_params=pltpu.CompilerParams(dimension_semantics=("parallel", "arbitrary"),
                                             vmem_limit_bytes=VMEM_LIMIT),
        name="gla_core",
    )(proj, proj, proj, proj, proj, wg, bg, ng, tri, s0)


def gla_mixer(h, gain, s0, w_in, w_gate_up, b_gate, norm_g, w_out):
    B, T, D = h.shape
    proj, _ = norm_mm(h.reshape(B * T, D), gain, w_in, keep_cols=True)
    assert proj.shape[1] >= GZ_COL + LANE
    Tp = -(-T // 16) * 16
    proj = jnp.pad(proj.reshape(B, T, -1), ((0, 0), (0, Tp - T), (0, 0)))
    o, s_new = gla_core(proj, s0.astype(jnp.float32), w_gate_up, b_gate, norm_g, T)
    return mm_res(o[:, :T], w_out, h), s_new.astype(s0.dtype)


WSUM_UNROLL = 16
PEER_CHUNK = 1024


def _wsum_kernel(a_ref, b_ref, g_ref, o_ref, *, tb):
    sub = lax.broadcasted_iota(jnp.int32, (PEER_NKEYS, PEER_NKEYS), 0)

    def body(tt, c):
        t0 = pl.multiple_of(tt * WSUM_UNROLL, WSUM_UNROLL)
        a8 = a_ref[pl.ds(t0, WSUM_UNROLL), :]
        b8 = b_ref[pl.ds(t0, WSUM_UNROLL), :]
        g8 = g_ref[pl.ds(t0, WSUM_UNROLL), :]
        for u in range(WSUM_UNROLL):
            at = jnp.where(sub == a8[u:u + 1], 1.0, 0.0).astype(jnp.bfloat16)
            bt = jnp.where(sub == b8[u:u + 1], g8[u:u + 1], 0.0).astype(jnp.bfloat16)
            o_ref[t0 + u] = lax.dot_general(at, bt, (((1,), (1,)), ((), ())), preferred_element_type=jnp.float32)
        return c

    lax.fori_loop(0, tb // WSUM_UNROLL, body, 0)


def peer_wsum(i1, i2, gate, tb=64):
    n, K = i1.shape
    tb = math.gcd(n, tb)
    assert tb % WSUM_UNROLL == 0
    return pl.pallas_call(
        functools.partial(_wsum_kernel, tb=tb),
        grid=(n // tb,),
        in_specs=[pl.BlockSpec((tb, K), lambda i: (i, 0))] * 3,
        out_specs=pl.BlockSpec((tb, PEER_NKEYS, PEER_NKEYS), lambda i: (i, 0, 0)),
        out_shape=jax.ShapeDtypeStruct((n, PEER_NKEYS, PEER_NKEYS), jnp.float32),
        compiler_params=pltpu.CompilerParams(dimension_semantics=("parallel",), vmem_limit_bytes=VMEM_LIMIT),
        name="peer_wsum",
    )(i1, i2, gate)


def _peer_kernel(x_ref, u_ref, v_ref, w_ref, o_ref, *, te):
    j = pl.program_id(1)
    x = x_ref[...]
    acc = None
    for c in range(te // PEER_CHUNK):
        e0 = c * PEER_CHUNK
        hid = lax.dot_general(x, u_ref[e0:e0 + PEER_CHUNK, :], (((1,), (1,)), ((), ())),
                              preferred_element_type=jnp.float32)
        parts = []
        for r in range(PEER_CHUNK // PEER_NKEYS):
            hr = hid[:, r * PEER_NKEYS:(r + 1) * PEER_NKEYS]
            ar = 0.5 * hr * (1.0 + lax.erf(hr * (2.0 ** -0.5))) * w_ref[:, e0 // PEER_NKEYS + r, :]
            parts.append(ar.astype(jnp.bfloat16))
        d = jnp.dot(jnp.concatenate(parts, axis=1), v_ref[e0:e0 + PEER_CHUNK, :], preferred_element_type=jnp.float32)
        acc = d if acc is None else acc + d
    @pl.when(j == 0)
    def _():
        o_ref[...] = acc

    @pl.when(j > 0)
    def _():
        o_ref[...] += acc


def peer_dense(xn, u, v, layer, wsum, tb=1024, te=1024):
    n, D = xn.shape
    E = u.shape[1]
    tb = math.gcd(n, tb)
    return pl.pallas_call(
        functools.partial(_peer_kernel, te=te),
        grid=(n // tb, E // te),
        in_specs=[pl.BlockSpec((tb, D), lambda i, j: (i, 0), pipeline_mode=pl.Buffered(1)),
                  pl.BlockSpec((None, te, D), lambda i, j: (layer, j, 0)),
                  pl.BlockSpec((None, te, D), lambda i, j: (layer, j, 0)),
                  pl.BlockSpec((tb, te // PEER_NKEYS, PEER_NKEYS), lambda i, j: (i, j, 0))],
        out_specs=pl.BlockSpec((tb, D), lambda i, j: (i, 0), pipeline_mode=pl.Buffered(1)),
        out_shape=jax.ShapeDtypeStruct((n, D), jnp.float32),
        compiler_params=pltpu.CompilerParams(dimension_semantics=("parallel", "arbitrary"),
                                             vmem_limit_bytes=VMEM_LIMIT),
        name="peer_dense",
    )(xn, u, v, wsum)


RT = LANE
NHC = 2 * PEER_HEADS
ROUTE_CHAINS = 4
_CAND_GROUPS = [(0, 0), (0, 8), (1, 0)] + [(a, 0) for a in range(2, 8)] + [(-1, 0)]
assert PEER_TOPK == 16 and PEER_KEY_DIM // 2 == LANE and PEER_NKEYS == LANE


def _top_rounds(s, n_rounds, extra=()):
    R = s.shape[0]
    iota = lax.broadcasted_iota(jnp.int32, s.shape, 0)
    vals, idxs, ex = [], [], [[] for _ in extra]
    for _ in range(n_rounds):
        m = jnp.max(s, axis=0, keepdims=True)
        idx = jnp.min(jnp.where(s == m, iota, R), axis=0, keepdims=True)
        sel = iota == idx
        for e, lst in zip(extra, ex):
            lst.append(jnp.max(jnp.where(sel, e, -1), axis=0, keepdims=True))
        s = jnp.where(sel, -jnp.inf, s)
        vals.append(m)
        idxs.append(idx)
    cat = lambda l: jnp.concatenate(l, axis=0)
    return cat(vals), cat(idxs), [cat(l) for l in ex]


def _route_kernel(q_ref, sub_ref, e1_ref, e2_ref, g_ref, v_scr, i_scr):
    K = PEER_TOPK

    def stage1(hp, c):
        for u in range(ROUTE_CHAINS):
            hc = ROUTE_CHAINS * hp + u
            off = pl.multiple_of(hc * LANE, LANE)
            qb = q_ref[:, pl.ds(off, LANE)].astype(jnp.bfloat16)
            s = lax.dot_general(sub_ref[hc], qb, (((1,), (1,)), ((), ())), preferred_element_type=jnp.float32)
            v, i, _ = _top_rounds(s, K)
            v_scr[hc] = v
            i_scr[hc] = i
        return c

    lax.fori_loop(0, NHC // ROUTE_CHAINS, stage1, 0)

    row8 = lax.broadcasted_iota(jnp.int32, (8, RT), 0)

    def stage2(h):
        v1, v2 = v_scr[2 * h], v_scr[2 * h + 1]
        i1, i2 = i_scr[2 * h], i_scr[2 * h + 1]
        cand, c1, c2 = [], [], []
        for a, b0 in _CAND_GROUPS:
            if a >= 0:
                nb = K // (a + 1)
                sm = v1[a:a + 1] + v2[b0:b0 + 8]
                if nb - b0 < 8:
                    sm = jnp.where(row8 < nb - b0, sm, -jnp.inf)
                cand.append(sm)
                c1.append(jnp.broadcast_to(i1[a:a + 1], (8, RT)))
                c2.append(i2[b0:b0 + 8])
            else:
                cand.append(v1[8:16] + v2[0:1])
                c1.append(i1[8:16])
                c2.append(jnp.broadcast_to(i2[0:1], (8, RT)))
        cat = lambda l: jnp.concatenate(l, axis=0)
        top, _, (e12,) = _top_rounds(cat(cand), K, extra=(cat(c1) * PEER_NKEYS + cat(c2),))
        ex = jnp.exp(top - top[0:1])
        g = ex / jnp.sum(ex, axis=0, keepdims=True)
        r0 = pl.multiple_of(h * K, K)
        e1_ref[0, pl.ds(r0, K), :] = e12 // PEER_NKEYS
        e2_ref[0, pl.ds(r0, K), :] = e12 % PEER_NKEYS
        g_ref[0, pl.ds(r0, K), :] = g

    def stage2_pair(hp, c):
        stage2(2 * hp)
        stage2(2 * hp + 1)
        return c

    lax.fori_loop(0, PEER_HEADS // 2, stage2_pair, 0)


def peer_route(q, sub_bf):
    n = q.shape[0]
    assert n % RT == 0
    nb = n // RT
    slots = PEER_HEADS * PEER_TOPK
    out = jax.ShapeDtypeStruct((nb, slots, RT), jnp.int32)
    ospec = pl.BlockSpec((1, slots, RT), lambda i: (i, 0, 0))
    return pl.pallas_call(
        _route_kernel,
        grid=(nb,),
        in_specs=[pl.BlockSpec((RT, q.shape[1]), lambda i: (i, 0)),
                  pl.BlockSpec(sub_bf.shape, lambda i: (0, 0, 0))],
        out_specs=[ospec, ospec, ospec],
        out_shape=[out, out, jax.ShapeDtypeStruct((nb, slots, RT), jnp.float32)],
        scratch_shapes=[pltpu.VMEM((NHC, PEER_TOPK, RT), jnp.float32),
                        pltpu.VMEM((NHC, PEER_TOPK, RT), jnp.int32)],
        compiler_params=pltpu.CompilerParams(dimension_semantics=("parallel",), vmem_limit_bytes=VMEM_LIMIT),
        name="peer_route",
    )(q, sub_bf)


def peer_ffn(h, gain, w_q, subkeys, u_bf, v_bf, layer):
    B, T, D = h.shape
    n = B * T
    q, xt = norm_mm(h.reshape(n, D), gain, w_q)
    sub_bf = subkeys.astype(jnp.bfloat16).reshape(NHC, PEER_NKEYS, PEER_KEY_DIM // 2)
    e1, e2, gate = peer_route(q, sub_bf)
    tok_major = lambda t: jnp.transpose(t, (0, 2, 1)).reshape(n, PEER_HEADS * PEER_TOPK)
    wsum = peer_wsum(tok_major(e1), tok_major(e2), tok_major(gate))
    out = peer_dense(xt, u_bf, v_bf, layer, wsum)
    return h + out.reshape(B, T, D)


KVG = 2 * NSA_GROUPS
ROW_W = KVG * NSA_HEAD_DIM
CPP = PAGE_SIZE // CMP_STRIDE
PPS = 16
assert CPP == 8 and PAGE_SIZE == LANE and NSA_HEAD_DIM == LANE


def _compress_kernel(pt_ref, *refs, n_steps):
    pages = refs[:PPS]
    w1_ref, c1_ref, w2_ref, b2_ref, kc_ref, vc_ref, xc_ref = refs[PPS:]
    s_idx = pl.program_id(1)
    for pair in range(PPS // 2):
        row0 = pl.multiple_of((s_idx * (PPS // 2) + pair) * 2 * CPP, 2 * CPP)
        for kvg in range(KVG):
            for s in range(CMP_STRIDE):
                lo = pages[2 * pair][0, pl.ds(s * KVG + kvg, CPP, stride=KVG * CMP_STRIDE), :]
                hi = pages[2 * pair + 1][0, pl.ds(s * KVG + kvg, CPP, stride=KVG * CMP_STRIDE), :]
                xc_ref[kvg, pl.ds(row0, 2 * CPP), pl.ds(s * LANE, LANE)] = (
                    jnp.concatenate([lo, hi], axis=0).astype(jnp.bfloat16))

    @pl.when(s_idx == n_steps - 1)
    def _():
        nch = xc_ref.shape[1]
        for kv in range(2):
            out_ref = kc_ref if kv == 0 else vc_ref
            for g in range(NSA_GROUPS):
                hh = jnp.dot(xc_ref[kv * NSA_GROUPS + g], w1_ref[kv], preferred_element_type=jnp.float32)
                h1 = hh[:, :CMP_HIDDEN]
                h2 = pltpu.roll(hh[:, CMP_HIDDEN:], nch - 1, 0)
                x = h1 + h2 + c1_ref[kv]
                hid = 0.5 * x * (1.0 + lax.erf(x * (2.0 ** -0.5)))
                o = jnp.dot(hid.astype(jnp.bfloat16), w2_ref[kv], preferred_element_type=jnp.float32) + b2_ref[kv]
                out_ref[0, g] = o.astype(out_ref.dtype)


def compress_paged(pool, page_table, w1, c1, w2, b2):
    B, NP = page_table.shape
    assert NP % PPS == 0
    n_steps = NP // PPS
    nch = NP * CPP
    dh = NSA_HEAD_DIM
    pool = pool.reshape(pool.shape[0], PAGE_SIZE * KVG, dh)

    def page_spec(u):
        return pl.BlockSpec((1, PAGE_SIZE * KVG, dh), lambda b, s, pt: (pt[b, s * PPS + u], 0, 0))

    full = lambda a: pl.BlockSpec(a.shape, lambda b, s, pt: (0,) * a.ndim)
    out_spec = pl.BlockSpec((1, NSA_GROUPS, nch, dh), lambda b, s, pt: (b, 0, 0, 0))
    grid_spec = pltpu.PrefetchScalarGridSpec(
        num_scalar_prefetch=1,
        grid=(B, n_steps),
        in_specs=[page_spec(u) for u in range(PPS)] + [full(w1), full(c1), full(w2), full(b2)],
        out_specs=[out_spec, out_spec],
        scratch_shapes=[pltpu.VMEM((KVG, nch, CMP_STRIDE * dh), jnp.bfloat16)],
    )
    out = jax.ShapeDtypeStruct((B, NSA_GROUPS, nch, dh), jnp.bfloat16)
    return pl.pallas_call(
        functools.partial(_compress_kernel, n_steps=n_steps),
        grid_spec=grid_spec,
        out_shape=[out, out],
        compiler_params=pltpu.CompilerParams(dimension_semantics=("parallel", "arbitrary"),
                                             vmem_limit_bytes=VMEM_LIMIT),
        name="nsa_compress",
    )(page_table, *([pool] * PPS), w1, c1, w2, b2)


def compress_params(P):
    dh, F = NSA_HEAD_DIM, CMP_HIDDEN
    w1s, c1s, w2s, b2s = [], [], [], []
    for n in ('k', 'v'):
        w1h = P['cmp_%s_w1' % n].reshape(2, CMP_STRIDE * dh, F)
        pe = P['cmp_%s_pe' % n].reshape(2, CMP_STRIDE * dh)
        w1s.append(jnp.concatenate([w1h[0], w1h[1]], axis=1))
        c1s.append(P['cmp_%s_b1' % n] + jnp.einsum('hk,hkf->f', pe, w1h, precision=lax.Precision.HIGHEST))
        w2s.append(P['cmp_%s_w2' % n])
        b2s.append(P['cmp_%s_b2' % n])
    return (jnp.stack(w1s).astype(jnp.bfloat16), jnp.stack(c1s)[:, None, :],
            jnp.stack(w2s).astype(jnp.bfloat16), jnp.stack(b2s)[:, None, :])


def _softmax_rows(s, valid):
    s = jnp.where(valid, s, NEG)
    m = jnp.max(s, axis=-1, keepdims=True)
    e = jnp.where(valid, jnp.exp(s - m), 0.0)
    return e / jnp.maximum(jnp.sum(e, axis=-1, keepdims=True), 1e-30)


HQ = NSA_HPG * TQ
GATE_ROWS = 8


def _lanes4(x):
    return jnp.concatenate([x] * NSA_HPG, axis=1)


def _nsa_cmp_sel_kernel(q_ref, kc_ref, vct_ref, bias_ref, ovt_ref, oct_ref, msel_ref, *, nc, ns):
    i = pl.program_id(2)
    ncp = kc_ref.shape[2]
    nsp = ovt_ref.shape[0]
    q_all = q_ref[0, 0].reshape(HQ, NSA_HEAD_DIM)
    cidx = lax.broadcasted_iota(jnp.int32, (ncp, TQ), 0)
    qpos = i * TQ + lax.broadcasted_iota(jnp.int32, (ncp, TQ), 1)
    valid = _lanes4((cidx * CMP_STRIDE + (CMP_BLOCK - 1) <= qpos) & (cidx < nc))
    s = lax.dot_general(kc_ref[0, 0], q_all, (((1,), (1,)), ((), ())), preferred_element_type=jnp.float32)
    s = jnp.where(valid, s + bias_ref[0, 0], NEG)
    m = jnp.max(s, axis=0, keepdims=True)
    e = jnp.where(valid, jnp.exp(s - m), 0.0)
    p = e / jnp.maximum(jnp.sum(e, axis=0, keepdims=True), 1e-30)
    oct_ref[0, 0, 0] = jnp.dot(vct_ref[0, 0], p.astype(jnp.bfloat16), preferred_element_type=jnp.float32)
    psum = p[:, 0:TQ]
    for hh in range(1, NSA_HPG):
        psum = psum + p[:, hh * TQ:(hh + 1) * TQ]
    imp = jnp.dot(ovt_ref[...], psum.astype(jnp.bfloat16), preferred_element_type=jnp.float32)
    blk = lax.broadcasted_iota(jnp.int32, (nsp, TQ), 0)
    qp = i * TQ + lax.broadcasted_iota(jnp.int32, (nsp, TQ), 1)
    cur = qp // SEL_BLOCK
    ok_blk = (blk * SEL_BLOCK <= qp) & (blk < ns)
    forced = (blk == 0) | (blk == cur) | (blk == cur - 1)
    score = jnp.where(ok_blk, jnp.where(forced, BIG, imp), -BIG)
    score = jnp.where(blk < ns, score, -jnp.inf)
    msel = jnp.zeros((nsp, TQ), jnp.float32)
    for _ in range(min(N_SEL, ns)):
        mx = jnp.max(score, axis=0, keepdims=True)
        idx = jnp.min(jnp.where(score == mx, blk, nsp), axis=0, keepdims=True)
        sel = blk == idx
        msel = jnp.where(sel & (mx > -0.5 * BIG), 1.0, msel)
        score = jnp.where(sel, -jnp.inf, score)
    msel_ref[0, 0] = msel


def nsa_cmp_sel(qh, kc, vct, bias_ct, ovt, nc, ns):
    B, G, HPG, T, dh = qh.shape
    ncp = kc.shape[2]
    nsp = ovt.shape[0]
    nqt = T // TQ
    return pl.pallas_call(
        functools.partial(_nsa_cmp_sel_kernel, nc=nc, ns=ns),
        grid=(B, G, nqt),
        in_specs=[pl.BlockSpec((1, 1, HPG, TQ, dh), lambda b, g, i: (b, g, 0, i, 0)),
                  pl.BlockSpec((1, 1, ncp, dh), lambda b, g, i: (b, g, 0, 0)),
                  pl.BlockSpec((1, 1, dh, ncp), lambda b, g, i: (b, g, 0, 0)),
                  pl.BlockSpec((1, 1, ncp, HQ), lambda b, g, i: (g, i, 0, 0)),
                  pl.BlockSpec((nsp, ncp), lambda b, g, i: (0, 0))],
        out_specs=[pl.BlockSpec((1, 1, 1, dh, HQ), lambda b, g, i: (b, g, i, 0, 0)),
                   pl.BlockSpec((1, 1, nsp, TQ), lambda b, g, i: (b, g, 0, i))],
        out_shape=[jax.ShapeDtypeStruct((B, G, nqt, dh, HQ), jnp.float32),
                   jax.ShapeDtypeStruct((B, G, nsp, T), jnp.float32)],
        compiler_params=pltpu.CompilerParams(dimension_semantics=("parallel", "parallel", "parallel"),
                                             vmem_limit_bytes=VMEM_LIMIT),
        name="nsa_cmp_sel",
    )(qh, kc, vct, bias_ct, ovt)


def _flash_tile_t(q_all, k, vt, bias, valid, m_ref, l_ref, acc_ref):
    s = lax.dot_general(k, q_all, (((1,), (1,)), ((), ())), preferred_element_type=jnp.float32)
    s = jnp.where(valid, s + bias, NEG)
    m_old = m_ref[...]
    m_new = jnp.maximum(m_old, jnp.max(s, axis=0, keepdims=True))
    p = jnp.where(valid, jnp.exp(s - m_new), 0.0)
    alpha = jnp.exp(m_old - m_new)
    l_ref[...] = alpha * l_ref[...] + jnp.sum(p, axis=0, keepdims=True)
    acc_ref[...] = alpha * acc_ref[...] + jnp.dot(vt, p.astype(jnp.bfloat16), preferred_element_type=jnp.float32)
    m_ref[...] = m_new


def _nsa_sw_kernel(q_ref, ks_ref, vst_ref, kw_ref, vwt_ref, msel_ref, expt_ref, bias_ref, selm_ref, winm_ref,
                   oct_ref, gate_ref, o_ref, m_ref, l_ref, acc_ref):
    i = pl.program_id(2)
    q_all = q_ref[0, 0].reshape(HQ, NSA_HEAD_DIM)
    msel = msel_ref[0, 0].astype(jnp.bfloat16)

    def reset():
        m_ref[...] = jnp.full_like(m_ref, NEG)
        l_ref[...] = jnp.zeros_like(l_ref)
        acc_ref[...] = jnp.zeros_like(acc_ref)

    def result():
        return acc_ref[...] / jnp.maximum(l_ref[...], 1e-30)

    def pair(k_ref, vt_ref, j1, dd1, valid_of):
        j0 = j1 - 1
        j0c = jnp.maximum(j0, 0)
        k2 = jnp.concatenate([k_ref[0, 0, j1], k_ref[0, 0, j0c]], axis=0)
        vt2 = jnp.concatenate([vt_ref[0, 0, j1], vt_ref[0, 0, j0c]], axis=1)
        bias2 = jnp.concatenate([bias_ref[0, jnp.minimum(dd1, 2)], bias_ref[0, jnp.minimum(dd1 + 1, 2)]], axis=0)
        valid2 = _lanes4(jnp.concatenate([valid_of(j1, dd1), valid_of(j0c, dd1 + 1) & (j0 >= 0)], axis=0))
        _flash_tile_t(q_all, k2, vt2, bias2, valid2, m_ref, l_ref, acc_ref)

    reset()

    def sel_valid(j, dd):
        keys = jnp.dot(expt_ref[j], msel, preferred_element_type=jnp.float32)
        return (keys * selm_ref[jnp.minimum(dd, 2)]) > 0.5

    def sel_body(jj, c):
        pair(ks_ref, vst_ref, i - 2 * jj, 2 * jj, sel_valid)
        return c

    lax.fori_loop(0, i // 2 + 1, sel_body, 0)
    mix = gate_ref[0, 0, 0, 0:1, :] * oct_ref[0, 0, 0] + gate_ref[0, 0, 0, 1:2, :] * result()
    reset()
    for pp in range(WINDOW // TQ // 2 + 1):
        j1 = i - 2 * pp
        pair(kw_ref, vwt_ref, jnp.maximum(j1, 0), 2 * pp, lambda j, dd: (winm_ref[dd] > 0.5) & (j1 >= 0))
    mix = mix + gate_ref[0, 0, 0, 2:3, :] * result()
    for hh in range(NSA_HPG):
        o_ref[0, :, hh * NSA_HEAD_DIM:(hh + 1) * NSA_HEAD_DIM] = jnp.transpose(mix[:, hh * TQ:(hh + 1) * TQ])


def nsa_sel_win(qh, ks, vst, kw, vwt, msel_t, exp_t, bias_tt, selm_t, winm_t, oct, gates_t):
    B, G, HPG, T, dh = qh.shape
    nt = T // TQ
    nsp = msel_t.shape[2]
    k_spec = pl.BlockSpec((1, 1, nt, TQ, dh), lambda b, g, i: (b, g, 0, 0, 0))
    vt_spec = pl.BlockSpec((1, 1, nt, dh, TQ), lambda b, g, i: (b, g, 0, 0, 0))
    return pl.pallas_call(
        _nsa_sw_kernel,
        grid=(B, G, nt),
        in_specs=[pl.BlockSpec((1, 1, HPG, TQ, dh), lambda b, g, i: (b, g, 0, i, 0)),
                  k_spec, vt_spec, k_spec, vt_spec,
                  pl.BlockSpec((1, 1, nsp, TQ), lambda b, g, i: (b, g, 0, i)),
                  pl.BlockSpec((nt, TQ, nsp), lambda b, g, i: (0, 0, 0)),
                  pl.BlockSpec((1, 3, TQ, HQ), lambda b, g, i: (g, 0, 0, 0)),
                  pl.BlockSpec((3, TQ, TQ), lambda b, g, i: (0, 0, 0)),
                  pl.BlockSpec(winm_t.shape, lambda b, g, i: (0, 0, 0)),
                  pl.BlockSpec((1, 1, 1, dh, HQ), lambda b, g, i: (b, g, i, 0, 0)),
                  pl.BlockSpec((1, 1, 1, GATE_ROWS, HQ), lambda b, g, i: (b, g, i, 0, 0))],
        out_specs=pl.BlockSpec((1, TQ, HPG * dh), lambda b, g, i: (b, i, g)),
        out_shape=jax.ShapeDtypeStruct((B, T, G * HPG * dh), jnp.float32),
        scratch_shapes=[pltpu.VMEM((1, HQ), jnp.float32),
                        pltpu.VMEM((1, HQ), jnp.float32),
                        pltpu.VMEM((dh, HQ), jnp.float32)],
        compiler_params=pltpu.CompilerParams(dimension_semantics=("parallel", "parallel", "arbitrary"),
                                             vmem_limit_bytes=VMEM_LIMIT),
        name="nsa_sel_win",
    )(qh, ks, vst, kw, vwt, msel_t, exp_t, bias_tt, selm_t, winm_t, oct, gates_t)


def nsa_prompt(q, gates, kc, vc, ks, vs, kw_rows, vw_rows, table):
    B, T = q.shape[0], q.shape[1]
    G, HPG, dh = NSA_GROUPS, NSA_HPG, NSA_HEAD_DIM
    Nc = T // CMP_STRIDE - 1
    Ns = -(-T // SEL_BLOCK)
    ncp = kc.shape[2]
    assert ncp % LANE == 0 and ncp >= Nc
    nsp = -(-Ns // 16) * 16
    nqt = T // TQ
    bf = jnp.bfloat16
    qh = jnp.transpose(q.reshape(B, T, G, HPG, dh), (0, 2, 3, 1, 4)).astype(bf)
    qpos = jnp.arange(T)
    cidx = jnp.arange(ncp)
    bias_c = head_bias(table, qpos[:, None] - (cidx * CMP_STRIDE + (CMP_BLOCK - 1))[None, :])
    bias_ct = jnp.transpose(bias_c.reshape(G, HPG, nqt, TQ, ncp), (0, 2, 4, 1, 3)).reshape(G, nqt, ncp, HQ)
    sidx = jnp.arange(nsp)
    ovt = (((cidx * CMP_STRIDE)[None, :] < (sidx * SEL_BLOCK + SEL_BLOCK)[:, None])
           & ((cidx * CMP_STRIDE + CMP_BLOCK)[None, :] > (sidx * SEL_BLOCK)[:, None])
           & (cidx[None, :] < Nc) & (sidx[:, None] < Ns)).astype(bf)
    oct, msel_t = nsa_cmp_sel(qh, kc, jnp.transpose(vc, (0, 1, 3, 2)), bias_ct, ovt, Nc, Ns)
    exp_t = ((jnp.arange(T) // SEL_BLOCK)[:, None] == sidx[None, :]).astype(bf).reshape(nqt, TQ, nsp)
    r = jnp.arange(TQ)
    rel3 = (jnp.arange(3) * TQ)[:, None, None] + r[None, None, :] - r[None, :, None]
    bias_tt = jnp.einsum('dkqn,nh->hdkq', jax.nn.one_hot(t5_bucket(rel3), REL_BUCKETS, dtype=jnp.float32),
                         table.astype(jnp.float32), precision=lax.Precision.HIGHEST)
    bias_tt = jnp.transpose(bias_tt.reshape(G, HPG, 3, TQ, TQ), (0, 2, 3, 1, 4)).reshape(G, 3, TQ, HQ)
    selm_t = (rel3 >= 0).astype(jnp.float32)
    nw = 2 * (WINDOW // TQ // 2 + 1)
    relw = (jnp.arange(nw) * TQ)[:, None, None] + r[None, None, :] - r[None, :, None]
    winm_t = ((relw >= 0) & (relw < WINDOW)).astype(jnp.float32)

    def k_tiles(t):
        return jnp.transpose(t, (0, 2, 1, 3)).astype(bf).reshape(B, G, nqt, TQ, dh)

    def vt_tiles(t):
        return jnp.transpose(t.astype(bf).reshape(B, nqt, TQ, G, dh), (0, 3, 1, 4, 2))

    gates_t = jnp.transpose(gates.reshape(B, nqt, TQ, G, HPG, 3), (0, 3, 1, 5, 4, 2)).reshape(B, G, nqt, 3, HQ)
    gates_t = jnp.pad(gates_t, ((0, 0), (0, 0), (0, 0), (0, GATE_ROWS - 3), (0, 0)))
    return nsa_sel_win(qh, k_tiles(ks), vt_tiles(vs), k_tiles(kw_rows), vt_tiles(vw_rows), msel_t, exp_t, bias_tt,
                       selm_t, winm_t, oct, gates_t)


def _dec_cmp_kernel(q_ref, kc_ref, vc_ref, bias_ref, valid_ref, ov_ref, oc_ref, msel_ref, *, t, q0, ns):
    valid = valid_ref[...] > 0.5
    nsp = ov_ref.shape[1]
    imps = []
    for g in range(NSA_GROUPS):
        s = lax.dot_general(q_ref[0, g], kc_ref[0, g], (((1,), (1,)), ((), ())), preferred_element_type=jnp.float32)
        p = _softmax_rows(s + bias_ref[g], valid)
        oc_ref[0, g] = jnp.dot(p.astype(jnp.bfloat16), vc_ref[0, g], preferred_element_type=jnp.float32)
        psum = p[0:t]
        for hh in range(1, NSA_HPG):
            psum = psum + p[hh * t:(hh + 1) * t]
        imps.append(jnp.dot(psum.astype(jnp.bfloat16), ov_ref[...], preferred_element_type=jnp.float32))
    imp = jnp.concatenate(imps, axis=0)
    blk = lax.broadcasted_iota(jnp.int32, (NSA_GROUPS * t, nsp), 1)
    qp = q0 + jnp.concatenate([lax.broadcasted_iota(jnp.int32, (t, nsp), 0)] * NSA_GROUPS, axis=0)
    cur = qp // SEL_BLOCK
    ok_blk = (blk * SEL_BLOCK <= qp) & (blk < ns)
    forced = (blk == 0) | (blk == cur) | (blk == cur - 1)
    score = jnp.where(ok_blk, jnp.where(forced, BIG, imp), -BIG)
    score = jnp.where(blk < ns, score, -jnp.inf)
    msel = jnp.zeros((NSA_GROUPS * t, nsp), jnp.float32)
    for _ in range(min(N_SEL, ns)):
        mx = jnp.max(score, axis=1, keepdims=True)
        idx = jnp.min(jnp.where(score == mx, blk, nsp), axis=1, keepdims=True)
        sel = blk == idx
        msel = jnp.where(sel & (mx > -0.5 * BIG), 1.0, msel)
        score = jnp.where(sel, -jnp.inf, score)
    for g in range(NSA_GROUPS):
        msel_ref[0, g] = msel[g * t:(g + 1) * t]


def dec_cmp(qg, kc, vc, bias_c, valid_c, overlap, t, q0, ns):
    B, G, R_, dh = qg.shape
    ncp, nsp = overlap.shape
    per_b = lambda a: pl.BlockSpec((1,) + a.shape[1:], lambda b: (b,) + (0,) * (a.ndim - 1))
    full = lambda a: pl.BlockSpec(a.shape, lambda b: (0,) * a.ndim)
    return pl.pallas_call(
        functools.partial(_dec_cmp_kernel, t=t, q0=q0, ns=ns),
        grid=(B,),
        in_specs=[per_b(qg), per_b(kc), per_b(vc), full(bias_c), full(valid_c), full(overlap)],
        out_specs=[pl.BlockSpec((1, G, R_, dh), lambda b: (b, 0, 0, 0)),
                   pl.BlockSpec((1, G, t, nsp), lambda b: (b, 0, 0, 0))],
        out_shape=[jax.ShapeDtypeStruct((B, G, R_, dh), jnp.float32),
                   jax.ShapeDtypeStruct((B, G, t, nsp), jnp.float32)],
        compiler_params=pltpu.CompilerParams(dimension_semantics=("parallel",), vmem_limit_bytes=VMEM_LIMIT),
        name="nsa_dec_cmp",
    )(qg, kc, vc, bias_c, valid_c, overlap)


def _flash_update(s, valid, v, m_ref, l_ref, acc_ref, g):
    s = jnp.where(valid, s, NEG)
    m_old = m_ref[g]
    m_new = jnp.maximum(m_old, jnp.max(s, axis=-1, keepdims=True))
    p = jnp.where(valid, jnp.exp(s - m_new), 0.0)
    alpha = jnp.exp(m_old - m_new)
    l_ref[g] = alpha * l_ref[g] + jnp.sum(p, axis=-1, keepdims=True)
    acc_ref[g] = alpha * acc_ref[g] + jnp.dot(p.astype(jnp.bfloat16), v, preferred_element_type=jnp.float32)
    m_ref[g] = m_new


def _dec_sel_kernel(pt_ref, q_ref, *refs, n_steps, t):
    pages = refs[:PPS]
    new_ref, bias_ref, mask_ref, biasn_ref, maskn_ref, o_ref, m_ref, l_ref, acc_ref = refs[PPS:]
    s_idx = pl.program_id(1)
    dh = NSA_HEAD_DIM

    @pl.when(s_idx == 0)
    def _():
        m_ref[...] = jnp.full_like(m_ref, NEG)
        l_ref[...] = jnp.zeros_like(l_ref)
        acc_ref[...] = jnp.zeros_like(acc_ref)

    def tile_mask(mref, g):
        mk = mref[0, g] > 0.5
        return jnp.concatenate([mk] * NSA_HPG, axis=0)

    for g in range(NSA_GROUPS):
        rows_of = lambda pg, kvg: pg[0, pl.ds(kvg, PAGE_SIZE, stride=KVG), :]
        k = jnp.concatenate([rows_of(pg, g) for pg in pages], axis=0).astype(jnp.bfloat16)
        v = jnp.concatenate([rows_of(pg, NSA_GROUPS + g) for pg in pages], axis=0).astype(jnp.bfloat16)
        s = lax.dot_general(q_ref[0, g], k, (((1,), (1,)), ((), ())), preferred_element_type=jnp.float32)
        _flash_update(s + bias_ref[g], tile_mask(mask_ref, g), v, m_ref, l_ref, acc_ref, g)

    @pl.when(s_idx == n_steps - 1)
    def _():
        for g in range(NSA_GROUPS):
            k = new_ref[0, :, pl.ds(g * dh, dh)].astype(jnp.bfloat16)
            v = new_ref[0, :, pl.ds((NSA_GROUPS + g) * dh, dh)].astype(jnp.bfloat16)
            s = lax.dot_general(q_ref[0, g], k, (((1,), (1,)), ((), ())), preferred_element_type=jnp.float32)
            _flash_update(s + biasn_ref[g], tile_mask(maskn_ref, g), v, m_ref, l_ref, acc_ref, g)
            o_ref[0, g] = acc_ref[g] / jnp.maximum(l_ref[g], 1e-30)


def dec_sel(qg, pool, page_table, new_rows, bias_p, mask_p, bias_n, mask_n, t):
    B, G, R_, dh = qg.shape
    NP = page_table.shape[1]
    assert NP % PPS == 0
    n_steps = NP // PPS
    W = PPS * PAGE_SIZE
    pool = pool.reshape(pool.shape[0], PAGE_SIZE * KVG, dh)

    def page_spec(u):
        return pl.BlockSpec((1, PAGE_SIZE * KVG, dh), lambda b, s, pt: (pt[b, s * PPS + u], 0, 0))

    grid_spec = pltpu.PrefetchScalarGridSpec(
        num_scalar_prefetch=1,
        grid=(B, n_steps),
        in_specs=[pl.BlockSpec((1, G, R_, dh), lambda b, s, pt: (b, 0, 0, 0))]
        + [page_spec(u) for u in range(PPS)]
        + [pl.BlockSpec((1, PAGE_SIZE, ROW_W), lambda b, s, pt: (b, 0, 0)),
           pl.BlockSpec((G, R_, W), lambda b, s, pt: (0, 0, s)),
           pl.BlockSpec((1, G, t, W), lambda b, s, pt: (b, 0, 0, s)),
           pl.BlockSpec((G, R_, PAGE_SIZE), lambda b, s, pt: (0, 0, 0)),
           pl.BlockSpec((1, G, t, PAGE_SIZE), lambda b, s, pt: (b, 0, 0, 0))],
        out_specs=pl.BlockSpec((1, G, R_, dh), lambda b, s, pt: (b, 0, 0, 0)),
        scratch_shapes=[pltpu.VMEM((G, R_, 1), jnp.float32), pltpu.VMEM((G, R_, 1), jnp.float32),
                        pltpu.VMEM((G, R_, dh), jnp.float32)],
    )
    return pl.pallas_call(
        functools.partial(_dec_sel_kernel, n_steps=n_steps, t=t),
        grid_spec=grid_spec,
        out_shape=jax.ShapeDtypeStruct((B, G, R_, dh), jnp.float32),
        compiler_params=pltpu.CompilerParams(dimension_semantics=("parallel", "arbitrary"),
                                             vmem_limit_bytes=VMEM_LIMIT),
        name="nsa_dec_sel",
    )(page_table, qg, *([pool] * PPS), new_rows, bias_p, mask_p, bias_n, mask_n)


def _dec_win_kernel(q_ref, win_ref, new_ref, bias_ref, valid_ref, o_ref):
    dh = NSA_HEAD_DIM
    valid = valid_ref[...] > 0.5
    for g in range(NSA_GROUPS):
        wn = win_ref.shape[1] // KVG
        k = jnp.concatenate([win_ref[0, pl.ds(g, wn, stride=KVG), :], new_ref[0, :, pl.ds(g * dh, dh)]], axis=0)
        v = jnp.concatenate([win_ref[0, pl.ds(NSA_GROUPS + g, wn, stride=KVG), :],
                             new_ref[0, :, pl.ds((NSA_GROUPS + g) * dh, dh)]], axis=0)
        s = lax.dot_general(q_ref[0, g], k.astype(jnp.bfloat16), (((1,), (1,)), ((), ())),
                            preferred_element_type=jnp.float32)
        p = _softmax_rows(s + bias_ref[g], valid)
        o_ref[0, g] = jnp.dot(p.astype(jnp.bfloat16), v.astype(jnp.bfloat16), preferred_element_type=jnp.float32)


def dec_win(qg, win_rows, new_rows, bias_w, valid_w):
    B, G, R_, dh = qg.shape
    per_b = lambda a: pl.BlockSpec((1,) + a.shape[1:], lambda b: (b,) + (0,) * (a.ndim - 1))
    full = lambda a: pl.BlockSpec(a.shape, lambda b: (0,) * a.ndim)
    return pl.pallas_call(
        _dec_win_kernel,
        grid=(B,),
        in_specs=[per_b(qg), per_b(win_rows), per_b(new_rows), full(bias_w), full(valid_w)],
        out_specs=pl.BlockSpec((1, G, R_, dh), lambda b: (b, 0, 0, 0)),
        out_shape=jax.ShapeDtypeStruct((B, G, R_, dh), jnp.float32),
        compiler_params=pltpu.CompilerParams(dimension_semantics=("parallel",), vmem_limit_bytes=VMEM_LIMIT),
        name="nsa_dec_win",
    )(qg, win_rows, new_rows, bias_w, valid_w)


def nsa_decode(q, gates, kc, vc, slc_pool, page_table, slc_new, win_buf, win_new_rows, table):
    B, t = q.shape[0], q.shape[1]
    G, HPG, dh = NSA_GROUPS, NSA_HPG, NSA_HEAD_DIM
    NP = page_table.shape[1]
    past = NP * PAGE_SIZE
    Wn = win_buf.shape[1]
    assert t <= PAGE_SIZE and Wn == WINDOW
    L = past + t
    nc = L // CMP_STRIDE - 1
    ncp = kc.shape[2]
    Ns = -(-L // SEL_BLOCK)
    nsp = -(-Ns // LANE) * LANE
    bf = jnp.bfloat16
    R_ = HPG * t
    qg = jnp.transpose(q.reshape(B, t, G, HPG, dh), (0, 2, 3, 1, 4)).reshape(B, G, R_, dh).astype(bf)
    qpos = past + jnp.arange(t)

    r_lo, r_hi = -(PAGE_SIZE + WINDOW), past + t
    lut_rev = table[t5_bucket(jnp.arange(r_hi, r_lo - 1, -1))].astype(jnp.float32)

    def rows_bias(p0, step, n):
        rows = []
        for i in range(t):
            j0 = r_hi - (past + i - p0)
            assert j0 >= 0 and j0 + step * (n - 1) < lut_rev.shape[0]
            rows.append(lax.slice(lut_rev, (j0, 0), (j0 + step * (n - 1) + 1, NSA_HEADS), (step, 1)))
        b = jnp.transpose(jnp.stack(rows), (2, 0, 1))
        return b.reshape(G, R_, n)

    rep = lambda m: jnp.tile(m, (HPG, 1))
    cidx = jnp.arange(ncp)
    cend = cidx * CMP_STRIDE + (CMP_BLOCK - 1)
    rel_c = qpos[:, None] - cend[None, :]
    valid_c = rep(((rel_c >= 0) & (cidx[None, :] < nc)).astype(jnp.float32))
    cstart = cidx[:, None] * CMP_STRIDE
    sstart = jnp.arange(nsp)[None, :] * SEL_BLOCK
    overlap = ((cstart < sstart + SEL_BLOCK) & (cstart + CMP_BLOCK > sstart)
               & (cidx[:, None] < nc) & (jnp.arange(nsp)[None, :] < Ns)).astype(bf)
    o_c, msel = dec_cmp(qg, kc, vc, rows_bias(CMP_BLOCK - 1, CMP_STRIDE, ncp), valid_c, overlap, t, past, Ns)
    msel = msel[..., :Ns]
    kpos = jnp.arange(past + PAGE_SIZE)
    nblk = -(-(past + PAGE_SIZE) // SEL_BLOCK)
    mkey = jnp.repeat(jnp.pad(msel, ((0, 0), (0, 0), (0, 0), (0, nblk - Ns))), SEL_BLOCK, axis=-1)
    mkey = mkey[..., :past + PAGE_SIZE]
    mkey = mkey * ((kpos[None, :] <= qpos[:, None]) & (kpos[None, :] < L)).astype(jnp.float32)
    bias_s = rows_bias(0, 1, past + PAGE_SIZE)
    pad_new = lambda r: jnp.pad(r, ((0, 0), (0, PAGE_SIZE - t), (0, 0)))
    o_s = dec_sel(qg, slc_pool, page_table, pad_new(slc_new), bias_s[..., :past], mkey[..., :past],
                  bias_s[..., past:], mkey[..., past:], t)
    wpos = jnp.concatenate([past - Wn + jnp.arange(Wn), past + jnp.arange(PAGE_SIZE)])
    rel_w = qpos[:, None] - wpos[None, :]
    valid_w = rep(((rel_w >= 0) & (rel_w < WINDOW) & (wpos[None, :] >= 0) & (wpos[None, :] < L)).astype(jnp.float32))
    bias_w = jnp.concatenate([rows_bias(past - Wn, 1, Wn), rows_bias(past, 1, PAGE_SIZE)], axis=-1)
    o_w = dec_win(qg, win_buf.reshape(B, Wn * KVG, dh), pad_new(win_new_rows), bias_w, valid_w)
    back = lambda o: jnp.transpose(o.reshape(B, G, HPG, t, dh), (0, 3, 1, 2, 4)).reshape(B, t, G * HPG * dh)
    gx = jnp.repeat(gates.reshape(B, t, G * HPG, 3), dh, axis=2)
    return gx[..., 0] * back(o_c) + gx[..., 1] * back(o_s) + gx[..., 2] * back(o_w)


def kv_rows(h, P):
    B, T, D = h.shape
    return norm_mm(h.reshape(B * T, D), P['norm_kv'], P['kv_w'])[0].reshape(B, T, 3, ROW_W)


def run_trunk(x, gla_s0, past, P):
    B, T, _ = x.shape
    h = x
    gla_states = []
    kv5 = lambda r: r.reshape(r.shape[0], r.shape[1], 2, NSA_GROUPS, NSA_HEAD_DIM)
    for layer in range(DEPTH):
        if layer < N_A_LAYERS:
            h, s = gla_mixer(h, P['norm_mix'][layer], gla_s0[layer], P['gla_w_in'][layer],
                             P['gla_w_gate_up'][layer], P['gla_b_gate'][layer], P['gla_norm'][layer],
                             P['gla_w_out'][layer])
            gla_states.append(s)
        else:
            j = layer - N_A_LAYERS
            qd = NSA_HEADS * NSA_HEAD_DIM
            proj = norm_mm(h.reshape(B * T, -1), P['norm_mix'][layer], P['nsa_w_in'][j])[0].reshape(B, T, -1)
            q = proj[..., :qd].reshape(B, T, NSA_HEADS, NSA_HEAD_DIM) * NSA_HEAD_DIM ** -0.5
            gates = jax.nn.sigmoid(proj[..., qd:].astype(jnp.float32)).reshape(B, T, NSA_HEADS, 3).astype(x.dtype)
            if past is None:
                r5 = rows.reshape(B, T, 3, 2, NSA_GROUPS, NSA_HEAD_DIM)
                att = nsa_prompt(q, gates, kc, vc, r5[:, :, 1, 0], r5[:, :, 1, 1], r5[:, :, 2, 0], r5[:, :, 2, 1],
                                 P['rel_bias'])
            else:
                att = nsa_decode(q, gates, kc, vc, past['slc_pool'], past['page_table'], rows[:, :, 1],
                                 past['win_buf'], rows[:, :, 2], P['rel_bias'])
            h = mm_res(att, P['nsa_w_out'][j], h)
        h = peer_ffn(h, P['norm_ffn'][layer], P['peer_w_q'][layer], P['peer_subkeys'][layer],
                     P['peer_u_bf'], P['peer_v_bf'], layer)
        if layer == N_A_LAYERS - 1:
            rows = kv_rows(h, P)
            if past is None:
                assert T % (PAGE_SIZE * PPS) == 0 and T % TQ == 0
                npg = T // PAGE_SIZE
                pool = rows[:, :, 0].reshape(B * npg, PAGE_SIZE, 2, NSA_GROUPS, NSA_HEAD_DIM)
                table = jnp.arange(B * npg, dtype=jnp.int32).reshape(B, npg)
                win_new = kv5(rows[:, T - min(WINDOW, T):, 2])
            else:
                past_len = past['page_table'].shape[1] * PAGE_SIZE
                assert (past_len + T) // CMP_STRIDE == past_len // CMP_STRIDE
                pool, table = past['cmp_pool'], past['page_table']
                win_all = jnp.concatenate([past['win_buf'], kv5(rows[:, :, 2])], axis=1)
                win_new = win_all[:, win_all.shape[1] - min(WINDOW, win_all.shape[1]):]
            kc, vc = compress_paged(pool, table, *P['cmp_stack'])
    return (rmsnorm(h, P['norm_final']), jnp.stack(gla_states), kv5(rows[:, :, 0]), kv5(rows[:, :, 1]), win_new)


def kernel(x_prompt, x_sample, state_gla, cache_cmp_kv, cache_slc_kv, cache_win_kv, page_table,
           norm_mix, norm_ffn, norm_kv, norm_final, gla_w_in, gla_w_gate_up, gla_b_gate, gla_norm, gla_w_out,
           kv_w, cmp_k_w1, cmp_k_b1, cmp_k_w2, cmp_k_b2, cmp_k_pe, cmp_v_w1, cmp_v_b1, cmp_v_w2, cmp_v_b2, cmp_v_pe,
           nsa_w_in, nsa_w_out, rel_bias, peer_w_q, peer_subkeys, peer_u, peer_v):
    P = dict(norm_mix=norm_mix, norm_ffn=norm_ffn, norm_kv=norm_kv, norm_final=norm_final,
             gla_w_in=gla_w_in, gla_w_gate_up=gla_w_gate_up, gla_b_gate=gla_b_gate, gla_norm=gla_norm,
             gla_w_out=gla_w_out, kv_w=kv_w,
             cmp_k_w1=cmp_k_w1, cmp_k_b1=cmp_k_b1, cmp_k_w2=cmp_k_w2, cmp_k_b2=cmp_k_b2, cmp_k_pe=cmp_k_pe,
             cmp_v_w1=cmp_v_w1, cmp_v_b1=cmp_v_b1, cmp_v_w2=cmp_v_w2, cmp_v_b2=cmp_v_b2, cmp_v_pe=cmp_v_pe,
             nsa_w_in=nsa_w_in, nsa_w_out=nsa_w_out, rel_bias=rel_bias,
             peer_w_q=peer_w_q, peer_subkeys=peer_subkeys,
             peer_u_bf=peer_u.astype(jnp.bfloat16), peer_v_bf=peer_v.astype(jnp.bfloat16))
    P['cmp_stack'] = compress_params(P)
    past = dict(cmp_pool=cache_cmp_kv, slc_pool=cache_slc_kv, win_buf=cache_win_kv,
                page_table=page_table.astype(jnp.int32))
    gla_zero = jnp.zeros((N_A_LAYERS, BATCH, GLA_HEADS, GLA_DK, GLA_DV), x_prompt.dtype)
    y_prompt, gla_p, cmp_p, slc_p, win_p = run_trunk(x_prompt, gla_zero, None, P)
    y_sample, gla_s, cmp_s, slc_s, win_s = run_trunk(x_sample, state_gla, past, P)
    return (y_prompt, y_sample, gla_p, gla_s, cmp_p, cmp_s, slc_p, slc_s, win_p, win_s)
```
